```python
import jax, jax.numpy as jnp
from jax import lax
import numpy as np

D_MODEL = 2048
BATCH = 2
SEQ = 4096
DEPTH = 1

ATTN_HEAD_DIM = 64
ATTN_HEADS = D_MODEL // ATTN_HEAD_DIM
ATTN_KV_HEADS = ATTN_HEADS // 8
ATTN_GROUP = ATTN_HEADS // ATTN_KV_HEADS
WINDOW = 128
RET_HEADS = 8
RET_QK_DIM = D_MODEL // RET_HEADS
RET_V_DIM = D_MODEL // RET_HEADS
RET_CHUNK = 128
ROPE_BASE = 10000.0
N_GROUPS = 4
EXPERTS_PER_GROUP = 16
N_EXPERTS = N_GROUPS * EXPERTS_PER_GROUP
TOP_K = 2
EXPERT_DIM = D_MODEL // 2
MOE_BLOCK = 128
EPS = 1e-6

kernel_name = "hybrid_swa_retention_hmoe_block"


def rms_norm(x, gain):
    xf = x.astype(jnp.float32)
    y = xf * lax.rsqrt(jnp.mean(xf * xf, axis=-1, keepdims=True) + EPS)
    return (y * gain.astype(jnp.float32)).astype(x.dtype)


def modulate(h, shift, scale):
    return h * (1.0 + scale[:, None, :]) + shift[:, None, :]


def sliding_window_sink_attention(q, k, v, sinks):
    b, s, kvh, g, dh = q.shape
    nb = s // WINDOW
    qb = q.reshape(b, nb, WINDOW, kvh, g, dh)

    def band(t):
        tb = t.reshape(b, nb, WINDOW, kvh, dh)
        prev = jnp.concatenate([jnp.zeros_like(tb[:, :1]), tb[:, :-1]], axis=1)
        return jnp.concatenate([prev, tb], axis=2)

    kb, vb = band(k), band(v)
    scores = jnp.einsum('bnqkgd,bnskd->bnkgqs', qb, kb,
                        preferred_element_type=jnp.float32) * (dh ** -0.5)
    qi = jnp.arange(WINDOW)[:, None]
    sj = jnp.arange(2 * WINDOW)[None, :]
    in_band = (sj > qi) & (sj <= qi + WINDOW)
    has_prev = (jnp.arange(nb) > 0)[:, None, None] | (sj >= WINDOW)[None]
    valid = in_band[None] & has_prev
    scores = jnp.where(valid[None, :, None, None], scores, -jnp.inf)
    sink = sinks.astype(jnp.float32).reshape(kvh, g)[None, None, :, :, None, None]
    m = jnp.maximum(jnp.max(scores, axis=-1, keepdims=True), sink)
    p = jnp.exp(scores - m)
    denom = jnp.sum(p, axis=-1, keepdims=True) + jnp.exp(sink - m)
    probs = (p / denom).astype(v.dtype)
    out = jnp.einsum('bnkgqs,bnskd->bnqkgd', probs, vb)
    return out.reshape(b, s, kvh * g * dh)


def rotary(t, positions):
    half = t.shape[-1] // 2
    inv_freq = ROPE_BASE ** (-jnp.arange(half, dtype=jnp.float32) / half)
    ang = positions.astype(jnp.float32)[..., None] * inv_freq
    cos = jnp.cos(ang)[:, :, None, :]
    sin = jnp.sin(ang)[:, :, None, :]
    t1, t2 = t[..., :half], t[..., half:]
    return jnp.concatenate([t1 * cos - t2 * sin, t1 * sin + t2 * cos], axis=-1)


def chunkwise_retention(q, k, v):
    b, s, h, dk = q.shape
    dv = v.shape[-1]
    nc = s // RET_CHUNK
    log_gamma = jnp.log1p(-(2.0 ** (-5.0 - jnp.arange(h, dtype=jnp.float32))))
    idx = jnp.arange(RET_CHUNK, dtype=jnp.float32)
    diff = idx[:, None] - idx[None, :]
    decay_intra = jnp.where(diff >= 0,
                            jnp.exp(jnp.maximum(diff, 0.0) * log_gamma[:, None, None]), 0.0)
    decay_q = jnp.exp((idx + 1.0) * log_gamma[:, None])
    decay_k = jnp.exp((RET_CHUNK - 1.0 - idx) * log_gamma[:, None])
    decay_chunk = jnp.exp(RET_CHUNK * log_gamma)

    def to_chunks(t):
        return t.reshape(b, nc, RET_CHUNK, h, -1).transpose(1, 0, 3, 2, 4)

    def step(state, qkv):
        qc, kc, vc = qkv
        intra = jnp.einsum('bhcd,bhed->bhce', qc, kc) * decay_intra
        o = (jnp.einsum('bhce,bhef->bhcf', intra, vc)
             + jnp.einsum('bhcd,bhdf->bhcf', qc, state) * decay_q[None, :, :, None])
        state = (state * decay_chunk[None, :, None, None]
                 + jnp.einsum('bhcd,bhcf->bhdf', kc * decay_k[None, :, :, None], vc))
        return state, o

    state0 = jnp.zeros((b, h, dk, dv), jnp.float32)
    _, o = lax.scan(step, state0, (to_chunks(q), to_chunks(k), to_chunks(v)))
    return o.transpose(1, 0, 3, 2, 4).reshape(b, s, h, dv)


def token_mixers(h, positions, w_in, attn_sinks, ret_norm_gain, w_branch_attn, w_branch_ret, w_out):
    b, s, _ = h.shape
    widths = [ATTN_HEADS * ATTN_HEAD_DIM, ATTN_KV_HEADS * ATTN_HEAD_DIM, ATTN_KV_HEADS * ATTN_HEAD_DIM,
              RET_HEADS * RET_QK_DIM, RET_HEADS * RET_QK_DIM, RET_HEADS * RET_V_DIM, RET_HEADS * RET_V_DIM,
              D_MODEL, D_MODEL]
    splits = [int(v) for v in np.cumsum(widths)[:-1]]
    proj = h @ w_in
    q_a, k_a, v_a, q_r, k_r, v_r, g_r, gate_a, gate_r = jnp.split(proj, splits, axis=-1)

    attn = sliding_window_sink_attention(
        q_a.reshape(b, s, ATTN_KV_HEADS, ATTN_GROUP, ATTN_HEAD_DIM),
        k_a.reshape(b, s, ATTN_KV_HEADS, ATTN_HEAD_DIM),
        v_a.reshape(b, s, ATTN_KV_HEADS, ATTN_HEAD_DIM),
        attn_sinks)

    qr = rotary(q_r.reshape(b, s, RET_HEADS, RET_QK_DIM).astype(jnp.float32), positions)
    kr = rotary(k_r.reshape(b, s, RET_HEADS, RET_QK_DIM).astype(jnp.float32), positions) * (RET_QK_DIM ** -0.5)
    vr = v_r.reshape(b, s, RET_HEADS, RET_V_DIM).astype(jnp.float32)
    ret = chunkwise_retention(qr, kr, vr)
    ret = ret * lax.rsqrt(jnp.mean(ret * ret, axis=-1, keepdims=True) + EPS)
    ret = ret * ret_norm_gain.astype(jnp.float32).reshape(RET_HEADS, RET_V_DIM)
    ret = (jax.nn.silu(g_r.astype(jnp.float32)) * ret.reshape(b, s, -1)).astype(h.dtype)

    merged = (jax.nn.sigmoid(gate_a) * (attn @ w_branch_attn)
              + jax.nn.sigmoid(gate_r) * (ret @ w_branch_ret))
    return merged @ w_out


def hierarchical_moe(h, w_router_group, b_router_group, w_router_expert, b_router_expert,
                     w_expert_gate, w_expert_up, w_expert_down):
    b, s, d = h.shape
    t = b * s
    hf = h.reshape(t, d)
    g_logits = (hf @ w_router_group).astype(jnp.float32) + b_router_group.astype(jnp.float32)
    g_prob = jax.nn.softmax(g_logits, axis=-1)
    g_sel = jnp.argmax(g_logits, axis=-1).astype(jnp.int32)
    g_w = jnp.take_along_axis(g_prob, g_sel[:, None], axis=1)
    e_logits = ((hf @ w_router_expert).astype(jnp.float32)
                + b_router_expert.astype(jnp.float32)).reshape(t, N_GROUPS, EXPERTS_PER_GROUP)
    e_logits = jnp.take_along_axis(e_logits, g_sel[:, None, None], axis=1)[:, 0]
    top_v, top_i = lax.top_k(e_logits, TOP_K)
    weights = (jax.nn.softmax(top_v, axis=-1) * g_w).reshape(-1)
    expert_ids = (g_sel[:, None] * EXPERTS_PER_GROUP + top_i).reshape(-1).astype(jnp.int32)
    token_ids = jnp.repeat(jnp.arange(t, dtype=jnp.int32), TOP_K)

    n_assign = t * TOP_K
    n_pad = -(-(n_assign + N_EXPERTS * (MOE_BLOCK - 1)) // MOE_BLOCK) * MOE_BLOCK
    n_blocks = n_pad // MOE_BLOCK
    order = jnp.argsort(expert_ids)
    sorted_e = expert_ids[order]
    counts = jnp.bincount(expert_ids, length=N_EXPERTS)
    padded = (counts + MOE_BLOCK - 1) // MOE_BLOCK * MOE_BLOCK
    start = jnp.cumsum(counts) - counts
    pad_end = jnp.cumsum(padded)
    pad_start = pad_end - padded
    dest = pad_start[sorted_e] + jnp.arange(n_assign, dtype=jnp.int32) - start[sorted_e]
    row_token = jnp.full((n_pad,), t, jnp.int32).at[dest].set(token_ids[order])
    row_weight = jnp.zeros((n_pad,), jnp.float32).at[dest].set(weights[order])
    block_expert = jnp.minimum(
        jnp.searchsorted(pad_end, jnp.arange(n_blocks, dtype=jnp.int32) * MOE_BLOCK, side='right'),
        N_EXPERTS - 1).astype(jnp.int32)
    x_rows = jnp.concatenate([hf, jnp.zeros((1, d), hf.dtype)], axis=0)[row_token]
    x_rows = x_rows.reshape(n_blocks, MOE_BLOCK, d)

    def expert_block(args):
        xb, e = args
        return (jax.nn.silu(xb @ w_expert_gate[e]) * (xb @ w_expert_up[e])) @ w_expert_down[e]

    y = lax.map(expert_block, (x_rows, block_expert)).reshape(n_pad, d)
    out = jnp.zeros((t + 1, d), h.dtype).at[row_token].add(y * row_weight[:, None].astype(y.dtype))
    return out[:t].reshape(b, s, d)


def setup_inputs(seed: int = 0) -> dict:
    key = jax.random.key(seed)
    ks = jax.random.split(key, 24)
    f32 = jnp.float32
    d = D_MODEL
    p_in = (ATTN_HEADS + 2 * ATTN_KV_HEADS) * ATTN_HEAD_DIM + RET_HEADS * (2 * RET_QK_DIM + 2 * RET_V_DIM) + 2 * d
    attn_w = ATTN_HEADS * ATTN_HEAD_DIM
    ret_w = RET_HEADS * RET_V_DIM

    def nrm(k, shape, scale):
        return jax.random.normal(k, shape, f32) * scale

    offset = jax.random.randint(ks[2], (BATCH, 1), 0, 1024, dtype=jnp.int32)
    positions = (offset + jnp.arange(SEQ, dtype=jnp.int32)[None, :]).astype(jnp.int32)
    return {
        "x": nrm(ks[0], (BATCH, SEQ, d), 1.0),
        "c": nrm(ks[1], (BATCH, d), 1.0),
        "positions": positions,
        "norm1_gain": 1.0 + nrm(ks[3], (DEPTH, d), 0.05),
        "norm2_gain": 1.0 + nrm(ks[4], (DEPTH, d), 0.05),
        "final_norm_gain": 1.0 + nrm(ks[5], (d,), 0.05),
        "w_ada": nrm(ks[6], (DEPTH, d, 6 * d), 0.5 * d ** -0.5),
        "b_ada": nrm(ks[7], (DEPTH, 6 * d), 0.02),
        "w_in": nrm(ks[8], (DEPTH, d, p_in), d ** -0.5),
        "attn_sinks": nrm(ks[9], (DEPTH, ATTN_HEADS), 1.0),
        "ret_norm_gain": 1.0 + nrm(ks[10], (DEPTH, ret_w), 0.05),
        "w_branch_attn": nrm(ks[11], (DEPTH, attn_w, d), attn_w ** -0.5),
        "w_branch_ret": nrm(ks[12], (DEPTH, ret_w, d), ret_w ** -0.5),
        "w_out": nrm(ks[13], (DEPTH, d, d), d ** -0.5),
        "w_router_group": nrm(ks[14], (DEPTH, d, N_GROUPS), d ** -0.5),
        "b_router_group": nrm(ks[15], (DEPTH, N_GROUPS), 0.01),
        "w_router_expert": nrm(ks[16], (DEPTH, d, N_EXPERTS), d ** -0.5),
        "b_router_expert": nrm(ks[17], (DEPTH, N_EXPERTS), 0.01),
        "w_expert_gate": nrm(ks[18], (DEPTH, N_EXPERTS, d, EXPERT_DIM), d ** -0.5),
        "w_expert_up": nrm(ks[19], (DEPTH, N_EXPERTS, d, EXPERT_DIM), d ** -0.5),
        "w_expert_down": nrm(ks[20], (DEPTH, N_EXPERTS, EXPERT_DIM, d), EXPERT_DIM ** -0.5),
    }


def reference(x, c, positions, norm1_gain, norm2_gain, final_norm_gain, w_ada, b_ada, w_in,
              attn_sinks, ret_norm_gain, w_branch_attn, w_branch_ret, w_out,
              w_router_group, b_router_group, w_router_expert, b_router_expert,
              w_expert_gate, w_expert_up, w_expert_down):
    for layer in range(DEPTH):
        mod = jax.nn.silu(c) @ w_ada[layer] + b_ada[layer]
        shift1, scale1, gate1, shift2, scale2, gate2 = jnp.split(mod, 6, axis=-1)
        h = modulate(rms_norm(x, norm1_gain[layer]), shift1, scale1)
        mix = token_mixers(h, positions, w_in[layer], attn_sinks[layer], ret_norm_gain[layer],
                           w_branch_attn[layer], w_branch_ret[layer], w_out[layer])
        x = x + gate1[:, None, :] * mix
        h2 = modulate(rms_norm(x, norm2_gain[layer]), shift2, scale2)
        ffn = hierarchical_moe(h2, w_router_group[layer], b_router_group[layer],
                               w_router_expert[layer], b_router_expert[layer],
                               w_expert_gate[layer], w_expert_up[layer], w_expert_down[layer])
        x = x + gate2[:, None, :] * ffn
    return rms_norm(x, final_norm_gain)
```

```python
import functools
import math

import jax
import jax.numpy as jnp
import numpy as np
from jax import lax
from jax.experimental import pallas as pl
from jax.experimental.pallas import tpu as pltpu

F32 = jnp.float32
BF16 = jnp.bfloat16

D_MODEL = 2048
ATTN_HEAD_DIM = 64
ATTN_HEADS = 32
ATTN_KV_HEADS = 4
ATTN_GROUP = 8
WINDOW = 128
RET_HEADS = 8
RET_DIM = 256
RET_CHUNK = 128
ROPE_BASE = 10000.0
N_GROUPS = 4
EXPERTS_PER_GROUP = 16
N_EXPERTS = 64
TOP_K = 2
EXPERT_DIM = 1024
MOE_BLOCK = 128
EPS = 1e-6
NEG = -1e30

MIB = 1024 * 1024
LANES = 128
PROJ_TN = 512
PROJ_TM = 1024
KV_TILE = 28
MERGE_TM = 256
RANK_TM = 256
ITEM_BLOCKS = 4
EXPERT_FC = 512
COMBINE_TM = 128

COL_QA, COL_QR, COL_KR, COL_VR, COL_GR, COL_GA, COL_GRT = 0, 1, 2, 3, 4, 5, 6

LOG_GAMMA = [math.log1p(-(2.0 ** (-5.0 - h))) for h in range(RET_HEADS)]


def _params(sem, vmem_mib):
    return pltpu.CompilerParams(dimension_semantics=sem, vmem_limit_bytes=vmem_mib * MIB)


def _ada_kernel(c_ref, w_ref, b_ref, o_ref):
    c = c_ref[...]
    a = (c * jax.nn.sigmoid(c)).astype(BF16)
    o_ref[...] = jnp.dot(a, w_ref[...].astype(BF16), preferred_element_type=F32) + b_ref[...]


def _ada(c8, w_ada, b_ada):
    n = w_ada.shape[1]
    tn = 1024
    return pl.pallas_call(
        _ada_kernel,
        grid=(n // tn,),
        in_specs=[pl.BlockSpec((8, D_MODEL), lambda j: (0, 0)),
                  pl.BlockSpec((D_MODEL, tn), lambda j: (0, j)),
                  pl.BlockSpec((1, tn), lambda j: (0, j))],
        out_specs=pl.BlockSpec((8, tn), lambda j: (0, j)),
        out_shape=jax.ShapeDtypeStruct((8, n), F32),
        compiler_params=_params(("arbitrary",), 40),
        name="ada",
    )(c8, w_ada, b_ada)


def _proj_kernel(x_ref, g_ref, mod_ref, w_ref, o_ref, h_ref):
    @pl.when(pl.program_id(1) == 0)
    def _():
        x = x_ref[...]
        var = jnp.mean(x * x, axis=-1, keepdims=True)
        y = x * lax.rsqrt(var + EPS) * g_ref[...]
        h_ref[...] = (y * (1.0 + mod_ref[0, 1:2, :]) + mod_ref[0, 0:1, :]).astype(BF16)

    o_ref[...] = jnp.dot(h_ref[...], w_ref[...].astype(BF16),
                         preferred_element_type=F32).astype(BF16)


def _proj_out_tile(j):
    return jnp.where(j < 4, j, jnp.where(j == 4, KV_TILE, j - 1))


def _proj(x2, gain, mod6, w_in, seq):
    t = x2.shape[0]
    n = w_in.shape[1]
    tiles_per_batch = seq // PROJ_TM
    return pl.pallas_call(
        _proj_kernel,
        grid=(t // PROJ_TM, n // PROJ_TN),
        in_specs=[pl.BlockSpec((PROJ_TM, D_MODEL), lambda i, j: (i, 0)),
                  pl.BlockSpec((1, D_MODEL), lambda i, j: (0, 0)),
                  pl.BlockSpec((1, 6, D_MODEL), lambda i, j: (i // tiles_per_batch, 0, 0)),
                  pl.BlockSpec((D_MODEL, PROJ_TN), lambda i, j: (0, j))],
        out_specs=pl.BlockSpec((PROJ_TM, PROJ_TN), lambda i, j: (i, _proj_out_tile(j))),
        out_shape=jax.ShapeDtypeStruct((t, n), BF16),
        scratch_shapes=[pltpu.VMEM((PROJ_TM, D_MODEL), BF16)],
        compiler_params=_params(("arbitrary", "arbitrary"), 48),
        name="proj",
    )(x2, gain, mod6, w_in)


def _attn_kernel(sink_ref, q_ref, kvp_ref, kvc_ref, o_ref):
    n = pl.program_id(1)
    kvp = kvp_ref[...]
    kvc = kvc_ref[...]
    qi = lax.broadcasted_iota(jnp.int32, (WINDOW, 2 * WINDOW), 0)
    sj = lax.broadcasted_iota(jnp.int32, (WINDOW, 2 * WINDOW), 1)
    valid = (sj > qi) & (sj <= qi + WINDOW) & ((sj >= WINDOW) | (n > 0))
    dh = ATTN_HEAD_DIM
    kv_w = ATTN_KV_HEADS * dh
    for kv in range(ATTN_KV_HEADS):
        kband = jnp.concatenate([kvp[:, kv * dh:(kv + 1) * dh], kvc[:, kv * dh:(kv + 1) * dh]], axis=0)
        vband = jnp.concatenate([kvp[:, kv_w + kv * dh:kv_w + (kv + 1) * dh],
                                 kvc[:, kv_w + kv * dh:kv_w + (kv + 1) * dh]], axis=0)
        outs = []
        for g in range(ATTN_GROUP):
            h = kv * ATTN_GROUP + g
            qh = q_ref[:, h * dh:(h + 1) * dh]
            s = lax.dot_general(qh, kband, (((1,), (1,)), ((), ())),
                                preferred_element_type=F32) * (dh ** -0.5)
            s = jnp.where(valid, s, NEG)
            sink = sink_ref[h]
            m = jnp.maximum(jnp.max(s, axis=-1, keepdims=True), sink)
            p = jnp.exp(s - m)
            denom = jnp.sum(p, axis=-1, keepdims=True) + jnp.exp(sink - m)
            o = jnp.dot(p.astype(BF16), vband, preferred_element_type=F32)
            outs.append(o / denom)
        o_ref[:, kv * ATTN_GROUP * dh:(kv + 1) * ATTN_GROUP * dh] = (
            jnp.concatenate(outs, axis=-1).astype(BF16))


def _attn(proj, sinks, batch, seq):
    nb = seq // WINDOW
    t = batch * seq
    kv_blk = KV_TILE

    def cur(b, n):
        return (b * nb + n, kv_blk)

    def prev(b, n):
        return (b * nb + jnp.maximum(n - 1, 0), kv_blk)

    return pl.pallas_call(
        _attn_kernel,
        grid=(batch, nb),
        in_specs=[pl.BlockSpec(memory_space=pltpu.SMEM),
                  pl.BlockSpec((WINDOW, D_MODEL), lambda b, n: (b * nb + n, COL_QA)),
                  pl.BlockSpec((WINDOW, PROJ_TN), prev),
                  pl.BlockSpec((WINDOW, PROJ_TN), cur)],
        out_specs=pl.BlockSpec((WINDOW, D_MODEL), lambda b, n: (b * nb + n, 0)),
        out_shape=jax.ShapeDtypeStruct((t, D_MODEL), BF16),
        compiler_params=_params(("arbitrary", "arbitrary"), 32),
        name="attn",
    )(sinks, proj, proj, proj)


def _ret_kernel(pos_ref, invf_ref, q_ref, k_ref, v_ref, g_ref, gain_ref, o_ref, state_ref):
    @pl.when(pl.program_id(1) == 0)
    def _():
        state_ref[...] = jnp.zeros_like(state_ref)

    half = RET_DIM // 2
    ang = pos_ref[...].astype(F32) * invf_ref[...]
    cos = jnp.cos(ang)
    sin = jnp.sin(ang)
    idx = lax.broadcasted_iota(jnp.int32, (RET_CHUNK, 1), 0).astype(F32)
    ii = lax.broadcasted_iota(jnp.int32, (RET_CHUNK, RET_CHUNK), 0)
    jj = lax.broadcasted_iota(jnp.int32, (RET_CHUNK, RET_CHUNK), 1)
    diff = (ii - jj).astype(F32)

    def rot(t):
        t1, t2 = t[:, :half], t[:, half:]
        return jnp.concatenate([t1 * cos - t2 * sin, t1 * sin + t2 * cos], axis=-1)

    for h in range(RET_HEADS):
        lg = LOG_GAMMA[h]
        sl = slice(h * RET_DIM, (h + 1) * RET_DIM)
        qr = rot(q_ref[:, sl].astype(F32))
        kr = rot(k_ref[:, sl].astype(F32)) * (RET_DIM ** -0.5)
        vb = v_ref[:, sl]
        d_intra = jnp.where(diff >= 0, jnp.exp(jnp.maximum(diff, 0.0) * lg), 0.0)
        d_q = jnp.exp((idx + 1.0) * lg)
        d_k = jnp.exp((RET_CHUNK - 1.0 - idx) * lg)
        d_chunk = math.exp(RET_CHUNK * lg)
        qb = qr.astype(BF16)
        kb = kr.astype(BF16)
        intra = lax.dot_general(qb, kb, (((1,), (1,)), ((), ())),
                                preferred_element_type=F32) * d_intra
        st = state_ref[h]
        o = (jnp.dot(intra.astype(BF16), vb, preferred_element_type=F32)
             + jnp.dot(qb, st.astype(BF16), preferred_element_type=F32) * d_q)
        kd = (kr * d_k).astype(BF16)
        state_ref[h] = st * d_chunk + lax.dot_general(kd, vb, (((0,), (0,)), ((), ())),
                                                      preferred_element_type=F32)
        o = o * lax.rsqrt(jnp.mean(o * o, axis=-1, keepdims=True) + EPS) * gain_ref[:, sl]
        gg = g_ref[:, sl].astype(F32)
        o_ref[:, sl] = (gg * jax.nn.sigmoid(gg) * o).astype(BF16)


def _ret(proj, pos, inv_freq, ret_gain, batch, seq):
    nc = seq // RET_CHUNK
    t = batch * seq

    def col(cb):
        return lambda b, c: (b * nc + c, cb)

    return pl.pallas_call(
        _ret_kernel,
        grid=(batch, nc),
        in_specs=[pl.BlockSpec((RET_CHUNK, 1), lambda b, c: (b * nc + c, 0)),
                  pl.BlockSpec((1, RET_DIM // 2), lambda b, c: (0, 0)),
                  pl.BlockSpec((RET_CHUNK, D_MODEL), col(COL_QR)),
                  pl.BlockSpec((RET_CHUNK, D_MODEL), col(COL_KR)),
                  pl.BlockSpec((RET_CHUNK, D_MODEL), col(COL_VR)),
                  pl.BlockSpec((RET_CHUNK, D_MODEL), col(COL_GR)),
                  pl.BlockSpec((1, D_MODEL), lambda b, c: (0, 0))],
        out_specs=pl.BlockSpec((RET_CHUNK, D_MODEL), lambda b, c: (b * nc + c, 0)),
        out_shape=jax.ShapeDtypeStruct((t, D_MODEL), BF16),
        scratch_shapes=[pltpu.VMEM((RET_HEADS, RET_DIM, RET_DIM), F32)],
        compiler_params=_params(("arbitrary", "arbitrary"), 32),
        name="ret",
    )(pos, inv_freq, proj, proj, proj, proj, ret_gain)


def _route(logits):
    lane = lax.broadcasted_iota(jnp.int32, logits.shape, 1)
    lane_f = lane.astype(F32)
    is_g = lane < N_GROUPS
    gl = jnp.where(is_g, logits, NEG)
    gmax = jnp.max(gl, axis=-1, keepdims=True)
    gsel = jnp.min(jnp.where(gl == gmax, lane_f, float(LANES)), axis=-1, keepdims=True)
    gsum = jnp.sum(jnp.where(is_g, jnp.exp(gl - gmax), 0.0), axis=-1, keepdims=True)
    g_w = 1.0 / gsum
    grp = ((lane - N_GROUPS) >> 4).astype(F32)
    is_e = (lane >= N_GROUPS) & (lane < N_GROUPS + N_EXPERTS) & (grp == gsel)
    el = jnp.where(is_e, logits, NEG)
    v1 = jnp.max(el, axis=-1, keepdims=True)
    i1 = jnp.min(jnp.where(el == v1, lane_f, float(LANES)), axis=-1, keepdims=True)
    el2 = jnp.where(lane_f == i1, NEG, el)
    v2 = jnp.max(el2, axis=-1, keepdims=True)
    i2 = jnp.min(jnp.where(el2 == v2, lane_f, float(LANES)), axis=-1, keepdims=True)
    tt = jnp.exp(v2 - v1)
    w1 = g_w / (1.0 + tt)
    w2 = g_w * tt / (1.0 + tt)
    return jnp.where(lane == 0, i1 - N_GROUPS,
                     jnp.where(lane == 1, i2 - N_GROUPS,
                               jnp.where(lane == 2, w1, jnp.where(lane == 3, w2, 0.0))))


def _merge_kernel(attn_ref, ret_ref, ga_ref, gr_ref, x_ref, mod_ref, g2_ref,
                  wa_ref, wr_ref, wo_ref, wrt_ref, brt_ref, x1_ref, h2_ref, route_ref):
    a = jnp.dot(attn_ref[...], wa_ref[...], preferred_element_type=F32)
    r = jnp.dot(ret_ref[...], wr_ref[...], preferred_element_type=F32)
    merged = (jax.nn.sigmoid(ga_ref[...].astype(F32)) * a
              + jax.nn.sigmoid(gr_ref[...].astype(F32)) * r)
    mix = jnp.dot(merged.astype(BF16), wo_ref[...], preferred_element_type=F32)
    x1 = x_ref[...] + mod_ref[0, 2:3, :] * mix
    x1_ref[...] = x1
    var = jnp.mean(x1 * x1, axis=-1, keepdims=True)
    h2 = x1 * lax.rsqrt(var + EPS) * g2_ref[...]
    h2 = h2 * (1.0 + mod_ref[0, 4:5, :]) + mod_ref[0, 3:4, :]
    h2_ref[...] = h2
    logits = jnp.dot(h2, wrt_ref[...], preferred_element_type=F32,
                     precision=lax.Precision.HIGHEST) + brt_ref[...]
    route_ref[...] = _route(logits)


def _merge(attn, ret, proj, x2, mod6, gain2, wa, wr, wo, w_rt, b_rt, seq):
    t = x2.shape[0]
    tm = MERGE_TM
    tiles_per_batch = seq // tm
    row = lambda i: (i, 0)
    const = lambda i: (0, 0)
    wspec = pl.BlockSpec((D_MODEL, D_MODEL), const, pipeline_mode=pl.Buffered(1))
    return pl.pallas_call(
        _merge_kernel,
        grid=(t // tm,),
        in_specs=[pl.BlockSpec((tm, D_MODEL), row),
                  pl.BlockSpec((tm, D_MODEL), row),
                  pl.BlockSpec((tm, D_MODEL), lambda i: (i, COL_GA)),
                  pl.BlockSpec((tm, D_MODEL), lambda i: (i, COL_GRT)),
                  pl.BlockSpec((tm, D_MODEL), row),
                  pl.BlockSpec((1, 6, D_MODEL), lambda i: (i // tiles_per_batch, 0, 0)),
                  pl.BlockSpec((1, D_MODEL), const),
                  wspec, wspec, wspec,
                  pl.BlockSpec((D_MODEL, LANES), const),
                  pl.BlockSpec((1, LANES), const)],
        out_specs=[pl.BlockSpec((tm, D_MODEL), row),
                   pl.BlockSpec((tm, D_MODEL), row),
                   pl.BlockSpec((tm, LANES), row)],
        out_shape=[jax.ShapeDtypeStruct((t, D_MODEL), F32),
                   jax.ShapeDtypeStruct((t, D_MODEL), F32),
                   jax.ShapeDtypeStruct((t, LANES), F32)],
        compiler_params=_params(("arbitrary",), 56),
        name="merge",
    )(attn, ret, proj, proj, x2, mod6, gain2, wa, wr, wo, w_rt, b_rt)


def _rank_kernel(route_ref, rank_ref, count_ref, carry_ref):
    @pl.when(pl.program_id(0) == 0)
    def _():
        carry_ref[...] = jnp.zeros_like(carry_ref)

    route = route_ref[...]
    tm = route.shape[0]
    lane = lax.broadcasted_iota(jnp.int32, route.shape, 1)
    lane_f = lane.astype(F32)
    hot1 = lane_f == route[:, 0:1]
    hot2 = lane_f == route[:, 1:2]
    both = jnp.where(hot1 | hot2, 1.0, 0.0)
    ii = lax.broadcasted_iota(jnp.int32, (tm, tm), 0)
    jj = lax.broadcasted_iota(jnp.int32, (tm, tm), 1)
    lower = jnp.where(ii > jj, 1.0, 0.0).astype(BF16)
    before = jnp.dot(lower, both.astype(BF16), preferred_element_type=F32) + carry_ref[...]
    r1 = jnp.sum(jnp.where(hot1, before, 0.0), axis=-1, keepdims=True)
    r2 = jnp.sum(jnp.where(hot2, before, 0.0), axis=-1, keepdims=True)
    rank_ref[...] = jnp.where(lane == 0, r1, jnp.where(lane == 1, r2, 0.0))
    carry = carry_ref[...] + jnp.sum(both, axis=0, keepdims=True)
    carry_ref[...] = carry
    count_ref[...] = carry


def _rank(route):
    t = route.shape[0]
    tm = RANK_TM
    return pl.pallas_call(
        _rank_kernel,
        grid=(t // tm,),
        in_specs=[pl.BlockSpec((tm, LANES), lambda i: (i, 0))],
        out_specs=[pl.BlockSpec((tm, LANES), lambda i: (i, 0)),
                   pl.BlockSpec((1, LANES), lambda i: (0, 0))],
        out_shape=[jax.ShapeDtypeStruct((t, LANES), F32),
                   jax.ShapeDtypeStruct((1, LANES), F32)],
        scratch_shapes=[pltpu.VMEM((1, LANES), F32)],
        compiler_params=_params(("arbitrary",), 16),
        name="rank",
    )(route)


def _gather_kernel(tok_ref, h2_ref, o_ref, sem):
    base = pl.program_id(0) * MOE_BLOCK

    def row_copy(r, tok):
        return pltpu.make_async_copy(h2_ref.at[pl.ds(tok, 1), :], o_ref.at[pl.ds(r, 1), :], sem)

    def issue(r, carry):
        row_copy(r, tok_ref[base + r]).start()
        return carry

    lax.fori_loop(0, MOE_BLOCK, issue, 0)

    def drain(r, carry):
        row_copy(r, 0).wait()
        return carry

    lax.fori_loop(0, MOE_BLOCK, drain, 0)


def _gather(row_token, h2, n_blocks):
    d = h2.shape[1]
    return pl.pallas_call(
        _gather_kernel,
        grid_spec=pltpu.PrefetchScalarGridSpec(
            num_scalar_prefetch=1,
            grid=(n_blocks,),
            in_specs=[pl.BlockSpec(memory_space=pl.ANY)],
            out_specs=pl.BlockSpec((MOE_BLOCK, d), lambda b, tok: (b, 0)),
            scratch_shapes=[pltpu.SemaphoreType.DMA(())]),
        out_shape=jax.ShapeDtypeStruct((n_blocks * MOE_BLOCK, d), F32),
        compiler_params=_params(("arbitrary",), 16),
        name="gather",
    )(row_token, h2)


def _experts_kernel(item_e_ref, item_blk_ref, item_nb_ref,
                    x0_ref, x1_ref, x2_ref, x3_ref, wg_ref, wu_ref, wd_ref, y_ref,
                    xb_ref, wgb_ref, wub_ref, wdb_ref, acc_ref, sem):
    w = pl.program_id(0)
    c = pl.program_id(1)
    last_c = pl.num_programs(1) - 1
    nb = item_nb_ref[w]
    blk0 = item_blk_ref[w]
    x_refs = (x0_ref, x1_ref, x2_ref, x3_ref)

    def out_copy(s):
        return pltpu.make_async_copy(acc_ref.at[s], y_ref.at[blk0 + s], sem.at[s])

    @pl.when(nb > 0)
    def _():
        wgb_ref[...] = wg_ref[...].astype(BF16)
        wub_ref[...] = wu_ref[...].astype(BF16)
        wdb_ref[...] = wd_ref[...].astype(BF16)

    for s in range(ITEM_BLOCKS):
        @pl.when(s < nb)
        def _(s=s):
            @pl.when(c == 0)
            def _():
                xb_ref[s] = x_refs[s][0].astype(BF16)

            xs = xb_ref[s]
            g = jnp.dot(xs, wgb_ref[...], preferred_element_type=F32)
            u = jnp.dot(xs, wub_ref[...], preferred_element_type=F32)
            hid = (g * jax.nn.sigmoid(g) * u).astype(BF16)
            y = jnp.dot(hid, wdb_ref[...], preferred_element_type=F32)

            @pl.when(c == 0)
            def _():
                acc_ref[s] = y

            @pl.when(c > 0)
            def _():
                acc_ref[s] += y

            @pl.when(c == last_c)
            def _():
                out_copy(s).start()

    @pl.when(c == last_c)
    def _():
        for s in range(ITEM_BLOCKS):
            @pl.when(s < nb)
            def _(s=s):
                out_copy(s).wait()


def _experts(item_e, item_blk, item_nb, x_sorted, w_gate, w_up, w_down, n_blocks):
    n_items = item_e.shape[0]
    d = D_MODEL
    fc = EXPERT_FC
    x3 = x_sorted.reshape(n_blocks, MOE_BLOCK, d)

    n_chunks = EXPERT_DIM // fc

    def x_spec(s):
        return pl.BlockSpec((1, MOE_BLOCK, d),
                            lambda w, c, ie, ib, inb: (jnp.minimum(ib[w] + s, n_blocks - 1), 0, 0))

    def chunk(w, c, inb):
        return jnp.where(inb[w] > 0, c, n_chunks - 1)

    return pl.pallas_call(
        _experts_kernel,
        grid_spec=pltpu.PrefetchScalarGridSpec(
            num_scalar_prefetch=3,
            grid=(n_items, n_chunks),
            in_specs=[x_spec(0), x_spec(1), x_spec(2), x_spec(3),
                      pl.BlockSpec((None, d, fc), lambda w, c, ie, ib, inb: (ie[w], 0, chunk(w, c, inb))),
                      pl.BlockSpec((None, d, fc), lambda w, c, ie, ib, inb: (ie[w], 0, chunk(w, c, inb))),
                      pl.BlockSpec((None, fc, d), lambda w, c, ie, ib, inb: (ie[w], chunk(w, c, inb), 0))],
            out_specs=pl.BlockSpec(memory_space=pl.ANY),
            scratch_shapes=[pltpu.VMEM((ITEM_BLOCKS, MOE_BLOCK, d), BF16),
                            pltpu.VMEM((d, fc), BF16),
                            pltpu.VMEM((d, fc), BF16),
                            pltpu.VMEM((fc, d), BF16),
                            pltpu.VMEM((ITEM_BLOCKS, MOE_BLOCK, d), F32),
                            pltpu.SemaphoreType.DMA((ITEM_BLOCKS,))]),
        out_shape=jax.ShapeDtypeStruct((n_blocks, MOE_BLOCK, d), F32),
        compiler_params=_params(("arbitrary", "arbitrary"), 56),
        name="experts",
    )(item_e, item_blk, item_nb, x3, x3, x3, x3, w_gate, w_up, w_down)


def _combine_kernel(dest_ref, x1_ref, route_ref, mod_ref, gain_ref, y_ref, o_ref, ybuf_ref, sem):
    tm = COMBINE_TM
    base = pl.program_id(0) * tm * TOP_K

    def row_copy(k, r, src):
        return pltpu.make_async_copy(y_ref.at[pl.ds(src, 1), :], ybuf_ref.at[k, pl.ds(r, 1), :], sem)

    def issue(r, carry):
        for k in range(TOP_K):
            row_copy(k, r, dest_ref[base + r * TOP_K + k]).start()
        return carry

    lax.fori_loop(0, tm, issue, 0)

    def drain(r, carry):
        for k in range(TOP_K):
            row_copy(k, r, 0).wait()
        return carry

    lax.fori_loop(0, tm, drain, 0)

    route = route_ref[...]
    ffn = route[:, 2:3] * ybuf_ref[0] + route[:, 3:4] * ybuf_ref[1]
    x2 = x1_ref[...] + mod_ref[0, 5:6, :] * ffn
    var = jnp.mean(x2 * x2, axis=-1, keepdims=True)
    o_ref[...] = x2 * lax.rsqrt(var + EPS) * gain_ref[...]


def _combine(dest, x1, route, mod6, gain, y_sorted, seq):
    t, d = x1.shape
    tm = COMBINE_TM
    tiles_per_batch = seq // tm
    return pl.pallas_call(
        _combine_kernel,
        grid_spec=pltpu.PrefetchScalarGridSpec(
            num_scalar_prefetch=1,
            grid=(t // tm,),
            in_specs=[pl.BlockSpec((tm, d), lambda i, dst: (i, 0)),
                      pl.BlockSpec((tm, LANES), lambda i, dst: (i, 0)),
                      pl.BlockSpec((1, 6, d), lambda i, dst: (i // tiles_per_batch, 0, 0)),
                      pl.BlockSpec((1, d), lambda i, dst: (0, 0)),
                      pl.BlockSpec(memory_space=pl.ANY)],
            out_specs=pl.BlockSpec((tm, d), lambda i, dst: (i, 0)),
            scratch_shapes=[pltpu.VMEM((TOP_K, tm, d), F32),
                            pltpu.SemaphoreType.DMA(())]),
        out_shape=jax.ShapeDtypeStruct((t, d), F32),
        compiler_params=_params(("arbitrary",), 24),
        name="combine",
    )(dest, x1, route, mod6, gain, y_sorted)


def _dispatch_tables(route, rank, counts, t):
    n_assign = t * TOP_K
    n_pad = -(-(n_assign + N_EXPERTS * (MOE_BLOCK - 1)) // MOE_BLOCK) * MOE_BLOCK
    n_blocks = n_pad // MOE_BLOCK
    n_items = N_EXPERTS + n_assign // (ITEM_BLOCKS * MOE_BLOCK)

    eid = route[:, :TOP_K].astype(jnp.int32)
    rk = rank[:, :TOP_K].astype(jnp.int32)
    cnt = counts[0, :N_EXPERTS].astype(jnp.int32)
    blocks_e = (cnt + MOE_BLOCK - 1) // MOE_BLOCK
    blk_end = jnp.cumsum(blocks_e)
    blk_start = blk_end - blocks_e
    dest = (blk_start[eid] * MOE_BLOCK + rk).reshape(-1)
    token_ids = jnp.repeat(jnp.arange(t, dtype=jnp.int32), TOP_K)
    row_token = jnp.zeros((n_pad,), jnp.int32).at[dest].set(token_ids)

    items_e = (blocks_e + ITEM_BLOCKS - 1) // ITEM_BLOCKS
    item_end = jnp.cumsum(items_e)
    item_start = item_end - items_e
    w = jnp.arange(n_items, dtype=jnp.int32)
    live = w < item_end[-1]
    w_live = jnp.minimum(w, item_end[-1] - 1)
    e_w = jnp.minimum(jnp.searchsorted(item_end, w_live, side='right'), N_EXPERTS - 1).astype(jnp.int32)
    j_w = w_live - item_start[e_w]
    item_blk = (blk_start[e_w] + ITEM_BLOCKS * j_w).astype(jnp.int32)
    item_nb = jnp.where(live, jnp.clip(blocks_e[e_w] - ITEM_BLOCKS * j_w, 0, ITEM_BLOCKS), 0).astype(jnp.int32)
    return dest.astype(jnp.int32), row_token, e_w, item_blk, item_nb, n_blocks


def kernel(x, c, positions, norm1_gain, norm2_gain, final_norm_gain, w_ada, b_ada, w_in, attn_sinks,
           ret_norm_gain, w_branch_attn, w_branch_ret, w_out, w_router_group, b_router_group,
           w_router_expert, b_router_expert, w_expert_gate, w_expert_up, w_expert_down):
    batch, seq, d = x.shape
    t = batch * seq
    depth = w_ada.shape[0]
    half = RET_DIM // 2
    inv_freq = (ROPE_BASE ** (-jnp.arange(half, dtype=F32) / half)).reshape(1, half)
    pos = positions.reshape(t, 1)
    c8 = jnp.pad(c, ((0, 8 - batch), (0, 0)))
    xf = x.reshape(t, d)

    assert depth == 1, "the fused final norm assumes a single layer"
    for layer in range(depth):
        mod6 = _ada(c8, w_ada[layer], b_ada[layer].reshape(1, -1))[:batch].reshape(batch, 6, d)
        proj = _proj(xf, norm1_gain[layer].reshape(1, d), mod6, w_in[layer], seq)
        attn = _attn(proj, attn_sinks[layer], batch, seq)
        ret = _ret(proj, pos, inv_freq, ret_norm_gain[layer].reshape(1, d), batch, seq)

        pad = LANES - N_GROUPS - N_EXPERTS
        w_rt = jnp.concatenate([w_router_group[layer], w_router_expert[layer],
                                jnp.zeros((d, pad), F32)], axis=1)
        b_rt = jnp.concatenate([b_router_group[layer], b_router_expert[layer],
                                jnp.zeros((pad,), F32)]).reshape(1, LANES)
        x1, h2, route = _merge(attn, ret, proj, xf, mod6, norm2_gain[layer].reshape(1, d),
                               w_branch_attn[layer].astype(BF16), w_branch_ret[layer].astype(BF16),
                               w_out[layer].astype(BF16), w_rt, b_rt, seq)

        rank, counts = _rank(route)
        dest, row_token, item_e, item_blk, item_nb, n_blocks = _dispatch_tables(route, rank, counts, t)
        x_sorted = _gather(row_token, h2, n_blocks)
        y_sorted = _experts(item_e, item_blk, item_nb, x_sorted,
                            w_expert_gate[layer], w_expert_up[layer], w_expert_down[layer], n_blocks)
        xf = _combine(dest, x1, route, mod6, final_norm_gain.reshape(1, d),
                      y_sorted.reshape(n_blocks * MOE_BLOCK, d), seq)
    return xf.reshape(batch, seq, d)
```

```python
import functools
import math

import jax
import jax.numpy as jnp
import numpy as np
from jax import lax
from jax.experimental import pallas as pl
from jax.experimental.pallas import tpu as pltpu

F32 = jnp.float32
BF16 = jnp.bfloat16

D_MODEL = 2048
ATTN_HEAD_DIM = 64
ATTN_HEADS = 32
ATTN_KV_HEADS = 4
ATTN_GROUP = 8
WINDOW = 128
RET_HEADS = 8
RET_DIM = 256
RET_CHUNK = 128
ROPE_BASE = 10000.0
N_GROUPS = 4
EXPERTS_PER_GROUP = 16
N_EXPERTS = 64
TOP_K = 2
EXPERT_DIM = 1024
MOE_BLOCK = 128
EPS = 1e-6
NEG = -1e30

MIB = 1024 * 1024
LANES = 128
PROJ_TN = 512
PROJ_TM = 1024
KV_TILE = 28
MERGE_TM = 256
RANK_TM = 256
ITEM_BLOCKS = 4
EXPERT_FC = 512
COMBINE_TM = 128
SLAB_ROWS = 8
SLAB_LANES = LANES
HALF_D = D_MODEL // 2
ROWS_PER_WAIT = 128
I32 = jnp.int32

COL_QA, COL_QR, COL_KR, COL_VR, COL_GR, COL_GA, COL_GRT = 0, 1, 2, 3, 4, 5, 6

LOG_GAMMA = [math.log1p(-(2.0 ** (-5.0 - h))) for h in range(RET_HEADS)]


def _params(sem, vmem_mib):
    return pltpu.CompilerParams(dimension_semantics=sem, vmem_limit_bytes=vmem_mib * MIB)


def _pack_pair(lo, hi):
    lo_b = lax.bitcast_convert_type(lo.astype(BF16).astype(F32), I32)
    hi_b = lax.bitcast_convert_type(hi.astype(BF16).astype(F32), I32)
    return hi_b | lax.shift_right_logical(lo_b, jnp.full_like(lo_b, 16))


def _unpack_pair(w):
    lo = lax.bitcast_convert_type(w << 16, F32)
    hi = lax.bitcast_convert_type(w & jnp.int32(-65536), F32)
    return lo, hi


def _slab_rows(j, n_tokens):
    return pl.ds(j, n_tokens, stride=SLAB_ROWS)


def _ada_kernel(c_ref, w_ref, b_ref, o_ref):
    c = c_ref[...]
    a = (c * jax.nn.sigmoid(c)).astype(BF16)
    o_ref[...] = jnp.dot(a, w_ref[...].astype(BF16), preferred_element_type=F32) + b_ref[...]


def _ada(c8, w_ada, b_ada):
    n = w_ada.shape[1]
    tn = 1024
    return pl.pallas_call(
        _ada_kernel,
        grid=(n // tn,),
        in_specs=[pl.BlockSpec((8, D_MODEL), lambda j: (0, 0)),
                  pl.BlockSpec((D_MODEL, tn), lambda j: (0, j)),
                  pl.BlockSpec((1, tn), lambda j: (0, j))],
        out_specs=pl.BlockSpec((8, tn), lambda j: (0, j)),
        out_shape=jax.ShapeDtypeStruct((8, n), F32),
        compiler_params=_params(("arbitrary",), 40),
        name="ada",
    )(c8, w_ada, b_ada)


def _proj_kernel(x_ref, g_ref, mod_ref, w_ref, o_ref, h_ref):
    @pl.when(pl.program_id(1) == 0)
    def _():
        x = x_ref[...]
        var = jnp.mean(x * x, axis=-1, keepdims=True)
        y = x * lax.rsqrt(var + EPS) * g_ref[...]
        h_ref[...] = (y * (1.0 + mod_ref[0, 1:2, :]) + mod_ref[0, 0:1, :]).astype(BF16)

    o_ref[...] = jnp.dot(h_ref[...], w_ref[...].astype(BF16),
                         preferred_element_type=F32).astype(BF16)


def _proj_out_tile(j):
    return jnp.where(j < 4, j, jnp.where(j == 4, KV_TILE, j - 1))


def _proj(x2, gain, mod6, w_in, seq):
    t = x2.shape[0]
    n = w_in.shape[1]
    tiles_per_batch = seq // PROJ_TM
    return pl.pallas_call(
        _proj_kernel,
        grid=(t // PROJ_TM, n // PROJ_TN),
        in_specs=[pl.BlockSpec((PROJ_TM, D_MODEL), lambda i, j: (i, 0)),
                  pl.BlockSpec((1, D_MODEL), lambda i, j: (0, 0)),
                  pl.BlockSpec((1, 6, D_MODEL), lambda i, j: (i // tiles_per_batch, 0, 0)),
                  pl.BlockSpec((D_MODEL, PROJ_TN), lambda i, j: (0, j))],
        out_specs=pl.BlockSpec((PROJ_TM, PROJ_TN), lambda i, j: (i, _proj_out_tile(j))),
        out_shape=jax.ShapeDtypeStruct((t, n), BF16),
        scratch_shapes=[pltpu.VMEM((PROJ_TM, D_MODEL), BF16)],
        compiler_params=_params(("arbitrary", "arbitrary"), 48),
        name="proj",
    )(x2, gain, mod6, w_in)


def _attn_kernel(sink_ref, q_ref, kvp_ref, kvc_ref, o_ref):
    n = pl.program_id(1)
    kvp = kvp_ref[...]
    kvc = kvc_ref[...]
    qi = lax.broadcasted_iota(jnp.int32, (WINDOW, 2 * WINDOW), 0)
    sj = lax.broadcasted_iota(jnp.int32, (WINDOW, 2 * WINDOW), 1)
    valid = (sj > qi) & (sj <= qi + WINDOW) & ((sj >= WINDOW) | (n > 0))
    dh = ATTN_HEAD_DIM
    kv_w = ATTN_KV_HEADS * dh
    for kv in range(ATTN_KV_HEADS):
        kband = jnp.concatenate([kvp[:, kv * dh:(kv + 1) * dh], kvc[:, kv * dh:(kv + 1) * dh]], axis=0)
        vband = jnp.concatenate([kvp[:, kv_w + kv * dh:kv_w + (kv + 1) * dh],
                                 kvc[:, kv_w + kv * dh:kv_w + (kv + 1) * dh]], axis=0)
        outs = []
        for g in range(ATTN_GROUP):
            h = kv * ATTN_GROUP + g
            qh = q_ref[:, h * dh:(h + 1) * dh]
            s = lax.dot_general(qh, kband, (((1,), (1,)), ((), ())),
                                preferred_element_type=F32) * (dh ** -0.5)
            s = jnp.where(valid, s, NEG)
            sink = sink_ref[h]
            m = jnp.maximum(jnp.max(s, axis=-1, keepdims=True), sink)
            p = jnp.exp(s - m)
            denom = jnp.sum(p, axis=-1, keepdims=True) + jnp.exp(sink - m)
            o = jnp.dot(p.astype(BF16), vband, preferred_element_type=F32)
            outs.append(o / denom)
        o_ref[:, kv * ATTN_GROUP * dh:(kv + 1) * ATTN_GROUP * dh] = (
            jnp.concatenate(outs, axis=-1).astype(BF16))


def _attn(proj, sinks, batch, seq):
    nb = seq // WINDOW
    t = batch * seq
    kv_blk = KV_TILE

    def cur(b, n):
        return (b * nb + n, kv_blk)

    def prev(b, n):
        return (b * nb + jnp.maximum(n - 1, 0), kv_blk)

    return pl.pallas_call(
        _attn_kernel,
        grid=(batch, nb),
        in_specs=[pl.BlockSpec(memory_space=pltpu.SMEM),
                  pl.BlockSpec((WINDOW, D_MODEL), lambda b, n: (b * nb + n, COL_QA)),
                  pl.BlockSpec((WINDOW, PROJ_TN), prev),
                  pl.BlockSpec((WINDOW, PROJ_TN), cur)],
        out_specs=pl.BlockSpec((WINDOW, D_MODEL), lambda b, n: (b * nb + n, 0)),
        out_shape=jax.ShapeDtypeStruct((t, D_MODEL), BF16),
        compiler_params=_params(("arbitrary", "arbitrary"), 32),
        name="attn",
    )(sinks, proj, proj, proj)


def _ret_kernel(pos_ref, invf_ref, q_ref, k_ref, v_ref, g_ref, gain_ref, o_ref, state_ref):
    @pl.when(pl.program_id(1) == 0)
    def _():
        state_ref[...] = jnp.zeros_like(state_ref)

    half = RET_DIM // 2
    ang = pos_ref[...].astype(F32) * invf_ref[...]
    cos = jnp.cos(ang)
    sin = jnp.sin(ang)
    idx = lax.broadcasted_iota(jnp.int32, (RET_CHUNK, 1), 0).astype(F32)
    ii = lax.broadcasted_iota(jnp.int32, (RET_CHUNK, RET_CHUNK), 0)
    jj = lax.broadcasted_iota(jnp.int32, (RET_CHUNK, RET_CHUNK), 1)
    diff = (ii - jj).astype(F32)

    def rot(t):
        t1, t2 = t[:, :half], t[:, half:]
        return jnp.concatenate([t1 * cos - t2 * sin, t1 * sin + t2 * cos], axis=-1)

    for h in range(RET_HEADS):
        lg = LOG_GAMMA[h]
        sl = slice(h * RET_DIM, (h + 1) * RET_DIM)
        qr = rot(q_ref[:, sl].astype(F32))
        kr = rot(k_ref[:, sl].astype(F32)) * (RET_DIM ** -0.5)
        vb = v_ref[:, sl]
        d_intra = jnp.where(diff >= 0, jnp.exp(jnp.maximum(diff, 0.0) * lg), 0.0)
        d_q = jnp.exp((idx + 1.0) * lg)
        d_k = jnp.exp((RET_CHUNK - 1.0 - idx) * lg)
        d_chunk = math.exp(RET_CHUNK * lg)
        qb = qr.astype(BF16)
        kb = kr.astype(BF16)
        intra = lax.dot_general(qb, kb, (((1,), (1,)), ((), ())),
                                preferred_element_type=F32) * d_intra
        st = state_ref[h]
        o = (jnp.dot(intra.astype(BF16), vb, preferred_element_type=F32)
             + jnp.dot(qb, st.astype(BF16), preferred_element_type=F32) * d_q)
        kd = (kr * d_k).astype(BF16)
        state_ref[h] = st * d_chunk + lax.dot_general(kd, vb, (((0,), (0,)), ((), ())),
                                                      preferred_element_type=F32)
        o = o * lax.rsqrt(jnp.mean(o * o, axis=-1, keepdims=True) + EPS) * gain_ref[:, sl]
        gg = g_ref[:, sl].astype(F32)
        o_ref[:, sl] = (gg * jax.nn.sigmoid(gg) * o).astype(BF16)


def _ret(proj, pos, inv_freq, ret_gain, batch, seq):
    nc = seq // RET_CHUNK
    t = batch * seq

    def col(cb):
        return lambda b, c: (b * nc + c, cb)

    return pl.pallas_call(
        _ret_kernel,
        grid=(batch, nc),
        in_specs=[pl.BlockSpec((RET_CHUNK, 1), lambda b, c: (b * nc + c, 0)),
                  pl.BlockSpec((1, RET_DIM // 2), lambda b, c: (0, 0)),
                  pl.BlockSpec((RET_CHUNK, D_MODEL), col(COL_QR)),
                  pl.BlockSpec((RET_CHUNK, D_MODEL), col(COL_KR)),
                  pl.BlockSpec((RET_CHUNK, D_MODEL), col(COL_VR)),
                  pl.BlockSpec((RET_CHUNK, D_MODEL), col(COL_GR)),
                  pl.BlockSpec((1, D_MODEL), lambda b, c: (0, 0))],
        out_specs=pl.BlockSpec((RET_CHUNK, D_MODEL), lambda b, c: (b * nc + c, 0)),
        out_shape=jax.ShapeDtypeStruct((t, D_MODEL), BF16),
        scratch_shapes=[pltpu.VMEM((RET_HEADS, RET_DIM, RET_DIM), F32)],
        compiler_params=_params(("arbitrary", "arbitrary"), 32),
        name="ret",
    )(pos, inv_freq, proj, proj, proj, proj, ret_gain)


def _route(logits):
    lane = lax.broadcasted_iota(jnp.int32, logits.shape, 1)
    lane_f = lane.astype(F32)
    is_g = lane < N_GROUPS
    gl = jnp.where(is_g, logits, NEG)
    gmax = jnp.max(gl, axis=-1, keepdims=True)
    gsel = jnp.min(jnp.where(gl == gmax, lane_f, float(LANES)), axis=-1, keepdims=True)
    gsum = jnp.sum(jnp.where(is_g, jnp.exp(gl - gmax), 0.0), axis=-1, keepdims=True)
    g_w = 1.0 / gsum
    grp = ((lane - N_GROUPS) >> 4).astype(F32)
    is_e = (lane >= N_GROUPS) & (lane < N_GROUPS + N_EXPERTS) & (grp == gsel)
    el = jnp.where(is_e, logits, NEG)
    v1 = jnp.max(el, axis=-1, keepdims=True)
    i1 = jnp.min(jnp.where(el == v1, lane_f, float(LANES)), axis=-1, keepdims=True)
    el2 = jnp.where(lane_f == i1, NEG, el)
    v2 = jnp.max(el2, axis=-1, keepdims=True)
    i2 = jnp.min(jnp.where(el2 == v2, lane_f, float(LANES)), axis=-1, keepdims=True)
    tt = jnp.exp(v2 - v1)
    w1 = g_w / (1.0 + tt)
    w2 = g_w * tt / (1.0 + tt)
    return jnp.where(lane == 0, i1 - N_GROUPS,
                     jnp.where(lane == 1, i2 - N_GROUPS,
                               jnp.where(lane == 2, w1, jnp.where(lane == 3, w2, 0.0))))


def _merge_kernel(attn_ref, ret_ref, ga_ref, gr_ref, x_ref, mod_ref, g2_ref,
                  wa_ref, wr_ref, wo_ref, wrt_ref, brt_ref, x1_ref, h2_ref, route_ref):
    a = jnp.dot(attn_ref[...], wa_ref[...], preferred_element_type=F32)
    r = jnp.dot(ret_ref[...], wr_ref[...], preferred_element_type=F32)
    merged = (jax.nn.sigmoid(ga_ref[...].astype(F32)) * a
              + jax.nn.sigmoid(gr_ref[...].astype(F32)) * r)
    mix = jnp.dot(merged.astype(BF16), wo_ref[...], preferred_element_type=F32)
    x1 = x_ref[...] + mod_ref[0, 2:3, :] * mix
    x1_ref[...] = x1
    var = jnp.mean(x1 * x1, axis=-1, keepdims=True)
    h2 = x1 * lax.rsqrt(var + EPS) * g2_ref[...]
    h2 = h2 * (1.0 + mod_ref[0, 4:5, :]) + mod_ref[0, 3:4, :]
    tm = h2.shape[0]
    for j in range(SLAB_ROWS):
        lo = h2[:, j * LANES:(j + 1) * LANES]
        hi = h2[:, HALF_D + j * LANES:HALF_D + (j + 1) * LANES]
        h2_ref[_slab_rows(j, tm), :] = _pack_pair(lo, hi)
    logits = jnp.dot(h2, wrt_ref[...], preferred_element_type=F32,
                     precision=lax.Precision.HIGHEST) + brt_ref[...]
    route_ref[...] = _route(logits)


def _merge(attn, ret, proj, x2, mod6, gain2, wa, wr, wo, w_rt, b_rt, seq):
    t = x2.shape[0]
    tm = MERGE_TM
    tiles_per_batch = seq // tm
    row = lambda i: (i, 0)
    const = lambda i: (0, 0)
    wspec = pl.BlockSpec((D_MODEL, D_MODEL), const, pipeline_mode=pl.Buffered(1))
    return pl.pallas_call(
        _merge_kernel,
        grid=(t // tm,),
        in_specs=[pl.BlockSpec((tm, D_MODEL), row),
                  pl.BlockSpec((tm, D_MODEL), row),
                  pl.BlockSpec((tm, D_MODEL), lambda i: (i, COL_GA)),
                  pl.BlockSpec((tm, D_MODEL), lambda i: (i, COL_GRT)),
                  pl.BlockSpec((tm, D_MODEL), row),
                  pl.BlockSpec((1, 6, D_MODEL), lambda i: (i // tiles_per_batch, 0, 0)),
                  pl.BlockSpec((1, D_MODEL), const),
                  wspec, wspec, wspec,
                  pl.BlockSpec((D_MODEL, LANES), const),
                  pl.BlockSpec((1, LANES), const)],
        out_specs=[pl.BlockSpec((tm, D_MODEL), row),
                   pl.BlockSpec((tm * SLAB_ROWS, SLAB_LANES), row),
                   pl.BlockSpec((tm, LANES), row)],
        out_shape=[jax.ShapeDtypeStruct((t, D_MODEL), F32),
                   jax.ShapeDtypeStruct((t * SLAB_ROWS, SLAB_LANES), I32),
                   jax.ShapeDtypeStruct((t, LANES), F32)],
        compiler_params=_params(("arbitrary",), 56),
        name="merge",
    )(attn, ret, proj, proj, x2, mod6, gain2, wa, wr, wo, w_rt, b_rt)


def _rank_kernel(route_ref, rank_ref, count_ref, carry_ref):
    @pl.when(pl.program_id(0) == 0)
    def _():
        carry_ref[...] = jnp.zeros_like(carry_ref)

    route = route_ref[...]
    tm = route.shape[0]
    lane = lax.broadcasted_iota(jnp.int32, route.shape, 1)
    lane_f = lane.astype(F32)
    hot1 = lane_f == route[:, 0:1]
    hot2 = lane_f == route[:, 1:2]
    both = jnp.where(hot1 | hot2, 1.0, 0.0)
    ii = lax.broadcasted_iota(jnp.int32, (tm, tm), 0)
    jj = lax.broadcasted_iota(jnp.int32, (tm, tm), 1)
    lower = jnp.where(ii > jj, 1.0, 0.0).astype(BF16)
    before = jnp.dot(lower, both.astype(BF16), preferred_element_type=F32) + carry_ref[...]
    r1 = jnp.sum(jnp.where(hot1, before, 0.0), axis=-1, keepdims=True)
    r2 = jnp.sum(jnp.where(hot2, before, 0.0), axis=-1, keepdims=True)
    rank_ref[...] = jnp.where(lane == 0, r1, jnp.where(lane == 1, r2, 0.0))
    carry = carry_ref[...] + jnp.sum(both, axis=0, keepdims=True)
    carry_ref[...] = carry
    count_ref[...] = carry


def _rank(route):
    t = route.shape[0]
    tm = RANK_TM
    return pl.pallas_call(
        _rank_kernel,
        grid=(t // tm,),
        in_specs=[pl.BlockSpec((tm, LANES), lambda i: (i, 0))],
        out_specs=[pl.BlockSpec((tm, LANES), lambda i: (i, 0)),
                   pl.BlockSpec((1, LANES), lambda i: (0, 0))],
        out_shape=[jax.ShapeDtypeStruct((t, LANES), F32),
                   jax.ShapeDtypeStruct((1, LANES), F32)],
        scratch_shapes=[pltpu.VMEM((1, LANES), F32)],
        compiler_params=_params(("arbitrary",), 16),
        name="rank",
    )(route)


PAD_BITS = (64, 32, 16, 8, 4, 2, 1)


def _dispatch_kernel(dest_ref, zstart_ref, zcount_ref, h2_ref, xs_ref, zero_ref, sem, zsem):
    n_assign = dest_ref.shape[0]
    zero_ref[...] = jnp.zeros_like(zero_ref)

    def zero_copy(start, rows):
        return pltpu.make_async_copy(zero_ref.at[pl.ds(0, rows * SLAB_ROWS), :],
                                     xs_ref.at[pl.ds(start * SLAB_ROWS, rows * SLAB_ROWS), :], zsem)

    def fill(e, wait):
        start = zstart_ref[e]
        pad = zcount_ref[e]
        for bit in PAD_BITS:
            @pl.when((pad & bit) != 0)
            def _(start=start, bit=bit):
                cp = zero_copy(start, bit)
                cp.wait() if wait else cp.start()
            start = start + (pad & bit)

    lax.fori_loop(0, N_EXPERTS, lambda e, c: (fill(e, False), c)[1], 0)

    def row_copy(tok, dst):
        return pltpu.make_async_copy(h2_ref.at[pl.ds(tok * SLAB_ROWS, SLAB_ROWS), :],
                                     xs_ref.at[pl.ds(dst * SLAB_ROWS, SLAB_ROWS), :], sem)

    def issue(a, carry):
        row_copy(a // TOP_K, dest_ref[a]).start()
        return carry

    lax.fori_loop(0, n_assign, issue, 0, unroll=8)

    def drain(i, carry):
        pltpu.make_async_copy(h2_ref.at[pl.ds(0, ROWS_PER_WAIT * SLAB_ROWS), :],
                              xs_ref.at[pl.ds(0, ROWS_PER_WAIT * SLAB_ROWS), :], sem).wait()
        return carry

    lax.fori_loop(0, n_assign // ROWS_PER_WAIT, drain, 0)
    lax.fori_loop(0, N_EXPERTS, lambda e, c: (fill(e, True), c)[1], 0)


def _dispatch(dest, zstart, zcount, h2_slab, n_pad):
    return pl.pallas_call(
        _dispatch_kernel,
        grid_spec=pltpu.PrefetchScalarGridSpec(
            num_scalar_prefetch=3,
            grid=(1,),
            in_specs=[pl.BlockSpec(memory_space=pl.ANY)],
            out_specs=pl.BlockSpec(memory_space=pl.ANY),
            scratch_shapes=[pltpu.VMEM((PAD_BITS[0] * SLAB_ROWS, SLAB_LANES), I32),
                            pltpu.SemaphoreType.DMA(()),
                            pltpu.SemaphoreType.DMA(())]),
        out_shape=jax.ShapeDtypeStruct((n_pad * SLAB_ROWS, SLAB_LANES), I32),
        compiler_params=_params(("arbitrary",), 16),
        name="dispatch",
    )(dest, zstart, zcount, h2_slab)


def _experts_kernel(item_e_ref, item_blk_ref, item_nb_ref,
                    x0_ref, x1_ref, x2_ref, x3_ref, wg_ref, wu_ref, wd_ref, y_ref,
                    xb_ref, wgb_ref, wub_ref, wdb_ref, acc_ref, yp_ref, sem):
    w = pl.program_id(0)
    c = pl.program_id(1)
    last_c = pl.num_programs(1) - 1
    nb = item_nb_ref[w]
    blk0 = item_blk_ref[w]
    x_refs = (x0_ref, x1_ref, x2_ref, x3_ref)

    blk_rows = MOE_BLOCK * SLAB_ROWS

    def out_copy(s):
        return pltpu.make_async_copy(yp_ref.at[s], y_ref.at[pl.ds((blk0 + s) * blk_rows, blk_rows), :],
                                     sem.at[s])

    @pl.when(nb > 0)
    def _():
        wgb_ref[...] = wg_ref[...].astype(BF16)
        wub_ref[...] = wu_ref[...].astype(BF16)
        wdb_ref[...] = wd_ref[...].astype(BF16)

    for s in range(ITEM_BLOCKS):
        @pl.when(s < nb)
        def _(s=s):
            @pl.when(c == 0)
            def _():
                for j in range(SLAB_ROWS):
                    lo, hi = _unpack_pair(x_refs[s][_slab_rows(j, MOE_BLOCK), :])
                    xb_ref[s, :, j * LANES:(j + 1) * LANES] = lo.astype(BF16)
                    xb_ref[s, :, HALF_D + j * LANES:HALF_D + (j + 1) * LANES] = hi.astype(BF16)

            xs = xb_ref[s]
            g = jnp.dot(xs, wgb_ref[...], preferred_element_type=F32)
            u = jnp.dot(xs, wub_ref[...], preferred_element_type=F32)
            hid = (g * jax.nn.sigmoid(g) * u).astype(BF16)
            y = jnp.dot(hid, wdb_ref[...], preferred_element_type=F32)

            @pl.when(c == 0)
            def _():
                acc_ref[s] = y

            @pl.when((c > 0) & (c < last_c))
            def _():
                acc_ref[s] += y

            @pl.when(c == last_c)
            def _():
                tot = acc_ref[s] + y
                for j in range(SLAB_ROWS):
                    yp_ref[s, _slab_rows(j, MOE_BLOCK), :] = _pack_pair(
                        tot[:, j * LANES:(j + 1) * LANES],
                        tot[:, HALF_D + j * LANES:HALF_D + (j + 1) * LANES])
                out_copy(s).start()

    @pl.when(c == last_c)
    def _():
        for s in range(ITEM_BLOCKS):
            @pl.when(s < nb)
            def _(s=s):
                out_copy(s).wait()


def _experts(item_e, item_blk, item_nb, x_sorted, w_gate, w_up, w_down, n_blocks):
    n_items = item_e.shape[0]
    d = D_MODEL
    fc = EXPERT_FC
    x3 = x_sorted
    blk_rows = MOE_BLOCK * SLAB_ROWS
    n_chunks = EXPERT_DIM // fc
    assert n_chunks >= 2, "the last hidden chunk adds onto the accumulator of the earlier ones"

    def x_spec(s):
        return pl.BlockSpec((blk_rows, SLAB_LANES),
                            lambda w, c, ie, ib, inb: (jnp.minimum(ib[w] + s, n_blocks - 1), 0))

    def chunk(w, c, inb):
        return jnp.where(inb[w] > 0, c, n_chunks - 1)

    return pl.pallas_call(
        _experts_kernel,
        grid_spec=pltpu.PrefetchScalarGridSpec(
            num_scalar_prefetch=3,
            grid=(n_items, n_chunks),
            in_specs=[x_spec(0), x_spec(1), x_spec(2), x_spec(3),
                      pl.BlockSpec((None, d, fc), lambda w, c, ie, ib, inb: (ie[w], 0, chunk(w, c, inb))),
                      pl.BlockSpec((None, d, fc), lambda w, c, ie, ib, inb: (ie[w], 0, chunk(w, c, inb))),
                      pl.BlockSpec((None, fc, d), lambda w, c, ie, ib, inb: (ie[w], chunk(w, c, inb), 0))],
            out_specs=pl.BlockSpec(memory_space=pl.ANY),
            scratch_shapes=[pltpu.VMEM((ITEM_BLOCKS, MOE_BLOCK, d), BF16),
                            pltpu.VMEM((d, fc), BF16),
                            pltpu.VMEM((d, fc), BF16),
                            pltpu.VMEM((fc, d), BF16),
                            pltpu.VMEM((ITEM_BLOCKS, MOE_BLOCK, d), F32),
                            pltpu.VMEM((ITEM_BLOCKS, blk_rows, SLAB_LANES), I32),
                            pltpu.SemaphoreType.DMA((ITEM_BLOCKS,))]),
        out_shape=jax.ShapeDtypeStruct((n_blocks * blk_rows, SLAB_LANES), I32),
        compiler_params=_params(("arbitrary", "arbitrary"), 56),
        name="experts",
    )(item_e, item_blk, item_nb, x3, x3, x3, x3, w_gate, w_up, w_down)


def _combine_kernel(dest_ref, x1_ref, route_ref, mod_ref, gain_ref, y_ref, o_ref, ybuf_ref, sem):
    tm = COMBINE_TM
    i = pl.program_id(0)
    n_tiles = pl.num_programs(0)

    def row_copy(slot, k, r, src):
        return pltpu.make_async_copy(y_ref.at[pl.ds(src * SLAB_ROWS, SLAB_ROWS), :],
                                     ybuf_ref.at[slot, k, pl.ds(r * SLAB_ROWS, SLAB_ROWS), :],
                                     sem.at[slot])

    def issue_tile(tile, slot):
        base = tile * tm * TOP_K

        def issue(r, carry):
            for k in range(TOP_K):
                row_copy(slot, k, r, dest_ref[base + r * TOP_K + k]).start()
            return carry

        lax.fori_loop(0, tm, issue, 0, unroll=4)

    @pl.when(i == 0)
    def _():
        issue_tile(0, 0)

    @pl.when(i + 1 < n_tiles)
    def _():
        issue_tile(i + 1, (i + 1) % 2)

    slot = i % 2
    for k in range(TOP_K):
        pltpu.make_async_copy(y_ref.at[pl.ds(0, tm * SLAB_ROWS), :], ybuf_ref.at[slot, k],
                              sem.at[slot]).wait()

    route = route_ref[...]
    w0 = route[:, 2:3]
    w1 = route[:, 3:4]
    ssq = jnp.zeros((tm, 1), F32)
    for j in range(SLAB_ROWS):
        rows = _slab_rows(j, tm)
        y0 = _unpack_pair(ybuf_ref[slot, 0, rows, :])
        y1 = _unpack_pair(ybuf_ref[slot, 1, rows, :])
        for part, off in ((0, j * LANES), (1, HALF_D + j * LANES)):
            cols = slice(off, off + LANES)
            ffn = w0 * y0[part] + w1 * y1[part]
            x2 = x1_ref[:, cols] + mod_ref[0, 5:6, cols] * ffn
            o_ref[:, cols] = x2
            ssq = ssq + jnp.sum(x2 * x2, axis=-1, keepdims=True)
    o_ref[...] = o_ref[...] * lax.rsqrt(ssq * (1.0 / D_MODEL) + EPS) * gain_ref[...]


def _combine(dest, x1, route, mod6, gain, y_sorted, seq):
    t, d = x1.shape
    tm = COMBINE_TM
    tiles_per_batch = seq // tm
    return pl.pallas_call(
        _combine_kernel,
        grid_spec=pltpu.PrefetchScalarGridSpec(
            num_scalar_prefetch=1,
            grid=(t // tm,),
            in_specs=[pl.BlockSpec((tm, d), lambda i, dst: (i, 0)),
                      pl.BlockSpec((tm, LANES), lambda i, dst: (i, 0)),
                      pl.BlockSpec((1, 6, d), lambda i, dst: (i // tiles_per_batch, 0, 0)),
                      pl.BlockSpec((1, d), lambda i, dst: (0, 0)),
                      pl.BlockSpec(memory_space=pl.ANY)],
            out_specs=pl.BlockSpec((tm, d), lambda i, dst: (i, 0)),
            scratch_shapes=[pltpu.VMEM((2, TOP_K, tm * SLAB_ROWS, SLAB_LANES), I32),
                            pltpu.SemaphoreType.DMA((2,))]),
        out_shape=jax.ShapeDtypeStruct((t, d), F32),
        compiler_params=_params(("arbitrary",), 24),
        name="combine",
    )(dest, x1, route, mod6, gain, y_sorted)


def _dispatch_tables(route, rank, counts, t):
    n_assign = t * TOP_K
    n_pad = -(-(n_assign + N_EXPERTS * (MOE_BLOCK - 1)) // MOE_BLOCK) * MOE_BLOCK
    n_blocks = n_pad // MOE_BLOCK
    n_items = N_EXPERTS + n_assign // (ITEM_BLOCKS * MOE_BLOCK)

    eid = route[:, :TOP_K].astype(jnp.int32)
    rk = rank[:, :TOP_K].astype(jnp.int32)
    cnt = counts[0, :N_EXPERTS].astype(jnp.int32)
    blocks_e = (cnt + MOE_BLOCK - 1) // MOE_BLOCK
    blk_end = jnp.cumsum(blocks_e)
    blk_start = blk_end - blocks_e
    dest = (blk_start[eid] * MOE_BLOCK + rk).reshape(-1)
    zstart = (blk_start * MOE_BLOCK + cnt).astype(jnp.int32)
    zcount = (blocks_e * MOE_BLOCK - cnt).astype(jnp.int32)

    items_e = (blocks_e + ITEM_BLOCKS - 1) // ITEM_BLOCKS
    item_end = jnp.cumsum(items_e)
    item_start = item_end - items_e
    w = jnp.arange(n_items, dtype=jnp.int32)
    live = w < item_end[-1]
    w_live = jnp.minimum(w, item_end[-1] - 1)
    e_w = jnp.minimum(jnp.searchsorted(item_end, w_live, side='right'), N_EXPERTS - 1).astype(jnp.int32)
    j_w = w_live - item_start[e_w]
    item_blk = (blk_start[e_w] + ITEM_BLOCKS * j_w).astype(jnp.int32)
    item_nb = jnp.where(live, jnp.clip(blocks_e[e_w] - ITEM_BLOCKS * j_w, 0, ITEM_BLOCKS), 0).astype(jnp.int32)
    return dest.astype(jnp.int32), zstart, zcount, e_w, item_blk, item_nb, n_blocks


def kernel(x, c, positions, norm1_gain, norm2_gain, final_norm_gain, w_ada, b_ada, w_in, attn_sinks,
           ret_norm_gain, w_branch_attn, w_branch_ret, w_out, w_router_group, b_router_group,
           w_router_expert, b_router_expert, w_expert_gate, w_expert_up, w_expert_down):
    batch, seq, d = x.shape
    t = batch * seq
    depth = w_ada.shape[0]
    half = RET_DIM // 2
    inv_freq = (ROPE_BASE ** (-jnp.arange(half, dtype=F32) / half)).reshape(1, half)
    pos = positions.reshape(t, 1)
    c8 = jnp.pad(c, ((0, 8 - batch), (0, 0)))
    xf = x.reshape(t, d)

    assert depth == 1, "the fused final norm assumes a single layer"
    for layer in range(depth):
        mod6 = _ada(c8, w_ada[layer], b_ada[layer].reshape(1, -1))[:batch].reshape(batch, 6, d)
        proj = _proj(xf, norm1_gain[layer].reshape(1, d), mod6, w_in[layer], seq)
        attn = _attn(proj, attn_sinks[layer], batch, seq)
        ret = _ret(proj, pos, inv_freq, ret_norm_gain[layer].reshape(1, d), batch, seq)

        pad = LANES - N_GROUPS - N_EXPERTS
        w_rt = jnp.concatenate([w_router_group[layer], w_router_expert[layer],
                                jnp.zeros((d, pad), F32)], axis=1)
        b_rt = jnp.concatenate([b_router_group[layer], b_router_expert[layer],
                                jnp.zeros((pad,), F32)]).reshape(1, LANES)
        x1, h2, route = _merge(attn, ret, proj, xf, mod6, norm2_gain[layer].reshape(1, d),
                               w_branch_attn[layer].astype(BF16), w_branch_ret[layer].astype(BF16),
                               w_out[layer].astype(BF16), w_rt, b_rt, seq)

        rank, counts = _rank(route)
        dest, zstart, zcount, item_e, item_blk, item_nb, n_blocks = _dispatch_tables(route, rank, counts, t)
        x_sorted = _dispatch(dest, zstart, zcount, h2, n_blocks * MOE_BLOCK)
        y_sorted = _experts(item_e, item_blk, item_nb, x_sorted,
                            w_expert_gate[layer], w_expert_up[layer], w_expert_down[layer], n_blocks)
        xf = _combine(dest, x1, route, mod6, final_norm_gain.reshape(1, d), y_sorted, seq)
    return xf.reshape(batch, seq, d)
```

```python
import functools
import math

import jax
import jax.numpy as jnp
import numpy as np
from jax import lax
from jax.experimental import pallas as pl
from jax.experimental.pallas import tpu as pltpu

F32 = jnp.float32
BF16 = jnp.bfloat16

D_MODEL = 2048
ATTN_HEAD_DIM = 64
ATTN_HEADS = 32
ATTN_KV_HEADS = 4
ATTN_GROUP = 8
WINDOW = 128
RET_HEADS = 8
RET_DIM = 256
RET_CHUNK = 128
ROPE_BASE = 10000.0
N_GROUPS = 4
EXPERTS_PER_GROUP = 16
N_EXPERTS = 64
TOP_K = 2
EXPERT_DIM = 1024
MOE_BLOCK = 128
EPS = 1e-6
NEG = -1e30

MIB = 1024 * 1024
LANES = 128
PROJ_TN = 512
PROJ_TM = 1024
KV_TILE = 28
BRANCH_TM = 1024
BRANCH_TN = 512
MIXOUT_TM = 512
ITEM_BLOCKS = 4
EXPERT_FC = 512
COMBINE_TM = 128
SLAB_ROWS = 8
SLAB_LANES = LANES
HALF_D = D_MODEL // 2
ROWS_PER_WAIT = 128
I32 = jnp.int32

COL_QA, COL_QR, COL_KR, COL_VR, COL_GR, COL_GA, COL_GRT = 0, 1, 2, 3, 4, 5, 6

LOG_GAMMA = [math.log1p(-(2.0 ** (-5.0 - h))) for h in range(RET_HEADS)]


def _params(sem, vmem_mib):
    return pltpu.CompilerParams(dimension_semantics=sem, vmem_limit_bytes=vmem_mib * MIB)


def _pack_pair(lo, hi):
    lo_b = lax.bitcast_convert_type(lo.astype(BF16).astype(F32), I32)
    hi_b = lax.bitcast_convert_type(hi.astype(BF16).astype(F32), I32)
    return hi_b | lax.shift_right_logical(lo_b, jnp.full_like(lo_b, 16))


def _unpack_pair(w):
    lo = lax.bitcast_convert_type(w << 16, F32)
    hi = lax.bitcast_convert_type(w & jnp.int32(-65536), F32)
    return lo, hi


def _slab_rows(j, n_tokens):
    return pl.ds(j, n_tokens, stride=SLAB_ROWS)


def _ada_kernel(c_ref, w_ref, b_ref, o_ref):
    c = c_ref[...]
    a = (c * jax.nn.sigmoid(c)).astype(BF16)
    o_ref[...] = jnp.dot(a, w_ref[...].astype(BF16), preferred_element_type=F32) + b_ref[...]


def _ada(c8, w_ada, b_ada):
    n = w_ada.shape[1]
    tn = 1024
    return pl.pallas_call(
        _ada_kernel,
        grid=(n // tn,),
        in_specs=[pl.BlockSpec((8, D_MODEL), lambda j: (0, 0)),
                  pl.BlockSpec((D_MODEL, tn), lambda j: (0, j)),
                  pl.BlockSpec((1, tn), lambda j: (0, j))],
        out_specs=pl.BlockSpec((8, tn), lambda j: (0, j)),
        out_shape=jax.ShapeDtypeStruct((8, n), F32),
        compiler_params=_params(("arbitrary",), 40),
        name="ada",
    )(c8, w_ada, b_ada)


def _proj_kernel(x_ref, g_ref, mod_ref, w_ref, o_ref, h_ref):
    @pl.when(pl.program_id(1) == 0)
    def _():
        x = x_ref[...]
        var = jnp.mean(x * x, axis=-1, keepdims=True)
        y = x * lax.rsqrt(var + EPS) * g_ref[...]
        h_ref[...] = (y * (1.0 + mod_ref[0, 1:2, :]) + mod_ref[0, 0:1, :]).astype(BF16)

    o_ref[...] = jnp.dot(h_ref[...], w_ref[...].astype(BF16),
                         preferred_element_type=F32).astype(BF16)


def _proj_out_tile(j):
    return jnp.where(j < 4, j, jnp.where(j == 4, KV_TILE, j - 1))


def _proj(x2, gain, mod6, w_in, seq):
    t = x2.shape[0]
    n = w_in.shape[1]
    tiles_per_batch = seq // PROJ_TM
    return pl.pallas_call(
        _proj_kernel,
        grid=(t // PROJ_TM, n // PROJ_TN),
        in_specs=[pl.BlockSpec((PROJ_TM, D_MODEL), lambda i, j: (i, 0)),
                  pl.BlockSpec((1, D_MODEL), lambda i, j: (0, 0)),
                  pl.BlockSpec((1, 6, D_MODEL), lambda i, j: (i // tiles_per_batch, 0, 0)),
                  pl.BlockSpec((D_MODEL, PROJ_TN), lambda i, j: (0, j))],
        out_specs=pl.BlockSpec((PROJ_TM, PROJ_TN), lambda i, j: (i, _proj_out_tile(j))),
        out_shape=jax.ShapeDtypeStruct((t, n), BF16),
        scratch_shapes=[pltpu.VMEM((PROJ_TM, D_MODEL), BF16)],
        compiler_params=_params(("arbitrary", "arbitrary"), 48),
        name="proj",
    )(x2, gain, mod6, w_in)


def _attn_kernel(sink_ref, q_ref, kvp_ref, kvc_ref, o_ref):
    n = pl.program_id(1)
    kvp = kvp_ref[...]
    kvc = kvc_ref[...]
    qi = lax.broadcasted_iota(jnp.int32, (WINDOW, 2 * WINDOW), 0)
    sj = lax.broadcasted_iota(jnp.int32, (WINDOW, 2 * WINDOW), 1)
    valid = (sj > qi) & (sj <= qi + WINDOW) & ((sj >= WINDOW) | (n > 0))
    dh = ATTN_HEAD_DIM
    kv_w = ATTN_KV_HEADS * dh
    for kv in range(ATTN_KV_HEADS):
        kband = jnp.concatenate([kvp[:, kv * dh:(kv + 1) * dh], kvc[:, kv * dh:(kv + 1) * dh]], axis=0)
        vband = jnp.concatenate([kvp[:, kv_w + kv * dh:kv_w + (kv + 1) * dh],
                                 kvc[:, kv_w + kv * dh:kv_w + (kv + 1) * dh]], axis=0)
        outs = []
        for g in range(ATTN_GROUP):
            h = kv * ATTN_GROUP + g
            qh = q_ref[:, h * dh:(h + 1) * dh]
            s = lax.dot_general(qh, kband, (((1,), (1,)), ((), ())),
                                preferred_element_type=F32) * (dh ** -0.5)
            s = jnp.where(valid, s, NEG)
            sink = sink_ref[h]
            m = jnp.maximum(jnp.max(s, axis=-1, keepdims=True), sink)
            p = jnp.exp(s - m)
            denom = jnp.sum(p, axis=-1, keepdims=True) + jnp.exp(sink - m)
            o = jnp.dot(p.astype(BF16), vband, preferred_element_type=F32)
            outs.append(o / denom)
        o_ref[:, kv * ATTN_GROUP * dh:(kv + 1) * ATTN_GROUP * dh] = (
            jnp.concatenate(outs, axis=-1).astype(BF16))


def _attn(proj, sinks, batch, seq):
    nb = seq // WINDOW
    t = batch * seq
    kv_blk = KV_TILE

    def cur(b, n):
        return (b * nb + n, kv_blk)

    def prev(b, n):
        return (b * nb + jnp.maximum(n - 1, 0), kv_blk)

    return pl.pallas_call(
        _attn_kernel,
        grid=(batch, nb),
        in_specs=[pl.BlockSpec(memory_space=pltpu.SMEM),
                  pl.BlockSpec((WINDOW, D_MODEL), lambda b, n: (b * nb + n, COL_QA)),
                  pl.BlockSpec((WINDOW, PROJ_TN), prev),
                  pl.BlockSpec((WINDOW, PROJ_TN), cur)],
        out_specs=pl.BlockSpec((WINDOW, D_MODEL), lambda b, n: (b * nb + n, 0)),
        out_shape=jax.ShapeDtypeStruct((t, D_MODEL), BF16),
        compiler_params=_params(("arbitrary", "arbitrary"), 32),
        name="attn",
    )(sinks, proj, proj, proj)


def _ret_kernel(pos_ref, invf_ref, q_ref, k_ref, v_ref, g_ref, gain_ref, o_ref, state_ref):
    @pl.when(pl.program_id(1) == 0)
    def _():
        state_ref[...] = jnp.zeros_like(state_ref)

    half = RET_DIM // 2
    ang = pos_ref[...].astype(F32) * invf_ref[...]
    cos = jnp.cos(ang)
    sin = jnp.sin(ang)
    idx = lax.broadcasted_iota(jnp.int32, (RET_CHUNK, 1), 0).astype(F32)
    ii = lax.broadcasted_iota(jnp.int32, (RET_CHUNK, RET_CHUNK), 0)
    jj = lax.broadcasted_iota(jnp.int32, (RET_CHUNK, RET_CHUNK), 1)
    diff = (ii - jj).astype(F32)

    def rot(t):
        t1, t2 = t[:, :half], t[:, half:]
        return jnp.concatenate([t1 * cos - t2 * sin, t1 * sin + t2 * cos], axis=-1)

    for h in range(RET_HEADS):
        lg = LOG_GAMMA[h]
        sl = slice(h * RET_DIM, (h + 1) * RET_DIM)
        qr = rot(q_ref[:, sl].astype(F32))
        kr = rot(k_ref[:, sl].astype(F32)) * (RET_DIM ** -0.5)
        vb = v_ref[:, sl]
        d_intra = jnp.where(diff >= 0, jnp.exp(jnp.maximum(diff, 0.0) * lg), 0.0)
        d_q = jnp.exp((idx + 1.0) * lg)
        d_k = jnp.exp((RET_CHUNK - 1.0 - idx) * lg)
        d_chunk = math.exp(RET_CHUNK * lg)
        qb = qr.astype(BF16)
        kb = kr.astype(BF16)
        intra = lax.dot_general(qb, kb, (((1,), (1,)), ((), ())),
                                preferred_element_type=F32) * d_intra
        st = state_ref[h]
        o = (jnp.dot(intra.astype(BF16), vb, preferred_element_type=F32)
             + jnp.dot(qb, st.astype(BF16), preferred_element_type=F32) * d_q)
        kd = (kr * d_k).astype(BF16)
        state_ref[h] = st * d_chunk + lax.dot_general(kd, vb, (((0,), (0,)), ((), ())),
                                                      preferred_element_type=F32)
        o = o * lax.rsqrt(jnp.mean(o * o, axis=-1, keepdims=True) + EPS) * gain_ref[:, sl]
        gg = g_ref[:, sl].astype(F32)
        o_ref[:, sl] = (gg * jax.nn.sigmoid(gg) * o).astype(BF16)


def _ret(proj, pos, inv_freq, ret_gain, batch, seq):
    nc = seq // RET_CHUNK
    t = batch * seq

    def col(cb):
        return lambda b, c: (b * nc + c, cb)

    return pl.pallas_call(
        _ret_kernel,
        grid=(batch, nc),
        in_specs=[pl.BlockSpec((RET_CHUNK, 1), lambda b, c: (b * nc + c, 0)),
                  pl.BlockSpec((1, RET_DIM // 2), lambda b, c: (0, 0)),
                  pl.BlockSpec((RET_CHUNK, D_MODEL), col(COL_QR)),
                  pl.BlockSpec((RET_CHUNK, D_MODEL), col(COL_KR)),
                  pl.BlockSpec((RET_CHUNK, D_MODEL), col(COL_VR)),
                  pl.BlockSpec((RET_CHUNK, D_MODEL), col(COL_GR)),
                  pl.BlockSpec((1, D_MODEL), lambda b, c: (0, 0))],
        out_specs=pl.BlockSpec((RET_CHUNK, D_MODEL), lambda b, c: (b * nc + c, 0)),
        out_shape=jax.ShapeDtypeStruct((t, D_MODEL), BF16),
        scratch_shapes=[pltpu.VMEM((RET_HEADS, RET_DIM, RET_DIM), F32)],
        compiler_params=_params(("arbitrary", "arbitrary"), 32),
        name="ret",
    )(pos, inv_freq, proj, proj, proj, proj, ret_gain)


def _route(logits):
    lane = lax.broadcasted_iota(jnp.int32, logits.shape, 1)
    lane_f = lane.astype(F32)
    is_g = lane < N_GROUPS
    gl = jnp.where(is_g, logits, NEG)
    gmax = jnp.max(gl, axis=-1, keepdims=True)
    gsel = jnp.min(jnp.where(gl == gmax, lane_f, float(LANES)), axis=-1, keepdims=True)
    gsum = jnp.sum(jnp.where(is_g, jnp.exp(gl - gmax), 0.0), axis=-1, keepdims=True)
    g_w = 1.0 / gsum
    grp = ((lane - N_GROUPS) >> 4).astype(F32)
    is_e = (lane >= N_GROUPS) & (lane < N_GROUPS + N_EXPERTS) & (grp == gsel)
    el = jnp.where(is_e, logits, NEG)
    v1 = jnp.max(el, axis=-1, keepdims=True)
    i1 = jnp.min(jnp.where(el == v1, lane_f, float(LANES)), axis=-1, keepdims=True)
    el2 = jnp.where(lane_f == i1, NEG, el)
    v2 = jnp.max(el2, axis=-1, keepdims=True)
    i2 = jnp.min(jnp.where(el2 == v2, lane_f, float(LANES)), axis=-1, keepdims=True)
    tt = jnp.exp(v2 - v1)
    w1 = g_w / (1.0 + tt)
    w2 = g_w * tt / (1.0 + tt)
    return jnp.where(lane == 0, i1 - N_GROUPS,
                     jnp.where(lane == 1, i2 - N_GROUPS,
                               jnp.where(lane == 2, w1, jnp.where(lane == 3, w2, 0.0))))


def _branch_kernel(attn_ref, ret_ref, ga_ref, gr_ref, wa_ref, wr_ref, o_ref):
    a = jnp.dot(attn_ref[...], wa_ref[...].astype(BF16), preferred_element_type=F32)
    r = jnp.dot(ret_ref[...], wr_ref[...].astype(BF16), preferred_element_type=F32)
    o_ref[...] = (jax.nn.sigmoid(ga_ref[...].astype(F32)) * a
                  + jax.nn.sigmoid(gr_ref[...].astype(F32)) * r).astype(BF16)


def _branch(attn, ret, proj, wa, wr):
    t = attn.shape[0]
    tm, tn = BRANCH_TM, BRANCH_TN
    per_slab = D_MODEL // tn
    return pl.pallas_call(
        _branch_kernel,
        grid=(t // tm, D_MODEL // tn),
        in_specs=[pl.BlockSpec((tm, D_MODEL), lambda i, j: (i, 0)),
                  pl.BlockSpec((tm, D_MODEL), lambda i, j: (i, 0)),
                  pl.BlockSpec((tm, tn), lambda i, j: (i, COL_GA * per_slab + j)),
                  pl.BlockSpec((tm, tn), lambda i, j: (i, COL_GRT * per_slab + j)),
                  pl.BlockSpec((D_MODEL, tn), lambda i, j: (0, j)),
                  pl.BlockSpec((D_MODEL, tn), lambda i, j: (0, j))],
        out_specs=pl.BlockSpec((tm, tn), lambda i, j: (i, j)),
        out_shape=jax.ShapeDtypeStruct((t, D_MODEL), BF16),
        compiler_params=_params(("arbitrary", "arbitrary"), 48),
        name="branch",
    )(attn, ret, proj, proj, wa, wr)


def _mixout_kernel(m_ref, x_ref, mod_ref, g2_ref, wo_ref, wrt_ref, brt_ref,
                   x1_ref, h2_ref, route_ref, count_ref, carry_ref):
    @pl.when(pl.program_id(0) == 0)
    def _():
        carry_ref[...] = jnp.zeros_like(carry_ref)

    mix = jnp.dot(m_ref[...], wo_ref[...], preferred_element_type=F32)
    x1 = x_ref[...] + mod_ref[0, 2:3, :] * mix
    x1_ref[...] = x1
    var = jnp.mean(x1 * x1, axis=-1, keepdims=True)
    h2 = x1 * lax.rsqrt(var + EPS) * g2_ref[...]
    h2 = h2 * (1.0 + mod_ref[0, 4:5, :]) + mod_ref[0, 3:4, :]
    tm = h2.shape[0]
    for j in range(SLAB_ROWS):
        lo = h2[:, j * LANES:(j + 1) * LANES]
        hi = h2[:, HALF_D + j * LANES:HALF_D + (j + 1) * LANES]
        h2_ref[_slab_rows(j, tm), :] = _pack_pair(lo, hi)
    logits = jnp.dot(h2, wrt_ref[...], preferred_element_type=F32,
                     precision=lax.Precision.HIGHEST) + brt_ref[...]
    route = _route(logits)

    lane = lax.broadcasted_iota(jnp.int32, route.shape, 1)
    lane_f = lane.astype(F32)
    hot1 = lane_f == route[:, 0:1]
    hot2 = lane_f == route[:, 1:2]
    both = jnp.where(hot1 | hot2, 1.0, 0.0)
    ii = lax.broadcasted_iota(jnp.int32, (tm, tm), 0)
    jj = lax.broadcasted_iota(jnp.int32, (tm, tm), 1)
    lower = jnp.where(ii > jj, 1.0, 0.0).astype(BF16)
    before = jnp.dot(lower, both.astype(BF16), preferred_element_type=F32) + carry_ref[...]
    r1 = jnp.sum(jnp.where(hot1, before, 0.0), axis=-1, keepdims=True)
    r2 = jnp.sum(jnp.where(hot2, before, 0.0), axis=-1, keepdims=True)
    route_ref[...] = jnp.where(lane == 4, r1, jnp.where(lane == 5, r2, route))
    carry = carry_ref[...] + jnp.sum(both, axis=0, keepdims=True)
    carry_ref[...] = carry
    count_ref[...] = carry


def _mixout(merged, x2, mod6, gain2, wo, w_rt, b_rt, seq):
    t = x2.shape[0]
    tm = MIXOUT_TM
    tiles_per_batch = seq // tm
    row = lambda i: (i, 0)
    const = lambda i: (0, 0)
    return pl.pallas_call(
        _mixout_kernel,
        grid=(t // tm,),
        in_specs=[pl.BlockSpec((tm, D_MODEL), row),
                  pl.BlockSpec((tm, D_MODEL), row),
                  pl.BlockSpec((1, 6, D_MODEL), lambda i: (i // tiles_per_batch, 0, 0)),
                  pl.BlockSpec((1, D_MODEL), const),
                  pl.BlockSpec((D_MODEL, D_MODEL), const, pipeline_mode=pl.Buffered(1)),
                  pl.BlockSpec((D_MODEL, LANES), const),
                  pl.BlockSpec((1, LANES), const)],
        out_specs=[pl.BlockSpec((tm, D_MODEL), row),
                   pl.BlockSpec((tm * SLAB_ROWS, SLAB_LANES), row),
                   pl.BlockSpec((tm, LANES), row),
                   pl.BlockSpec((1, LANES), const)],
        out_shape=[jax.ShapeDtypeStruct((t, D_MODEL), F32),
                   jax.ShapeDtypeStruct((t * SLAB_ROWS, SLAB_LANES), I32),
                   jax.ShapeDtypeStruct((t, LANES), F32),
                   jax.ShapeDtypeStruct((1, LANES), F32)],
        scratch_shapes=[pltpu.VMEM((1, LANES), F32)],
        compiler_params=_params(("arbitrary",), 56),
        name="mixout",
    )(merged, x2, mod6, gain2, wo, w_rt, b_rt)


PAD_BITS = (64, 32, 16, 8, 4, 2, 1)


def _dispatch_kernel(dest_ref, zstart_ref, zcount_ref, h2_ref, xs_ref, zero_ref, sem, zsem):
    n_assign = dest_ref.shape[0]
    zero_ref[...] = jnp.zeros_like(zero_ref)

    def zero_copy(start, rows):
        return pltpu.make_async_copy(zero_ref.at[pl.ds(0, rows * SLAB_ROWS), :],
                                     xs_ref.at[pl.ds(start * SLAB_ROWS, rows * SLAB_ROWS), :], zsem)

    def fill(e, wait):
        start = zstart_ref[e]
        pad = zcount_ref[e]
        for bit in PAD_BITS:
            @pl.when((pad & bit) != 0)
            def _(start=start, bit=bit):
                cp = zero_copy(start, bit)
                cp.wait() if wait else cp.start()
            start = start + (pad & bit)

    lax.fori_loop(0, N_EXPERTS, lambda e, c: (fill(e, False), c)[1], 0)

    def row_copy(tok, dst):
        return pltpu.make_async_copy(h2_ref.at[pl.ds(tok * SLAB_ROWS, SLAB_ROWS), :],
                                     xs_ref.at[pl.ds(dst * SLAB_ROWS, SLAB_ROWS), :], sem)

    def issue(a, carry):
        row_copy(a // TOP_K, dest_ref[a]).start()
        return carry

    lax.fori_loop(0, n_assign, issue, 0, unroll=8)

    def drain(i, carry):
        pltpu.make_async_copy(h2_ref.at[pl.ds(0, ROWS_PER_WAIT * SLAB_ROWS), :],
                              xs_ref.at[pl.ds(0, ROWS_PER_WAIT * SLAB_ROWS), :], sem).wait()
        return carry

    lax.fori_loop(0, n_assign // ROWS_PER_WAIT, drain, 0)
    lax.fori_loop(0, N_EXPERTS, lambda e, c: (fill(e, True), c)[1], 0)


def _dispatch(dest, zstart, zcount, h2_slab, n_pad):
    return pl.pallas_call(
        _dispatch_kernel,
        grid_spec=pltpu.PrefetchScalarGridSpec(
            num_scalar_prefetch=3,
            grid=(1,),
            in_specs=[pl.BlockSpec(memory_space=pl.ANY)],
            out_specs=pl.BlockSpec(memory_space=pl.ANY),
            scratch_shapes=[pltpu.VMEM((PAD_BITS[0] * SLAB_ROWS, SLAB_LANES), I32),
                            pltpu.SemaphoreType.DMA(()),
                            pltpu.SemaphoreType.DMA(())]),
        out_shape=jax.ShapeDtypeStruct((n_pad * SLAB_ROWS, SLAB_LANES), I32),
        compiler_params=_params(("arbitrary",), 16),
        name="dispatch",
    )(dest, zstart, zcount, h2_slab)


def _experts_kernel(item_e_ref, item_blk_ref, item_nb_ref,
                    x0_ref, x1_ref, x2_ref, x3_ref, wg_ref, wu_ref, wd_ref, y_ref,
                    xb_ref, wgb_ref, wub_ref, wdb_ref, acc_ref, yp_ref, sem):
    w = pl.program_id(0)
    c = pl.program_id(1)
    last_c = pl.num_programs(1) - 1
    nb = item_nb_ref[w]
    blk0 = item_blk_ref[w]
    x_refs = (x0_ref, x1_ref, x2_ref, x3_ref)

    blk_rows = MOE_BLOCK * SLAB_ROWS

    def out_copy(s):
        return pltpu.make_async_copy(yp_ref.at[s], y_ref.at[pl.ds((blk0 + s) * blk_rows, blk_rows), :],
                                     sem.at[s])

    def wait_out(count):
        for s in range(ITEM_BLOCKS):
            @pl.when(s < count)
            def _(s=s):
                out_copy(s).wait()

    @pl.when(nb > 0)
    def _():
        wgb_ref[...] = wg_ref[...].astype(BF16)
        wub_ref[...] = wu_ref[...].astype(BF16)
        wdb_ref[...] = wd_ref[...].astype(BF16)

    @pl.when((c == last_c) & (w > 0))
    def _():
        wait_out(item_nb_ref[jnp.maximum(w - 1, 0)])

    def run(n_live):
        rows = n_live * MOE_BLOCK

        @pl.when(c == 0)
        def _():
            for s in range(n_live):
                for j in range(SLAB_ROWS):
                    lo, hi = _unpack_pair(x_refs[s][_slab_rows(j, MOE_BLOCK), :])
                    r0 = s * MOE_BLOCK
                    xb_ref[r0:r0 + MOE_BLOCK, j * LANES:(j + 1) * LANES] = lo.astype(BF16)
                    xb_ref[r0:r0 + MOE_BLOCK, HALF_D + j * LANES:HALF_D + (j + 1) * LANES] = hi.astype(BF16)

        xs = xb_ref[0:rows, :]
        g = jnp.dot(xs, wgb_ref[...], preferred_element_type=F32)
        u = jnp.dot(xs, wub_ref[...], preferred_element_type=F32)
        hid = (g * jax.nn.sigmoid(g) * u).astype(BF16)
        y = jnp.dot(hid, wdb_ref[...], preferred_element_type=F32)

        @pl.when(c == 0)
        def _():
            acc_ref[0:rows, :] = y

        @pl.when((c > 0) & (c < last_c))
        def _():
            acc_ref[0:rows, :] += y

        @pl.when(c == last_c)
        def _():
            for s in range(n_live):
                r0 = s * MOE_BLOCK
                for j in range(SLAB_ROWS):
                    lo_cols = slice(j * LANES, (j + 1) * LANES)
                    hi_cols = slice(HALF_D + j * LANES, HALF_D + (j + 1) * LANES)
                    yp_ref[s, _slab_rows(j, MOE_BLOCK), :] = _pack_pair(
                        acc_ref[r0:r0 + MOE_BLOCK, lo_cols] + y[r0:r0 + MOE_BLOCK, lo_cols],
                        acc_ref[r0:r0 + MOE_BLOCK, hi_cols] + y[r0:r0 + MOE_BLOCK, hi_cols])
                out_copy(s).start()

    for n_live in range(1, ITEM_BLOCKS + 1):
        pl.when(nb == n_live)(functools.partial(run, n_live))

    @pl.when((c == last_c) & (w == pl.num_programs(0) - 1))
    def _():
        wait_out(nb)


def _experts(item_e, item_blk, item_nb, x_sorted, w_gate, w_up, w_down, n_blocks):
    n_items = item_e.shape[0]
    d = D_MODEL
    fc = EXPERT_FC
    x3 = x_sorted
    blk_rows = MOE_BLOCK * SLAB_ROWS
    n_chunks = EXPERT_DIM // fc
    assert n_chunks >= 2, "the last hidden chunk adds onto the accumulator of the earlier ones"

    def x_spec(s):
        return pl.BlockSpec((blk_rows, SLAB_LANES),
                            lambda w, c, ie, ib, inb: (jnp.minimum(ib[w] + s, n_blocks - 1), 0))

    def chunk(w, c, inb):
        return jnp.where(inb[w] > 0, c, n_chunks - 1)

    return pl.pallas_call(
        _experts_kernel,
        grid_spec=pltpu.PrefetchScalarGridSpec(
            num_scalar_prefetch=3,
            grid=(n_items, n_chunks),
            in_specs=[x_spec(0), x_spec(1), x_spec(2), x_spec(3),
                      pl.BlockSpec((None, d, fc), lambda w, c, ie, ib, inb: (ie[w], 0, chunk(w, c, inb))),
                      pl.BlockSpec((None, d, fc), lambda w, c, ie, ib, inb: (ie[w], 0, chunk(w, c, inb))),
                      pl.BlockSpec((None, fc, d), lambda w, c, ie, ib, inb: (ie[w], chunk(w, c, inb), 0))],
            out_specs=pl.BlockSpec(memory_space=pl.ANY),
            scratch_shapes=[pltpu.VMEM((ITEM_BLOCKS * MOE_BLOCK, d), BF16),
                            pltpu.VMEM((d, fc), BF16),
                            pltpu.VMEM((d, fc), BF16),
                            pltpu.VMEM((fc, d), BF16),
                            pltpu.VMEM((ITEM_BLOCKS * MOE_BLOCK, d), F32),
                            pltpu.VMEM((ITEM_BLOCKS, blk_rows, SLAB_LANES), I32),
                            pltpu.SemaphoreType.DMA((ITEM_BLOCKS,))]),
        out_shape=jax.ShapeDtypeStruct((n_blocks * blk_rows, SLAB_LANES), I32),
        compiler_params=_params(("arbitrary", "arbitrary"), 56),
        name="experts",
    )(item_e, item_blk, item_nb, x3, x3, x3, x3, w_gate, w_up, w_down)


def _combine_kernel(dest_ref, x1_ref, route_ref, mod_ref, gain_ref, y_ref, o_ref, ybuf_ref, sem):
    tm = COMBINE_TM
    i = pl.program_id(0)
    n_tiles = pl.num_programs(0)

    def row_copy(slot, k, r, src):
        return pltpu.make_async_copy(y_ref.at[pl.ds(src * SLAB_ROWS, SLAB_ROWS), :],
                                     ybuf_ref.at[slot, k, pl.ds(r * SLAB_ROWS, SLAB_ROWS), :],
                                     sem.at[slot])

    def issue_tile(tile, slot):
        base = tile * tm * TOP_K

        def issue(r, carry):
            for k in range(TOP_K):
                row_copy(slot, k, r, dest_ref[base + r * TOP_K + k]).start()
            return carry

        lax.fori_loop(0, tm, issue, 0, unroll=4)

    @pl.when(i == 0)
    def _():
        issue_tile(0, 0)

    @pl.when(i + 1 < n_tiles)
    def _():
        issue_tile(i + 1, (i + 1) % 2)

    slot = i % 2
    for k in range(TOP_K):
        pltpu.make_async_copy(y_ref.at[pl.ds(0, tm * SLAB_ROWS), :], ybuf_ref.at[slot, k],
                              sem.at[slot]).wait()

    route = route_ref[...]
    w0 = route[:, 2:3]
    w1 = route[:, 3:4]
    ssq = jnp.zeros((tm, 1), F32)
    for j in range(SLAB_ROWS):
        rows = _slab_rows(j, tm)
        y0 = _unpack_pair(ybuf_ref[slot, 0, rows, :])
        y1 = _unpack_pair(ybuf_ref[slot, 1, rows, :])
        for part, off in ((0, j * LANES), (1, HALF_D + j * LANES)):
            cols = slice(off, off + LANES)
            ffn = w0 * y0[part] + w1 * y1[part]
            x2 = x1_ref[:, cols] + mod_ref[0, 5:6, cols] * ffn
            o_ref[:, cols] = x2
            ssq = ssq + jnp.sum(x2 * x2, axis=-1, keepdims=True)
    o_ref[...] = o_ref[...] * lax.rsqrt(ssq * (1.0 / D_MODEL) + EPS) * gain_ref[...]


def _combine(dest, x1, route, mod6, gain, y_sorted, seq):
    t, d = x1.shape
    tm = COMBINE_TM
    tiles_per_batch = seq // tm
    return pl.pallas_call(
        _combine_kernel,
        grid_spec=pltpu.PrefetchScalarGridSpec(
            num_scalar_prefetch=1,
            grid=(t // tm,),
            in_specs=[pl.BlockSpec((tm, d), lambda i, dst: (i, 0)),
                      pl.BlockSpec((tm, LANES), lambda i, dst: (i, 0)),
                      pl.BlockSpec((1, 6, d), lambda i, dst: (i // tiles_per_batch, 0, 0)),
                      pl.BlockSpec((1, d), lambda i, dst: (0, 0)),
                      pl.BlockSpec(memory_space=pl.ANY)],
            out_specs=pl.BlockSpec((tm, d), lambda i, dst: (i, 0)),
            scratch_shapes=[pltpu.VMEM((2, TOP_K, tm * SLAB_ROWS, SLAB_LANES), I32),
                            pltpu.SemaphoreType.DMA((2,))]),
        out_shape=jax.ShapeDtypeStruct((t, d), F32),
        compiler_params=_params(("arbitrary",), 24),
        name="combine",
    )(dest, x1, route, mod6, gain, y_sorted)


def _dispatch_tables(route, counts, t):
    n_assign = t * TOP_K
    n_pad = -(-(n_assign + N_EXPERTS * (MOE_BLOCK - 1)) // MOE_BLOCK) * MOE_BLOCK
    n_blocks = n_pad // MOE_BLOCK
    n_items = N_EXPERTS + n_assign // (ITEM_BLOCKS * MOE_BLOCK)

    eid = route[:, :TOP_K].astype(jnp.int32)
    rk = route[:, 4:4 + TOP_K].astype(jnp.int32)
    cnt = counts[0, :N_EXPERTS].astype(jnp.int32)
    blocks_e = (cnt + MOE_BLOCK - 1) // MOE_BLOCK
    blk_end = jnp.cumsum(blocks_e)
    blk_start = blk_end - blocks_e
    hot = eid[:, :, None] == jnp.arange(N_EXPERTS, dtype=jnp.int32)
    row_start = jnp.sum(jnp.where(hot, blk_start * MOE_BLOCK, 0), axis=-1)
    dest = (row_start + rk).reshape(-1)
    zstart = (blk_start * MOE_BLOCK + cnt).astype(jnp.int32)
    zcount = (blocks_e * MOE_BLOCK - cnt).astype(jnp.int32)

    items_e = (blocks_e + ITEM_BLOCKS - 1) // ITEM_BLOCKS
    item_end = jnp.cumsum(items_e)
    item_start = item_end - items_e
    w = jnp.arange(n_items, dtype=jnp.int32)
    live = w < item_end[-1]
    w_live = jnp.minimum(w, item_end[-1] - 1)
    e_w = jnp.minimum(jnp.searchsorted(item_end, w_live, side='right'), N_EXPERTS - 1).astype(jnp.int32)
    j_w = w_live - item_start[e_w]
    item_blk = (blk_start[e_w] + ITEM_BLOCKS * j_w).astype(jnp.int32)
    item_nb = jnp.where(live, jnp.clip(blocks_e[e_w] - ITEM_BLOCKS * j_w, 0, ITEM_BLOCKS), 0).astype(jnp.int32)
    return dest.astype(jnp.int32), zstart, zcount, e_w, item_blk, item_nb, n_blocks


def kernel(x, c, positions, norm1_gain, norm2_gain, final_norm_gain, w_ada, b_ada, w_in, attn_sinks,
           ret_norm_gain, w_branch_attn, w_branch_ret, w_out, w_router_group, b_router_group,
           w_router_expert, b_router_expert, w_expert_gate, w_expert_up, w_expert_down):
    batch, seq, d = x.shape
    t = batch * seq
    depth = w_ada.shape[0]
    half = RET_DIM // 2
    inv_freq = (ROPE_BASE ** (-jnp.arange(half, dtype=F32) / half)).reshape(1, half)
    pos = positions.reshape(t, 1)
    c8 = jnp.pad(c, ((0, 8 - batch), (0, 0)))
    xf = x.reshape(t, d)

    assert depth == 1, "the fused final norm assumes a single layer"
    for layer in range(depth):
        mod6 = _ada(c8, w_ada[layer], b_ada[layer].reshape(1, -1))[:batch].reshape(batch, 6, d)
        proj = _proj(xf, norm1_gain[layer].reshape(1, d), mod6, w_in[layer], seq)
        attn = _attn(proj, attn_sinks[layer], batch, seq)
        ret = _ret(proj, pos, inv_freq, ret_norm_gain[layer].reshape(1, d), batch, seq)

        pad = LANES - N_GROUPS - N_EXPERTS
        w_rt = jnp.concatenate([w_router_group[layer], w_router_expert[layer],
                                jnp.zeros((d, pad), F32)], axis=1)
        b_rt = jnp.concatenate([b_router_group[layer], b_router_expert[layer],
                                jnp.zeros((pad,), F32)]).reshape(1, LANES)
        merged = _branch(attn, ret, proj, w_branch_attn[layer], w_branch_ret[layer])
        x1, h2, route, counts = _mixout(merged, xf, mod6, norm2_gain[layer].reshape(1, d),
                                        w_out[layer].astype(BF16), w_rt, b_rt, seq)
        dest, zstart, zcount, item_e, item_blk, item_nb, n_blocks = _dispatch_tables(route, counts, t)
        x_sorted = _dispatch(dest, zstart, zcount, h2, n_blocks * MOE_BLOCK)
        y_sorted = _experts(item_e, item_blk, item_nb, x_sorted,
                            w_expert_gate[layer], w_expert_up[layer], w_expert_down[layer], n_blocks)
        xf = _combine(dest, x1, route, mod6, final_norm_gain.reshape(1, d), y_sorted, seq)
    return xf.reshape(batch, seq, d)
```

```python
import functools
import math

import jax
import jax.numpy as jnp
import numpy as np
from jax import lax
from jax.experimental import pallas as pl
from jax.experimental.pallas import tpu as pltpu

F32 = jnp.float32
BF16 = jnp.bfloat16

D_MODEL = 2048
ATTN_HEAD_DIM = 64
ATTN_HEADS = 32
ATTN_KV_HEADS = 4
ATTN_GROUP = 8
WINDOW = 128
RET_HEADS = 8
RET_DIM = 256
RET_CHUNK = 128
ROPE_BASE = 10000.0
N_GROUPS = 4
EXPERTS_PER_GROUP = 16
N_EXPERTS = 64
TOP_K = 2
EXPERT_DIM = 1024
MOE_BLOCK = 128
EPS = 1e-6
NEG = -1e30

MIB = 1024 * 1024
LANES = 128
PROJ_TN = 512
PROJ_TM = 1024
KV_TILE = 28
BRANCH_TM = 1024
BRANCH_TN = 512
MIXOUT_TM = 512
ITEM_BLOCKS = 4
EXPERT_FC = 512
COMBINE_TM = 128
SLAB_ROWS = 8
SLAB_LANES = LANES
HALF_D = D_MODEL // 2
ROWS_PER_WAIT = 128
I32 = jnp.int32

COL_QA, COL_QR, COL_KR, COL_VR, COL_GR, COL_GA, COL_GRT = 0, 1, 2, 3, 4, 5, 6

LOG_GAMMA = [math.log1p(-(2.0 ** (-5.0 - h))) for h in range(RET_HEADS)]


def _params(sem, vmem_mib):
    return pltpu.CompilerParams(dimension_semantics=sem, vmem_limit_bytes=vmem_mib * MIB)


def _pack_pair(lo, hi):
    lo_b = lax.bitcast_convert_type(lo.astype(BF16).astype(F32), I32)
    hi_b = lax.bitcast_convert_type(hi.astype(BF16).astype(F32), I32)
    return hi_b | lax.shift_right_logical(lo_b, jnp.full_like(lo_b, 16))


def _unpack_pair(w):
    lo = lax.bitcast_convert_type(w << 16, F32)
    hi = lax.bitcast_convert_type(w & jnp.int32(-65536), F32)
    return lo, hi


def _slab_rows(j, n_tokens):
    return pl.ds(j, n_tokens, stride=SLAB_ROWS)


def _ada_kernel(c_ref, w_ref, b_ref, o_ref):
    c = c_ref[...]
    a = (c * jax.nn.sigmoid(c)).astype(BF16)
    o_ref[...] = jnp.dot(a, w_ref[...].astype(BF16), preferred_element_type=F32) + b_ref[...]


def _ada(c8, w_ada, b_ada):
    n = w_ada.shape[1]
    tn = 1024
    return pl.pallas_call(
        _ada_kernel,
        grid=(n // tn,),
        in_specs=[pl.BlockSpec((8, D_MODEL), lambda j: (0, 0)),
                  pl.BlockSpec((D_MODEL, tn), lambda j: (0, j)),
                  pl.BlockSpec((1, tn), lambda j: (0, j))],
        out_specs=pl.BlockSpec((8, tn), lambda j: (0, j)),
        out_shape=jax.ShapeDtypeStruct((8, n), F32),
        compiler_params=_params(("arbitrary",), 40),
        name="ada",
    )(c8, w_ada, b_ada)


def _proj_kernel(x_ref, g_ref, mod_ref, w_ref, o_ref, h_ref):
    @pl.when(pl.program_id(1) == 0)
    def _():
        x = x_ref[...]
        var = jnp.mean(x * x, axis=-1, keepdims=True)
        y = x * lax.rsqrt(var + EPS) * g_ref[...]
        h_ref[...] = (y * (1.0 + mod_ref[0, 1:2, :]) + mod_ref[0, 0:1, :]).astype(BF16)

    o_ref[...] = jnp.dot(h_ref[...], w_ref[...].astype(BF16),
                         preferred_element_type=F32).astype(BF16)


def _proj_out_tile(j):
    return jnp.where(j < 4, j, jnp.where(j == 4, KV_TILE, j - 1))


def _proj(x2, gain, mod6, w_in, seq):
    t = x2.shape[0]
    n = w_in.shape[1]
    tiles_per_batch = seq // PROJ_TM
    return pl.pallas_call(
        _proj_kernel,
        grid=(t // PROJ_TM, n // PROJ_TN),
        in_specs=[pl.BlockSpec((PROJ_TM, D_MODEL), lambda i, j: (i, 0)),
                  pl.BlockSpec((1, D_MODEL), lambda i, j: (0, 0)),
                  pl.BlockSpec((1, 6, D_MODEL), lambda i, j: (i // tiles_per_batch, 0, 0)),
                  pl.BlockSpec((D_MODEL, PROJ_TN), lambda i, j: (0, j))],
        out_specs=pl.BlockSpec((PROJ_TM, PROJ_TN), lambda i, j: (i, _proj_out_tile(j))),
        out_shape=jax.ShapeDtypeStruct((t, n), BF16),
        scratch_shapes=[pltpu.VMEM((PROJ_TM, D_MODEL), BF16)],
        compiler_params=_params(("arbitrary", "arbitrary"), 48),
        name="proj",
    )(x2, gain, mod6, w_in)


def _attn_kernel(sink_ref, q_ref, kvp_ref, kvc_ref, o_ref):
    n = pl.program_id(1)
    kvp = kvp_ref[...]
    kvc = kvc_ref[...]
    qi = lax.broadcasted_iota(jnp.int32, (WINDOW, WINDOW), 0)
    sj = lax.broadcasted_iota(jnp.int32, (WINDOW, WINDOW), 1)
    valid_prev = (sj > qi) & (n > 0)
    valid_cur = sj <= qi
    sink_col = sj == 0
    first_row = lax.broadcasted_iota(jnp.int32, (2 * WINDOW, 1), 0) == 0
    dh = ATTN_HEAD_DIM
    kv_w = ATTN_KV_HEADS * dh
    scale = jnp.asarray(dh ** -0.5, BF16)
    for kv in range(ATTN_KV_HEADS):
        kband = jnp.concatenate([kvp[:, kv * dh:(kv + 1) * dh],
                                 kvc[:, kv * dh:(kv + 1) * dh]], axis=0) * scale
        vband = jnp.concatenate([kvp[:, kv_w + kv * dh:kv_w + (kv + 1) * dh],
                                 kvc[:, kv_w + kv * dh:kv_w + (kv + 1) * dh]], axis=0)
        vband = jnp.where(first_row, jnp.zeros_like(vband), vband)
        def scores(g):
            h = kv * ATTN_GROUP + g
            return lax.dot_general(q_ref[:, h * dh:(h + 1) * dh], kband, (((1,), (1,)), ((), ())),
                                   preferred_element_type=F32)

        outs = []
        s_next = scores(0)
        for g in range(ATTN_GROUP):
            s = s_next
            if g + 1 < ATTN_GROUP:
                s_next = scores(g + 1)
            sink = sink_ref[kv * ATTN_GROUP + g]
            s_prev = jnp.where(sink_col, sink, jnp.where(valid_prev, s[:, :WINDOW], NEG))
            s_cur = jnp.where(valid_cur, s[:, WINDOW:], NEG)
            m = jnp.max(jnp.maximum(s_prev, s_cur), axis=-1, keepdims=True)
            p_prev = jnp.exp(s_prev - m)
            p_cur = jnp.exp(s_cur - m)
            denom = jnp.sum(p_prev + p_cur, axis=-1, keepdims=True)
            p = jnp.concatenate([p_prev, p_cur], axis=-1).astype(BF16)
            o = jnp.dot(p, vband, preferred_element_type=F32)
            outs.append(o * (1.0 / denom))
        o_ref[:, kv * ATTN_GROUP * dh:(kv + 1) * ATTN_GROUP * dh] = (
            jnp.concatenate(outs, axis=-1).astype(BF16))


def _attn(proj, sinks, batch, seq):
    nb = seq // WINDOW
    t = batch * seq
    kv_blk = KV_TILE

    def cur(b, n):
        return (b * nb + n, kv_blk)

    def prev(b, n):
        return (b * nb + jnp.maximum(n - 1, 0), kv_blk)

    return pl.pallas_call(
        _attn_kernel,
        grid=(batch, nb),
        in_specs=[pl.BlockSpec(memory_space=pltpu.SMEM),
                  pl.BlockSpec((WINDOW, D_MODEL), lambda b, n: (b * nb + n, COL_QA)),
                  pl.BlockSpec((WINDOW, PROJ_TN), prev),
                  pl.BlockSpec((WINDOW, PROJ_TN), cur)],
        out_specs=pl.BlockSpec((WINDOW, D_MODEL), lambda b, n: (b * nb + n, 0)),
        out_shape=jax.ShapeDtypeStruct((t, D_MODEL), BF16),
        compiler_params=_params(("arbitrary", "arbitrary"), 32),
        name="attn",
    )(sinks, proj, proj, proj)


def _ret_kernel(pos_ref, invf_ref, q_ref, k_ref, v_ref, g_ref, gain_ref, o_ref, state_ref):
    @pl.when(pl.program_id(1) == 0)
    def _():
        state_ref[...] = jnp.zeros_like(state_ref)

    half = RET_DIM // 2
    ang = pos_ref[...].astype(F32) * invf_ref[...]
    cos = jnp.cos(ang)
    sin = jnp.sin(ang)
    idx = lax.broadcasted_iota(jnp.int32, (RET_CHUNK, 1), 0).astype(F32)
    ii = lax.broadcasted_iota(jnp.int32, (RET_CHUNK, RET_CHUNK), 0)
    jj = lax.broadcasted_iota(jnp.int32, (RET_CHUNK, RET_CHUNK), 1)
    diff = (ii - jj).astype(F32)

    def rot(t):
        t1, t2 = t[:, :half], t[:, half:]
        return jnp.concatenate([t1 * cos - t2 * sin, t1 * sin + t2 * cos], axis=-1)

    for h in range(RET_HEADS):
        lg = LOG_GAMMA[h]
        sl = slice(h * RET_DIM, (h + 1) * RET_DIM)
        qr = rot(q_ref[:, sl].astype(F32))
        kr = rot(k_ref[:, sl].astype(F32)) * (RET_DIM ** -0.5)
        vb = v_ref[:, sl]
        d_intra = jnp.where(diff >= 0, jnp.exp(jnp.maximum(diff, 0.0) * lg), 0.0)
        d_q = jnp.exp((idx + 1.0) * lg)
        d_k = jnp.exp((RET_CHUNK - 1.0 - idx) * lg)
        d_chunk = math.exp(RET_CHUNK * lg)
        qb = qr.astype(BF16)
        kb = kr.astype(BF16)
        intra = lax.dot_general(qb, kb, (((1,), (1,)), ((), ())),
                                preferred_element_type=F32) * d_intra
        st = state_ref[h]
        o = (jnp.dot(intra.astype(BF16), vb, preferred_element_type=F32)
             + jnp.dot(qb, st.astype(BF16), preferred_element_type=F32) * d_q)
        kd = (kr * d_k).astype(BF16)
        state_ref[h] = st * d_chunk + lax.dot_general(kd, vb, (((0,), (0,)), ((), ())),
                                                      preferred_element_type=F32)
        o = o * lax.rsqrt(jnp.mean(o * o, axis=-1, keepdims=True) + EPS) * gain_ref[:, sl]
        gg = g_ref[:, sl].astype(F32)
        o_ref[:, sl] = (gg * jax.nn.sigmoid(gg) * o).astype(BF16)


def _ret(proj, pos, inv_freq, ret_gain, batch, seq):
    nc = seq // RET_CHUNK
    t = batch * seq

    def col(cb):
        return lambda b, c: (b * nc + c, cb)

    return pl.pallas_call(
        _ret_kernel,
        grid=(batch, nc),
        in_specs=[pl.BlockSpec((RET_CHUNK, 1), lambda b, c: (b * nc + c, 0)),
                  pl.BlockSpec((1, RET_DIM // 2), lambda b, c: (0, 0)),
                  pl.BlockSpec((RET_CHUNK, D_MODEL), col(COL_QR)),
                  pl.BlockSpec((RET_CHUNK, D_MODEL), col(COL_KR)),
                  pl.BlockSpec((RET_CHUNK, D_MODEL), col(COL_VR)),
                  pl.BlockSpec((RET_CHUNK, D_MODEL), col(COL_GR)),
                  pl.BlockSpec((1, D_MODEL), lambda b, c: (0, 0))],
        out_specs=pl.BlockSpec((RET_CHUNK, D_MODEL), lambda b, c: (b * nc + c, 0)),
        out_shape=jax.ShapeDtypeStruct((t, D_MODEL), BF16),
        scratch_shapes=[pltpu.VMEM((RET_HEADS, RET_DIM, RET_DIM), F32)],
        compiler_params=_params(("arbitrary", "arbitrary"), 32),
        name="ret",
    )(pos, inv_freq, proj, proj, proj, proj, ret_gain)


def _route(logits):
    lane = lax.broadcasted_iota(jnp.int32, logits.shape, 1)
    lane_f = lane.astype(F32)
    is_g = lane < N_GROUPS
    gl = jnp.where(is_g, logits, NEG)
    gmax = jnp.max(gl, axis=-1, keepdims=True)
    gsel = jnp.min(jnp.where(gl == gmax, lane_f, float(LANES)), axis=-1, keepdims=True)
    gsum = jnp.sum(jnp.where(is_g, jnp.exp(gl - gmax), 0.0), axis=-1, keepdims=True)
    g_w = 1.0 / gsum
    grp = ((lane - N_GROUPS) >> 4).astype(F32)
    is_e = (lane >= N_GROUPS) & (lane < N_GROUPS + N_EXPERTS) & (grp == gsel)
    el = jnp.where(is_e, logits, NEG)
    v1 = jnp.max(el, axis=-1, keepdims=True)
    i1 = jnp.min(jnp.where(el == v1, lane_f, float(LANES)), axis=-1, keepdims=True)
    el2 = jnp.where(lane_f == i1, NEG, el)
    v2 = jnp.max(el2, axis=-1, keepdims=True)
    i2 = jnp.min(jnp.where(el2 == v2, lane_f, float(LANES)), axis=-1, keepdims=True)
    tt = jnp.exp(v2 - v1)
    w1 = g_w / (1.0 + tt)
    w2 = g_w * tt / (1.0 + tt)
    return jnp.where(lane == 0, i1 - N_GROUPS,
                     jnp.where(lane == 1, i2 - N_GROUPS,
                               jnp.where(lane == 2, w1, jnp.where(lane == 3, w2, 0.0))))


def _branch_kernel(attn_ref, ret_ref, ga_ref, gr_ref, wa_ref, wr_ref, o_ref):
    a = jnp.dot(attn_ref[...], wa_ref[...].astype(BF16), preferred_element_type=F32)
    r = jnp.dot(ret_ref[...], wr_ref[...].astype(BF16), preferred_element_type=F32)
    o_ref[...] = (jax.nn.sigmoid(ga_ref[...].astype(F32)) * a
                  + jax.nn.sigmoid(gr_ref[...].astype(F32)) * r).astype(BF16)


def _branch(attn, ret, proj, wa, wr):
    t = attn.shape[0]
    tm, tn = BRANCH_TM, BRANCH_TN
    per_slab = D_MODEL // tn
    return pl.pallas_call(
        _branch_kernel,
        grid=(t // tm, D_MODEL // tn),
        in_specs=[pl.BlockSpec((tm, D_MODEL), lambda i, j: (i, 0)),
                  pl.BlockSpec((tm, D_MODEL), lambda i, j: (i, 0)),
                  pl.BlockSpec((tm, tn), lambda i, j: (i, COL_GA * per_slab + j)),
                  pl.BlockSpec((tm, tn), lambda i, j: (i, COL_GRT * per_slab + j)),
                  pl.BlockSpec((D_MODEL, tn), lambda i, j: (0, j)),
                  pl.BlockSpec((D_MODEL, tn), lambda i, j: (0, j))],
        out_specs=pl.BlockSpec((tm, tn), lambda i, j: (i, j)),
        out_shape=jax.ShapeDtypeStruct((t, D_MODEL), BF16),
        compiler_params=_params(("arbitrary", "arbitrary"), 48),
        name="branch",
    )(attn, ret, proj, proj, wa, wr)


def _mixout_kernel(m_ref, x_ref, mod_ref, g2_ref, wo_ref, wrt_ref, brt_ref,
                   x1_ref, h2_ref, route_ref, count_ref, carry_ref):
    @pl.when(pl.program_id(0) == 0)
    def _():
        carry_ref[...] = jnp.zeros_like(carry_ref)

    mix = jnp.dot(m_ref[...], wo_ref[...], preferred_element_type=F32)
    x1 = x_ref[...] + mod_ref[0, 2:3, :] * mix
    x1_ref[...] = x1
    var = jnp.mean(x1 * x1, axis=-1, keepdims=True)
    h2 = x1 * lax.rsqrt(var + EPS) * g2_ref[...]
    h2 = h2 * (1.0 + mod_ref[0, 4:5, :]) + mod_ref[0, 3:4, :]
    tm = h2.shape[0]
    for j in range(SLAB_ROWS):
        lo = h2[:, j * LANES:(j + 1) * LANES]
        hi = h2[:, HALF_D + j * LANES:HALF_D + (j + 1) * LANES]
        h2_ref[_slab_rows(j, tm), :] = _pack_pair(lo, hi)
    h_hi = h2.astype(BF16)
    h_lo = (h2 - h_hi.astype(F32)).astype(BF16)
    w_rt = wrt_ref[...]
    w_hi = w_rt.astype(BF16)
    w_lo = (w_rt - w_hi.astype(F32)).astype(BF16)
    hi_both = jnp.dot(h_hi, jnp.concatenate([w_hi, w_lo], axis=1), preferred_element_type=F32)
    logits = (hi_both[:, :LANES] + hi_both[:, LANES:]
              + jnp.dot(h_lo, w_hi, preferred_element_type=F32) + brt_ref[...])
    route = _route(logits)

    lane = lax.broadcasted_iota(jnp.int32, route.shape, 1)
    lane_f = lane.astype(F32)
    hot1 = lane_f == route[:, 0:1]
    hot2 = lane_f == route[:, 1:2]
    both = jnp.where(hot1 | hot2, 1.0, 0.0)
    ii = lax.broadcasted_iota(jnp.int32, (tm, tm), 0)
    jj = lax.broadcasted_iota(jnp.int32, (tm, tm), 1)
    lower = jnp.where(ii > jj, 1.0, 0.0).astype(BF16)
    before = jnp.dot(lower, both.astype(BF16), preferred_element_type=F32) + carry_ref[...]
    r1 = jnp.sum(jnp.where(hot1, before, 0.0), axis=-1, keepdims=True)
    r2 = jnp.sum(jnp.where(hot2, before, 0.0), axis=-1, keepdims=True)
    route_ref[...] = jnp.where(lane == 4, r1, jnp.where(lane == 5, r2, route))
    carry = carry_ref[...] + jnp.sum(both, axis=0, keepdims=True)
    carry_ref[...] = carry
    count_ref[...] = carry


def _mixout(merged, x2, mod6, gain2, wo, w_rt, b_rt, seq):
    t = x2.shape[0]
    tm = MIXOUT_TM
    tiles_per_batch = seq // tm
    row = lambda i: (i, 0)
    const = lambda i: (0, 0)
    return pl.pallas_call(
        _mixout_kernel,
        grid=(t // tm,),
        in_specs=[pl.BlockSpec((tm, D_MODEL), row),
                  pl.BlockSpec((tm, D_MODEL), row),
                  pl.BlockSpec((1, 6, D_MODEL), lambda i: (i // tiles_per_batch, 0, 0)),
                  pl.BlockSpec((1, D_MODEL), const),
                  pl.BlockSpec((D_MODEL, D_MODEL), const, pipeline_mode=pl.Buffered(1)),
                  pl.BlockSpec((D_MODEL, LANES), const),
                  pl.BlockSpec((1, LANES), const)],
        out_specs=[pl.BlockSpec((tm, D_MODEL), row),
                   pl.BlockSpec((tm * SLAB_ROWS, SLAB_LANES), row),
                   pl.BlockSpec((tm, LANES), row),
                   pl.BlockSpec((1, LANES), const)],
        out_shape=[jax.ShapeDtypeStruct((t, D_MODEL), F32),
                   jax.ShapeDtypeStruct((t * SLAB_ROWS, SLAB_LANES), I32),
                   jax.ShapeDtypeStruct((t, LANES), F32),
                   jax.ShapeDtypeStruct((1, LANES), F32)],
        scratch_shapes=[pltpu.VMEM((1, LANES), F32)],
        compiler_params=_params(("arbitrary",), 56),
        name="mixout",
    )(merged, x2, mod6, gain2, wo, w_rt, b_rt)


PAD_BITS = (64, 32, 16, 8, 4, 2, 1)


def _dispatch_kernel(dest_ref, zstart_ref, zcount_ref, h2_ref, xs_ref, zero_ref, sem, zsem):
    n_assign = dest_ref.shape[0]
    zero_ref[...] = jnp.zeros_like(zero_ref)

    def zero_copy(start, rows):
        return pltpu.make_async_copy(zero_ref.at[pl.ds(0, rows * SLAB_ROWS), :],
                                     xs_ref.at[pl.ds(start * SLAB_ROWS, rows * SLAB_ROWS), :], zsem)

    def fill(e, wait):
        start = zstart_ref[e]
        pad = zcount_ref[e]
        for bit in PAD_BITS:
            @pl.when((pad & bit) != 0)
            def _(start=start, bit=bit):
                cp = zero_copy(start, bit)
                cp.wait() if wait else cp.start()
            start = start + (pad & bit)

    lax.fori_loop(0, N_EXPERTS, lambda e, c: (fill(e, False), c)[1], 0)

    def row_copy(tok, dst):
        return pltpu.make_async_copy(h2_ref.at[pl.ds(tok * SLAB_ROWS, SLAB_ROWS), :],
                                     xs_ref.at[pl.ds(dst * SLAB_ROWS, SLAB_ROWS), :], sem)

    def issue(a, carry):
        row_copy(a // TOP_K, dest_ref[a]).start()
        return carry

    lax.fori_loop(0, n_assign, issue, 0, unroll=8)

    def drain(i, carry):
        pltpu.make_async_copy(h2_ref.at[pl.ds(0, ROWS_PER_WAIT * SLAB_ROWS), :],
                              xs_ref.at[pl.ds(0, ROWS_PER_WAIT * SLAB_ROWS), :], sem).wait()
        return carry

    lax.fori_loop(0, n_assign // ROWS_PER_WAIT, drain, 0)
    lax.fori_loop(0, N_EXPERTS, lambda e, c: (fill(e, True), c)[1], 0)


def _dispatch(dest, zstart, zcount, h2_slab, n_pad):
    return pl.pallas_call(
        _dispatch_kernel,
        grid_spec=pltpu.PrefetchScalarGridSpec(
            num_scalar_prefetch=3,
            grid=(1,),
            in_specs=[pl.BlockSpec(memory_space=pl.ANY)],
            out_specs=pl.BlockSpec(memory_space=pl.ANY),
            scratch_shapes=[pltpu.VMEM((PAD_BITS[0] * SLAB_ROWS, SLAB_LANES), I32),
                            pltpu.SemaphoreType.DMA(()),
                            pltpu.SemaphoreType.DMA(())]),
        out_shape=jax.ShapeDtypeStruct((n_pad * SLAB_ROWS, SLAB_LANES), I32),
        compiler_params=_params(("arbitrary",), 16),
        name="dispatch",
    )(dest, zstart, zcount, h2_slab)


def _experts_kernel(item_e_ref, item_blk_ref, item_nb_ref,
                    x0_ref, x1_ref, x2_ref, x3_ref, wg_ref, wu_ref, wd_ref, y_ref,
                    xb_ref, wgb_ref, wub_ref, wdb_ref, acc_ref, yp_ref, sem):
    w = pl.program_id(0)
    c = pl.program_id(1)
    last_c = pl.num_programs(1) - 1
    nb = item_nb_ref[w]
    blk0 = item_blk_ref[w]
    x_refs = (x0_ref, x1_ref, x2_ref, x3_ref)

    blk_rows = MOE_BLOCK * SLAB_ROWS

    def out_copy(s):
        return pltpu.make_async_copy(yp_ref.at[s], y_ref.at[pl.ds((blk0 + s) * blk_rows, blk_rows), :],
                                     sem.at[s])

    def wait_out(count):
        for s in range(ITEM_BLOCKS):
            @pl.when(s < count)
            def _(s=s):
                out_copy(s).wait()

    @pl.when(nb > 0)
    def _():
        wgb_ref[...] = wg_ref[...].astype(BF16)
        wub_ref[...] = wu_ref[...].astype(BF16)
        wdb_ref[...] = wd_ref[...].astype(BF16)

    @pl.when((c == last_c) & (w > 0))
    def _():
        wait_out(item_nb_ref[jnp.maximum(w - 1, 0)])

    def run(n_live):
        rows = n_live * MOE_BLOCK

        @pl.when(c == 0)
        def _():
            for s in range(n_live):
                for j in range(SLAB_ROWS):
                    lo, hi = _unpack_pair(x_refs[s][_slab_rows(j, MOE_BLOCK), :])
                    r0 = s * MOE_BLOCK
                    xb_ref[r0:r0 + MOE_BLOCK, j * LANES:(j + 1) * LANES] = lo.astype(BF16)
                    xb_ref[r0:r0 + MOE_BLOCK, HALF_D + j * LANES:HALF_D + (j + 1) * LANES] = hi.astype(BF16)

        xs = xb_ref[0:rows, :]
        g = jnp.dot(xs, wgb_ref[...], preferred_element_type=F32)
        u = jnp.dot(xs, wub_ref[...], preferred_element_type=F32)
        hid = (g * jax.nn.sigmoid(g) * u).astype(BF16)
        y = jnp.dot(hid, wdb_ref[...], preferred_element_type=F32)

        @pl.when(c == 0)
        def _():
            acc_ref[0:rows, :] = y

        @pl.when((c > 0) & (c < last_c))
        def _():
            acc_ref[0:rows, :] += y

        @pl.when(c == last_c)
        def _():
            for s in range(n_live):
                r0 = s * MOE_BLOCK
                for j in range(SLAB_ROWS):
                    lo_cols = slice(j * LANES, (j + 1) * LANES)
                    hi_cols = slice(HALF_D + j * LANES, HALF_D + (j + 1) * LANES)
                    yp_ref[s, _slab_rows(j, MOE_BLOCK), :] = _pack_pair(
                        acc_ref[r0:r0 + MOE_BLOCK, lo_cols] + y[r0:r0 + MOE_BLOCK, lo_cols],
                        acc_ref[r0:r0 + MOE_BLOCK, hi_cols] + y[r0:r0 + MOE_BLOCK, hi_cols])
                out_copy(s).start()

    for n_live in range(1, ITEM_BLOCKS + 1):
        pl.when(nb == n_live)(functools.partial(run, n_live))

    @pl.when((c == last_c) & (w == pl.num_programs(0) - 1))
    def _():
        wait_out(nb)


def _experts(item_e, item_blk, item_nb, x_sorted, w_gate, w_up, w_down, n_blocks):
    n_items = item_e.shape[0]
    d = D_MODEL
    fc = EXPERT_FC
    x3 = x_sorted
    blk_rows = MOE_BLOCK * SLAB_ROWS
    n_chunks = EXPERT_DIM // fc
    assert n_chunks >= 2, "the last hidden chunk adds onto the accumulator of the earlier ones"

    def x_spec(s):
        return pl.BlockSpec((blk_rows, SLAB_LANES),
                            lambda w, c, ie, ib, inb: (jnp.minimum(ib[w] + s, n_blocks - 1), 0))

    def chunk(w, c, inb):
        return jnp.where(inb[w] > 0, c, n_chunks - 1)

    return pl.pallas_call(
        _experts_kernel,
        grid_spec=pltpu.PrefetchScalarGridSpec(
            num_scalar_prefetch=3,
            grid=(n_items, n_chunks),
            in_specs=[x_spec(0), x_spec(1), x_spec(2), x_spec(3),
                      pl.BlockSpec((None, d, fc), lambda w, c, ie, ib, inb: (ie[w], 0, chunk(w, c, inb))),
                      pl.BlockSpec((None, d, fc), lambda w, c, ie, ib, inb: (ie[w], 0, chunk(w, c, inb))),
                      pl.BlockSpec((None, fc, d), lambda w, c, ie, ib, inb: (ie[w], chunk(w, c, inb), 0))],
            out_specs=pl.BlockSpec(memory_space=pl.ANY),
            scratch_shapes=[pltpu.VMEM((ITEM_BLOCKS * MOE_BLOCK, d), BF16),
                            pltpu.VMEM((d, fc), BF16),
                            pltpu.VMEM((d, fc), BF16),
                            pltpu.VMEM((fc, d), BF16),
                            pltpu.VMEM((ITEM_BLOCKS * MOE_BLOCK, d), F32),
                            pltpu.VMEM((ITEM_BLOCKS, blk_rows, SLAB_LANES), I32),
                            pltpu.SemaphoreType.DMA((ITEM_BLOCKS,))]),
        out_shape=jax.ShapeDtypeStruct((n_blocks * blk_rows, SLAB_LANES), I32),
        compiler_params=_params(("arbitrary", "arbitrary"), 56),
        name="experts",
    )(item_e, item_blk, item_nb, x3, x3, x3, x3, w_gate, w_up, w_down)


def _combine_kernel(dest_ref, x1_ref, route_ref, mod_ref, gain_ref, y_ref, o_ref, ybuf_ref, sem):
    tm = COMBINE_TM
    i = pl.program_id(0)
    n_tiles = pl.num_programs(0)

    def row_copy(slot, k, r, src):
        return pltpu.make_async_copy(y_ref.at[pl.ds(src * SLAB_ROWS, SLAB_ROWS), :],
                                     ybuf_ref.at[slot, k, pl.ds(r * SLAB_ROWS, SLAB_ROWS), :],
                                     sem.at[slot])

    def issue_tile(tile, slot):
        base = tile * tm * TOP_K

        def issue(r, carry):
            for k in range(TOP_K):
                row_copy(slot, k, r, dest_ref[base + r * TOP_K + k]).start()
            return carry

        lax.fori_loop(0, tm, issue, 0, unroll=4)

    @pl.when(i == 0)
    def _():
        issue_tile(0, 0)

    @pl.when(i + 1 < n_tiles)
    def _():
        issue_tile(i + 1, (i + 1) % 2)

    slot = i % 2
    for k in range(TOP_K):
        pltpu.make_async_copy(y_ref.at[pl.ds(0, tm * SLAB_ROWS), :], ybuf_ref.at[slot, k],
                              sem.at[slot]).wait()

    route = route_ref[...]
    w0 = route[:, 2:3]
    w1 = route[:, 3:4]
    ssq = jnp.zeros((tm, 1), F32)
    for j in range(SLAB_ROWS):
        rows = _slab_rows(j, tm)
        y0 = _unpack_pair(ybuf_ref[slot, 0, rows, :])
        y1 = _unpack_pair(ybuf_ref[slot, 1, rows, :])
        for part, off in ((0, j * LANES), (1, HALF_D + j * LANES)):
            cols = slice(off, off + LANES)
            ffn = w0 * y0[part] + w1 * y1[part]
            x2 = x1_ref[:, cols] + mod_ref[0, 5:6, cols] * ffn
            o_ref[:, cols] = x2
            ssq = ssq + jnp.sum(x2 * x2, axis=-1, keepdims=True)
    o_ref[...] = o_ref[...] * lax.rsqrt(ssq * (1.0 / D_MODEL) + EPS) * gain_ref[...]


def _combine(dest, x1, route, mod6, gain, y_sorted, seq):
    t, d = x1.shape
    tm = COMBINE_TM
    tiles_per_batch = seq // tm
    return pl.pallas_call(
        _combine_kernel,
        grid_spec=pltpu.PrefetchScalarGridSpec(
            num_scalar_prefetch=1,
            grid=(t // tm,),
            in_specs=[pl.BlockSpec((tm, d), lambda i, dst: (i, 0)),
                      pl.BlockSpec((tm, LANES), lambda i, dst: (i, 0)),
                      pl.BlockSpec((1, 6, d), lambda i, dst: (i // tiles_per_batch, 0, 0)),
                      pl.BlockSpec((1, d), lambda i, dst: (0, 0)),
                      pl.BlockSpec(memory_space=pl.ANY)],
            out_specs=pl.BlockSpec((tm, d), lambda i, dst: (i, 0)),
            scratch_shapes=[pltpu.VMEM((2, TOP_K, tm * SLAB_ROWS, SLAB_LANES), I32),
                            pltpu.SemaphoreType.DMA((2,))]),
        out_shape=jax.ShapeDtypeStruct((t, d), F32),
        compiler_params=_params(("arbitrary",), 24),
        name="combine",
    )(dest, x1, route, mod6, gain, y_sorted)


def _dispatch_tables(route, counts, t):
    n_assign = t * TOP_K
    n_pad = -(-(n_assign + N_EXPERTS * (MOE_BLOCK - 1)) // MOE_BLOCK) * MOE_BLOCK
    n_blocks = n_pad // MOE_BLOCK
    n_items = N_EXPERTS + n_assign // (ITEM_BLOCKS * MOE_BLOCK)

    eid = route[:, :TOP_K].astype(jnp.int32)
    rk = route[:, 4:4 + TOP_K].astype(jnp.int32)
    cnt = counts[0, :N_EXPERTS].astype(jnp.int32)
    blocks_e = (cnt + MOE_BLOCK - 1) // MOE_BLOCK
    blk_end = jnp.cumsum(blocks_e)
    blk_start = blk_end - blocks_e
    hot = eid[:, :, None] == jnp.arange(N_EXPERTS, dtype=jnp.int32)
    row_start = jnp.sum(jnp.where(hot, blk_start * MOE_BLOCK, 0), axis=-1)
    dest = (row_start + rk).reshape(-1)
    zstart = (blk_start * MOE_BLOCK + cnt).astype(jnp.int32)
    zcount = (blocks_e * MOE_BLOCK - cnt).astype(jnp.int32)

    items_e = (blocks_e + ITEM_BLOCKS - 1) // ITEM_BLOCKS
    item_end = jnp.cumsum(items_e)
    item_start = item_end - items_e
    w = jnp.arange(n_items, dtype=jnp.int32)
    live = w < item_end[-1]
    w_live = jnp.minimum(w, item_end[-1] - 1)
    e_w = jnp.minimum(jnp.searchsorted(item_end, w_live, side='right'), N_EXPERTS - 1).astype(jnp.int32)
    j_w = w_live - item_start[e_w]
    item_blk = (blk_start[e_w] + ITEM_BLOCKS * j_w).astype(jnp.int32)
    item_nb = jnp.where(live, jnp.clip(blocks_e[e_w] - ITEM_BLOCKS * j_w, 0, ITEM_BLOCKS), 0).astype(jnp.int32)
    return dest.astype(jnp.int32), zstart, zcount, e_w, item_blk, item_nb, n_blocks


def kernel(x, c, positions, norm1_gain, norm2_gain, final_norm_gain, w_ada, b_ada, w_in, attn_sinks,
           ret_norm_gain, w_branch_attn, w_branch_ret, w_out, w_router_group, b_router_group,
           w_router_expert, b_router_expert, w_expert_gate, w_expert_up, w_expert_down):
    batch, seq, d = x.shape
    t = batch * seq
    depth = w_ada.shape[0]
    half = RET_DIM // 2
    inv_freq = (ROPE_BASE ** (-jnp.arange(half, dtype=F32) / half)).reshape(1, half)
    pos = positions.reshape(t, 1)
    c8 = jnp.pad(c, ((0, 8 - batch), (0, 0)))
    xf = x.reshape(t, d)

    assert depth == 1, "the fused final norm assumes a single layer"
    for layer in range(depth):
        mod6 = _ada(c8, w_ada[layer], b_ada[layer].reshape(1, -1))[:batch].reshape(batch, 6, d)
        proj = _proj(xf, norm1_gain[layer].reshape(1, d), mod6, w_in[layer], seq)
        attn = _attn(proj, attn_sinks[layer], batch, seq)
        ret = _ret(proj, pos, inv_freq, ret_norm_gain[layer].reshape(1, d), batch, seq)

        pad = LANES - N_GROUPS - N_EXPERTS
        w_rt = jnp.concatenate([w_router_group[layer], w_router_expert[layer],
                                jnp.zeros((d, pad), F32)], axis=1)
        b_rt = jnp.concatenate([b_router_group[layer], b_router_expert[layer],
                                jnp.zeros((pad,), F32)]).reshape(1, LANES)
        merged = _branch(attn, ret, proj, w_branch_attn[layer], w_branch_ret[layer])
        x1, h2, route, counts = _mixout(merged, xf, mod6, norm2_gain[layer].reshape(1, d),
                                        w_out[layer].astype(BF16), w_rt, b_rt, seq)
        dest, zstart, zcount, item_e, item_blk, item_nb, n_blocks = _dispatch_tables(route, counts, t)
        x_sorted = _dispatch(dest, zstart, zcount, h2, n_blocks * MOE_BLOCK)
        y_sorted = _experts(item_e, item_blk, item_nb, x_sorted,
                            w_expert_gate[layer], w_expert_up[layer], w_expert_down[layer], n_blocks)
        xf = _combine(dest, x1, route, mod6, final_norm_gain.reshape(1, d), y_sorted, seq)
    return xf.reshape(batch, seq, d)
```

```python
import functools
import math

import jax
import jax.numpy as jnp
import numpy as np
from jax import lax
from jax.experimental import pallas as pl
from jax.experimental.pallas import tpu as pltpu

F32 = jnp.float32
BF16 = jnp.bfloat16

D_MODEL = 2048
ATTN_HEAD_DIM = 64
ATTN_HEADS = 32
ATTN_KV_HEADS = 4
ATTN_GROUP = 8
WINDOW = 128
RET_HEADS = 8
RET_DIM = 256
RET_CHUNK = 128
ROPE_BASE = 10000.0
N_GROUPS = 4
EXPERTS_PER_GROUP = 16
N_EXPERTS = 64
TOP_K = 2
EXPERT_DIM = 1024
MOE_BLOCK = 128
EPS = 1e-6
NEG = -1e30

MIB = 1024 * 1024
LANES = 128
PROJ_TN = 512
PROJ_TM = 1024
KV_TILE = 28
BRANCH_TM = 1024
BRANCH_TN = 512
MIXOUT_TM = 512
ITEM_BLOCKS = 4
EXPERT_FC = 512
COMBINE_TM = 128
SLAB_ROWS = 8
SLAB_LANES = LANES
HALF_D = D_MODEL // 2
ROWS_PER_WAIT = 128
I32 = jnp.int32

COL_QA, COL_QR, COL_KR, COL_VR, COL_GR, COL_GA, COL_GRT = 0, 1, 2, 3, 4, 5, 6

LOG_GAMMA = [math.log1p(-(2.0 ** (-5.0 - h))) for h in range(RET_HEADS)]


def _params(sem, vmem_mib):
    return pltpu.CompilerParams(dimension_semantics=sem, vmem_limit_bytes=vmem_mib * MIB)


def _pack_pair(lo, hi):
    lo_b = lax.bitcast_convert_type(lo.astype(BF16).astype(F32), I32)
    hi_b = lax.bitcast_convert_type(hi.astype(BF16).astype(F32), I32)
    return hi_b | lax.shift_right_logical(lo_b, jnp.full_like(lo_b, 16))


def _unpack_pair(w):
    lo = lax.bitcast_convert_type(w << 16, F32)
    hi = lax.bitcast_convert_type(w & jnp.int32(-65536), F32)
    return lo, hi


def _slab_rows(j, n_tokens):
    return pl.ds(j, n_tokens, stride=SLAB_ROWS)


def _ada_kernel(c_ref, w_ref, b_ref, o_ref):
    c = c_ref[...]
    a = (c * jax.nn.sigmoid(c)).astype(BF16)
    o_ref[...] = jnp.dot(a, w_ref[...].astype(BF16), preferred_element_type=F32) + b_ref[...]


def _ada(c8, w_ada, b_ada):
    n = w_ada.shape[1]
    tn = 1024
    return pl.pallas_call(
        _ada_kernel,
        grid=(n // tn,),
        in_specs=[pl.BlockSpec((8, D_MODEL), lambda j: (0, 0)),
                  pl.BlockSpec((D_MODEL, tn), lambda j: (0, j)),
                  pl.BlockSpec((1, tn), lambda j: (0, j))],
        out_specs=pl.BlockSpec((8, tn), lambda j: (0, j)),
        out_shape=jax.ShapeDtypeStruct((8, n), F32),
        compiler_params=_params(("arbitrary",), 40),
        name="ada",
    )(c8, w_ada, b_ada)


def _proj_kernel(x_ref, g_ref, mod_ref, w_ref, o_ref, h_ref):
    @pl.when(pl.program_id(1) == 0)
    def _():
        x = x_ref[...]
        var = jnp.mean(x * x, axis=-1, keepdims=True)
        y = x * lax.rsqrt(var + EPS) * g_ref[...]
        h_ref[...] = (y * (1.0 + mod_ref[0, 1:2, :]) + mod_ref[0, 0:1, :]).astype(BF16)

    o_ref[...] = jnp.dot(h_ref[...], w_ref[...].astype(BF16),
                         preferred_element_type=F32).astype(BF16)


def _proj_out_tile(j):
    return jnp.where(j < 4, j, jnp.where(j == 4, KV_TILE, j - 1))


def _proj(x2, gain, mod6, w_in, seq):
    t = x2.shape[0]
    n = w_in.shape[1]
    tiles_per_batch = seq // PROJ_TM
    return pl.pallas_call(
        _proj_kernel,
        grid=(t // PROJ_TM, n // PROJ_TN),
        in_specs=[pl.BlockSpec((PROJ_TM, D_MODEL), lambda i, j: (i, 0)),
                  pl.BlockSpec((1, D_MODEL), lambda i, j: (0, 0)),
                  pl.BlockSpec((1, 6, D_MODEL), lambda i, j: (i // tiles_per_batch, 0, 0)),
                  pl.BlockSpec((D_MODEL, PROJ_TN), lambda i, j: (0, j))],
        out_specs=pl.BlockSpec((PROJ_TM, PROJ_TN), lambda i, j: (i, _proj_out_tile(j))),
        out_shape=jax.ShapeDtypeStruct((t, n), BF16),
        scratch_shapes=[pltpu.VMEM((PROJ_TM, D_MODEL), BF16)],
        compiler_params=_params(("arbitrary", "arbitrary"), 48),
        name="proj",
    )(x2, gain, mod6, w_in)


def _attn_kernel(sink_ref, q_ref, kvp_ref, kvc_ref, o_ref):
    n = pl.program_id(1)
    kvp = kvp_ref[...]
    kvc = kvc_ref[...]
    qi = lax.broadcasted_iota(jnp.int32, (WINDOW, WINDOW), 0)
    sj = lax.broadcasted_iota(jnp.int32, (WINDOW, WINDOW), 1)
    valid_prev = (sj > qi) & (n > 0)
    valid_cur = sj <= qi
    sink_col = sj == 0
    first_row = lax.broadcasted_iota(jnp.int32, (2 * WINDOW, 1), 0) == 0
    dh = ATTN_HEAD_DIM
    kv_w = ATTN_KV_HEADS * dh
    scale = jnp.asarray(dh ** -0.5, BF16)
    for kv in range(ATTN_KV_HEADS):
        kband = jnp.concatenate([kvp[:, kv * dh:(kv + 1) * dh],
                                 kvc[:, kv * dh:(kv + 1) * dh]], axis=0) * scale
        vband = jnp.concatenate([kvp[:, kv_w + kv * dh:kv_w + (kv + 1) * dh],
                                 kvc[:, kv_w + kv * dh:kv_w + (kv + 1) * dh]], axis=0)
        vband = jnp.where(first_row, jnp.zeros_like(vband), vband)
        def scores(g):
            h = kv * ATTN_GROUP + g
            return lax.dot_general(q_ref[:, h * dh:(h + 1) * dh], kband, (((1,), (1,)), ((), ())),
                                   preferred_element_type=F32)

        outs = []
        s_next = scores(0)
        for g in range(ATTN_GROUP):
            s = s_next
            if g + 1 < ATTN_GROUP:
                s_next = scores(g + 1)
            sink = sink_ref[kv * ATTN_GROUP + g]
            s_prev = jnp.where(sink_col, sink, jnp.where(valid_prev, s[:, :WINDOW], NEG))
            s_cur = jnp.where(valid_cur, s[:, WINDOW:], NEG)
            m = jnp.max(jnp.maximum(s_prev, s_cur), axis=-1, keepdims=True)
            p_prev = jnp.exp(s_prev - m)
            p_cur = jnp.exp(s_cur - m)
            denom = jnp.sum(p_prev + p_cur, axis=-1, keepdims=True)
            p = jnp.concatenate([p_prev, p_cur], axis=-1).astype(BF16)
            o = jnp.dot(p, vband, preferred_element_type=F32)
            outs.append(o * (1.0 / denom))
        o_ref[:, kv * ATTN_GROUP * dh:(kv + 1) * ATTN_GROUP * dh] = (
            jnp.concatenate(outs, axis=-1).astype(BF16))


def _attn(proj, sinks, batch, seq):
    nb = seq // WINDOW
    t = batch * seq
    kv_blk = KV_TILE

    def cur(b, n):
        return (b * nb + n, kv_blk)

    def prev(b, n):
        return (b * nb + jnp.maximum(n - 1, 0), kv_blk)

    return pl.pallas_call(
        _attn_kernel,
        grid=(batch, nb),
        in_specs=[pl.BlockSpec(memory_space=pltpu.SMEM),
                  pl.BlockSpec((WINDOW, D_MODEL), lambda b, n: (b * nb + n, COL_QA)),
                  pl.BlockSpec((WINDOW, PROJ_TN), prev),
                  pl.BlockSpec((WINDOW, PROJ_TN), cur)],
        out_specs=pl.BlockSpec((WINDOW, D_MODEL), lambda b, n: (b * nb + n, 0)),
        out_shape=jax.ShapeDtypeStruct((t, D_MODEL), BF16),
        compiler_params=_params(("arbitrary", "arbitrary"), 32),
        name="attn",
    )(sinks, proj, proj, proj)


def _ret_kernel(pos_ref, invf_ref, q_ref, k_ref, v_ref, g_ref, gain_ref, o_ref, state_ref):
    @pl.when(pl.program_id(1) == 0)
    def _():
        state_ref[...] = jnp.zeros_like(state_ref)

    half = RET_DIM // 2
    ang = pos_ref[...].astype(F32) * invf_ref[...]
    cos = jnp.cos(ang)
    sin = jnp.sin(ang)
    idx = lax.broadcasted_iota(jnp.int32, (RET_CHUNK, 1), 0).astype(F32)
    ii = lax.broadcasted_iota(jnp.int32, (RET_CHUNK, RET_CHUNK), 0)
    jj = lax.broadcasted_iota(jnp.int32, (RET_CHUNK, RET_CHUNK), 1)
    diff = (ii - jj).astype(F32)

    def rot(t):
        t1, t2 = t[:, :half], t[:, half:]
        return jnp.concatenate([t1 * cos - t2 * sin, t1 * sin + t2 * cos], axis=-1)

    for h in range(RET_HEADS):
        lg = LOG_GAMMA[h]
        sl = slice(h * RET_DIM, (h + 1) * RET_DIM)
        qr = rot(q_ref[:, sl].astype(F32))
        kr = rot(k_ref[:, sl].astype(F32)) * (RET_DIM ** -0.5)
        vb = v_ref[:, sl]
        d_intra = jnp.where(diff >= 0, jnp.exp(jnp.maximum(diff, 0.0) * lg), 0.0)
        d_q = jnp.exp((idx + 1.0) * lg)
        d_k = jnp.exp((RET_CHUNK - 1.0 - idx) * lg)
        d_chunk = math.exp(RET_CHUNK * lg)
        qb = qr.astype(BF16)
        kb = kr.astype(BF16)
        intra = lax.dot_general(qb, kb, (((1,), (1,)), ((), ())),
                                preferred_element_type=F32) * d_intra
        st = state_ref[h]
        o = (jnp.dot(intra.astype(BF16), vb, preferred_element_type=F32)
             + jnp.dot(qb, st.astype(BF16), preferred_element_type=F32) * d_q)
        kd = (kr * d_k).astype(BF16)
        state_ref[h] = st * d_chunk + lax.dot_general(kd, vb, (((0,), (0,)), ((), ())),
                                                      preferred_element_type=F32)
        o = o * lax.rsqrt(jnp.mean(o * o, axis=-1, keepdims=True) + EPS) * gain_ref[:, sl]
        gg = g_ref[:, sl].astype(F32)
        o_ref[:, sl] = (gg * jax.nn.sigmoid(gg) * o).astype(BF16)


def _ret(proj, pos, inv_freq, ret_gain, batch, seq):
    nc = seq // RET_CHUNK
    t = batch * seq

    def col(cb):
        return lambda b, c: (b * nc + c, cb)

    return pl.pallas_call(
        _ret_kernel,
        grid=(batch, nc),
        in_specs=[pl.BlockSpec((RET_CHUNK, 1), lambda b, c: (b * nc + c, 0)),
                  pl.BlockSpec((1, RET_DIM // 2), lambda b, c: (0, 0)),
                  pl.BlockSpec((RET_CHUNK, D_MODEL), col(COL_QR)),
                  pl.BlockSpec((RET_CHUNK, D_MODEL), col(COL_KR)),
                  pl.BlockSpec((RET_CHUNK, D_MODEL), col(COL_VR)),
                  pl.BlockSpec((RET_CHUNK, D_MODEL), col(COL_GR)),
                  pl.BlockSpec((1, D_MODEL), lambda b, c: (0, 0))],
        out_specs=pl.BlockSpec((RET_CHUNK, D_MODEL), lambda b, c: (b * nc + c, 0)),
        out_shape=jax.ShapeDtypeStruct((t, D_MODEL), BF16),
        scratch_shapes=[pltpu.VMEM((RET_HEADS, RET_DIM, RET_DIM), F32)],
        compiler_params=_params(("arbitrary", "arbitrary"), 32),
        name="ret",
    )(pos, inv_freq, proj, proj, proj, proj, ret_gain)


def _route(logits):
    lane = lax.broadcasted_iota(jnp.int32, logits.shape, 1)
    lane_f = lane.astype(F32)
    is_g = lane < N_GROUPS
    gl = jnp.where(is_g, logits, NEG)
    gmax = jnp.max(gl, axis=-1, keepdims=True)
    gsel = jnp.min(jnp.where(gl == gmax, lane_f, float(LANES)), axis=-1, keepdims=True)
    gsum = jnp.sum(jnp.where(is_g, jnp.exp(gl - gmax), 0.0), axis=-1, keepdims=True)
    g_w = 1.0 / gsum
    grp = ((lane - N_GROUPS) >> 4).astype(F32)
    is_e = (lane >= N_GROUPS) & (lane < N_GROUPS + N_EXPERTS) & (grp == gsel)
    el = jnp.where(is_e, logits, NEG)
    v1 = jnp.max(el, axis=-1, keepdims=True)
    i1 = jnp.min(jnp.where(el == v1, lane_f, float(LANES)), axis=-1, keepdims=True)
    el2 = jnp.where(lane_f == i1, NEG, el)
    v2 = jnp.max(el2, axis=-1, keepdims=True)
    i2 = jnp.min(jnp.where(el2 == v2, lane_f, float(LANES)), axis=-1, keepdims=True)
    tt = jnp.exp(v2 - v1)
    w1 = g_w / (1.0 + tt)
    w2 = g_w * tt / (1.0 + tt)
    return jnp.where(lane == 0, i1 - N_GROUPS,
                     jnp.where(lane == 1, i2 - N_GROUPS,
                               jnp.where(lane == 2, w1, jnp.where(lane == 3, w2, 0.0))))


def _branch_kernel(attn_ref, ret_ref, ga_ref, gr_ref, wa_ref, wr_ref, o_ref):
    a = jnp.dot(attn_ref[...], wa_ref[...].astype(BF16), preferred_element_type=F32)
    r = jnp.dot(ret_ref[...], wr_ref[...].astype(BF16), preferred_element_type=F32)
    o_ref[...] = (jax.nn.sigmoid(ga_ref[...].astype(F32)) * a
                  + jax.nn.sigmoid(gr_ref[...].astype(F32)) * r).astype(BF16)


def _branch(attn, ret, proj, wa, wr):
    t = attn.shape[0]
    tm, tn = BRANCH_TM, BRANCH_TN
    per_slab = D_MODEL // tn
    return pl.pallas_call(
        _branch_kernel,
        grid=(t // tm, D_MODEL // tn),
        in_specs=[pl.BlockSpec((tm, D_MODEL), lambda i, j: (i, 0)),
                  pl.BlockSpec((tm, D_MODEL), lambda i, j: (i, 0)),
                  pl.BlockSpec((tm, tn), lambda i, j: (i, COL_GA * per_slab + j)),
                  pl.BlockSpec((tm, tn), lambda i, j: (i, COL_GRT * per_slab + j)),
                  pl.BlockSpec((D_MODEL, tn), lambda i, j: (0, j)),
                  pl.BlockSpec((D_MODEL, tn), lambda i, j: (0, j))],
        out_specs=pl.BlockSpec((tm, tn), lambda i, j: (i, j)),
        out_shape=jax.ShapeDtypeStruct((t, D_MODEL), BF16),
        compiler_params=_params(("arbitrary", "arbitrary"), 48),
        name="branch",
    )(attn, ret, proj, proj, wa, wr)


def _mixout_kernel(m_ref, x_ref, mod_ref, g2_ref, wo_ref, wrt_ref, brt_ref,
                   x1_ref, h2_ref, route_ref, count_ref, carry_ref):
    @pl.when(pl.program_id(0) == 0)
    def _():
        carry_ref[...] = jnp.zeros_like(carry_ref)

    mix = jnp.dot(m_ref[...], wo_ref[...], preferred_element_type=F32)
    x1 = x_ref[...] + mod_ref[0, 2:3, :] * mix
    x1_ref[...] = x1
    var = jnp.mean(x1 * x1, axis=-1, keepdims=True)
    h2 = x1 * lax.rsqrt(var + EPS) * g2_ref[...]
    h2 = h2 * (1.0 + mod_ref[0, 4:5, :]) + mod_ref[0, 3:4, :]
    tm = h2.shape[0]
    for j in range(SLAB_ROWS):
        lo = h2[:, j * LANES:(j + 1) * LANES]
        hi = h2[:, HALF_D + j * LANES:HALF_D + (j + 1) * LANES]
        h2_ref[_slab_rows(j, tm), :] = _pack_pair(lo, hi)
    h_hi = h2.astype(BF16)
    h_lo = (h2 - h_hi.astype(F32)).astype(BF16)
    w_rt = wrt_ref[...]
    w_hi = w_rt.astype(BF16)
    w_lo = (w_rt - w_hi.astype(F32)).astype(BF16)
    hi_both = jnp.dot(h_hi, jnp.concatenate([w_hi, w_lo], axis=1), preferred_element_type=F32)
    logits = (hi_both[:, :LANES] + hi_both[:, LANES:]
              + jnp.dot(h_lo, w_hi, preferred_element_type=F32) + brt_ref[...])
    route = _route(logits)

    lane = lax.broadcasted_iota(jnp.int32, route.shape, 1)
    lane_f = lane.astype(F32)
    hot1 = lane_f == route[:, 0:1]
    hot2 = lane_f == route[:, 1:2]
    both = jnp.where(hot1 | hot2, 1.0, 0.0)
    ii = lax.broadcasted_iota(jnp.int32, (tm, tm), 0)
    jj = lax.broadcasted_iota(jnp.int32, (tm, tm), 1)
    lower = jnp.where(ii > jj, 1.0, 0.0).astype(BF16)
    before = jnp.dot(lower, both.astype(BF16), preferred_element_type=F32) + carry_ref[...]
    r1 = jnp.sum(jnp.where(hot1, before, 0.0), axis=-1, keepdims=True)
    r2 = jnp.sum(jnp.where(hot2, before, 0.0), axis=-1, keepdims=True)
    route_ref[...] = jnp.where(lane == 4, r1, jnp.where(lane == 5, r2, route))
    carry = carry_ref[...] + jnp.sum(both, axis=0, keepdims=True)
    carry_ref[...] = carry
    count_ref[...] = carry


def _mixout(merged, x2, mod6, gain2, wo, w_rt, b_rt, seq):
    t = x2.shape[0]
    tm = MIXOUT_TM
    tiles_per_batch = seq // tm
    row = lambda i: (i, 0)
    const = lambda i: (0, 0)
    return pl.pallas_call(
        _mixout_kernel,
        grid=(t // tm,),
        in_specs=[pl.BlockSpec((tm, D_MODEL), row),
                  pl.BlockSpec((tm, D_MODEL), row),
                  pl.BlockSpec((1, 6, D_MODEL), lambda i: (i // tiles_per_batch, 0, 0)),
                  pl.BlockSpec((1, D_MODEL), const),
                  pl.BlockSpec((D_MODEL, D_MODEL), const, pipeline_mode=pl.Buffered(1)),
                  pl.BlockSpec((D_MODEL, LANES), const),
                  pl.BlockSpec((1, LANES), const)],
        out_specs=[pl.BlockSpec((tm, D_MODEL), row),
                   pl.BlockSpec((tm * SLAB_ROWS, SLAB_LANES), row),
                   pl.BlockSpec((tm, LANES), row),
                   pl.BlockSpec((1, LANES), const)],
        out_shape=[jax.ShapeDtypeStruct((t, D_MODEL), F32),
                   jax.ShapeDtypeStruct((t * SLAB_ROWS, SLAB_LANES), I32),
                   jax.ShapeDtypeStruct((t, LANES), F32),
                   jax.ShapeDtypeStruct((1, LANES), F32)],
        scratch_shapes=[pltpu.VMEM((1, LANES), F32)],
        compiler_params=_params(("arbitrary",), 56),
        name="mixout",
    )(merged, x2, mod6, gain2, wo, w_rt, b_rt)


PAD_BITS = (64, 32, 16, 8, 4, 2, 1)


def _dispatch_kernel(dest_ref, zstart_ref, zcount_ref, h2_ref, xs_ref, zero_ref, sem, zsem):
    n_assign = dest_ref.shape[0]
    zero_ref[...] = jnp.zeros_like(zero_ref)

    def zero_copy(start, rows):
        return pltpu.make_async_copy(zero_ref.at[pl.ds(0, rows * SLAB_ROWS), :],
                                     xs_ref.at[pl.ds(start * SLAB_ROWS, rows * SLAB_ROWS), :], zsem)

    def fill(e, wait):
        start = zstart_ref[e]
        pad = zcount_ref[e]
        for bit in PAD_BITS:
            @pl.when((pad & bit) != 0)
            def _(start=start, bit=bit):
                cp = zero_copy(start, bit)
                cp.wait() if wait else cp.start()
            start = start + (pad & bit)

    lax.fori_loop(0, N_EXPERTS, lambda e, c: (fill(e, False), c)[1], 0)

    def issue(tok, carry):
        src = h2_ref.at[pl.ds(pl.multiple_of(tok * SLAB_ROWS, SLAB_ROWS), SLAB_ROWS), :]
        for k in range(TOP_K):
            dst = pl.multiple_of(dest_ref[tok * TOP_K + k] * SLAB_ROWS, SLAB_ROWS)
            pltpu.make_async_copy(src, xs_ref.at[pl.ds(dst, SLAB_ROWS), :], sem).start()
        return carry

    lax.fori_loop(0, n_assign // TOP_K, issue, 0, unroll=8)

    def drain(i, carry):
        pltpu.make_async_copy(h2_ref.at[pl.ds(0, ROWS_PER_WAIT * SLAB_ROWS), :],
                              xs_ref.at[pl.ds(0, ROWS_PER_WAIT * SLAB_ROWS), :], sem).wait()
        return carry

    lax.fori_loop(0, n_assign // ROWS_PER_WAIT, drain, 0)
    lax.fori_loop(0, N_EXPERTS, lambda e, c: (fill(e, True), c)[1], 0)


def _dispatch(dest, zstart, zcount, h2_slab, n_pad):
    return pl.pallas_call(
        _dispatch_kernel,
        grid_spec=pltpu.PrefetchScalarGridSpec(
            num_scalar_prefetch=3,
            grid=(1,),
            in_specs=[pl.BlockSpec(memory_space=pl.ANY)],
            out_specs=pl.BlockSpec(memory_space=pl.ANY),
            scratch_shapes=[pltpu.VMEM((PAD_BITS[0] * SLAB_ROWS, SLAB_LANES), I32),
                            pltpu.SemaphoreType.DMA(()),
                            pltpu.SemaphoreType.DMA(())]),
        out_shape=jax.ShapeDtypeStruct((n_pad * SLAB_ROWS, SLAB_LANES), I32),
        compiler_params=_params(("arbitrary",), 16),
        name="dispatch",
    )(dest, zstart, zcount, h2_slab)


def _experts_kernel(item_e_ref, item_blk_ref, item_nb_ref,
                    x0_ref, x1_ref, x2_ref, x3_ref, wg_ref, wu_ref, wd_ref, y_ref,
                    xb_ref, wgb_ref, wub_ref, wdb_ref, acc_ref, yp_ref, sem):
    w = pl.program_id(0)
    c = pl.program_id(1)
    last_c = pl.num_programs(1) - 1
    nb = item_nb_ref[w]
    blk0 = item_blk_ref[w]
    x_refs = (x0_ref, x1_ref, x2_ref, x3_ref)

    blk_rows = MOE_BLOCK * SLAB_ROWS

    def out_copy(s):
        return pltpu.make_async_copy(yp_ref.at[s], y_ref.at[pl.ds((blk0 + s) * blk_rows, blk_rows), :],
                                     sem.at[s])

    def wait_out(count):
        for s in range(ITEM_BLOCKS):
            @pl.when(s < count)
            def _(s=s):
                out_copy(s).wait()

    @pl.when(nb > 0)
    def _():
        wgb_ref[...] = wg_ref[...].astype(BF16)
        wub_ref[...] = wu_ref[...].astype(BF16)
        wdb_ref[...] = wd_ref[...].astype(BF16)

    @pl.when((c == last_c) & (w > 0))
    def _():
        wait_out(item_nb_ref[jnp.maximum(w - 1, 0)])

    def run(n_live):
        rows = n_live * MOE_BLOCK

        @pl.when(c == 0)
        def _():
            for s in range(n_live):
                for j in range(SLAB_ROWS):
                    lo, hi = _unpack_pair(x_refs[s][_slab_rows(j, MOE_BLOCK), :])
                    r0 = s * MOE_BLOCK
                    xb_ref[r0:r0 + MOE_BLOCK, j * LANES:(j + 1) * LANES] = lo.astype(BF16)
                    xb_ref[r0:r0 + MOE_BLOCK, HALF_D + j * LANES:HALF_D + (j + 1) * LANES] = hi.astype(BF16)

        xs = xb_ref[0:rows, :]
        g = jnp.dot(xs, wgb_ref[...], preferred_element_type=F32)
        u = jnp.dot(xs, wub_ref[...], preferred_element_type=F32)
        hid = (g * jax.nn.sigmoid(g) * u).astype(BF16)
        y = jnp.dot(hid, wdb_ref[...], preferred_element_type=F32)

        @pl.when(c == 0)
        def _():
            acc_ref[0:rows, :] = y

        @pl.when((c > 0) & (c < last_c))
        def _():
            acc_ref[0:rows, :] += y

        @pl.when(c == last_c)
        def _():
            for s in range(n_live):
                r0 = s * MOE_BLOCK
                for j in range(SLAB_ROWS):
                    lo_cols = slice(j * LANES, (j + 1) * LANES)
                    hi_cols = slice(HALF_D + j * LANES, HALF_D + (j + 1) * LANES)
                    yp_ref[s, _slab_rows(j, MOE_BLOCK), :] = _pack_pair(
                        acc_ref[r0:r0 + MOE_BLOCK, lo_cols] + y[r0:r0 + MOE_BLOCK, lo_cols],
                        acc_ref[r0:r0 + MOE_BLOCK, hi_cols] + y[r0:r0 + MOE_BLOCK, hi_cols])
                out_copy(s).start()

    for n_live in range(1, ITEM_BLOCKS + 1):
        pl.when(nb == n_live)(functools.partial(run, n_live))

    @pl.when((c == last_c) & (w == pl.num_programs(0) - 1))
    def _():
        wait_out(nb)


def _experts(item_e, item_blk, item_nb, x_sorted, w_gate, w_up, w_down, n_blocks):
    n_items = item_e.shape[0]
    d = D_MODEL
    fc = EXPERT_FC
    x3 = x_sorted
    blk_rows = MOE_BLOCK * SLAB_ROWS
    n_chunks = EXPERT_DIM // fc
    assert n_chunks >= 2, "the last hidden chunk adds onto the accumulator of the earlier ones"

    def x_spec(s):
        return pl.BlockSpec((blk_rows, SLAB_LANES),
                            lambda w, c, ie, ib, inb: (jnp.minimum(ib[w] + s, n_blocks - 1), 0))

    def chunk(w, c, inb):
        return jnp.where(inb[w] > 0, c, n_chunks - 1)

    return pl.pallas_call(
        _experts_kernel,
        grid_spec=pltpu.PrefetchScalarGridSpec(
            num_scalar_prefetch=3,
            grid=(n_items, n_chunks),
            in_specs=[x_spec(0), x_spec(1), x_spec(2), x_spec(3),
                      pl.BlockSpec((None, d, fc), lambda w, c, ie, ib, inb: (ie[w], 0, chunk(w, c, inb))),
                      pl.BlockSpec((None, d, fc), lambda w, c, ie, ib, inb: (ie[w], 0, chunk(w, c, inb))),
                      pl.BlockSpec((None, fc, d), lambda w, c, ie, ib, inb: (ie[w], chunk(w, c, inb), 0))],
            out_specs=pl.BlockSpec(memory_space=pl.ANY),
            scratch_shapes=[pltpu.VMEM((ITEM_BLOCKS * MOE_BLOCK, d), BF16),
                            pltpu.VMEM((d, fc), BF16),
                            pltpu.VMEM((d, fc), BF16),
                            pltpu.VMEM((fc, d), BF16),
                            pltpu.VMEM((ITEM_BLOCKS * MOE_BLOCK, d), F32),
                            pltpu.VMEM((ITEM_BLOCKS, blk_rows, SLAB_LANES), I32),
                            pltpu.SemaphoreType.DMA((ITEM_BLOCKS,))]),
        out_shape=jax.ShapeDtypeStruct((n_blocks * blk_rows, SLAB_LANES), I32),
        compiler_params=_params(("arbitrary", "arbitrary"), 56),
        name="experts",
    )(item_e, item_blk, item_nb, x3, x3, x3, x3, w_gate, w_up, w_down)


def _combine_kernel(dest_ref, x1_ref, route_ref, mod_ref, gain_ref, y_ref, o_ref, ybuf_ref, sem):
    tm = COMBINE_TM
    i = pl.program_id(0)
    n_tiles = pl.num_programs(0)

    def row_copy(slot, k, r, src):
        return pltpu.make_async_copy(y_ref.at[pl.ds(src * SLAB_ROWS, SLAB_ROWS), :],
                                     ybuf_ref.at[slot, k, pl.ds(r * SLAB_ROWS, SLAB_ROWS), :],
                                     sem.at[slot])

    def wait_slot(slot):
        for k in range(TOP_K):
            pltpu.make_async_copy(y_ref.at[pl.ds(0, tm * SLAB_ROWS), :], ybuf_ref.at[slot, k],
                                  sem.at[slot]).wait()

    @pl.when(i == 0)
    def _():
        def issue(r, carry):
            for k in range(TOP_K):
                row_copy(0, k, r, dest_ref[r * TOP_K + k]).start()
            return carry

        lax.fori_loop(0, tm, issue, 0, unroll=4)

    slot = i % 2
    nxt = 1 - slot
    nxt_base = jnp.minimum(i + 1, n_tiles - 1) * (tm * TOP_K)
    wait_slot(slot)

    route = route_ref[...]
    w0 = route[:, 2:3]
    w1 = route[:, 3:4]
    ssq = jnp.zeros((tm, 1), F32)
    rows_per_chunk = tm // (2 * SLAB_ROWS)
    chunk = 0
    for j in range(SLAB_ROWS):
        rows = _slab_rows(j, tm)
        y0 = _unpack_pair(ybuf_ref[slot, 0, rows, :])
        y1 = _unpack_pair(ybuf_ref[slot, 1, rows, :])
        for part, off in ((0, j * LANES), (1, HALF_D + j * LANES)):
            cols = slice(off, off + LANES)
            ffn = w0 * y0[part] + w1 * y1[part]
            x2 = x1_ref[:, cols] + mod_ref[0, 5:6, cols] * ffn
            o_ref[:, cols] = x2
            ssq = ssq + jnp.sum(x2 * x2, axis=-1, keepdims=True)
            for r in range(chunk * rows_per_chunk, (chunk + 1) * rows_per_chunk):
                for k in range(TOP_K):
                    row_copy(nxt, k, r, dest_ref[nxt_base + r * TOP_K + k]).start()
            chunk += 1
    o_ref[...] = o_ref[...] * lax.rsqrt(ssq * (1.0 / D_MODEL) + EPS) * gain_ref[...]

    @pl.when(i == n_tiles - 1)
    def _():
        wait_slot(nxt)


def _combine(dest, x1, route, mod6, gain, y_sorted, seq):
    t, d = x1.shape
    tm = COMBINE_TM
    tiles_per_batch = seq // tm
    return pl.pallas_call(
        _combine_kernel,
        grid_spec=pltpu.PrefetchScalarGridSpec(
            num_scalar_prefetch=1,
            grid=(t // tm,),
            in_specs=[pl.BlockSpec((tm, d), lambda i, dst: (i, 0)),
                      pl.BlockSpec((tm, LANES), lambda i, dst: (i, 0)),
                      pl.BlockSpec((1, 6, d), lambda i, dst: (i // tiles_per_batch, 0, 0)),
                      pl.BlockSpec((1, d), lambda i, dst: (0, 0)),
                      pl.BlockSpec(memory_space=pl.ANY)],
            out_specs=pl.BlockSpec((tm, d), lambda i, dst: (i, 0)),
            scratch_shapes=[pltpu.VMEM((2, TOP_K, tm * SLAB_ROWS, SLAB_LANES), I32),
                            pltpu.SemaphoreType.DMA((2,))]),
        out_shape=jax.ShapeDtypeStruct((t, d), F32),
        compiler_params=_params(("arbitrary",), 24),
        name="combine",
    )(dest, x1, route, mod6, gain, y_sorted)


def _dispatch_tables(route, counts, t):
    n_assign = t * TOP_K
    n_pad = -(-(n_assign + N_EXPERTS * (MOE_BLOCK - 1)) // MOE_BLOCK) * MOE_BLOCK
    n_blocks = n_pad // MOE_BLOCK
    n_items = N_EXPERTS + n_assign // (ITEM_BLOCKS * MOE_BLOCK)

    eid = route[:, :TOP_K].astype(jnp.int32)
    rk = route[:, 4:4 + TOP_K].astype(jnp.int32)
    cnt = counts[0, :N_EXPERTS].astype(jnp.int32)
    blocks_e = (cnt + MOE_BLOCK - 1) // MOE_BLOCK
    blk_end = jnp.cumsum(blocks_e)
    blk_start = blk_end - blocks_e
    hot = eid[:, :, None] == jnp.arange(N_EXPERTS, dtype=jnp.int32)
    row_start = jnp.sum(jnp.where(hot, blk_start * MOE_BLOCK, 0), axis=-1)
    dest = (row_start + rk).reshape(-1)
    zstart = (blk_start * MOE_BLOCK + cnt).astype(jnp.int32)
    zcount = (blocks_e * MOE_BLOCK - cnt).astype(jnp.int32)

    items_e = (blocks_e + ITEM_BLOCKS - 1) // ITEM_BLOCKS
    item_end = jnp.cumsum(items_e)
    item_start = item_end - items_e
    w = jnp.arange(n_items, dtype=jnp.int32)
    live = w < item_end[-1]
    w_live = jnp.minimum(w, item_end[-1] - 1)
    e_w = jnp.minimum(jnp.searchsorted(item_end, w_live, side='right'), N_EXPERTS - 1).astype(jnp.int32)
    j_w = w_live - item_start[e_w]
    item_blk = (blk_start[e_w] + ITEM_BLOCKS * j_w).astype(jnp.int32)
    item_nb = jnp.where(live, jnp.clip(blocks_e[e_w] - ITEM_BLOCKS * j_w, 0, ITEM_BLOCKS), 0).astype(jnp.int32)
    return dest.astype(jnp.int32), zstart, zcount, e_w, item_blk, item_nb, n_blocks


def kernel(x, c, positions, norm1_gain, norm2_gain, final_norm_gain, w_ada, b_ada, w_in, attn_sinks,
           ret_norm_gain, w_branch_attn, w_branch_ret, w_out, w_router_group, b_router_group,
           w_router_expert, b_router_expert, w_expert_gate, w_expert_up, w_expert_down):
    batch, seq, d = x.shape
    t = batch * seq
    depth = w_ada.shape[0]
    half = RET_DIM // 2
    inv_freq = (ROPE_BASE ** (-jnp.arange(half, dtype=F32) / half)).reshape(1, half)
    pos = positions.reshape(t, 1)
    c8 = jnp.pad(c, ((0, 8 - batch), (0, 0)))
    xf = x.reshape(t, d)

    assert depth == 1, "the fused final norm assumes a single layer"
    for layer in range(depth):
        mod6 = _ada(c8, w_ada[layer], b_ada[layer].reshape(1, -1))[:batch].reshape(batch, 6, d)
        proj = _proj(xf, norm1_gain[layer].reshape(1, d), mod6, w_in[layer], seq)
        attn = _attn(proj, attn_sinks[layer], batch, seq)
        ret = _ret(proj, pos, inv_freq, ret_norm_gain[layer].reshape(1, d), batch, seq)

        pad = LANES - N_GROUPS - N_EXPERTS
        w_rt = jnp.concatenate([w_router_group[layer], w_router_expert[layer],
                                jnp.zeros((d, pad), F32)], axis=1)
        b_rt = jnp.concatenate([b_router_group[layer], b_router_expert[layer],
                                jnp.zeros((pad,), F32)]).reshape(1, LANES)
        merged = _branch(attn, ret, proj, w_branch_attn[layer], w_branch_ret[layer])
        x1, h2, route, counts = _mixout(merged, xf, mod6, norm2_gain[layer].reshape(1, d),
                                        w_out[layer].astype(BF16), w_rt, b_rt, seq)
        dest, zstart, zcount, item_e, item_blk, item_nb, n_blocks = _dispatch_tables(route, counts, t)
        x_sorted = _dispatch(dest, zstart, zcount, h2, n_blocks * MOE_BLOCK)
        y_sorted = _experts(item_e, item_blk, item_nb, x_sorted,
                            w_expert_gate[layer], w_expert_up[layer], w_expert_down[layer], n_blocks)
        xf = _combine(dest, x1, route, mod6, final_norm_gain.reshape(1, d), y_sorted, seq)
    return xf.reshape(batch, seq, d)
```

```python
import functools
import math

import jax
import jax.numpy as jnp
import numpy as np
from jax import lax
from jax.experimental import pallas as pl
from jax.experimental.pallas import tpu as pltpu

F32 = jnp.float32
BF16 = jnp.bfloat16

D_MODEL = 2048
ATTN_HEAD_DIM = 64
ATTN_HEADS = 32
ATTN_KV_HEADS = 4
ATTN_GROUP = 8
WINDOW = 128
RET_HEADS = 8
RET_DIM = 256
RET_CHUNK = 128
ROPE_BASE = 10000.0
N_GROUPS = 4
EXPERTS_PER_GROUP = 16
N_EXPERTS = 64
TOP_K = 2
EXPERT_DIM = 1024
MOE_BLOCK = 128
EPS = 1e-6
NEG = -1e30

MIB = 1024 * 1024
LANES = 128
PROJ_TN = 512
PROJ_TM = 1024
KV_TILE = 28
BRANCH_TM = 1024
BRANCH_TN = 512
MIXOUT_TM = 512
ITEM_BLOCKS = 4
EXPERT_FC = 512
COMBINE_TM = 128
SLAB_ROWS = 8
SLAB_LANES = LANES
HALF_D = D_MODEL // 2
ROWS_PER_WAIT = 128
I32 = jnp.int32

COL_QA, COL_QR, COL_KR, COL_VR, COL_GR, COL_GA, COL_GRT = 0, 1, 2, 3, 4, 5, 6

LOG_GAMMA = [math.log1p(-(2.0 ** (-5.0 - h))) for h in range(RET_HEADS)]


def _params(sem, vmem_mib):
    return pltpu.CompilerParams(dimension_semantics=sem, vmem_limit_bytes=vmem_mib * MIB)


def _pack_pair(lo, hi):
    lo_b = lax.bitcast_convert_type(lo.astype(BF16).astype(F32), I32)
    hi_b = lax.bitcast_convert_type(hi.astype(BF16).astype(F32), I32)
    return hi_b | lax.shift_right_logical(lo_b, jnp.full_like(lo_b, 16))


def _unpack_pair(w):
    lo = lax.bitcast_convert_type(w << 16, F32)
    hi = lax.bitcast_convert_type(w & jnp.int32(-65536), F32)
    return lo, hi


def _slab_rows(j, n_tokens):
    return pl.ds(j, n_tokens, stride=SLAB_ROWS)


def _ada_kernel(c_ref, w_ref, b_ref, o_ref):
    c = c_ref[...]
    a = (c * jax.nn.sigmoid(c)).astype(BF16)
    o_ref[...] = jnp.dot(a, w_ref[...].astype(BF16), preferred_element_type=F32) + b_ref[...]


def _ada(c8, w_ada, b_ada):
    n = w_ada.shape[1]
    tn = 1024
    return pl.pallas_call(
        _ada_kernel,
        grid=(n // tn,),
        in_specs=[pl.BlockSpec((8, D_MODEL), lambda j: (0, 0)),
                  pl.BlockSpec((D_MODEL, tn), lambda j: (0, j)),
                  pl.BlockSpec((1, tn), lambda j: (0, j))],
        out_specs=pl.BlockSpec((8, tn), lambda j: (0, j)),
        out_shape=jax.ShapeDtypeStruct((8, n), F32),
        compiler_params=_params(("arbitrary",), 40),
        name="ada",
    )(c8, w_ada, b_ada)


def _proj_kernel(x_ref, g_ref, mod_ref, w_ref, o_ref, h_ref):
    @pl.when(pl.program_id(1) == 0)
    def _():
        x = x_ref[...]
        var = jnp.mean(x * x, axis=-1, keepdims=True)
        y = x * lax.rsqrt(var + EPS) * g_ref[...]
        h_ref[...] = (y * (1.0 + mod_ref[0, 1:2, :]) + mod_ref[0, 0:1, :]).astype(BF16)

    o_ref[...] = jnp.dot(h_ref[...], w_ref[...].astype(BF16),
                         preferred_element_type=F32).astype(BF16)


def _proj_out_tile(j):
    return jnp.where(j < 4, j, jnp.where(j == 4, KV_TILE, j - 1))


def _proj(x2, gain, mod6, w_in, seq):
    t = x2.shape[0]
    n = w_in.shape[1]
    tiles_per_batch = seq // PROJ_TM
    return pl.pallas_call(
        _proj_kernel,
        grid=(t // PROJ_TM, n // PROJ_TN),
        in_specs=[pl.BlockSpec((PROJ_TM, D_MODEL), lambda i, j: (i, 0)),
                  pl.BlockSpec((1, D_MODEL), lambda i, j: (0, 0)),
                  pl.BlockSpec((1, 6, D_MODEL), lambda i, j: (i // tiles_per_batch, 0, 0)),
                  pl.BlockSpec((D_MODEL, PROJ_TN), lambda i, j: (0, j))],
        out_specs=pl.BlockSpec((PROJ_TM, PROJ_TN), lambda i, j: (i, _proj_out_tile(j))),
        out_shape=jax.ShapeDtypeStruct((t, n), BF16),
        scratch_shapes=[pltpu.VMEM((PROJ_TM, D_MODEL), BF16)],
        compiler_params=_params(("arbitrary", "arbitrary"), 48),
        name="proj",
    )(x2, gain, mod6, w_in)


def _attn_kernel(sink_ref, q_ref, kvp_ref, kvc_ref, o_ref):
    n = pl.program_id(1)
    kvp = kvp_ref[...]
    kvc = kvc_ref[...]
    qi = lax.broadcasted_iota(jnp.int32, (WINDOW, WINDOW), 0)
    sj = lax.broadcasted_iota(jnp.int32, (WINDOW, WINDOW), 1)
    valid_prev = (sj > qi) & (n > 0)
    valid_cur = sj <= qi
    sink_col = sj == 0
    first_row = lax.broadcasted_iota(jnp.int32, (2 * WINDOW, 1), 0) == 0
    dh = ATTN_HEAD_DIM
    kv_w = ATTN_KV_HEADS * dh
    scale = jnp.asarray(dh ** -0.5, BF16)
    for kv in range(ATTN_KV_HEADS):
        kband = jnp.concatenate([kvp[:, kv * dh:(kv + 1) * dh],
                                 kvc[:, kv * dh:(kv + 1) * dh]], axis=0) * scale
        vband = jnp.concatenate([kvp[:, kv_w + kv * dh:kv_w + (kv + 1) * dh],
                                 kvc[:, kv_w + kv * dh:kv_w + (kv + 1) * dh]], axis=0)
        vband = jnp.where(first_row, jnp.zeros_like(vband), vband)
        zeros = jnp.zeros_like(kband)
        ones = jnp.ones_like(vband)
        k_pad = (jnp.concatenate([kband, zeros], axis=1), jnp.concatenate([zeros, kband], axis=1))
        pv_rhs = jnp.concatenate(
            [jnp.concatenate([vband, zeros, ones, zeros], axis=1),
             jnp.concatenate([zeros, vband, zeros, ones], axis=1)], axis=0)
        for pair in range(ATTN_GROUP // 2):
            h0 = kv * ATTN_GROUP + 2 * pair
            q_pair = q_ref[:, h0 * dh:(h0 + 2) * dh]
            probs = []
            for idx in range(2):
                s = lax.dot_general(q_pair, k_pad[idx], (((1,), (1,)), ((), ())),
                                    preferred_element_type=F32)
                sink = sink_ref[h0 + idx]
                s_prev = jnp.where(sink_col, sink, jnp.where(valid_prev, s[:, :WINDOW], NEG))
                s_cur = jnp.where(valid_cur, s[:, WINDOW:], NEG)
                m = jnp.max(jnp.maximum(s_prev, s_cur), axis=-1, keepdims=True)
                probs += [jnp.exp(s_prev - m).astype(BF16), jnp.exp(s_cur - m).astype(BF16)]
            r = jnp.dot(jnp.concatenate(probs, axis=-1), pv_rhs, preferred_element_type=F32)
            o_ref[:, h0 * dh:(h0 + 2) * dh] = (r[:, :2 * dh] * (1.0 / r[:, 2 * dh:])).astype(BF16)


def _attn(proj, sinks, batch, seq):
    nb = seq // WINDOW
    t = batch * seq
    kv_blk = KV_TILE

    def cur(b, n):
        return (b * nb + n, kv_blk)

    def prev(b, n):
        return (b * nb + jnp.maximum(n - 1, 0), kv_blk)

    return pl.pallas_call(
        _attn_kernel,
        grid=(batch, nb),
        in_specs=[pl.BlockSpec(memory_space=pltpu.SMEM),
                  pl.BlockSpec((WINDOW, D_MODEL), lambda b, n: (b * nb + n, COL_QA)),
                  pl.BlockSpec((WINDOW, PROJ_TN), prev),
                  pl.BlockSpec((WINDOW, PROJ_TN), cur)],
        out_specs=pl.BlockSpec((WINDOW, D_MODEL), lambda b, n: (b * nb + n, 0)),
        out_shape=jax.ShapeDtypeStruct((t, D_MODEL), BF16),
        compiler_params=_params(("arbitrary", "arbitrary"), 32),
        name="attn",
    )(sinks, proj, proj, proj)


def _ret_kernel(pos_ref, invf_ref, q_ref, k_ref, v_ref, g_ref, gain_ref, o_ref, state_ref):
    @pl.when(pl.program_id(1) == 0)
    def _():
        state_ref[...] = jnp.zeros_like(state_ref)

    half = RET_DIM // 2
    ang = pos_ref[...].astype(F32) * invf_ref[...]
    cos = jnp.cos(ang)
    sin = jnp.sin(ang)
    idx = lax.broadcasted_iota(jnp.int32, (RET_CHUNK, 1), 0).astype(F32)
    ii = lax.broadcasted_iota(jnp.int32, (RET_CHUNK, RET_CHUNK), 0)
    jj = lax.broadcasted_iota(jnp.int32, (RET_CHUNK, RET_CHUNK), 1)
    diff = (ii - jj).astype(F32)

    def rot(t):
        t1, t2 = t[:, :half], t[:, half:]
        return jnp.concatenate([t1 * cos - t2 * sin, t1 * sin + t2 * cos], axis=-1)

    for h in range(RET_HEADS):
        lg = LOG_GAMMA[h]
        sl = slice(h * RET_DIM, (h + 1) * RET_DIM)
        qr = rot(q_ref[:, sl].astype(F32))
        kr = rot(k_ref[:, sl].astype(F32)) * (RET_DIM ** -0.5)
        vb = v_ref[:, sl]
        d_intra = jnp.where(diff >= 0, jnp.exp(jnp.maximum(diff, 0.0) * lg), 0.0)
        d_q = jnp.exp((idx + 1.0) * lg)
        d_k = jnp.exp((RET_CHUNK - 1.0 - idx) * lg)
        d_chunk = math.exp(RET_CHUNK * lg)
        qb = qr.astype(BF16)
        kb = kr.astype(BF16)
        intra = lax.dot_general(qb, kb, (((1,), (1,)), ((), ())),
                                preferred_element_type=F32) * d_intra
        st = state_ref[h]
        o = (jnp.dot(intra.astype(BF16), vb, preferred_element_type=F32)
             + jnp.dot(qb, st.astype(BF16), preferred_element_type=F32) * d_q)
        kd = (kr * d_k).astype(BF16)
        state_ref[h] = st * d_chunk + lax.dot_general(kd, vb, (((0,), (0,)), ((), ())),
                                                      preferred_element_type=F32)
        o = o * lax.rsqrt(jnp.mean(o * o, axis=-1, keepdims=True) + EPS) * gain_ref[:, sl]
        gg = g_ref[:, sl].astype(F32)
        o_ref[:, sl] = (gg * jax.nn.sigmoid(gg) * o).astype(BF16)


def _ret(proj, pos, inv_freq, ret_gain, batch, seq):
    nc = seq // RET_CHUNK
    t = batch * seq

    def col(cb):
        return lambda b, c: (b * nc + c, cb)

    return pl.pallas_call(
        _ret_kernel,
        grid=(batch, nc),
        in_specs=[pl.BlockSpec((RET_CHUNK, 1), lambda b, c: (b * nc + c, 0)),
                  pl.BlockSpec((1, RET_DIM // 2), lambda b, c: (0, 0)),
                  pl.BlockSpec((RET_CHUNK, D_MODEL), col(COL_QR)),
                  pl.BlockSpec((RET_CHUNK, D_MODEL), col(COL_KR)),
                  pl.BlockSpec((RET_CHUNK, D_MODEL), col(COL_VR)),
                  pl.BlockSpec((RET_CHUNK, D_MODEL), col(COL_GR)),
                  pl.BlockSpec((1, D_MODEL), lambda b, c: (0, 0))],
        out_specs=pl.BlockSpec((RET_CHUNK, D_MODEL), lambda b, c: (b * nc + c, 0)),
        out_shape=jax.ShapeDtypeStruct((t, D_MODEL), BF16),
        scratch_shapes=[pltpu.VMEM((RET_HEADS, RET_DIM, RET_DIM), F32)],
        compiler_params=_params(("arbitrary", "arbitrary"), 32),
        name="ret",
    )(pos, inv_freq, proj, proj, proj, proj, ret_gain)


def _route(logits):
    lane = lax.broadcasted_iota(jnp.int32, logits.shape, 1)
    lane_f = lane.astype(F32)
    is_g = lane < N_GROUPS
    gl = jnp.where(is_g, logits, NEG)
    gmax = jnp.max(gl, axis=-1, keepdims=True)
    gsel = jnp.min(jnp.where(gl == gmax, lane_f, float(LANES)), axis=-1, keepdims=True)
    gsum = jnp.sum(jnp.where(is_g, jnp.exp(gl - gmax), 0.0), axis=-1, keepdims=True)
    g_w = 1.0 / gsum
    grp = ((lane - N_GROUPS) >> 4).astype(F32)
    is_e = (lane >= N_GROUPS) & (lane < N_GROUPS + N_EXPERTS) & (grp == gsel)
    el = jnp.where(is_e, logits, NEG)
    v1 = jnp.max(el, axis=-1, keepdims=True)
    i1 = jnp.min(jnp.where(el == v1, lane_f, float(LANES)), axis=-1, keepdims=True)
    el2 = jnp.where(lane_f == i1, NEG, el)
    v2 = jnp.max(el2, axis=-1, keepdims=True)
    i2 = jnp.min(jnp.where(el2 == v2, lane_f, float(LANES)), axis=-1, keepdims=True)
    tt = jnp.exp(v2 - v1)
    w1 = g_w / (1.0 + tt)
    w2 = g_w * tt / (1.0 + tt)
    return jnp.where(lane == 0, i1 - N_GROUPS,
                     jnp.where(lane == 1, i2 - N_GROUPS,
                               jnp.where(lane == 2, w1, jnp.where(lane == 3, w2, 0.0))))


def _branch_kernel(attn_ref, ret_ref, ga_ref, gr_ref, wa_ref, wr_ref, o_ref):
    a = jnp.dot(attn_ref[...], wa_ref[...].astype(BF16), preferred_element_type=F32)
    r = jnp.dot(ret_ref[...], wr_ref[...].astype(BF16), preferred_element_type=F32)
    o_ref[...] = (jax.nn.sigmoid(ga_ref[...].astype(F32)) * a
                  + jax.nn.sigmoid(gr_ref[...].astype(F32)) * r).astype(BF16)


def _branch(attn, ret, proj, wa, wr):
    t = attn.shape[0]
    tm, tn = BRANCH_TM, BRANCH_TN
    per_slab = D_MODEL // tn
    return pl.pallas_call(
        _branch_kernel,
        grid=(t // tm, D_MODEL // tn),
        in_specs=[pl.BlockSpec((tm, D_MODEL), lambda i, j: (i, 0)),
                  pl.BlockSpec((tm, D_MODEL), lambda i, j: (i, 0)),
                  pl.BlockSpec((tm, tn), lambda i, j: (i, COL_GA * per_slab + j)),
                  pl.BlockSpec((tm, tn), lambda i, j: (i, COL_GRT * per_slab + j)),
                  pl.BlockSpec((D_MODEL, tn), lambda i, j: (0, j)),
                  pl.BlockSpec((D_MODEL, tn), lambda i, j: (0, j))],
        out_specs=pl.BlockSpec((tm, tn), lambda i, j: (i, j)),
        out_shape=jax.ShapeDtypeStruct((t, D_MODEL), BF16),
        compiler_params=_params(("arbitrary", "arbitrary"), 48),
        name="branch",
    )(attn, ret, proj, proj, wa, wr)


def _mixout_kernel(m_ref, x_ref, mod_ref, g2_ref, wo_ref, wrt_ref, brt_ref,
                   x1_ref, h2_ref, route_ref, count_ref, carry_ref):
    @pl.when(pl.program_id(0) == 0)
    def _():
        carry_ref[...] = jnp.zeros_like(carry_ref)

    mix = jnp.dot(m_ref[...], wo_ref[...], preferred_element_type=F32)
    x1 = x_ref[...] + mod_ref[0, 2:3, :] * mix
    x1_ref[...] = x1
    var = jnp.mean(x1 * x1, axis=-1, keepdims=True)
    h2 = x1 * lax.rsqrt(var + EPS) * g2_ref[...]
    h2 = h2 * (1.0 + mod_ref[0, 4:5, :]) + mod_ref[0, 3:4, :]
    tm = h2.shape[0]
    for j in range(SLAB_ROWS):
        lo = h2[:, j * LANES:(j + 1) * LANES]
        hi = h2[:, HALF_D + j * LANES:HALF_D + (j + 1) * LANES]
        h2_ref[_slab_rows(j, tm), :] = _pack_pair(lo, hi)
    h_hi = h2.astype(BF16)
    h_lo = (h2 - h_hi.astype(F32)).astype(BF16)
    w_rt = wrt_ref[...]
    w_hi = w_rt.astype(BF16)
    w_lo = (w_rt - w_hi.astype(F32)).astype(BF16)
    hi_both = jnp.dot(h_hi, jnp.concatenate([w_hi, w_lo], axis=1), preferred_element_type=F32)
    logits = (hi_both[:, :LANES] + hi_both[:, LANES:]
              + jnp.dot(h_lo, w_hi, preferred_element_type=F32) + brt_ref[...])
    route = _route(logits)

    lane = lax.broadcasted_iota(jnp.int32, route.shape, 1)
    lane_f = lane.astype(F32)
    hot1 = lane_f == route[:, 0:1]
    hot2 = lane_f == route[:, 1:2]
    both = jnp.where(hot1 | hot2, 1.0, 0.0)
    ii = lax.broadcasted_iota(jnp.int32, (tm, tm), 0)
    jj = lax.broadcasted_iota(jnp.int32, (tm, tm), 1)
    lower = jnp.where(ii > jj, 1.0, 0.0).astype(BF16)
    before = jnp.dot(lower, both.astype(BF16), preferred_element_type=F32) + carry_ref[...]
    r1 = jnp.sum(jnp.where(hot1, before, 0.0), axis=-1, keepdims=True)
    r2 = jnp.sum(jnp.where(hot2, before, 0.0), axis=-1, keepdims=True)
    route_ref[...] = jnp.where(lane == 4, r1, jnp.where(lane == 5, r2, route))
    carry = carry_ref[...] + jnp.sum(both, axis=0, keepdims=True)
    carry_ref[...] = carry
    count_ref[...] = carry


def _mixout(merged, x2, mod6, gain2, wo, w_rt, b_rt, seq):
    t = x2.shape[0]
    tm = MIXOUT_TM
    tiles_per_batch = seq // tm
    row = lambda i: (i, 0)
    const = lambda i: (0, 0)
    return pl.pallas_call(
        _mixout_kernel,
        grid=(t // tm,),
        in_specs=[pl.BlockSpec((tm, D_MODEL), row),
                  pl.BlockSpec((tm, D_MODEL), row),
                  pl.BlockSpec((1, 6, D_MODEL), lambda i: (i // tiles_per_batch, 0, 0)),
                  pl.BlockSpec((1, D_MODEL), const),
                  pl.BlockSpec((D_MODEL, D_MODEL), const, pipeline_mode=pl.Buffered(1)),
                  pl.BlockSpec((D_MODEL, LANES), const),
                  pl.BlockSpec((1, LANES), const)],
        out_specs=[pl.BlockSpec((tm, D_MODEL), row),
                   pl.BlockSpec((tm * SLAB_ROWS, SLAB_LANES), row),
                   pl.BlockSpec((tm, LANES), row),
                   pl.BlockSpec((1, LANES), const)],
        out_shape=[jax.ShapeDtypeStruct((t, D_MODEL), F32),
                   jax.ShapeDtypeStruct((t * SLAB_ROWS, SLAB_LANES), I32),
                   jax.ShapeDtypeStruct((t, LANES), F32),
                   jax.ShapeDtypeStruct((1, LANES), F32)],
        scratch_shapes=[pltpu.VMEM((1, LANES), F32)],
        compiler_params=_params(("arbitrary",), 56),
        name="mixout",
    )(merged, x2, mod6, gain2, wo, w_rt, b_rt)


PAD_BITS = (64, 32, 16, 8, 4, 2, 1)


def _dispatch_kernel(dest_ref, zstart_ref, zcount_ref, h2_ref, xs_ref, zero_ref, sem, zsem):
    n_assign = dest_ref.shape[0]
    zero_ref[...] = jnp.zeros_like(zero_ref)

    def zero_copy(start, rows):
        return pltpu.make_async_copy(zero_ref.at[pl.ds(0, rows * SLAB_ROWS), :],
                                     xs_ref.at[pl.ds(start * SLAB_ROWS, rows * SLAB_ROWS), :], zsem)

    def fill(e, wait):
        start = zstart_ref[e]
        pad = zcount_ref[e]
        for bit in PAD_BITS:
            @pl.when((pad & bit) != 0)
            def _(start=start, bit=bit):
                cp = zero_copy(start, bit)
                cp.wait() if wait else cp.start()
            start = start + (pad & bit)

    lax.fori_loop(0, N_EXPERTS, lambda e, c: (fill(e, False), c)[1], 0)

    def issue(tok, carry):
        src = h2_ref.at[pl.ds(pl.multiple_of(tok * SLAB_ROWS, SLAB_ROWS), SLAB_ROWS), :]
        for k in range(TOP_K):
            dst = pl.multiple_of(dest_ref[tok * TOP_K + k] * SLAB_ROWS, SLAB_ROWS)
            pltpu.make_async_copy(src, xs_ref.at[pl.ds(dst, SLAB_ROWS), :], sem).start()
        return carry

    lax.fori_loop(0, n_assign // TOP_K, issue, 0, unroll=8)

    def drain(i, carry):
        pltpu.make_async_copy(h2_ref.at[pl.ds(0, ROWS_PER_WAIT * SLAB_ROWS), :],
                              xs_ref.at[pl.ds(0, ROWS_PER_WAIT * SLAB_ROWS), :], sem).wait()
        return carry

    lax.fori_loop(0, n_assign // ROWS_PER_WAIT, drain, 0)
    lax.fori_loop(0, N_EXPERTS, lambda e, c: (fill(e, True), c)[1], 0)


def _dispatch(dest, zstart, zcount, h2_slab, n_pad):
    return pl.pallas_call(
        _dispatch_kernel,
        grid_spec=pltpu.PrefetchScalarGridSpec(
            num_scalar_prefetch=3,
            grid=(1,),
            in_specs=[pl.BlockSpec(memory_space=pl.ANY)],
            out_specs=pl.BlockSpec(memory_space=pl.ANY),
            scratch_shapes=[pltpu.VMEM((PAD_BITS[0] * SLAB_ROWS, SLAB_LANES), I32),
                            pltpu.SemaphoreType.DMA(()),
                            pltpu.SemaphoreType.DMA(())]),
        out_shape=jax.ShapeDtypeStruct((n_pad * SLAB_ROWS, SLAB_LANES), I32),
        compiler_params=_params(("arbitrary",), 16),
        name="dispatch",
    )(dest, zstart, zcount, h2_slab)


def _experts_kernel(item_e_ref, item_blk_ref, item_nb_ref,
                    x0_ref, x1_ref, x2_ref, x3_ref, wg_ref, wu_ref, wd_ref, y_ref,
                    xb_ref, acc_ref, yp_ref, sem):
    w = pl.program_id(0)
    c = pl.program_id(1)
    last_c = pl.num_programs(1) - 1
    nb = item_nb_ref[w]
    blk0 = item_blk_ref[w]
    x_refs = (x0_ref, x1_ref, x2_ref, x3_ref)

    blk_rows = MOE_BLOCK * SLAB_ROWS

    def out_copy(s):
        return pltpu.make_async_copy(yp_ref.at[s], y_ref.at[pl.ds((blk0 + s) * blk_rows, blk_rows), :],
                                     sem.at[s])

    def wait_out(count):
        for s in range(ITEM_BLOCKS):
            @pl.when(s < count)
            def _(s=s):
                out_copy(s).wait()

    @pl.when((c == last_c) & (w > 0))
    def _():
        wait_out(item_nb_ref[jnp.maximum(w - 1, 0)])

    def run(n_live):
        rows = n_live * MOE_BLOCK

        @pl.when(c == 0)
        def _():
            for s in range(n_live):
                for j in range(SLAB_ROWS):
                    lo, hi = _unpack_pair(x_refs[s][_slab_rows(j, MOE_BLOCK), :])
                    r0 = s * MOE_BLOCK
                    xb_ref[r0:r0 + MOE_BLOCK, j * LANES:(j + 1) * LANES] = lo.astype(BF16)
                    xb_ref[r0:r0 + MOE_BLOCK, HALF_D + j * LANES:HALF_D + (j + 1) * LANES] = hi.astype(BF16)

        xs = xb_ref[0:rows, :]
        g = jnp.dot(xs, wg_ref[...].astype(BF16), preferred_element_type=F32)
        u = jnp.dot(xs, wu_ref[...].astype(BF16), preferred_element_type=F32)
        hid = (g * jax.nn.sigmoid(g) * u).astype(BF16)
        y = jnp.dot(hid, wd_ref[...].astype(BF16), preferred_element_type=F32)

        @pl.when(c == 0)
        def _():
            acc_ref[0:rows, :] = y

        @pl.when((c > 0) & (c < last_c))
        def _():
            acc_ref[0:rows, :] += y

        @pl.when(c == last_c)
        def _():
            for s in range(n_live):
                r0 = s * MOE_BLOCK
                for j in range(SLAB_ROWS):
                    lo_cols = slice(j * LANES, (j + 1) * LANES)
                    hi_cols = slice(HALF_D + j * LANES, HALF_D + (j + 1) * LANES)
                    yp_ref[s, _slab_rows(j, MOE_BLOCK), :] = _pack_pair(
                        acc_ref[r0:r0 + MOE_BLOCK, lo_cols] + y[r0:r0 + MOE_BLOCK, lo_cols],
                        acc_ref[r0:r0 + MOE_BLOCK, hi_cols] + y[r0:r0 + MOE_BLOCK, hi_cols])
                out_copy(s).start()

    for n_live in range(1, ITEM_BLOCKS + 1):
        pl.when(nb == n_live)(functools.partial(run, n_live))

    @pl.when((c == last_c) & (w == pl.num_programs(0) - 1))
    def _():
        wait_out(nb)


def _experts(item_e, item_blk, item_nb, x_sorted, w_gate, w_up, w_down, n_blocks):
    n_items = item_e.shape[0]
    d = D_MODEL
    fc = EXPERT_FC
    x3 = x_sorted
    blk_rows = MOE_BLOCK * SLAB_ROWS
    n_chunks = EXPERT_DIM // fc
    assert n_chunks >= 2, "the last hidden chunk adds onto the accumulator of the earlier ones"

    def x_spec(s):
        return pl.BlockSpec((blk_rows, SLAB_LANES),
                            lambda w, c, ie, ib, inb: (jnp.minimum(ib[w] + s, n_blocks - 1), 0))

    def chunk(w, c, inb):
        return jnp.where(inb[w] > 0, c, n_chunks - 1)

    return pl.pallas_call(
        _experts_kernel,
        grid_spec=pltpu.PrefetchScalarGridSpec(
            num_scalar_prefetch=3,
            grid=(n_items, n_chunks),
            in_specs=[x_spec(0), x_spec(1), x_spec(2), x_spec(3),
                      pl.BlockSpec((None, d, fc), lambda w, c, ie, ib, inb: (ie[w], 0, chunk(w, c, inb))),
                      pl.BlockSpec((None, d, fc), lambda w, c, ie, ib, inb: (ie[w], 0, chunk(w, c, inb))),
                      pl.BlockSpec((None, fc, d), lambda w, c, ie, ib, inb: (ie[w], chunk(w, c, inb), 0))],
            out_specs=pl.BlockSpec(memory_space=pl.ANY),
            scratch_shapes=[pltpu.VMEM((ITEM_BLOCKS * MOE_BLOCK, d), BF16),
                            pltpu.VMEM((ITEM_BLOCKS * MOE_BLOCK, d), F32),
                            pltpu.VMEM((ITEM_BLOCKS, blk_rows, SLAB_LANES), I32),
                            pltpu.SemaphoreType.DMA((ITEM_BLOCKS,))]),
        out_shape=jax.ShapeDtypeStruct((n_blocks * blk_rows, SLAB_LANES), I32),
        compiler_params=_params(("arbitrary", "arbitrary"), 56),
        name="experts",
    )(item_e, item_blk, item_nb, x3, x3, x3, x3, w_gate, w_up, w_down)


def _combine_kernel(dest_ref, x1_ref, route_ref, mod_ref, gain_ref, y_ref, o_ref, ybuf_ref, sem):
    tm = COMBINE_TM
    i = pl.program_id(0)
    n_tiles = pl.num_programs(0)

    def row_copy(slot, k, r, src):
        return pltpu.make_async_copy(y_ref.at[pl.ds(src * SLAB_ROWS, SLAB_ROWS), :],
                                     ybuf_ref.at[slot, k, pl.ds(r * SLAB_ROWS, SLAB_ROWS), :],
                                     sem.at[slot])

    def wait_slot(slot):
        for k in range(TOP_K):
            pltpu.make_async_copy(y_ref.at[pl.ds(0, tm * SLAB_ROWS), :], ybuf_ref.at[slot, k],
                                  sem.at[slot]).wait()

    def issue_tile(tile, slot):
        base = tile * tm * TOP_K

        def issue(r, carry):
            for k in range(TOP_K):
                row_copy(slot, k, r, dest_ref[base + r * TOP_K + k]).start()
            return carry

        lax.fori_loop(0, tm, issue, 0, unroll=4)

    @pl.when(i == 0)
    def _():
        issue_tile(0, 0)

    @pl.when(i + 1 < n_tiles)
    def _():
        issue_tile(i + 1, (i + 1) % 2)

    slot = i % 2
    wait_slot(slot)

    route = route_ref[...]
    w0 = route[:, 2:3]
    w1 = route[:, 3:4]
    ssq = jnp.zeros((tm, 1), F32)
    for j in range(SLAB_ROWS):
        rows = _slab_rows(j, tm)
        y0 = _unpack_pair(ybuf_ref[slot, 0, rows, :])
        y1 = _unpack_pair(ybuf_ref[slot, 1, rows, :])
        for part, off in ((0, j * LANES), (1, HALF_D + j * LANES)):
            cols = slice(off, off + LANES)
            ffn = w0 * y0[part] + w1 * y1[part]
            x2 = x1_ref[:, cols] + mod_ref[0, 5:6, cols] * ffn
            o_ref[:, cols] = x2
            ssq = ssq + jnp.sum(x2 * x2, axis=-1, keepdims=True)
    o_ref[...] = o_ref[...] * lax.rsqrt(ssq * (1.0 / D_MODEL) + EPS) * gain_ref[...]


def _combine(dest, x1, route, mod6, gain, y_sorted, seq):
    t, d = x1.shape
    tm = COMBINE_TM
    tiles_per_batch = seq // tm
    return pl.pallas_call(
        _combine_kernel,
        grid_spec=pltpu.PrefetchScalarGridSpec(
            num_scalar_prefetch=1,
            grid=(t // tm,),
            in_specs=[pl.BlockSpec((tm, d), lambda i, dst: (i, 0)),
                      pl.BlockSpec((tm, LANES), lambda i, dst: (i, 0)),
                      pl.BlockSpec((1, 6, d), lambda i, dst: (i // tiles_per_batch, 0, 0)),
                      pl.BlockSpec((1, d), lambda i, dst: (0, 0)),
                      pl.BlockSpec(memory_space=pl.ANY)],
            out_specs=pl.BlockSpec((tm, d), lambda i, dst: (i, 0)),
            scratch_shapes=[pltpu.VMEM((2, TOP_K, tm * SLAB_ROWS, SLAB_LANES), I32),
                            pltpu.SemaphoreType.DMA((2,))]),
        out_shape=jax.ShapeDtypeStruct((t, d), F32),
        compiler_params=_params(("arbitrary",), 24),
        name="combine",
    )(dest, x1, route, mod6, gain, y_sorted)


def _dispatch_tables(route, counts, t):
    n_assign = t * TOP_K
    n_pad = -(-(n_assign + N_EXPERTS * (MOE_BLOCK - 1)) // MOE_BLOCK) * MOE_BLOCK
    n_blocks = n_pad // MOE_BLOCK
    n_items = N_EXPERTS + n_assign // (ITEM_BLOCKS * MOE_BLOCK)

    eid = route[:, :TOP_K].astype(jnp.int32)
    rk = route[:, 4:4 + TOP_K].astype(jnp.int32)
    cnt = counts[0, :N_EXPERTS].astype(jnp.int32)
    blocks_e = (cnt + MOE_BLOCK - 1) // MOE_BLOCK
    blk_end = jnp.cumsum(blocks_e)
    blk_start = blk_end - blocks_e
    hot = eid[:, :, None] == jnp.arange(N_EXPERTS, dtype=jnp.int32)
    row_start = jnp.sum(jnp.where(hot, blk_start * MOE_BLOCK, 0), axis=-1)
    dest = (row_start + rk).reshape(-1)
    zstart = (blk_start * MOE_BLOCK + cnt).astype(jnp.int32)
    zcount = (blocks_e * MOE_BLOCK - cnt).astype(jnp.int32)

    items_e = (blocks_e + ITEM_BLOCKS - 1) // ITEM_BLOCKS
    item_end = jnp.cumsum(items_e)
    item_start = item_end - items_e
    w = jnp.arange(n_items, dtype=jnp.int32)
    live = w < item_end[-1]
    w_live = jnp.minimum(w, item_end[-1] - 1)
    e_w = jnp.minimum(jnp.searchsorted(item_end, w_live, side='right'), N_EXPERTS - 1).astype(jnp.int32)
    j_w = w_live - item_start[e_w]
    item_blk = (blk_start[e_w] + ITEM_BLOCKS * j_w).astype(jnp.int32)
    item_nb = jnp.where(live, jnp.clip(blocks_e[e_w] - ITEM_BLOCKS * j_w, 0, ITEM_BLOCKS), 0).astype(jnp.int32)
    return dest.astype(jnp.int32), zstart, zcount, e_w, item_blk, item_nb, n_blocks


def kernel(x, c, positions, norm1_gain, norm2_gain, final_norm_gain, w_ada, b_ada, w_in, attn_sinks,
           ret_norm_gain, w_branch_attn, w_branch_ret, w_out, w_router_group, b_router_group,
           w_router_expert, b_router_expert, w_expert_gate, w_expert_up, w_expert_down):
    batch, seq, d = x.shape
    t = batch * seq
    depth = w_ada.shape[0]
    half = RET_DIM // 2
    inv_freq = (ROPE_BASE ** (-jnp.arange(half, dtype=F32) / half)).reshape(1, half)
    pos = positions.reshape(t, 1)
    c8 = jnp.pad(c, ((0, 8 - batch), (0, 0)))
    xf = x.reshape(t, d)

    assert depth == 1, "the fused final norm assumes a single layer"
    for layer in range(depth):
        mod6 = _ada(c8, w_ada[layer], b_ada[layer].reshape(1, -1))[:batch].reshape(batch, 6, d)
        proj = _proj(xf, norm1_gain[layer].reshape(1, d), mod6, w_in[layer], seq)
        attn = _attn(proj, attn_sinks[layer], batch, seq)
        ret = _ret(proj, pos, inv_freq, ret_norm_gain[layer].reshape(1, d), batch, seq)

        pad = LANES - N_GROUPS - N_EXPERTS
        w_rt = jnp.concatenate([w_router_group[layer], w_router_expert[layer],
                                jnp.zeros((d, pad), F32)], axis=1)
        b_rt = jnp.concatenate([b_router_group[layer], b_router_expert[layer],
                                jnp.zeros((pad,), F32)]).reshape(1, LANES)
        merged = _branch(attn, ret, proj, w_branch_attn[layer], w_branch_ret[layer])
        x1, h2, route, counts = _mixout(merged, xf, mod6, norm2_gain[layer].reshape(1, d),
                                        w_out[layer].astype(BF16), w_rt, b_rt, seq)
        dest, zstart, zcount, item_e, item_blk, item_nb, n_blocks = _dispatch_tables(route, counts, t)
        x_sorted = _dispatch(dest, zstart, zcount, h2, n_blocks * MOE_BLOCK)
        y_sorted = _experts(item_e, item_blk, item_nb, x_sorted,
                            w_expert_gate[layer], w_expert_up[layer], w_expert_down[layer], n_blocks)
        xf = _combine(dest, x1, route, mod6, final_norm_gain.reshape(1, d), y_sorted, seq)
    return xf.reshape(batch, seq, d)
```

```python
import functools
import math

import jax
import jax.numpy as jnp
import numpy as np
from jax import lax
from jax.experimental import pallas as pl
from jax.experimental.pallas import tpu as pltpu

F32 = jnp.float32
BF16 = jnp.bfloat16

D_MODEL = 2048
ATTN_HEAD_DIM = 64
ATTN_HEADS = 32
ATTN_KV_HEADS = 4
ATTN_GROUP = 8
WINDOW = 128
RET_HEADS = 8
RET_DIM = 256
RET_CHUNK = 128
ROPE_BASE = 10000.0
N_GROUPS = 4
EXPERTS_PER_GROUP = 16
N_EXPERTS = 64
TOP_K = 2
EXPERT_DIM = 1024
MOE_BLOCK = 128
EPS = 1e-6
NEG = -1e30

MIB = 1024 * 1024
LANES = 128
PROJ_TN = 512
PROJ_TM = 1024
KV_SRC_TILE = 4
BRANCH_TM = 1024
BRANCH_TN = 512
MIXOUT_TM = 512
ITEM_BLOCKS = 4
EXPERT_FC = 512
COMBINE_TM = 128
SLAB_ROWS = 8
SLAB_LANES = LANES
HALF_D = D_MODEL // 2
ROWS_PER_WAIT = 128
I32 = jnp.int32

COL_QA, COL_QR, COL_KR, COL_VR, COL_GR, COL_GA, COL_GRT = 0, 1, 2, 3, 4, 5, 6

LOG_GAMMA = [math.log1p(-(2.0 ** (-5.0 - h))) for h in range(RET_HEADS)]


def _params(sem, vmem_mib):
    return pltpu.CompilerParams(dimension_semantics=sem, vmem_limit_bytes=vmem_mib * MIB)


def _pack_pair(lo, hi):
    lo_b = lax.bitcast_convert_type(lo.astype(BF16).astype(F32), I32)
    hi_b = lax.bitcast_convert_type(hi.astype(BF16).astype(F32), I32)
    return hi_b | lax.shift_right_logical(lo_b, jnp.full_like(lo_b, 16))


def _unpack_pair(w):
    lo = lax.bitcast_convert_type(w << 16, F32)
    hi = lax.bitcast_convert_type(w & jnp.int32(-65536), F32)
    return lo, hi


def _slab_rows(j, n_tokens):
    return pl.ds(j, n_tokens, stride=SLAB_ROWS)


def _ada_kernel(c_ref, w_ref, b_ref, o_ref):
    c = c_ref[...]
    a = (c * jax.nn.sigmoid(c)).astype(BF16)
    o_ref[...] = jnp.dot(a, w_ref[...].astype(BF16), preferred_element_type=F32) + b_ref[...]


def _ada(c8, w_ada, b_ada):
    n = w_ada.shape[1]
    tn = 1024
    return pl.pallas_call(
        _ada_kernel,
        grid=(n // tn,),
        in_specs=[pl.BlockSpec((8, D_MODEL), lambda j: (0, 0)),
                  pl.BlockSpec((D_MODEL, tn), lambda j: (0, j)),
                  pl.BlockSpec((1, tn), lambda j: (0, j))],
        out_specs=pl.BlockSpec((8, tn), lambda j: (0, j)),
        out_shape=jax.ShapeDtypeStruct((8, n), F32),
        compiler_params=_params(("arbitrary",), 40),
        name="ada",
    )(c8, w_ada, b_ada)


def _proj_kernel(x_ref, g_ref, mod_ref, wlo_ref, whi_ref, o_ref, kv_ref, h_ref):
    v = pl.program_id(1)
    last = pl.num_programs(1) - 1

    @pl.when(v == 0)
    def _():
        x = x_ref[...]
        var = jnp.mean(x * x, axis=-1, keepdims=True)
        y = x * lax.rsqrt(var + EPS) * g_ref[...]
        h_ref[...] = (y * (1.0 + mod_ref[0, 1:2, :]) + mod_ref[0, 0:1, :]).astype(BF16)

    @pl.when(v < last)
    def _():
        w = jnp.concatenate([wlo_ref[...].astype(BF16), whi_ref[...].astype(BF16)], axis=1)
        o_ref[...] = jnp.dot(h_ref[...], w, preferred_element_type=F32).astype(BF16)

    @pl.when(v == last)
    def _():
        kv_ref[...] = jnp.dot(h_ref[...], wlo_ref[...].astype(BF16),
                              preferred_element_type=F32).astype(BF16)


def _proj_w_tile(v, n_wide):
    return jnp.where(v < 2, 2 * v, jnp.where(v < n_wide, 2 * v + 1, KV_SRC_TILE))


def _proj(x2, gain, mod6, w_in, seq):
    t = x2.shape[0]
    n = w_in.shape[1]
    tiles_per_batch = seq // PROJ_TM
    n_wide = (n - PROJ_TN) // (2 * PROJ_TN)
    return pl.pallas_call(
        _proj_kernel,
        grid=(t // PROJ_TM, n_wide + 1),
        in_specs=[pl.BlockSpec((PROJ_TM, D_MODEL), lambda i, v: (i, 0)),
                  pl.BlockSpec((1, D_MODEL), lambda i, v: (0, 0)),
                  pl.BlockSpec((1, 6, D_MODEL), lambda i, v: (i // tiles_per_batch, 0, 0)),
                  pl.BlockSpec((D_MODEL, PROJ_TN), lambda i, v: (0, _proj_w_tile(v, n_wide))),
                  pl.BlockSpec((D_MODEL, PROJ_TN),
                               lambda i, v: (0, jnp.where(v < n_wide, _proj_w_tile(v, n_wide) + 1, KV_SRC_TILE)))],
        out_specs=[pl.BlockSpec((PROJ_TM, 2 * PROJ_TN), lambda i, v: (i, jnp.minimum(v, n_wide - 1))),
                   pl.BlockSpec((PROJ_TM, PROJ_TN), lambda i, v: (i, 0))],
        out_shape=[jax.ShapeDtypeStruct((t, n - PROJ_TN), BF16),
                   jax.ShapeDtypeStruct((t, PROJ_TN), BF16)],
        scratch_shapes=[pltpu.VMEM((PROJ_TM, D_MODEL), BF16)],
        compiler_params=_params(("arbitrary", "arbitrary"), 56),
        name="proj",
    )(x2, gain, mod6, w_in, w_in)


def _attn_kernel(sink_ref, q_ref, kvp_ref, kvc_ref, o_ref):
    n = pl.program_id(1)
    kvp = kvp_ref[...]
    kvc = kvc_ref[...]
    qi = lax.broadcasted_iota(jnp.int32, (WINDOW, WINDOW), 0)
    sj = lax.broadcasted_iota(jnp.int32, (WINDOW, WINDOW), 1)
    valid_prev = (sj > qi) & (n > 0)
    valid_cur = sj <= qi
    sink_col = sj == 0
    first_row = lax.broadcasted_iota(jnp.int32, (2 * WINDOW, 1), 0) == 0
    dh = ATTN_HEAD_DIM
    kv_w = ATTN_KV_HEADS * dh
    scale = jnp.asarray(dh ** -0.5, BF16)
    for kv in range(ATTN_KV_HEADS):
        kband = jnp.concatenate([kvp[:, kv * dh:(kv + 1) * dh],
                                 kvc[:, kv * dh:(kv + 1) * dh]], axis=0) * scale
        vband = jnp.concatenate([kvp[:, kv_w + kv * dh:kv_w + (kv + 1) * dh],
                                 kvc[:, kv_w + kv * dh:kv_w + (kv + 1) * dh]], axis=0)
        vband = jnp.where(first_row, jnp.zeros_like(vband), vband)
        zeros = jnp.zeros_like(kband)
        ones = jnp.ones_like(vband)
        k_pad = (jnp.concatenate([kband, zeros], axis=1), jnp.concatenate([zeros, kband], axis=1))
        pv_rhs = jnp.concatenate(
            [jnp.concatenate([vband, zeros, ones, zeros], axis=1),
             jnp.concatenate([zeros, vband, zeros, ones], axis=1)], axis=0)
        for pair in range(ATTN_GROUP // 2):
            h0 = kv * ATTN_GROUP + 2 * pair
            q_pair = q_ref[:, h0 * dh:(h0 + 2) * dh]
            probs = []
            for idx in range(2):
                s = lax.dot_general(q_pair, k_pad[idx], (((1,), (1,)), ((), ())),
                                    preferred_element_type=F32)
                sink = sink_ref[h0 + idx]
                s_prev = jnp.where(sink_col, sink, jnp.where(valid_prev, s[:, :WINDOW], NEG))
                s_cur = jnp.where(valid_cur, s[:, WINDOW:], NEG)
                m = jnp.max(jnp.maximum(s_prev, s_cur), axis=-1, keepdims=True)
                probs += [jnp.exp(s_prev - m).astype(BF16), jnp.exp(s_cur - m).astype(BF16)]
            r = jnp.dot(jnp.concatenate(probs, axis=-1), pv_rhs, preferred_element_type=F32)
            o_ref[:, h0 * dh:(h0 + 2) * dh] = (r[:, :2 * dh] * (1.0 / r[:, 2 * dh:])).astype(BF16)


def _attn(proj, proj_kv, sinks, batch, seq):
    nb = seq // WINDOW
    t = batch * seq

    def cur(b, n):
        return (b * nb + n, 0)

    def prev(b, n):
        return (b * nb + jnp.maximum(n - 1, 0), 0)

    return pl.pallas_call(
        _attn_kernel,
        grid=(batch, nb),
        in_specs=[pl.BlockSpec(memory_space=pltpu.SMEM),
                  pl.BlockSpec((WINDOW, D_MODEL), lambda b, n: (b * nb + n, COL_QA)),
                  pl.BlockSpec((WINDOW, PROJ_TN), prev),
                  pl.BlockSpec((WINDOW, PROJ_TN), cur)],
        out_specs=pl.BlockSpec((WINDOW, D_MODEL), lambda b, n: (b * nb + n, 0)),
        out_shape=jax.ShapeDtypeStruct((t, D_MODEL), BF16),
        compiler_params=_params(("arbitrary", "arbitrary"), 32),
        name="attn",
    )(sinks, proj, proj_kv, proj_kv)


def _ret_kernel(pos_ref, invf_ref, q_ref, k_ref, v_ref, g_ref, gain_ref, o_ref, state_ref):
    @pl.when(pl.program_id(1) == 0)
    def _():
        state_ref[...] = jnp.zeros_like(state_ref)

    half = RET_DIM // 2
    ang = pos_ref[...].astype(F32) * invf_ref[...]
    cos = jnp.cos(ang)
    sin = jnp.sin(ang)
    idx = lax.broadcasted_iota(jnp.int32, (RET_CHUNK, 1), 0).astype(F32)
    ii = lax.broadcasted_iota(jnp.int32, (RET_CHUNK, RET_CHUNK), 0)
    jj = lax.broadcasted_iota(jnp.int32, (RET_CHUNK, RET_CHUNK), 1)
    diff = (ii - jj).astype(F32)

    def rot(t):
        t1, t2 = t[:, :half], t[:, half:]
        return jnp.concatenate([t1 * cos - t2 * sin, t1 * sin + t2 * cos], axis=-1)

    for h in range(RET_HEADS):
        lg = LOG_GAMMA[h]
        sl = slice(h * RET_DIM, (h + 1) * RET_DIM)
        qr = rot(q_ref[:, sl].astype(F32))
        kr = rot(k_ref[:, sl].astype(F32)) * (RET_DIM ** -0.5)
        vb = v_ref[:, sl]
        d_intra = jnp.where(diff >= 0, jnp.exp(jnp.maximum(diff, 0.0) * lg), 0.0)
        d_q = jnp.exp((idx + 1.0) * lg)
        d_k = jnp.exp((RET_CHUNK - 1.0 - idx) * lg)
        d_chunk = math.exp(RET_CHUNK * lg)
        qb = qr.astype(BF16)
        kb = kr.astype(BF16)
        intra = lax.dot_general(qb, kb, (((1,), (1,)), ((), ())),
                                preferred_element_type=F32) * d_intra
        st = state_ref[h]
        o = (jnp.dot(intra.astype(BF16), vb, preferred_element_type=F32)
             + jnp.dot(qb, st.astype(BF16), preferred_element_type=F32) * d_q)
        kd = (kr * d_k).astype(BF16)
        state_ref[h] = st * d_chunk + lax.dot_general(kd, vb, (((0,), (0,)), ((), ())),
                                                      preferred_element_type=F32)
        o = o * lax.rsqrt(jnp.mean(o * o, axis=-1, keepdims=True) + EPS) * gain_ref[:, sl]
        gg = g_ref[:, sl].astype(F32)
        o_ref[:, sl] = (gg * jax.nn.sigmoid(gg) * o).astype(BF16)


def _ret(proj, pos, inv_freq, ret_gain, batch, seq):
    nc = seq // RET_CHUNK
    t = batch * seq

    def col(cb):
        return lambda b, c: (b * nc + c, cb)

    return pl.pallas_call(
        _ret_kernel,
        grid=(batch, nc),
        in_specs=[pl.BlockSpec((RET_CHUNK, 1), lambda b, c: (b * nc + c, 0)),
                  pl.BlockSpec((1, RET_DIM // 2), lambda b, c: (0, 0)),
                  pl.BlockSpec((RET_CHUNK, D_MODEL), col(COL_QR)),
                  pl.BlockSpec((RET_CHUNK, D_MODEL), col(COL_KR)),
                  pl.BlockSpec((RET_CHUNK, D_MODEL), col(COL_VR)),
                  pl.BlockSpec((RET_CHUNK, D_MODEL), col(COL_GR)),
                  pl.BlockSpec((1, D_MODEL), lambda b, c: (0, 0))],
        out_specs=pl.BlockSpec((RET_CHUNK, D_MODEL), lambda b, c: (b * nc + c, 0)),
        out_shape=jax.ShapeDtypeStruct((t, D_MODEL), BF16),
        scratch_shapes=[pltpu.VMEM((RET_HEADS, RET_DIM, RET_DIM), F32)],
        compiler_params=_params(("arbitrary", "arbitrary"), 32),
        name="ret",
    )(pos, inv_freq, proj, proj, proj, proj, ret_gain)


def _route(logits):
    lane = lax.broadcasted_iota(jnp.int32, logits.shape, 1)
    lane_f = lane.astype(F32)
    is_g = lane < N_GROUPS
    gl = jnp.where(is_g, logits, NEG)
    gmax = jnp.max(gl, axis=-1, keepdims=True)
    gsel = jnp.min(jnp.where(gl == gmax, lane_f, float(LANES)), axis=-1, keepdims=True)
    gsum = jnp.sum(jnp.where(is_g, jnp.exp(gl - gmax), 0.0), axis=-1, keepdims=True)
    g_w = 1.0 / gsum
    grp = ((lane - N_GROUPS) >> 4).astype(F32)
    is_e = (lane >= N_GROUPS) & (lane < N_GROUPS + N_EXPERTS) & (grp == gsel)
    el = jnp.where(is_e, logits, NEG)
    v1 = jnp.max(el, axis=-1, keepdims=True)
    i1 = jnp.min(jnp.where(el == v1, lane_f, float(LANES)), axis=-1, keepdims=True)
    el2 = jnp.where(lane_f == i1, NEG, el)
    v2 = jnp.max(el2, axis=-1, keepdims=True)
    i2 = jnp.min(jnp.where(el2 == v2, lane_f, float(LANES)), axis=-1, keepdims=True)
    tt = jnp.exp(v2 - v1)
    w1 = g_w / (1.0 + tt)
    w2 = g_w * tt / (1.0 + tt)
    return jnp.where(lane == 0, i1 - N_GROUPS,
                     jnp.where(lane == 1, i2 - N_GROUPS,
                               jnp.where(lane == 2, w1, jnp.where(lane == 3, w2, 0.0))))


def _branch_kernel(attn_ref, ret_ref, ga_ref, gr_ref, wa_ref, wr_ref, o_ref):
    a = jnp.dot(attn_ref[...], wa_ref[...].astype(BF16), preferred_element_type=F32)
    r = jnp.dot(ret_ref[...], wr_ref[...].astype(BF16), preferred_element_type=F32)
    o_ref[...] = (jax.nn.sigmoid(ga_ref[...].astype(F32)) * a
                  + jax.nn.sigmoid(gr_ref[...].astype(F32)) * r).astype(BF16)


def _branch(attn, ret, proj, wa, wr):
    t = attn.shape[0]
    tm, tn = BRANCH_TM, BRANCH_TN
    per_slab = D_MODEL // tn
    return pl.pallas_call(
        _branch_kernel,
        grid=(t // tm, D_MODEL // tn),
        in_specs=[pl.BlockSpec((tm, D_MODEL), lambda i, j: (i, 0)),
                  pl.BlockSpec((tm, D_MODEL), lambda i, j: (i, 0)),
                  pl.BlockSpec((tm, tn), lambda i, j: (i, COL_GA * per_slab + j)),
                  pl.BlockSpec((tm, tn), lambda i, j: (i, COL_GRT * per_slab + j)),
                  pl.BlockSpec((D_MODEL, tn), lambda i, j: (0, j)),
                  pl.BlockSpec((D_MODEL, tn), lambda i, j: (0, j))],
        out_specs=pl.BlockSpec((tm, tn), lambda i, j: (i, j)),
        out_shape=jax.ShapeDtypeStruct((t, D_MODEL), BF16),
        compiler_params=_params(("arbitrary", "arbitrary"), 48),
        name="branch",
    )(attn, ret, proj, proj, wa, wr)


def _mixout_kernel(m_ref, x_ref, mod_ref, g2_ref, wo_ref, wrt_ref, brt_ref,
                   x1_ref, h2_ref, route_ref, count_ref, carry_ref):
    @pl.when(pl.program_id(0) == 0)
    def _():
        carry_ref[...] = jnp.zeros_like(carry_ref)

    mix = jnp.dot(m_ref[...], wo_ref[...], preferred_element_type=F32)
    x1 = x_ref[...] + mod_ref[0, 2:3, :] * mix
    x1_ref[...] = x1
    var = jnp.mean(x1 * x1, axis=-1, keepdims=True)
    h2 = x1 * lax.rsqrt(var + EPS) * g2_ref[...]
    h2 = h2 * (1.0 + mod_ref[0, 4:5, :]) + mod_ref[0, 3:4, :]
    tm = h2.shape[0]
    for j in range(SLAB_ROWS):
        lo = h2[:, j * LANES:(j + 1) * LANES]
        hi = h2[:, HALF_D + j * LANES:HALF_D + (j + 1) * LANES]
        h2_ref[_slab_rows(j, tm), :] = _pack_pair(lo, hi)
    h_hi = h2.astype(BF16)
    h_lo = (h2 - h_hi.astype(F32)).astype(BF16)
    w_rt = wrt_ref[...]
    w_hi = w_rt.astype(BF16)
    w_lo = (w_rt - w_hi.astype(F32)).astype(BF16)
    hi_both = jnp.dot(h_hi, jnp.concatenate([w_hi, w_lo], axis=1), preferred_element_type=F32)
    logits = (hi_both[:, :LANES] + hi_both[:, LANES:]
              + jnp.dot(h_lo, w_hi, preferred_element_type=F32) + brt_ref[...])
    route = _route(logits)

    lane = lax.broadcasted_iota(jnp.int32, route.shape, 1)
    lane_f = lane.astype(F32)
    hot1 = lane_f == route[:, 0:1]
    hot2 = lane_f == route[:, 1:2]
    both = jnp.where(hot1 | hot2, 1.0, 0.0)
    ii = lax.broadcasted_iota(jnp.int32, (tm, tm), 0)
    jj = lax.broadcasted_iota(jnp.int32, (tm, tm), 1)
    lower = jnp.where(ii > jj, 1.0, 0.0).astype(BF16)
    before = jnp.dot(lower, both.astype(BF16), preferred_element_type=F32) + carry_ref[...]
    r1 = jnp.sum(jnp.where(hot1, before, 0.0), axis=-1, keepdims=True)
    r2 = jnp.sum(jnp.where(hot2, before, 0.0), axis=-1, keepdims=True)
    route_ref[...] = jnp.where(lane == 4, r1, jnp.where(lane == 5, r2, route))
    carry = carry_ref[...] + jnp.sum(both, axis=0, keepdims=True)
    carry_ref[...] = carry
    count_ref[...] = carry


def _mixout(merged, x2, mod6, gain2, wo, w_rt, b_rt, seq):
    t = x2.shape[0]
    tm = MIXOUT_TM
    tiles_per_batch = seq // tm
    row = lambda i: (i, 0)
    const = lambda i: (0, 0)
    return pl.pallas_call(
        _mixout_kernel,
        grid=(t // tm,),
        in_specs=[pl.BlockSpec((tm, D_MODEL), row),
                  pl.BlockSpec((tm, D_MODEL), row),
                  pl.BlockSpec((1, 6, D_MODEL), lambda i: (i // tiles_per_batch, 0, 0)),
                  pl.BlockSpec((1, D_MODEL), const),
                  pl.BlockSpec((D_MODEL, D_MODEL), const, pipeline_mode=pl.Buffered(1)),
                  pl.BlockSpec((D_MODEL, LANES), const),
                  pl.BlockSpec((1, LANES), const)],
        out_specs=[pl.BlockSpec((tm, D_MODEL), row),
                   pl.BlockSpec((tm * SLAB_ROWS, SLAB_LANES), row),
                   pl.BlockSpec((tm, LANES), row),
                   pl.BlockSpec((1, LANES), const)],
        out_shape=[jax.ShapeDtypeStruct((t, D_MODEL), F32),
                   jax.ShapeDtypeStruct((t * SLAB_ROWS, SLAB_LANES), I32),
                   jax.ShapeDtypeStruct((t, LANES), F32),
                   jax.ShapeDtypeStruct((1, LANES), F32)],
        scratch_shapes=[pltpu.VMEM((1, LANES), F32)],
        compiler_params=_params(("arbitrary",), 56),
        name="mixout",
    )(merged, x2, mod6, gain2, wo, w_rt, b_rt)


PAD_BITS = (64, 32, 16, 8, 4, 2, 1)


def _dispatch_kernel(dest_ref, zstart_ref, zcount_ref, h2_ref, xs_ref, zero_ref, sem, zsem):
    n_assign = dest_ref.shape[0]
    zero_ref[...] = jnp.zeros_like(zero_ref)

    def zero_copy(start, rows):
        return pltpu.make_async_copy(zero_ref.at[pl.ds(0, rows * SLAB_ROWS), :],
                                     xs_ref.at[pl.ds(start * SLAB_ROWS, rows * SLAB_ROWS), :], zsem)

    def fill(e, wait):
        start = zstart_ref[e]
        pad = zcount_ref[e]
        for bit in PAD_BITS:
            @pl.when((pad & bit) != 0)
            def _(start=start, bit=bit):
                cp = zero_copy(start, bit)
                cp.wait() if wait else cp.start()
            start = start + (pad & bit)

    lax.fori_loop(0, N_EXPERTS, lambda e, c: (fill(e, False), c)[1], 0)

    def issue(tok, carry):
        src = h2_ref.at[pl.ds(pl.multiple_of(tok * SLAB_ROWS, SLAB_ROWS), SLAB_ROWS), :]
        for k in range(TOP_K):
            dst = pl.multiple_of(dest_ref[tok * TOP_K + k] * SLAB_ROWS, SLAB_ROWS)
            pltpu.make_async_copy(src, xs_ref.at[pl.ds(dst, SLAB_ROWS), :], sem).start()
        return carry

    lax.fori_loop(0, n_assign // TOP_K, issue, 0, unroll=8)

    def drain(i, carry):
        pltpu.make_async_copy(h2_ref.at[pl.ds(0, ROWS_PER_WAIT * SLAB_ROWS), :],
                              xs_ref.at[pl.ds(0, ROWS_PER_WAIT * SLAB_ROWS), :], sem).wait()
        return carry

    lax.fori_loop(0, n_assign // ROWS_PER_WAIT, drain, 0)
    lax.fori_loop(0, N_EXPERTS, lambda e, c: (fill(e, True), c)[1], 0)


def _dispatch(dest, zstart, zcount, h2_slab, n_pad):
    return pl.pallas_call(
        _dispatch_kernel,
        grid_spec=pltpu.PrefetchScalarGridSpec(
            num_scalar_prefetch=3,
            grid=(1,),
            in_specs=[pl.BlockSpec(memory_space=pl.ANY)],
            out_specs=pl.BlockSpec(memory_space=pl.ANY),
            scratch_shapes=[pltpu.VMEM((PAD_BITS[0] * SLAB_ROWS, SLAB_LANES), I32),
                            pltpu.SemaphoreType.DMA(()),
                            pltpu.SemaphoreType.DMA(())]),
        out_shape=jax.ShapeDtypeStruct((n_pad * SLAB_ROWS, SLAB_LANES), I32),
        compiler_params=_params(("arbitrary",), 16),
        name="dispatch",
    )(dest, zstart, zcount, h2_slab)


def _experts_kernel(item_e_ref, item_blk_ref, item_nb_ref,
                    x0_ref, x1_ref, x2_ref, x3_ref, wg_ref, wu_ref, wd_ref, y_ref,
                    xb_ref, acc_ref, yp_ref, sem):
    w = pl.program_id(0)
    c = pl.program_id(1)
    last_c = pl.num_programs(1) - 1
    nb = item_nb_ref[w]
    blk0 = item_blk_ref[w]
    x_refs = (x0_ref, x1_ref, x2_ref, x3_ref)

    blk_rows = MOE_BLOCK * SLAB_ROWS

    def out_copy(s):
        return pltpu.make_async_copy(yp_ref.at[s], y_ref.at[pl.ds((blk0 + s) * blk_rows, blk_rows), :],
                                     sem.at[s])

    def wait_out(count):
        for s in range(ITEM_BLOCKS):
            @pl.when(s < count)
            def _(s=s):
                out_copy(s).wait()

    @pl.when((c == last_c) & (w > 0))
    def _():
        wait_out(item_nb_ref[jnp.maximum(w - 1, 0)])

    def run(n_live):
        rows = n_live * MOE_BLOCK

        @pl.when(c == 0)
        def _():
            for s in range(n_live):
                for j in range(SLAB_ROWS):
                    lo, hi = _unpack_pair(x_refs[s][_slab_rows(j, MOE_BLOCK), :])
                    r0 = s * MOE_BLOCK
                    xb_ref[r0:r0 + MOE_BLOCK, j * LANES:(j + 1) * LANES] = lo.astype(BF16)
                    xb_ref[r0:r0 + MOE_BLOCK, HALF_D + j * LANES:HALF_D + (j + 1) * LANES] = hi.astype(BF16)

        xs = xb_ref[0:rows, :]
        g = jnp.dot(xs, wg_ref[...].astype(BF16), preferred_element_type=F32)
        u = jnp.dot(xs, wu_ref[...].astype(BF16), preferred_element_type=F32)
        hid = (g * jax.nn.sigmoid(g) * u).astype(BF16)
        y = jnp.dot(hid, wd_ref[...].astype(BF16), preferred_element_type=F32)

        @pl.when(c == 0)
        def _():
            acc_ref[0:rows, :] = y

        @pl.when((c > 0) & (c < last_c))
        def _():
            acc_ref[0:rows, :] += y

        @pl.when(c == last_c)
        def _():
            for s in range(n_live):
                r0 = s * MOE_BLOCK
                for j in range(SLAB_ROWS):
                    lo_cols = slice(j * LANES, (j + 1) * LANES)
                    hi_cols = slice(HALF_D + j * LANES, HALF_D + (j + 1) * LANES)
                    yp_ref[s, _slab_rows(j, MOE_BLOCK), :] = _pack_pair(
                        acc_ref[r0:r0 + MOE_BLOCK, lo_cols] + y[r0:r0 + MOE_BLOCK, lo_cols],
                        acc_ref[r0:r0 + MOE_BLOCK, hi_cols] + y[r0:r0 + MOE_BLOCK, hi_cols])
                out_copy(s).start()

    for n_live in range(1, ITEM_BLOCKS + 1):
        pl.when(nb == n_live)(functools.partial(run, n_live))

    @pl.when((c == last_c) & (w == pl.num_programs(0) - 1))
    def _():
        wait_out(nb)


def _experts(item_e, item_blk, item_nb, x_sorted, w_gate, w_up, w_down, n_blocks):
    n_items = item_e.shape[0]
    d = D_MODEL
    fc = EXPERT_FC
    x3 = x_sorted
    blk_rows = MOE_BLOCK * SLAB_ROWS
    n_chunks = EXPERT_DIM // fc
    assert n_chunks >= 2, "the last hidden chunk adds onto the accumulator of the earlier ones"

    def x_spec(s):
        return pl.BlockSpec((blk_rows, SLAB_LANES),
                            lambda w, c, ie, ib, inb: (jnp.minimum(ib[w] + s, n_blocks - 1), 0))

    def chunk(w, c, inb):
        return jnp.where(inb[w] > 0, c, n_chunks - 1)

    return pl.pallas_call(
        _experts_kernel,
        grid_spec=pltpu.PrefetchScalarGridSpec(
            num_scalar_prefetch=3,
            grid=(n_items, n_chunks),
            in_specs=[x_spec(0), x_spec(1), x_spec(2), x_spec(3),
                      pl.BlockSpec((None, d, fc), lambda w, c, ie, ib, inb: (ie[w], 0, chunk(w, c, inb))),
                      pl.BlockSpec((None, d, fc), lambda w, c, ie, ib, inb: (ie[w], 0, chunk(w, c, inb))),
                      pl.BlockSpec((None, fc, d), lambda w, c, ie, ib, inb: (ie[w], chunk(w, c, inb), 0))],
            out_specs=pl.BlockSpec(memory_space=pl.ANY),
            scratch_shapes=[pltpu.VMEM((ITEM_BLOCKS * MOE_BLOCK, d), BF16),
                            pltpu.VMEM((ITEM_BLOCKS * MOE_BLOCK, d), F32),
                            pltpu.VMEM((ITEM_BLOCKS, blk_rows, SLAB_LANES), I32),
                            pltpu.SemaphoreType.DMA((ITEM_BLOCKS,))]),
        out_shape=jax.ShapeDtypeStruct((n_blocks * blk_rows, SLAB_LANES), I32),
        compiler_params=_params(("arbitrary", "arbitrary"), 56),
        name="experts",
    )(item_e, item_blk, item_nb, x3, x3, x3, x3, w_gate, w_up, w_down)


def _combine_kernel(dest_ref, x1_ref, route_ref, mod_ref, gain_ref, y_ref, o_ref, ybuf_ref, sem):
    tm = COMBINE_TM
    i = pl.program_id(0)
    n_tiles = pl.num_programs(0)

    def row_copy(slot, k, r, src):
        return pltpu.make_async_copy(y_ref.at[pl.ds(src * SLAB_ROWS, SLAB_ROWS), :],
                                     ybuf_ref.at[slot, k, pl.ds(r * SLAB_ROWS, SLAB_ROWS), :],
                                     sem.at[slot])

    def wait_slot(slot):
        for k in range(TOP_K):
            pltpu.make_async_copy(y_ref.at[pl.ds(0, tm * SLAB_ROWS), :], ybuf_ref.at[slot, k],
                                  sem.at[slot]).wait()

    def issue_tile(tile, slot):
        base = tile * tm * TOP_K

        def issue(r, carry):
            for k in range(TOP_K):
                row_copy(slot, k, r, dest_ref[base + r * TOP_K + k]).start()
            return carry

        lax.fori_loop(0, tm, issue, 0, unroll=4)

    @pl.when(i == 0)
    def _():
        issue_tile(0, 0)

    @pl.when(i + 1 < n_tiles)
    def _():
        issue_tile(i + 1, (i + 1) % 2)

    slot = i % 2
    wait_slot(slot)

    route = route_ref[...]
    w0 = route[:, 2:3]
    w1 = route[:, 3:4]
    ssq = jnp.zeros((tm, 1), F32)
    for j in range(SLAB_ROWS):
        rows = _slab_rows(j, tm)
        y0 = _unpack_pair(ybuf_ref[slot, 0, rows, :])
        y1 = _unpack_pair(ybuf_ref[slot, 1, rows, :])
        for part, off in ((0, j * LANES), (1, HALF_D + j * LANES)):
            cols = slice(off, off + LANES)
            ffn = w0 * y0[part] + w1 * y1[part]
            x2 = x1_ref[:, cols] + mod_ref[0, 5:6, cols] * ffn
            o_ref[:, cols] = x2
            ssq = ssq + jnp.sum(x2 * x2, axis=-1, keepdims=True)
    o_ref[...] = o_ref[...] * lax.rsqrt(ssq * (1.0 / D_MODEL) + EPS) * gain_ref[...]


def _combine(dest, x1, route, mod6, gain, y_sorted, seq):
    t, d = x1.shape
    tm = COMBINE_TM
    tiles_per_batch = seq // tm
    return pl.pallas_call(
        _combine_kernel,
        grid_spec=pltpu.PrefetchScalarGridSpec(
            num_scalar_prefetch=1,
            grid=(t // tm,),
            in_specs=[pl.BlockSpec((tm, d), lambda i, dst: (i, 0)),
                      pl.BlockSpec((tm, LANES), lambda i, dst: (i, 0)),
                      pl.BlockSpec((1, 6, d), lambda i, dst: (i // tiles_per_batch, 0, 0)),
                      pl.BlockSpec((1, d), lambda i, dst: (0, 0)),
                      pl.BlockSpec(memory_space=pl.ANY)],
            out_specs=pl.BlockSpec((tm, d), lambda i, dst: (i, 0)),
            scratch_shapes=[pltpu.VMEM((2, TOP_K, tm * SLAB_ROWS, SLAB_LANES), I32),
                            pltpu.SemaphoreType.DMA((2,))]),
        out_shape=jax.ShapeDtypeStruct((t, d), F32),
        compiler_params=_params(("arbitrary",), 24),
        name="combine",
    )(dest, x1, route, mod6, gain, y_sorted)


def _dispatch_tables(route, counts, t):
    n_assign = t * TOP_K
    n_pad = -(-(n_assign + N_EXPERTS * (MOE_BLOCK - 1)) // MOE_BLOCK) * MOE_BLOCK
    n_blocks = n_pad // MOE_BLOCK
    n_items = N_EXPERTS + n_assign // (ITEM_BLOCKS * MOE_BLOCK)

    eid = route[:, :TOP_K].astype(jnp.int32)
    rk = route[:, 4:4 + TOP_K].astype(jnp.int32)
    cnt = counts[0, :N_EXPERTS].astype(jnp.int32)
    blocks_e = (cnt + MOE_BLOCK - 1) // MOE_BLOCK
    blk_end = jnp.cumsum(blocks_e)
    blk_start = blk_end - blocks_e
    hot = eid[:, :, None] == jnp.arange(N_EXPERTS, dtype=jnp.int32)
    row_start = jnp.sum(jnp.where(hot, blk_start * MOE_BLOCK, 0), axis=-1)
    dest = (row_start + rk).reshape(-1)
    zstart = (blk_start * MOE_BLOCK + cnt).astype(jnp.int32)
    zcount = (blocks_e * MOE_BLOCK - cnt).astype(jnp.int32)

    items_e = (blocks_e + ITEM_BLOCKS - 1) // ITEM_BLOCKS
    item_end = jnp.cumsum(items_e)
    item_start = item_end - items_e
    w = jnp.arange(n_items, dtype=jnp.int32)
    live = w < item_end[-1]
    w_live = jnp.minimum(w, item_end[-1] - 1)
    e_w = jnp.minimum(jnp.sum((item_end[None, :] <= w_live[:, None]).astype(jnp.int32), axis=1), N_EXPERTS - 1)
    j_w = w_live - item_start[e_w]
    item_blk = (blk_start[e_w] + ITEM_BLOCKS * j_w).astype(jnp.int32)
    item_nb = jnp.where(live, jnp.clip(blocks_e[e_w] - ITEM_BLOCKS * j_w, 0, ITEM_BLOCKS), 0).astype(jnp.int32)
    return dest.astype(jnp.int32), zstart, zcount, e_w, item_blk, item_nb, n_blocks


def kernel(x, c, positions, norm1_gain, norm2_gain, final_norm_gain, w_ada, b_ada, w_in, attn_sinks,
           ret_norm_gain, w_branch_attn, w_branch_ret, w_out, w_router_group, b_router_group,
           w_router_expert, b_router_expert, w_expert_gate, w_expert_up, w_expert_down):
    batch, seq, d = x.shape
    t = batch * seq
    depth = w_ada.shape[0]
    half = RET_DIM // 2
    inv_freq = (ROPE_BASE ** (-jnp.arange(half, dtype=F32) / half)).reshape(1, half)
    pos = positions.reshape(t, 1)
    c8 = jnp.pad(c, ((0, 8 - batch), (0, 0)))
    xf = x.reshape(t, d)

    assert depth == 1, "the fused final norm assumes a single layer"
    for layer in range(depth):
        mod6 = _ada(c8, w_ada[layer], b_ada[layer].reshape(1, -1))[:batch].reshape(batch, 6, d)
        proj, proj_kv = _proj(xf, norm1_gain[layer].reshape(1, d), mod6, w_in[layer], seq)
        attn = _attn(proj, proj_kv, attn_sinks[layer], batch, seq)
        ret = _ret(proj, pos, inv_freq, ret_norm_gain[layer].reshape(1, d), batch, seq)

        pad = LANES - N_GROUPS - N_EXPERTS
        w_rt = jnp.concatenate([w_router_group[layer], w_router_expert[layer],
                                jnp.zeros((d, pad), F32)], axis=1)
        b_rt = jnp.concatenate([b_router_group[layer], b_router_expert[layer],
                                jnp.zeros((pad,), F32)]).reshape(1, LANES)
        merged = _branch(attn, ret, proj, w_branch_attn[layer], w_branch_ret[layer])
        x1, h2, route, counts = _mixout(merged, xf, mod6, norm2_gain[layer].reshape(1, d),
                                        w_out[layer].astype(BF16), w_rt, b_rt, seq)
        dest, zstart, zcount, item_e, item_blk, item_nb, n_blocks = _dispatch_tables(route, counts, t)
        x_sorted = _dispatch(dest, zstart, zcount, h2, n_blocks * MOE_BLOCK)
        y_sorted = _experts(item_e, item_blk, item_nb, x_sorted,
                            w_expert_gate[layer], w_expert_up[layer], w_expert_down[layer], n_blocks)
        xf = _combine(dest, x1, route, mod6, final_norm_gain.reshape(1, d), y_sorted, seq)
    return xf.reshape(batch, seq, d)
```

```python
import functools
import math

import jax
import jax.numpy as jnp
import numpy as np
from jax import lax
from jax.experimental import pallas as pl
from jax.experimental.pallas import tpu as pltpu

F32 = jnp.float32
BF16 = jnp.bfloat16

D_MODEL = 2048
ATTN_HEAD_DIM = 64
ATTN_HEADS = 32
ATTN_KV_HEADS = 4
ATTN_GROUP = 8
WINDOW = 128
RET_HEADS = 8
RET_DIM = 256
RET_CHUNK = 128
ROPE_BASE = 10000.0
N_GROUPS = 4
EXPERTS_PER_GROUP = 16
N_EXPERTS = 64
TOP_K = 2
EXPERT_DIM = 1024
MOE_BLOCK = 128
EPS = 1e-6
NEG = -1e30

MIB = 1024 * 1024
LANES = 128
PROJ_TN = 512
PROJ_TM = 1024
KV_SRC_TILE = 4
BRANCH_TM = 1024
BRANCH_TN = 512
MIXOUT_TM = 512
ITEM_BLOCKS = 4
EXPERT_FC = 512
COMBINE_TM = 128
SLAB_ROWS = 8
SLAB_LANES = LANES
HALF_D = D_MODEL // 2
ROWS_PER_WAIT = 128
I32 = jnp.int32

COL_QA, COL_QR, COL_KR, COL_VR, COL_GR, COL_GA, COL_GRT = 0, 1, 2, 3, 4, 5, 6

LOG_GAMMA = [math.log1p(-(2.0 ** (-5.0 - h))) for h in range(RET_HEADS)]


def _params(sem, vmem_mib):
    return pltpu.CompilerParams(dimension_semantics=sem, vmem_limit_bytes=vmem_mib * MIB)


def _pack_pair(lo, hi):
    lo_b = lax.bitcast_convert_type(lo.astype(BF16).astype(F32), I32)
    hi_b = lax.bitcast_convert_type(hi.astype(BF16).astype(F32), I32)
    return hi_b | lax.shift_right_logical(lo_b, jnp.full_like(lo_b, 16))


def _unpack_pair(w):
    lo = lax.bitcast_convert_type(w << 16, F32)
    hi = lax.bitcast_convert_type(w & jnp.int32(-65536), F32)
    return lo, hi


def _slab_rows(j, n_tokens):
    return pl.ds(j, n_tokens, stride=SLAB_ROWS)


def _ada_kernel(c_ref, w_ref, b_ref, o_ref):
    c = c_ref[...]
    a = (c * jax.nn.sigmoid(c)).astype(BF16)
    o_ref[...] = jnp.dot(a, w_ref[...].astype(BF16), preferred_element_type=F32) + b_ref[...]


def _ada(c8, w_ada, b_ada):
    n = w_ada.shape[1]
    tn = 1024
    return pl.pallas_call(
        _ada_kernel,
        grid=(n // tn,),
        in_specs=[pl.BlockSpec((8, D_MODEL), lambda j: (0, 0)),
                  pl.BlockSpec((D_MODEL, tn), lambda j: (0, j)),
                  pl.BlockSpec((1, tn), lambda j: (0, j))],
        out_specs=pl.BlockSpec((8, tn), lambda j: (0, j)),
        out_shape=jax.ShapeDtypeStruct((8, n), F32),
        compiler_params=_params(("arbitrary",), 40),
        name="ada",
    )(c8, w_ada, b_ada)


def _proj_kernel(x_ref, g_ref, mod_ref, wlo_ref, whi_ref, o_ref, kv_ref, h_ref):
    v = pl.program_id(1)
    last = pl.num_programs(1) - 1

    @pl.when(v == 0)
    def _():
        x = x_ref[...]
        var = jnp.mean(x * x, axis=-1, keepdims=True)
        y = x * lax.rsqrt(var + EPS) * g_ref[...]
        h_ref[...] = (y * (1.0 + mod_ref[0, 1:2, :]) + mod_ref[0, 0:1, :]).astype(BF16)

    @pl.when(v < last)
    def _():
        w = jnp.concatenate([wlo_ref[...].astype(BF16), whi_ref[...].astype(BF16)], axis=1)
        o_ref[...] = jnp.dot(h_ref[...], w, preferred_element_type=F32).astype(BF16)

    @pl.when(v == last)
    def _():
        kv_ref[...] = jnp.dot(h_ref[...], wlo_ref[...].astype(BF16),
                              preferred_element_type=F32).astype(BF16)


def _proj_w_tile(v, n_wide):
    return jnp.where(v < 2, 2 * v, jnp.where(v < n_wide, 2 * v + 1, KV_SRC_TILE))


def _proj(x2, gain, mod6, w_in, seq):
    t = x2.shape[0]
    n = w_in.shape[1]
    tiles_per_batch = seq // PROJ_TM
    n_wide = (n - PROJ_TN) // (2 * PROJ_TN)
    return pl.pallas_call(
        _proj_kernel,
        grid=(t // PROJ_TM, n_wide + 1),
        in_specs=[pl.BlockSpec((PROJ_TM, D_MODEL), lambda i, v: (i, 0)),
                  pl.BlockSpec((1, D_MODEL), lambda i, v: (0, 0)),
                  pl.BlockSpec((1, 6, D_MODEL), lambda i, v: (i // tiles_per_batch, 0, 0)),
                  pl.BlockSpec((D_MODEL, PROJ_TN), lambda i, v: (0, _proj_w_tile(v, n_wide))),
                  pl.BlockSpec((D_MODEL, PROJ_TN),
                               lambda i, v: (0, jnp.where(v < n_wide, _proj_w_tile(v, n_wide) + 1, KV_SRC_TILE)))],
        out_specs=[pl.BlockSpec((PROJ_TM, 2 * PROJ_TN), lambda i, v: (i, jnp.minimum(v, n_wide - 1))),
                   pl.BlockSpec((PROJ_TM, PROJ_TN), lambda i, v: (i, 0))],
        out_shape=[jax.ShapeDtypeStruct((t, n - PROJ_TN), BF16),
                   jax.ShapeDtypeStruct((t, PROJ_TN), BF16)],
        scratch_shapes=[pltpu.VMEM((PROJ_TM, D_MODEL), BF16)],
        compiler_params=_params(("arbitrary", "arbitrary"), 56),
        name="proj",
    )(x2, gain, mod6, w_in, w_in)


def _attn_kernel(sink_ref, q_ref, kvp_ref, kvc_ref, o_ref):
    n = pl.program_id(1)
    kvp = kvp_ref[...]
    kvc = kvc_ref[...]
    qi = lax.broadcasted_iota(jnp.int32, (WINDOW, WINDOW), 0)
    sj = lax.broadcasted_iota(jnp.int32, (WINDOW, WINDOW), 1)
    valid_prev = (sj > qi) & (n > 0)
    valid_cur = sj <= qi
    sink_col = sj == 0
    first_row = lax.broadcasted_iota(jnp.int32, (2 * WINDOW, 1), 0) == 0
    dh = ATTN_HEAD_DIM
    kv_w = ATTN_KV_HEADS * dh
    scale = jnp.asarray(dh ** -0.5, BF16)
    for kv in range(ATTN_KV_HEADS):
        kband = jnp.concatenate([kvp[:, kv * dh:(kv + 1) * dh],
                                 kvc[:, kv * dh:(kv + 1) * dh]], axis=0) * scale
        vband = jnp.concatenate([kvp[:, kv_w + kv * dh:kv_w + (kv + 1) * dh],
                                 kvc[:, kv_w + kv * dh:kv_w + (kv + 1) * dh]], axis=0)
        vband = jnp.where(first_row, jnp.zeros_like(vband), vband)
        zeros = jnp.zeros_like(kband)
        ones = jnp.ones_like(vband)
        k_pad = (jnp.concatenate([kband, zeros], axis=1), jnp.concatenate([zeros, kband], axis=1))
        pv_rhs = jnp.concatenate(
            [jnp.concatenate([vband, zeros, ones, zeros], axis=1),
             jnp.concatenate([zeros, vband, zeros, ones], axis=1)], axis=0)
        def pair_scores(pair):
            h0 = kv * ATTN_GROUP + 2 * pair
            q_pair = q_ref[:, h0 * dh:(h0 + 2) * dh]
            return [lax.dot_general(q_pair, k_pad[idx], (((1,), (1,)), ((), ())),
                                    preferred_element_type=F32) for idx in range(2)]

        s_next = pair_scores(0)
        for pair in range(ATTN_GROUP // 2):
            h0 = kv * ATTN_GROUP + 2 * pair
            s_pair = s_next
            if pair + 1 < ATTN_GROUP // 2:
                s_next = pair_scores(pair + 1)
            probs = []
            for idx in range(2):
                s = s_pair[idx]
                sink = sink_ref[h0 + idx]
                s_prev = jnp.where(sink_col, sink, jnp.where(valid_prev, s[:, :WINDOW], NEG))
                s_cur = jnp.where(valid_cur, s[:, WINDOW:], NEG)
                m = jnp.max(jnp.maximum(s_prev, s_cur), axis=-1, keepdims=True)
                probs += [jnp.exp(s_prev - m).astype(BF16), jnp.exp(s_cur - m).astype(BF16)]
            r = jnp.dot(jnp.concatenate(probs, axis=-1), pv_rhs, preferred_element_type=F32)
            o_ref[:, h0 * dh:(h0 + 2) * dh] = (r[:, :2 * dh] * (1.0 / r[:, 2 * dh:])).astype(BF16)


def _attn(proj, proj_kv, sinks, batch, seq):
    nb = seq // WINDOW
    t = batch * seq

    def cur(b, n):
        return (b * nb + n, 0)

    def prev(b, n):
        return (b * nb + jnp.maximum(n - 1, 0), 0)

    return pl.pallas_call(
        _attn_kernel,
        grid=(batch, nb),
        in_specs=[pl.BlockSpec(memory_space=pltpu.SMEM),
                  pl.BlockSpec((WINDOW, D_MODEL), lambda b, n: (b * nb + n, COL_QA)),
                  pl.BlockSpec((WINDOW, PROJ_TN), prev),
                  pl.BlockSpec((WINDOW, PROJ_TN), cur)],
        out_specs=pl.BlockSpec((WINDOW, D_MODEL), lambda b, n: (b * nb + n, 0)),
        out_shape=jax.ShapeDtypeStruct((t, D_MODEL), BF16),
        compiler_params=_params(("arbitrary", "arbitrary"), 32),
        name="attn",
    )(sinks, proj, proj_kv, proj_kv)


def _ret_decay_tables():
    lg = np.asarray(LOG_GAMMA, np.float64)[:, None, None]
    i = np.arange(RET_CHUNK, dtype=np.float64)
    diff = i[:, None] - i[None, :]
    k_scale = RET_DIM ** -0.5
    d_intra = np.where(diff >= 0, np.exp(np.maximum(diff, 0.0) * lg), 0.0) * k_scale
    lanes = np.ones((1, 1, RET_DIM // 2))
    d_q = np.exp((i[None, :, None] + 1.0) * lg) * lanes
    d_k = np.exp((RET_CHUNK - 1.0 - i[None, :, None]) * lg) * k_scale * lanes
    return jnp.asarray(d_intra, F32), jnp.asarray(d_q, F32), jnp.asarray(d_k, BF16)


def _ret_kernel(pos_ref, invf_ref, di_ref, dq_ref, dk_ref, q_ref, k_ref, v_ref, g_ref, gain_ref,
                o_ref, state_ref):
    @pl.when(pl.program_id(1) == 0)
    def _():
        state_ref[...] = jnp.zeros_like(state_ref)

    half = RET_DIM // 2
    ang = pos_ref[...].astype(F32) * invf_ref[...]
    cos = jnp.cos(ang).astype(BF16)
    sin = jnp.sin(ang).astype(BF16)

    def rot(t):
        t1, t2 = t[:, :half], t[:, half:]
        return jnp.concatenate([t1 * cos - t2 * sin, t1 * sin + t2 * cos], axis=-1)

    def both_halves(t, factor):
        return jnp.concatenate([t[:, :half] * factor, t[:, half:] * factor], axis=-1)

    for h in range(RET_HEADS):
        sl = slice(h * RET_DIM, (h + 1) * RET_DIM)
        qb = rot(q_ref[:, sl])
        kb = rot(k_ref[:, sl])
        vb = v_ref[:, sl]
        d_chunk = math.exp(RET_CHUNK * LOG_GAMMA[h])
        intra = lax.dot_general(qb, kb, (((1,), (1,)), ((), ())),
                                preferred_element_type=F32) * di_ref[h]
        st = state_ref[h]
        cross = jnp.dot(qb, st.astype(BF16), preferred_element_type=F32)
        o = jnp.dot(intra.astype(BF16), vb, preferred_element_type=F32) + both_halves(cross, dq_ref[h])
        kd = both_halves(kb, dk_ref[h])
        state_ref[h] = st * d_chunk + lax.dot_general(kd, vb, (((0,), (0,)), ((), ())),
                                                      preferred_element_type=F32)
        o = o * lax.rsqrt(jnp.mean(o * o, axis=-1, keepdims=True) + EPS) * gain_ref[:, sl]
        gg = g_ref[:, sl].astype(F32)
        o_ref[:, sl] = (gg * jax.nn.sigmoid(gg) * o).astype(BF16)


def _ret(proj, pos, inv_freq, ret_gain, batch, seq):
    nc = seq // RET_CHUNK
    t = batch * seq

    def col(cb):
        return lambda b, c: (b * nc + c, cb)

    d_intra, d_q, d_k = _ret_decay_tables()
    table = lambda lanes: pl.BlockSpec((RET_HEADS, RET_CHUNK, lanes), lambda b, c: (0, 0, 0))
    return pl.pallas_call(
        _ret_kernel,
        grid=(batch, nc),
        in_specs=[pl.BlockSpec((RET_CHUNK, 1), lambda b, c: (b * nc + c, 0)),
                  pl.BlockSpec((1, RET_DIM // 2), lambda b, c: (0, 0)),
                  table(RET_CHUNK), table(RET_DIM // 2), table(RET_DIM // 2),
                  pl.BlockSpec((RET_CHUNK, D_MODEL), col(COL_QR)),
                  pl.BlockSpec((RET_CHUNK, D_MODEL), col(COL_KR)),
                  pl.BlockSpec((RET_CHUNK, D_MODEL), col(COL_VR)),
                  pl.BlockSpec((RET_CHUNK, D_MODEL), col(COL_GR)),
                  pl.BlockSpec((1, D_MODEL), lambda b, c: (0, 0))],
        out_specs=pl.BlockSpec((RET_CHUNK, D_MODEL), lambda b, c: (b * nc + c, 0)),
        out_shape=jax.ShapeDtypeStruct((t, D_MODEL), BF16),
        scratch_shapes=[pltpu.VMEM((RET_HEADS, RET_DIM, RET_DIM), F32)],
        compiler_params=_params(("arbitrary", "arbitrary"), 32),
        name="ret",
    )(pos, inv_freq, d_intra, d_q, d_k, proj, proj, proj, proj, ret_gain)


def _route(logits):
    lane = lax.broadcasted_iota(jnp.int32, logits.shape, 1)
    lane_f = lane.astype(F32)
    is_g = lane < N_GROUPS
    gl = jnp.where(is_g, logits, NEG)
    gmax = jnp.max(gl, axis=-1, keepdims=True)
    gsel = jnp.min(jnp.where(gl == gmax, lane_f, float(LANES)), axis=-1, keepdims=True)
    gsum = jnp.sum(jnp.where(is_g, jnp.exp(gl - gmax), 0.0), axis=-1, keepdims=True)
    g_w = 1.0 / gsum
    grp = ((lane - N_GROUPS) >> 4).astype(F32)
    is_e = (lane >= N_GROUPS) & (lane < N_GROUPS + N_EXPERTS) & (grp == gsel)
    el = jnp.where(is_e, logits, NEG)
    v1 = jnp.max(el, axis=-1, keepdims=True)
    i1 = jnp.min(jnp.where(el == v1, lane_f, float(LANES)), axis=-1, keepdims=True)
    el2 = jnp.where(lane_f == i1, NEG, el)
    v2 = jnp.max(el2, axis=-1, keepdims=True)
    i2 = jnp.min(jnp.where(el2 == v2, lane_f, float(LANES)), axis=-1, keepdims=True)
    tt = jnp.exp(v2 - v1)
    w1 = g_w / (1.0 + tt)
    w2 = g_w * tt / (1.0 + tt)
    return jnp.where(lane == 0, i1 - N_GROUPS,
                     jnp.where(lane == 1, i2 - N_GROUPS,
                               jnp.where(lane == 2, w1, jnp.where(lane == 3, w2, 0.0))))


def _branch_kernel(attn_ref, ret_ref, ga_ref, gr_ref, wa_ref, wr_ref, o_ref):
    a = jnp.dot(attn_ref[...], wa_ref[...].astype(BF16), preferred_element_type=F32)
    r = jnp.dot(ret_ref[...], wr_ref[...].astype(BF16), preferred_element_type=F32)
    o_ref[...] = (jax.nn.sigmoid(ga_ref[...].astype(F32)) * a
                  + jax.nn.sigmoid(gr_ref[...].astype(F32)) * r).astype(BF16)


def _branch(attn, ret, proj, wa, wr):
    t = attn.shape[0]
    tm, tn = BRANCH_TM, BRANCH_TN
    per_slab = D_MODEL // tn
    return pl.pallas_call(
        _branch_kernel,
        grid=(t // tm, D_MODEL // tn),
        in_specs=[pl.BlockSpec((tm, D_MODEL), lambda i, j: (i, 0)),
                  pl.BlockSpec((tm, D_MODEL), lambda i, j: (i, 0)),
                  pl.BlockSpec((tm, tn), lambda i, j: (i, COL_GA * per_slab + j)),
                  pl.BlockSpec((tm, tn), lambda i, j: (i, COL_GRT * per_slab + j)),
                  pl.BlockSpec((D_MODEL, tn), lambda i, j: (0, j)),
                  pl.BlockSpec((D_MODEL, tn), lambda i, j: (0, j))],
        out_specs=pl.BlockSpec((tm, tn), lambda i, j: (i, j)),
        out_shape=jax.ShapeDtypeStruct((t, D_MODEL), BF16),
        compiler_params=_params(("arbitrary", "arbitrary"), 48),
        name="branch",
    )(attn, ret, proj, proj, wa, wr)


def _mixout_kernel(m_ref, x_ref, mod_ref, g2_ref, wo_ref, wrt_ref, brt_ref,
                   x1_ref, h2_ref, route_ref, count_ref, carry_ref):
    @pl.when(pl.program_id(0) == 0)
    def _():
        carry_ref[...] = jnp.zeros_like(carry_ref)

    mix = jnp.dot(m_ref[...], wo_ref[...], preferred_element_type=F32)
    x1 = x_ref[...] + mod_ref[0, 2:3, :] * mix
    x1_ref[...] = x1
    var = jnp.mean(x1 * x1, axis=-1, keepdims=True)
    h2 = x1 * lax.rsqrt(var + EPS) * g2_ref[...]
    h2 = h2 * (1.0 + mod_ref[0, 4:5, :]) + mod_ref[0, 3:4, :]
    tm = h2.shape[0]
    for j in range(SLAB_ROWS):
        lo = h2[:, j * LANES:(j + 1) * LANES]
        hi = h2[:, HALF_D + j * LANES:HALF_D + (j + 1) * LANES]
        h2_ref[_slab_rows(j, tm), :] = _pack_pair(lo, hi)
    h_hi = h2.astype(BF16)
    h_lo = (h2 - h_hi.astype(F32)).astype(BF16)
    w_rt = wrt_ref[...]
    w_hi = w_rt.astype(BF16)
    w_lo = (w_rt - w_hi.astype(F32)).astype(BF16)
    hi_both = jnp.dot(h_hi, jnp.concatenate([w_hi, w_lo], axis=1), preferred_element_type=F32)
    logits = (hi_both[:, :LANES] + hi_both[:, LANES:]
              + jnp.dot(h_lo, w_hi, preferred_element_type=F32) + brt_ref[...])
    route = _route(logits)

    lane = lax.broadcasted_iota(jnp.int32, route.shape, 1)
    lane_f = lane.astype(F32)
    hot1 = lane_f == route[:, 0:1]
    hot2 = lane_f == route[:, 1:2]
    both = jnp.where(hot1 | hot2, 1.0, 0.0)
    ii = lax.broadcasted_iota(jnp.int32, (tm, tm), 0)
    jj = lax.broadcasted_iota(jnp.int32, (tm, tm), 1)
    lower = jnp.where(ii > jj, 1.0, 0.0).astype(BF16)
    before = jnp.dot(lower, both.astype(BF16), preferred_element_type=F32) + carry_ref[...]
    r1 = jnp.sum(jnp.where(hot1, before, 0.0), axis=-1, keepdims=True)
    r2 = jnp.sum(jnp.where(hot2, before, 0.0), axis=-1, keepdims=True)
    route_ref[...] = jnp.where(lane == 4, r1, jnp.where(lane == 5, r2, route))
    carry = carry_ref[...] + jnp.sum(both, axis=0, keepdims=True)
    carry_ref[...] = carry
    count_ref[...] = carry


def _mixout(merged, x2, mod6, gain2, wo, w_rt, b_rt, seq):
    t = x2.shape[0]
    tm = MIXOUT_TM
    tiles_per_batch = seq // tm
    row = lambda i: (i, 0)
    const = lambda i: (0, 0)
    return pl.pallas_call(
        _mixout_kernel,
        grid=(t // tm,),
        in_specs=[pl.BlockSpec((tm, D_MODEL), row),
                  pl.BlockSpec((tm, D_MODEL), row),
                  pl.BlockSpec((1, 6, D_MODEL), lambda i: (i // tiles_per_batch, 0, 0)),
                  pl.BlockSpec((1, D_MODEL), const),
                  pl.BlockSpec((D_MODEL, D_MODEL), const, pipeline_mode=pl.Buffered(1)),
                  pl.BlockSpec((D_MODEL, LANES), const),
                  pl.BlockSpec((1, LANES), const)],
        out_specs=[pl.BlockSpec((tm, D_MODEL), row),
                   pl.BlockSpec((tm * SLAB_ROWS, SLAB_LANES), row),
                   pl.BlockSpec((tm, LANES), row),
                   pl.BlockSpec((1, LANES), const)],
        out_shape=[jax.ShapeDtypeStruct((t, D_MODEL), F32),
                   jax.ShapeDtypeStruct((t * SLAB_ROWS, SLAB_LANES), I32),
                   jax.ShapeDtypeStruct((t, LANES), F32),
                   jax.ShapeDtypeStruct((1, LANES), F32)],
        scratch_shapes=[pltpu.VMEM((1, LANES), F32)],
        compiler_params=_params(("arbitrary",), 56),
        name="mixout",
    )(merged, x2, mod6, gain2, wo, w_rt, b_rt)


PAD_BITS = (64, 32, 16, 8, 4, 2, 1)


def _dispatch_kernel(dest_ref, zstart_ref, zcount_ref, h2_ref, xs_ref, zero_ref, sem, zsem):
    n_assign = dest_ref.shape[0]
    zero_ref[...] = jnp.zeros_like(zero_ref)

    def zero_copy(start, rows):
        return pltpu.make_async_copy(zero_ref.at[pl.ds(0, rows * SLAB_ROWS), :],
                                     xs_ref.at[pl.ds(start * SLAB_ROWS, rows * SLAB_ROWS), :], zsem)

    def fill(e, wait):
        start = zstart_ref[e]
        pad = zcount_ref[e]
        for bit in PAD_BITS:
            @pl.when((pad & bit) != 0)
            def _(start=start, bit=bit):
                cp = zero_copy(start, bit)
                cp.wait() if wait else cp.start()
            start = start + (pad & bit)

    lax.fori_loop(0, N_EXPERTS, lambda e, c: (fill(e, False), c)[1], 0)

    def issue(tok, carry):
        src = h2_ref.at[pl.ds(pl.multiple_of(tok * SLAB_ROWS, SLAB_ROWS), SLAB_ROWS), :]
        for k in range(TOP_K):
            dst = pl.multiple_of(dest_ref[tok * TOP_K + k] * SLAB_ROWS, SLAB_ROWS)
            pltpu.make_async_copy(src, xs_ref.at[pl.ds(dst, SLAB_ROWS), :], sem).start()
        return carry

    lax.fori_loop(0, n_assign // TOP_K, issue, 0, unroll=8)

    def drain(i, carry):
        pltpu.make_async_copy(h2_ref.at[pl.ds(0, ROWS_PER_WAIT * SLAB_ROWS), :],
                              xs_ref.at[pl.ds(0, ROWS_PER_WAIT * SLAB_ROWS), :], sem).wait()
        return carry

    lax.fori_loop(0, n_assign // ROWS_PER_WAIT, drain, 0)
    lax.fori_loop(0, N_EXPERTS, lambda e, c: (fill(e, True), c)[1], 0)


def _dispatch(dest, zstart, zcount, h2_slab, n_pad):
    return pl.pallas_call(
        _dispatch_kernel,
        grid_spec=pltpu.PrefetchScalarGridSpec(
            num_scalar_prefetch=3,
            grid=(1,),
            in_specs=[pl.BlockSpec(memory_space=pltpu.HBM)],
            out_specs=pl.BlockSpec(memory_space=pl.ANY),
            scratch_shapes=[pltpu.VMEM((PAD_BITS[0] * SLAB_ROWS, SLAB_LANES), I32),
                            pltpu.SemaphoreType.DMA(()),
                            pltpu.SemaphoreType.DMA(())]),
        out_shape=jax.ShapeDtypeStruct((n_pad * SLAB_ROWS, SLAB_LANES), I32),
        compiler_params=_params(("arbitrary",), 16),
        name="dispatch",
    )(dest, zstart, zcount, h2_slab)


def _experts_kernel(item_e_ref, item_blk_ref, item_nb_ref, x_blk_ref,
                    x0_ref, x1_ref, x2_ref, x3_ref, wg_ref, wu_ref, wd_ref, y_ref,
                    xb_ref, acc_ref, yp_ref, sem):
    w = pl.program_id(0)
    c = pl.program_id(1)
    last_c = pl.num_programs(1) - 1
    nb = item_nb_ref[w]
    blk0 = item_blk_ref[w]
    x_refs = (x0_ref, x1_ref, x2_ref, x3_ref)

    blk_rows = MOE_BLOCK * SLAB_ROWS

    def out_copy(s):
        return pltpu.make_async_copy(yp_ref.at[s], y_ref.at[pl.ds((blk0 + s) * blk_rows, blk_rows), :],
                                     sem.at[s])

    def wait_out(count):
        for s in range(ITEM_BLOCKS):
            @pl.when(s < count)
            def _(s=s):
                out_copy(s).wait()

    @pl.when((c == last_c) & (w > 0))
    def _():
        wait_out(item_nb_ref[jnp.maximum(w - 1, 0)])

    def run(n_live):
        rows = n_live * MOE_BLOCK

        @pl.when(c == 0)
        def _():
            for s in range(n_live):
                for j in range(SLAB_ROWS):
                    lo, hi = _unpack_pair(x_refs[s][_slab_rows(j, MOE_BLOCK), :])
                    r0 = s * MOE_BLOCK
                    xb_ref[r0:r0 + MOE_BLOCK, j * LANES:(j + 1) * LANES] = lo.astype(BF16)
                    xb_ref[r0:r0 + MOE_BLOCK, HALF_D + j * LANES:HALF_D + (j + 1) * LANES] = hi.astype(BF16)

        xs = xb_ref[0:rows, :]
        g = jnp.dot(xs, wg_ref[...].astype(BF16), preferred_element_type=F32)
        u = jnp.dot(xs, wu_ref[...].astype(BF16), preferred_element_type=F32)
        hid = (g * jax.nn.sigmoid(g) * u).astype(BF16)
        y = jnp.dot(hid, wd_ref[...].astype(BF16), preferred_element_type=F32)

        @pl.when(c == 0)
        def _():
            acc_ref[0:rows, :] = y

        @pl.when((c > 0) & (c < last_c))
        def _():
            acc_ref[0:rows, :] += y

        @pl.when(c == last_c)
        def _():
            for s in range(n_live):
                r0 = s * MOE_BLOCK
                for j in range(SLAB_ROWS):
                    lo_cols = slice(j * LANES, (j + 1) * LANES)
                    hi_cols = slice(HALF_D + j * LANES, HALF_D + (j + 1) * LANES)
                    yp_ref[s, _slab_rows(j, MOE_BLOCK), :] = _pack_pair(
                        acc_ref[r0:r0 + MOE_BLOCK, lo_cols] + y[r0:r0 + MOE_BLOCK, lo_cols],
                        acc_ref[r0:r0 + MOE_BLOCK, hi_cols] + y[r0:r0 + MOE_BLOCK, hi_cols])
                out_copy(s).start()

    for n_live in range(1, ITEM_BLOCKS + 1):
        pl.when(nb == n_live)(functools.partial(run, n_live))

    @pl.when((c == last_c) & (w == pl.num_programs(0) - 1))
    def _():
        wait_out(nb)


def _experts(item_e, item_blk, item_nb, x_sorted, w_gate, w_up, w_down, n_blocks):
    n_items = item_e.shape[0]
    d = D_MODEL
    fc = EXPERT_FC
    x3 = x_sorted
    blk_rows = MOE_BLOCK * SLAB_ROWS
    n_chunks = EXPERT_DIM // fc
    assert n_chunks >= 2, "the last hidden chunk adds onto the accumulator of the earlier ones"

    slot = jnp.arange(ITEM_BLOCKS, dtype=jnp.int32)[:, None]
    x_blk = jnp.maximum(lax.cummax(jnp.where(slot < item_nb[None, :], item_blk[None, :] + slot, -1), axis=1), 0)
    x_blk = x_blk.reshape(-1).astype(jnp.int32)

    def x_spec(s):
        return pl.BlockSpec((blk_rows, SLAB_LANES),
                            lambda w, c, ie, ib, inb, xb: (xb[s * n_items + w], 0))

    def chunk(w, c, inb):
        return jnp.where(inb[w] > 0, c, n_chunks - 1)

    return pl.pallas_call(
        _experts_kernel,
        grid_spec=pltpu.PrefetchScalarGridSpec(
            num_scalar_prefetch=4,
            grid=(n_items, n_chunks),
            in_specs=[x_spec(0), x_spec(1), x_spec(2), x_spec(3),
                      pl.BlockSpec((None, d, fc), lambda w, c, ie, ib, inb, xb: (ie[w], 0, chunk(w, c, inb))),
                      pl.BlockSpec((None, d, fc), lambda w, c, ie, ib, inb, xb: (ie[w], 0, chunk(w, c, inb))),
                      pl.BlockSpec((None, fc, d), lambda w, c, ie, ib, inb, xb: (ie[w], chunk(w, c, inb), 0))],
            out_specs=pl.BlockSpec(memory_space=pl.ANY),
            scratch_shapes=[pltpu.VMEM((ITEM_BLOCKS * MOE_BLOCK, d), BF16),
                            pltpu.VMEM((ITEM_BLOCKS * MOE_BLOCK, d), F32),
                            pltpu.VMEM((ITEM_BLOCKS, blk_rows, SLAB_LANES), I32),
                            pltpu.SemaphoreType.DMA((ITEM_BLOCKS,))]),
        out_shape=jax.ShapeDtypeStruct((n_blocks * blk_rows, SLAB_LANES), I32),
        compiler_params=_params(("arbitrary", "arbitrary"), 56),
        name="experts",
    )(item_e, item_blk, item_nb, x_blk, x3, x3, x3, x3, w_gate, w_up, w_down)


def _combine_kernel(dest_ref, x1_ref, route_ref, mod_ref, gain_ref, y_ref, o_ref, ybuf_ref, sem):
    tm = COMBINE_TM
    i = pl.program_id(0)
    n_tiles = pl.num_programs(0)

    def row_copy(slot, k, r, src):
        return pltpu.make_async_copy(y_ref.at[pl.ds(src * SLAB_ROWS, SLAB_ROWS), :],
                                     ybuf_ref.at[slot, k, pl.ds(r * SLAB_ROWS, SLAB_ROWS), :],
                                     sem.at[slot])

    def wait_slot(slot):
        for k in range(TOP_K):
            pltpu.make_async_copy(y_ref.at[pl.ds(0, tm * SLAB_ROWS), :], ybuf_ref.at[slot, k],
                                  sem.at[slot]).wait()

    def issue_tile(tile, slot):
        base = tile * tm * TOP_K

        def issue(r, carry):
            for k in range(TOP_K):
                row_copy(slot, k, r, dest_ref[base + r * TOP_K + k]).start()
            return carry

        lax.fori_loop(0, tm, issue, 0, unroll=4)

    @pl.when(i == 0)
    def _():
        issue_tile(0, 0)

    @pl.when(i + 1 < n_tiles)
    def _():
        issue_tile(i + 1, (i + 1) % 2)

    slot = i % 2
    wait_slot(slot)

    route = route_ref[...]
    w0 = route[:, 2:3]
    w1 = route[:, 3:4]
    ssq = jnp.zeros((tm, 1), F32)
    for j in range(SLAB_ROWS):
        rows = _slab_rows(j, tm)
        y0 = _unpack_pair(ybuf_ref[slot, 0, rows, :])
        y1 = _unpack_pair(ybuf_ref[slot, 1, rows, :])
        for part, off in ((0, j * LANES), (1, HALF_D + j * LANES)):
            cols = slice(off, off + LANES)
            ffn = w0 * y0[part] + w1 * y1[part]
            x2 = x1_ref[:, cols] + mod_ref[0, 5:6, cols] * ffn
            o_ref[:, cols] = x2
            ssq = ssq + jnp.sum(x2 * x2, axis=-1, keepdims=True)
    o_ref[...] = o_ref[...] * lax.rsqrt(ssq * (1.0 / D_MODEL) + EPS) * gain_ref[...]


def _combine(dest, x1, route, mod6, gain, y_sorted, seq):
    t, d = x1.shape
    tm = COMBINE_TM
    tiles_per_batch = seq // tm
    return pl.pallas_call(
        _combine_kernel,
        grid_spec=pltpu.PrefetchScalarGridSpec(
            num_scalar_prefetch=1,
            grid=(t // tm,),
            in_specs=[pl.BlockSpec((tm, d), lambda i, dst: (i, 0)),
                      pl.BlockSpec((tm, LANES), lambda i, dst: (i, 0)),
                      pl.BlockSpec((1, 6, d), lambda i, dst: (i // tiles_per_batch, 0, 0)),
                      pl.BlockSpec((1, d), lambda i, dst: (0, 0)),
                      pl.BlockSpec(memory_space=pl.ANY)],
            out_specs=pl.BlockSpec((tm, d), lambda i, dst: (i, 0)),
            scratch_shapes=[pltpu.VMEM((2, TOP_K, tm * SLAB_ROWS, SLAB_LANES), I32),
                            pltpu.SemaphoreType.DMA((2,))]),
        out_shape=jax.ShapeDtypeStruct((t, d), F32),
        compiler_params=_params(("arbitrary",), 24),
        name="combine",
    )(dest, x1, route, mod6, gain, y_sorted)


def _dispatch_tables(route, counts, t):
    n_assign = t * TOP_K
    n_pad = -(-(n_assign + N_EXPERTS * (MOE_BLOCK - 1)) // MOE_BLOCK) * MOE_BLOCK
    n_blocks = n_pad // MOE_BLOCK
    n_items = N_EXPERTS + n_assign // (ITEM_BLOCKS * MOE_BLOCK)

    eid = route[:, :TOP_K].astype(jnp.int32)
    rk = route[:, 4:4 + TOP_K].astype(jnp.int32)
    cnt = counts[0, :N_EXPERTS].astype(jnp.int32)
    blocks_e = (cnt + MOE_BLOCK - 1) // MOE_BLOCK
    blk_end = jnp.cumsum(blocks_e)
    blk_start = blk_end - blocks_e
    hot = eid[:, :, None] == jnp.arange(N_EXPERTS, dtype=jnp.int32)
    row_start = jnp.sum(jnp.where(hot, blk_start * MOE_BLOCK, 0), axis=-1)
    dest = (row_start + rk).reshape(-1)
    zstart = (blk_start * MOE_BLOCK + cnt).astype(jnp.int32)
    zcount = (blocks_e * MOE_BLOCK - cnt).astype(jnp.int32)

    items_e = (blocks_e + ITEM_BLOCKS - 1) // ITEM_BLOCKS
    item_end = jnp.cumsum(items_e)
    item_start = item_end - items_e
    w = jnp.arange(n_items, dtype=jnp.int32)
    live = w < item_end[-1]
    w_live = jnp.minimum(w, item_end[-1] - 1)
    e_w = jnp.minimum(jnp.sum((item_end[None, :] <= w_live[:, None]).astype(jnp.int32), axis=1), N_EXPERTS - 1)
    j_w = w_live - item_start[e_w]
    item_blk = (blk_start[e_w] + ITEM_BLOCKS * j_w).astype(jnp.int32)
    item_nb = jnp.where(live, jnp.clip(blocks_e[e_w] - ITEM_BLOCKS * j_w, 0, ITEM_BLOCKS), 0).astype(jnp.int32)
    return dest.astype(jnp.int32), zstart, zcount, e_w, item_blk, item_nb, n_blocks


def kernel(x, c, positions, norm1_gain, norm2_gain, final_norm_gain, w_ada, b_ada, w_in, attn_sinks,
           ret_norm_gain, w_branch_attn, w_branch_ret, w_out, w_router_group, b_router_group,
           w_router_expert, b_router_expert, w_expert_gate, w_expert_up, w_expert_down):
    batch, seq, d = x.shape
    t = batch * seq
    depth = w_ada.shape[0]
    half = RET_DIM // 2
    inv_freq = (ROPE_BASE ** (-jnp.arange(half, dtype=F32) / half)).reshape(1, half)
    pos = positions.reshape(t, 1)
    c8 = jnp.pad(c, ((0, 8 - batch), (0, 0)))
    xf = x.reshape(t, d)

    assert depth == 1, "the fused final norm assumes a single layer"
    for layer in range(depth):
        mod6 = _ada(c8, w_ada[layer], b_ada[layer].reshape(1, -1))[:batch].reshape(batch, 6, d)
        proj, proj_kv = _proj(xf, norm1_gain[layer].reshape(1, d), mod6, w_in[layer], seq)
        attn = _attn(proj, proj_kv, attn_sinks[layer], batch, seq)
        ret = _ret(proj, pos, inv_freq, ret_norm_gain[layer].reshape(1, d), batch, seq)

        pad = LANES - N_GROUPS - N_EXPERTS
        w_rt = jnp.concatenate([w_router_group[layer], w_router_expert[layer],
                                jnp.zeros((d, pad), F32)], axis=1)
        b_rt = jnp.concatenate([b_router_group[layer], b_router_expert[layer],
                                jnp.zeros((pad,), F32)]).reshape(1, LANES)
        merged = _branch(attn, ret, proj, w_branch_attn[layer], w_branch_ret[layer])
        x1, h2, route, counts = _mixout(merged, xf, mod6, norm2_gain[layer].reshape(1, d),
                                        w_out[layer].astype(BF16), w_rt, b_rt, seq)
        dest, zstart, zcount, item_e, item_blk, item_nb, n_blocks = _dispatch_tables(route, counts, t)
        x_sorted = _dispatch(dest, zstart, zcount, h2, n_blocks * MOE_BLOCK)
        y_sorted = _experts(item_e, item_blk, item_nb, x_sorted,
                            w_expert_gate[layer], w_expert_up[layer], w_expert_down[layer], n_blocks)
        xf = _combine(dest, x1, route, mod6, final_norm_gain.reshape(1, d), y_sorted, seq)
    return xf.reshape(batch, seq, d)
```

```python
import functools
import math

import jax
import jax.numpy as jnp
import numpy as np
from jax import lax
from jax.experimental import pallas as pl
from jax.experimental.pallas import tpu as pltpu

F32 = jnp.float32
BF16 = jnp.bfloat16

D_MODEL = 2048
ATTN_HEAD_DIM = 64
ATTN_HEADS = 32
ATTN_KV_HEADS = 4
ATTN_GROUP = 8
WINDOW = 128
RET_HEADS = 8
RET_DIM = 256
RET_CHUNK = 128
ROPE_BASE = 10000.0
N_GROUPS = 4
EXPERTS_PER_GROUP = 16
N_EXPERTS = 64
TOP_K = 2
EXPERT_DIM = 1024
MOE_BLOCK = 128
EPS = 1e-6
NEG = -1e30

MIB = 1024 * 1024
LANES = 128
PROJ_TN = 512
PROJ_TM = 1024
KV_SRC_TILE = 4
BRANCH_TM = 1024
BRANCH_TN = 512
MIXOUT_TM = 512
ITEM_BLOCKS = 4
EXPERT_FC = 512
COMBINE_TM = 128
SLAB_ROWS = 8
SLAB_LANES = LANES
HALF_D = D_MODEL // 2
ROWS_PER_WAIT = 128
I32 = jnp.int32

COL_QA, COL_QR, COL_KR, COL_VR, COL_GR, COL_GA, COL_GRT = 0, 1, 2, 3, 4, 5, 6

LOG_GAMMA = [math.log1p(-(2.0 ** (-5.0 - h))) for h in range(RET_HEADS)]


def _params(sem, vmem_mib):
    return pltpu.CompilerParams(dimension_semantics=sem, vmem_limit_bytes=vmem_mib * MIB)


def _pack_pair(lo, hi):
    lo_b = lax.bitcast_convert_type(lo.astype(BF16).astype(F32), I32)
    hi_b = lax.bitcast_convert_type(hi.astype(BF16).astype(F32), I32)
    return hi_b | lax.shift_right_logical(lo_b, jnp.full_like(lo_b, 16))


def _unpack_pair(w):
    lo = lax.bitcast_convert_type(w << 16, F32)
    hi = lax.bitcast_convert_type(w & jnp.int32(-65536), F32)
    return lo, hi


def _slab_rows(j, n_tokens):
    return pl.ds(j, n_tokens, stride=SLAB_ROWS)


def _ada_kernel(c_ref, w_ref, b_ref, o_ref):
    c = c_ref[...]
    a = (c * jax.nn.sigmoid(c)).astype(BF16)
    o_ref[...] = jnp.dot(a, w_ref[...].astype(BF16), preferred_element_type=F32) + b_ref[...]


def _ada(c8, w_ada, b_ada):
    n = w_ada.shape[1]
    tn = 1024
    return pl.pallas_call(
        _ada_kernel,
        grid=(n // tn,),
        in_specs=[pl.BlockSpec((8, D_MODEL), lambda j: (0, 0)),
                  pl.BlockSpec((D_MODEL, tn), lambda j: (0, j)),
                  pl.BlockSpec((1, tn), lambda j: (0, j))],
        out_specs=pl.BlockSpec((8, tn), lambda j: (0, j)),
        out_shape=jax.ShapeDtypeStruct((8, n), F32),
        compiler_params=_params(("arbitrary",), 40),
        name="ada",
    )(c8, w_ada, b_ada)


def _proj_kernel(x_ref, g_ref, mod_ref, wlo_ref, whi_ref, o_ref, kv_ref, h_ref):
    v = pl.program_id(1)
    last = pl.num_programs(1) - 1

    @pl.when(v == 0)
    def _():
        x = x_ref[...]
        var = jnp.mean(x * x, axis=-1, keepdims=True)
        y = x * lax.rsqrt(var + EPS) * g_ref[...]
        h_ref[...] = (y * (1.0 + mod_ref[0, 1:2, :]) + mod_ref[0, 0:1, :]).astype(BF16)

    @pl.when(v < last)
    def _():
        w = jnp.concatenate([wlo_ref[...].astype(BF16), whi_ref[...].astype(BF16)], axis=1)
        o_ref[...] = jnp.dot(h_ref[...], w, preferred_element_type=F32).astype(BF16)

    @pl.when(v == last)
    def _():
        kv_ref[...] = jnp.dot(h_ref[...], wlo_ref[...].astype(BF16),
                              preferred_element_type=F32).astype(BF16)


def _proj_w_tile(v, n_wide):
    return jnp.where(v < 2, 2 * v, jnp.where(v < n_wide, 2 * v + 1, KV_SRC_TILE))


def _proj(x2, gain, mod6, w_in, seq):
    t = x2.shape[0]
    n = w_in.shape[1]
    tiles_per_batch = seq // PROJ_TM
    n_wide = (n - PROJ_TN) // (2 * PROJ_TN)
    return pl.pallas_call(
        _proj_kernel,
        grid=(t // PROJ_TM, n_wide + 1),
        in_specs=[pl.BlockSpec((PROJ_TM, D_MODEL), lambda i, v: (i, 0)),
                  pl.BlockSpec((1, D_MODEL), lambda i, v: (0, 0)),
                  pl.BlockSpec((1, 6, D_MODEL), lambda i, v: (i // tiles_per_batch, 0, 0)),
                  pl.BlockSpec((D_MODEL, PROJ_TN), lambda i, v: (0, _proj_w_tile(v, n_wide))),
                  pl.BlockSpec((D_MODEL, PROJ_TN),
                               lambda i, v: (0, jnp.where(v < n_wide, _proj_w_tile(v, n_wide) + 1, KV_SRC_TILE)))],
        out_specs=[pl.BlockSpec((PROJ_TM, 2 * PROJ_TN), lambda i, v: (i, jnp.minimum(v, n_wide - 1))),
                   pl.BlockSpec((PROJ_TM, PROJ_TN), lambda i, v: (i, 0))],
        out_shape=[jax.ShapeDtypeStruct((t, n - PROJ_TN), BF16),
                   jax.ShapeDtypeStruct((t, PROJ_TN), BF16)],
        scratch_shapes=[pltpu.VMEM((PROJ_TM, D_MODEL), BF16)],
        compiler_params=_params(("arbitrary", "arbitrary"), 56),
        name="proj",
    )(x2, gain, mod6, w_in, w_in)


def _attn_kernel(sink_ref, q_ref, kvp_ref, kvc_ref, o_ref):
    n = pl.program_id(1)
    kvp = kvp_ref[...]
    kvc = kvc_ref[...]
    qi = lax.broadcasted_iota(jnp.int32, (WINDOW, WINDOW), 0)
    sj = lax.broadcasted_iota(jnp.int32, (WINDOW, WINDOW), 1)
    valid_prev = (sj > qi) & (n > 0)
    valid_cur = sj <= qi
    sink_col = sj == 0
    first_row = lax.broadcasted_iota(jnp.int32, (2 * WINDOW, 1), 0) == 0
    dh = ATTN_HEAD_DIM
    kv_w = ATTN_KV_HEADS * dh
    scale = jnp.asarray(dh ** -0.5, BF16)
    for kv in range(ATTN_KV_HEADS):
        kband = jnp.concatenate([kvp[:, kv * dh:(kv + 1) * dh],
                                 kvc[:, kv * dh:(kv + 1) * dh]], axis=0) * scale
        vband = jnp.concatenate([kvp[:, kv_w + kv * dh:kv_w + (kv + 1) * dh],
                                 kvc[:, kv_w + kv * dh:kv_w + (kv + 1) * dh]], axis=0)
        vband = jnp.where(first_row, jnp.zeros_like(vband), vband)
        zeros = jnp.zeros_like(kband)
        ones = jnp.ones_like(vband)
        k_pad = (jnp.concatenate([kband, zeros], axis=1), jnp.concatenate([zeros, kband], axis=1))
        pv_rhs = jnp.concatenate(
            [jnp.concatenate([vband, zeros, ones, zeros], axis=1),
             jnp.concatenate([zeros, vband, zeros, ones], axis=1)], axis=0)
        def pair_scores(pair):
            h0 = kv * ATTN_GROUP + 2 * pair
            q_pair = q_ref[:, h0 * dh:(h0 + 2) * dh]
            return [lax.dot_general(q_pair, k_pad[idx], (((1,), (1,)), ((), ())),
                                    preferred_element_type=F32) for idx in range(2)]

        s_next = pair_scores(0)
        for pair in range(ATTN_GROUP // 2):
            h0 = kv * ATTN_GROUP + 2 * pair
            s_pair = s_next
            if pair + 1 < ATTN_GROUP // 2:
                s_next = pair_scores(pair + 1)
            probs = []
            for idx in range(2):
                s = s_pair[idx]
                sink = sink_ref[h0 + idx]
                s_prev = jnp.where(sink_col, sink, jnp.where(valid_prev, s[:, :WINDOW], NEG))
                s_cur = jnp.where(valid_cur, s[:, WINDOW:], NEG)
                m = jnp.max(jnp.maximum(s_prev, s_cur), axis=-1, keepdims=True)
                probs += [jnp.exp(s_prev - m).astype(BF16), jnp.exp(s_cur - m).astype(BF16)]
            r = jnp.dot(jnp.concatenate(probs, axis=-1), pv_rhs, preferred_element_type=F32)
            o_ref[:, h0 * dh:(h0 + 2) * dh] = (r[:, :2 * dh] * (1.0 / r[:, 2 * dh:])).astype(BF16)


def _attn(proj, proj_kv, sinks, batch, seq):
    nb = seq // WINDOW
    t = batch * seq

    def cur(b, n):
        return (b * nb + n, 0)

    def prev(b, n):
        return (b * nb + jnp.maximum(n - 1, 0), 0)

    return pl.pallas_call(
        _attn_kernel,
        grid=(batch, nb),
        in_specs=[pl.BlockSpec(memory_space=pltpu.SMEM),
                  pl.BlockSpec((WINDOW, D_MODEL), lambda b, n: (b * nb + n, COL_QA)),
                  pl.BlockSpec((WINDOW, PROJ_TN), prev),
                  pl.BlockSpec((WINDOW, PROJ_TN), cur)],
        out_specs=pl.BlockSpec((WINDOW, D_MODEL), lambda b, n: (b * nb + n, 0)),
        out_shape=jax.ShapeDtypeStruct((t, D_MODEL), BF16),
        compiler_params=_params(("arbitrary", "arbitrary"), 32),
        name="attn",
    )(sinks, proj, proj_kv, proj_kv)


def _ret_decay_tables():
    lg = np.asarray(LOG_GAMMA, np.float64)[:, None, None]
    i = np.arange(RET_CHUNK, dtype=np.float64)
    diff = i[:, None] - i[None, :]
    k_scale = RET_DIM ** -0.5
    d_intra = np.where(diff >= 0, np.exp(np.maximum(diff, 0.0) * lg), 0.0) * k_scale
    lanes = np.ones((1, 1, RET_DIM // 2))
    d_q = np.exp((i[None, :, None] + 1.0) * lg) * lanes
    d_k = np.exp((RET_CHUNK - 1.0 - i[None, :, None]) * lg) * k_scale * lanes
    return jnp.asarray(d_intra, F32), jnp.asarray(d_q, F32), jnp.asarray(d_k, BF16)


def _ret_kernel(pos_ref, invf_ref, di_ref, dq_ref, dk_ref, q_ref, k_ref, v_ref, g_ref, gain_ref,
                o_ref, state_ref):
    @pl.when(pl.program_id(1) == 0)
    def _():
        state_ref[...] = jnp.zeros_like(state_ref)

    half = RET_DIM // 2
    ang = pos_ref[...].astype(F32) * invf_ref[...]
    cos = jnp.cos(ang).astype(BF16)
    sin = jnp.sin(ang).astype(BF16)

    def rot(t):
        t1, t2 = t[:, :half], t[:, half:]
        return jnp.concatenate([t1 * cos - t2 * sin, t1 * sin + t2 * cos], axis=-1)

    def both_halves(t, factor):
        return jnp.concatenate([t[:, :half] * factor, t[:, half:] * factor], axis=-1)

    for h in range(RET_HEADS):
        sl = slice(h * RET_DIM, (h + 1) * RET_DIM)
        qb = rot(q_ref[:, sl])
        kb = rot(k_ref[:, sl])
        vb = v_ref[:, sl]
        d_chunk = math.exp(RET_CHUNK * LOG_GAMMA[h])
        intra = lax.dot_general(qb, kb, (((1,), (1,)), ((), ())),
                                preferred_element_type=F32) * di_ref[h]
        st = state_ref[h]
        cross = jnp.dot(qb, st.astype(BF16), preferred_element_type=F32)
        o = jnp.dot(intra.astype(BF16), vb, preferred_element_type=F32) + both_halves(cross, dq_ref[h])
        kd = both_halves(kb, dk_ref[h])
        state_ref[h] = st * d_chunk + lax.dot_general(kd, vb, (((0,), (0,)), ((), ())),
                                                      preferred_element_type=F32)
        o = o * lax.rsqrt(jnp.mean(o * o, axis=-1, keepdims=True) + EPS) * gain_ref[:, sl]
        gg = g_ref[:, sl].astype(F32)
        o_ref[:, sl] = (gg * jax.nn.sigmoid(gg) * o).astype(BF16)


def _ret(proj, pos, inv_freq, ret_gain, batch, seq):
    nc = seq // RET_CHUNK
    t = batch * seq

    def col(cb):
        return lambda b, c: (b * nc + c, cb)

    d_intra, d_q, d_k = _ret_decay_tables()
    table = lambda lanes: pl.BlockSpec((RET_HEADS, RET_CHUNK, lanes), lambda b, c: (0, 0, 0))
    return pl.pallas_call(
        _ret_kernel,
        grid=(batch, nc),
        in_specs=[pl.BlockSpec((RET_CHUNK, 1), lambda b, c: (b * nc + c, 0)),
                  pl.BlockSpec((1, RET_DIM // 2), lambda b, c: (0, 0)),
                  table(RET_CHUNK), table(RET_DIM // 2), table(RET_DIM // 2),
                  pl.BlockSpec((RET_CHUNK, D_MODEL), col(COL_QR)),
                  pl.BlockSpec((RET_CHUNK, D_MODEL), col(COL_KR)),
                  pl.BlockSpec((RET_CHUNK, D_MODEL), col(COL_VR)),
                  pl.BlockSpec((RET_CHUNK, D_MODEL), col(COL_GR)),
                  pl.BlockSpec((1, D_MODEL), lambda b, c: (0, 0))],
        out_specs=pl.BlockSpec((RET_CHUNK, D_MODEL), lambda b, c: (b * nc + c, 0)),
        out_shape=jax.ShapeDtypeStruct((t, D_MODEL), BF16),
        scratch_shapes=[pltpu.VMEM((RET_HEADS, RET_DIM, RET_DIM), F32)],
        compiler_params=_params(("arbitrary", "arbitrary"), 32),
        name="ret",
    )(pos, inv_freq, d_intra, d_q, d_k, proj, proj, proj, proj, ret_gain)


def _route(logits):
    lane = lax.broadcasted_iota(jnp.int32, logits.shape, 1)
    lane_f = lane.astype(F32)
    is_g = lane < N_GROUPS
    gl = jnp.where(is_g, logits, NEG)
    gmax = jnp.max(gl, axis=-1, keepdims=True)
    gsel = jnp.min(jnp.where(gl == gmax, lane_f, float(LANES)), axis=-1, keepdims=True)
    gsum = jnp.sum(jnp.where(is_g, jnp.exp(gl - gmax), 0.0), axis=-1, keepdims=True)
    g_w = 1.0 / gsum
    grp = ((lane - N_GROUPS) >> 4).astype(F32)
    is_e = (lane >= N_GROUPS) & (lane < N_GROUPS + N_EXPERTS) & (grp == gsel)
    el = jnp.where(is_e, logits, NEG)
    v1 = jnp.max(el, axis=-1, keepdims=True)
    i1 = jnp.min(jnp.where(el == v1, lane_f, float(LANES)), axis=-1, keepdims=True)
    el2 = jnp.where(lane_f == i1, NEG, el)
    v2 = jnp.max(el2, axis=-1, keepdims=True)
    i2 = jnp.min(jnp.where(el2 == v2, lane_f, float(LANES)), axis=-1, keepdims=True)
    tt = jnp.exp(v2 - v1)
    w1 = g_w / (1.0 + tt)
    w2 = g_w * tt / (1.0 + tt)
    return jnp.where(lane == 0, i1 - N_GROUPS,
                     jnp.where(lane == 1, i2 - N_GROUPS,
                               jnp.where(lane == 2, w1, jnp.where(lane == 3, w2, 0.0))))


def _branch_kernel(attn_ref, ret_ref, ga_ref, gr_ref, wa_ref, wr_ref, o_ref):
    a = jnp.dot(attn_ref[...], wa_ref[...].astype(BF16), preferred_element_type=F32)
    r = jnp.dot(ret_ref[...], wr_ref[...].astype(BF16), preferred_element_type=F32)
    o_ref[...] = (jax.nn.sigmoid(ga_ref[...].astype(F32)) * a
                  + jax.nn.sigmoid(gr_ref[...].astype(F32)) * r).astype(BF16)


def _branch(attn, ret, proj, wa, wr):
    t = attn.shape[0]
    tm, tn = BRANCH_TM, BRANCH_TN
    per_slab = D_MODEL // tn
    return pl.pallas_call(
        _branch_kernel,
        grid=(t // tm, D_MODEL // tn),
        in_specs=[pl.BlockSpec((tm, D_MODEL), lambda i, j: (i, 0)),
                  pl.BlockSpec((tm, D_MODEL), lambda i, j: (i, 0)),
                  pl.BlockSpec((tm, tn), lambda i, j: (i, COL_GA * per_slab + j)),
                  pl.BlockSpec((tm, tn), lambda i, j: (i, COL_GRT * per_slab + j)),
                  pl.BlockSpec((D_MODEL, tn), lambda i, j: (0, j)),
                  pl.BlockSpec((D_MODEL, tn), lambda i, j: (0, j))],
        out_specs=pl.BlockSpec((tm, tn), lambda i, j: (i, j)),
        out_shape=jax.ShapeDtypeStruct((t, D_MODEL), BF16),
        compiler_params=_params(("arbitrary", "arbitrary"), 48),
        name="branch",
    )(attn, ret, proj, proj, wa, wr)


def _mixout_kernel(m_ref, x_ref, mod_ref, g2_ref, wo_ref, wrt_ref, brt_ref,
                   x1_ref, h2_ref, route_ref, count_ref, meta_ref, carry_ref):
    @pl.when(pl.program_id(0) == 0)
    def _():
        carry_ref[...] = jnp.zeros_like(carry_ref)

    mix = jnp.dot(m_ref[...], wo_ref[...], preferred_element_type=F32)
    x1 = x_ref[...] + mod_ref[0, 2:3, :] * mix
    x1_ref[...] = x1
    var = jnp.mean(x1 * x1, axis=-1, keepdims=True)
    h2 = x1 * lax.rsqrt(var + EPS) * g2_ref[...]
    h2 = h2 * (1.0 + mod_ref[0, 4:5, :]) + mod_ref[0, 3:4, :]
    tm = h2.shape[0]
    for j in range(SLAB_ROWS):
        lo = h2[:, j * LANES:(j + 1) * LANES]
        hi = h2[:, HALF_D + j * LANES:HALF_D + (j + 1) * LANES]
        h2_ref[_slab_rows(j, tm), :] = _pack_pair(lo, hi)
    h_hi = h2.astype(BF16)
    h_lo = (h2 - h_hi.astype(F32)).astype(BF16)
    w_rt = wrt_ref[...]
    w_hi = w_rt.astype(BF16)
    w_lo = (w_rt - w_hi.astype(F32)).astype(BF16)
    hi_both = jnp.dot(h_hi, jnp.concatenate([w_hi, w_lo], axis=1), preferred_element_type=F32)
    logits = (hi_both[:, :LANES] + hi_both[:, LANES:]
              + jnp.dot(h_lo, w_hi, preferred_element_type=F32) + brt_ref[...])
    route = _route(logits)

    lane = lax.broadcasted_iota(jnp.int32, route.shape, 1)
    lane_f = lane.astype(F32)
    hot1 = lane_f == route[:, 0:1]
    hot2 = lane_f == route[:, 1:2]
    both = jnp.where(hot1 | hot2, 1.0, 0.0)
    ii = lax.broadcasted_iota(jnp.int32, (tm, tm), 0)
    jj = lax.broadcasted_iota(jnp.int32, (tm, tm), 1)
    lower = jnp.where(ii > jj, 1.0, 0.0).astype(BF16)
    before = jnp.dot(lower, both.astype(BF16), preferred_element_type=F32) + carry_ref[...]
    r1 = jnp.sum(jnp.where(hot1, before, 0.0), axis=-1, keepdims=True)
    r2 = jnp.sum(jnp.where(hot2, before, 0.0), axis=-1, keepdims=True)
    route = jnp.where(lane == 4, r1, jnp.where(lane == 5, r2, route))
    route_ref[...] = route
    meta_ref[...] = route.T[0:SLAB_ROWS, :].astype(I32)
    carry = carry_ref[...] + jnp.sum(both, axis=0, keepdims=True)
    carry_ref[...] = carry
    count_ref[...] = carry


def _mixout(merged, x2, mod6, gain2, wo, w_rt, b_rt, seq):
    t = x2.shape[0]
    tm = MIXOUT_TM
    tiles_per_batch = seq // tm
    row = lambda i: (i, 0)
    const = lambda i: (0, 0)
    return pl.pallas_call(
        _mixout_kernel,
        grid=(t // tm,),
        in_specs=[pl.BlockSpec((tm, D_MODEL), row),
                  pl.BlockSpec((tm, D_MODEL), row),
                  pl.BlockSpec((1, 6, D_MODEL), lambda i: (i // tiles_per_batch, 0, 0)),
                  pl.BlockSpec((1, D_MODEL), const),
                  pl.BlockSpec((D_MODEL, D_MODEL), const, pipeline_mode=pl.Buffered(1)),
                  pl.BlockSpec((D_MODEL, LANES), const),
                  pl.BlockSpec((1, LANES), const)],
        out_specs=[pl.BlockSpec((tm, D_MODEL), row),
                   pl.BlockSpec((tm * SLAB_ROWS, SLAB_LANES), row),
                   pl.BlockSpec((tm, LANES), row),
                   pl.BlockSpec((1, LANES), const),
                   pl.BlockSpec((SLAB_ROWS, tm), lambda i: (0, i))],
        out_shape=[jax.ShapeDtypeStruct((t, D_MODEL), F32),
                   jax.ShapeDtypeStruct((t * SLAB_ROWS, SLAB_LANES), I32),
                   jax.ShapeDtypeStruct((t, LANES), F32),
                   jax.ShapeDtypeStruct((1, LANES), F32),
                   jax.ShapeDtypeStruct((SLAB_ROWS, t), I32)],
        scratch_shapes=[pltpu.VMEM((1, LANES), F32)],
        compiler_params=_params(("arbitrary",), 56),
        name="mixout",
    )(merged, x2, mod6, gain2, wo, w_rt, b_rt)


PAD_BITS = (64, 32, 16, 8, 4, 2, 1)


def _sorted_row(tables, tok, k):
    e_refs, r_refs, blk_row_ref = tables[0:TOP_K], tables[TOP_K:2 * TOP_K], tables[2 * TOP_K]
    return blk_row_ref[e_refs[k][tok]] + r_refs[k][tok]


N_ROUTE_TABLES = 2 * TOP_K + 1


def _dispatch_kernel(*refs):
    tables = refs[:N_ROUTE_TABLES]
    zstart_ref, zcount_ref, h2_ref, xs_ref, dest_ref, zero_ref, sem, zsem = refs[N_ROUTE_TABLES:]
    n_assign = tables[0].shape[0] * TOP_K
    zero_ref[...] = jnp.zeros_like(zero_ref)

    def zero_copy(start, rows):
        return pltpu.make_async_copy(zero_ref.at[pl.ds(0, rows * SLAB_ROWS), :],
                                     xs_ref.at[pl.ds(start * SLAB_ROWS, rows * SLAB_ROWS), :], zsem)

    def fill(e, wait):
        start = zstart_ref[e]
        pad = zcount_ref[e]
        for bit in PAD_BITS:
            @pl.when((pad & bit) != 0)
            def _(start=start, bit=bit):
                cp = zero_copy(start, bit)
                cp.wait() if wait else cp.start()
            start = start + (pad & bit)

    lax.fori_loop(0, N_EXPERTS, lambda e, c: (fill(e, False), c)[1], 0)

    def issue(tok, carry):
        src = h2_ref.at[pl.ds(pl.multiple_of(tok * SLAB_ROWS, SLAB_ROWS), SLAB_ROWS), :]
        for k in range(TOP_K):
            row = _sorted_row(tables, tok, k)
            dest_ref[tok * TOP_K + k] = row
            dst = pl.multiple_of(row * SLAB_ROWS, SLAB_ROWS)
            pltpu.make_async_copy(src, xs_ref.at[pl.ds(dst, SLAB_ROWS), :], sem).start()
        return carry

    lax.fori_loop(0, n_assign // TOP_K, issue, 0, unroll=8)

    def drain(i, carry):
        pltpu.make_async_copy(h2_ref.at[pl.ds(0, ROWS_PER_WAIT * SLAB_ROWS), :],
                              xs_ref.at[pl.ds(0, ROWS_PER_WAIT * SLAB_ROWS), :], sem).wait()
        return carry

    lax.fori_loop(0, n_assign // ROWS_PER_WAIT, drain, 0)
    lax.fori_loop(0, N_EXPERTS, lambda e, c: (fill(e, True), c)[1], 0)


def _dispatch(tables, zstart, zcount, h2_slab, n_pad):
    n_assign = tables[0].shape[0] * TOP_K
    return pl.pallas_call(
        _dispatch_kernel,
        grid_spec=pltpu.PrefetchScalarGridSpec(
            num_scalar_prefetch=N_ROUTE_TABLES + 2,
            grid=(1,),
            in_specs=[pl.BlockSpec(memory_space=pltpu.HBM)],
            out_specs=[pl.BlockSpec(memory_space=pl.ANY),
                       pl.BlockSpec(memory_space=pltpu.SMEM)],
            scratch_shapes=[pltpu.VMEM((PAD_BITS[0] * SLAB_ROWS, SLAB_LANES), I32),
                            pltpu.SemaphoreType.DMA(()),
                            pltpu.SemaphoreType.DMA(())]),
        out_shape=[jax.ShapeDtypeStruct((n_pad * SLAB_ROWS, SLAB_LANES), I32),
                   jax.ShapeDtypeStruct((n_assign,), I32)],
        compiler_params=_params(("arbitrary",), 16),
        name="dispatch",
    )(*tables, zstart, zcount, h2_slab)


def _experts_kernel(item_e_ref, item_blk_ref, item_nb_ref, x_blk_ref,
                    x0_ref, x1_ref, x2_ref, x3_ref, wg_ref, wu_ref, wd_ref, y_ref,
                    xb_ref, acc_ref, yp_ref, sem):
    w = pl.program_id(0)
    c = pl.program_id(1)
    last_c = pl.num_programs(1) - 1
    nb = item_nb_ref[w]
    blk0 = item_blk_ref[w]
    x_refs = (x0_ref, x1_ref, x2_ref, x3_ref)

    blk_rows = MOE_BLOCK * SLAB_ROWS

    def out_copy(s):
        return pltpu.make_async_copy(yp_ref.at[s], y_ref.at[pl.ds((blk0 + s) * blk_rows, blk_rows), :],
                                     sem.at[s])

    def wait_out(count):
        for s in range(ITEM_BLOCKS):
            @pl.when(s < count)
            def _(s=s):
                out_copy(s).wait()

    @pl.when((c == last_c) & (w > 0))
    def _():
        wait_out(item_nb_ref[jnp.maximum(w - 1, 0)])

    def run(n_live):
        rows = n_live * MOE_BLOCK

        @pl.when(c == 0)
        def _():
            for s in range(n_live):
                for j in range(SLAB_ROWS):
                    lo, hi = _unpack_pair(x_refs[s][_slab_rows(j, MOE_BLOCK), :])
                    r0 = s * MOE_BLOCK
                    xb_ref[r0:r0 + MOE_BLOCK, j * LANES:(j + 1) * LANES] = lo.astype(BF16)
                    xb_ref[r0:r0 + MOE_BLOCK, HALF_D + j * LANES:HALF_D + (j + 1) * LANES] = hi.astype(BF16)

        xs = xb_ref[0:rows, :]
        g = jnp.dot(xs, wg_ref[...].astype(BF16), preferred_element_type=F32)
        u = jnp.dot(xs, wu_ref[...].astype(BF16), preferred_element_type=F32)
        hid = (g * jax.nn.sigmoid(g) * u).astype(BF16)
        y = jnp.dot(hid, wd_ref[...].astype(BF16), preferred_element_type=F32)

        @pl.when(c == 0)
        def _():
            acc_ref[0:rows, :] = y

        @pl.when((c > 0) & (c < last_c))
        def _():
            acc_ref[0:rows, :] += y

        @pl.when(c == last_c)
        def _():
            for s in range(n_live):
                r0 = s * MOE_BLOCK
                for j in range(SLAB_ROWS):
                    lo_cols = slice(j * LANES, (j + 1) * LANES)
                    hi_cols = slice(HALF_D + j * LANES, HALF_D + (j + 1) * LANES)
                    yp_ref[s, _slab_rows(j, MOE_BLOCK), :] = _pack_pair(
                        acc_ref[r0:r0 + MOE_BLOCK, lo_cols] + y[r0:r0 + MOE_BLOCK, lo_cols],
                        acc_ref[r0:r0 + MOE_BLOCK, hi_cols] + y[r0:r0 + MOE_BLOCK, hi_cols])
                out_copy(s).start()

    for n_live in range(1, ITEM_BLOCKS + 1):
        pl.when(nb == n_live)(functools.partial(run, n_live))

    @pl.when((c == last_c) & (w == pl.num_programs(0) - 1))
    def _():
        wait_out(nb)


def _experts(item_e, item_blk, item_nb, x_sorted, w_gate, w_up, w_down, n_blocks):
    n_items = item_e.shape[0]
    d = D_MODEL
    fc = EXPERT_FC
    x3 = x_sorted
    blk_rows = MOE_BLOCK * SLAB_ROWS
    n_chunks = EXPERT_DIM // fc
    assert n_chunks >= 2, "the last hidden chunk adds onto the accumulator of the earlier ones"

    slot = jnp.arange(ITEM_BLOCKS, dtype=jnp.int32)[:, None]
    x_blk = jnp.maximum(lax.cummax(jnp.where(slot < item_nb[None, :], item_blk[None, :] + slot, -1), axis=1), 0)
    x_blk = x_blk.reshape(-1).astype(jnp.int32)

    def x_spec(s):
        return pl.BlockSpec((blk_rows, SLAB_LANES),
                            lambda w, c, ie, ib, inb, xb: (xb[s * n_items + w], 0))

    def chunk(w, c, inb):
        return jnp.where(inb[w] > 0, c, n_chunks - 1)

    return pl.pallas_call(
        _experts_kernel,
        grid_spec=pltpu.PrefetchScalarGridSpec(
            num_scalar_prefetch=4,
            grid=(n_items, n_chunks),
            in_specs=[x_spec(0), x_spec(1), x_spec(2), x_spec(3),
                      pl.BlockSpec((None, d, fc), lambda w, c, ie, ib, inb, xb: (ie[w], 0, chunk(w, c, inb))),
                      pl.BlockSpec((None, d, fc), lambda w, c, ie, ib, inb, xb: (ie[w], 0, chunk(w, c, inb))),
                      pl.BlockSpec((None, fc, d), lambda w, c, ie, ib, inb, xb: (ie[w], chunk(w, c, inb), 0))],
            out_specs=pl.BlockSpec(memory_space=pl.ANY),
            scratch_shapes=[pltpu.VMEM((ITEM_BLOCKS * MOE_BLOCK, d), BF16),
                            pltpu.VMEM((ITEM_BLOCKS * MOE_BLOCK, d), F32),
                            pltpu.VMEM((ITEM_BLOCKS, blk_rows, SLAB_LANES), I32),
                            pltpu.SemaphoreType.DMA((ITEM_BLOCKS,))]),
        out_shape=jax.ShapeDtypeStruct((n_blocks * blk_rows, SLAB_LANES), I32),
        compiler_params=_params(("arbitrary", "arbitrary"), 56),
        name="experts",
    )(item_e, item_blk, item_nb, x_blk, x3, x3, x3, x3, w_gate, w_up, w_down)


def _combine_kernel(dest_ref, x1_ref, route_ref, mod_ref, gain_ref, y_ref, o_ref, ybuf_ref, sem):
    tm = COMBINE_TM
    i = pl.program_id(0)
    n_tiles = pl.num_programs(0)

    def row_copy(slot, k, r, src):
        return pltpu.make_async_copy(y_ref.at[pl.ds(src * SLAB_ROWS, SLAB_ROWS), :],
                                     ybuf_ref.at[slot, k, pl.ds(r * SLAB_ROWS, SLAB_ROWS), :],
                                     sem.at[slot])

    def wait_slot(slot):
        for k in range(TOP_K):
            pltpu.make_async_copy(y_ref.at[pl.ds(0, tm * SLAB_ROWS), :], ybuf_ref.at[slot, k],
                                  sem.at[slot]).wait()

    def issue_tile(tile, slot):
        base = tile * tm * TOP_K

        def issue(r, carry):
            for k in range(TOP_K):
                row_copy(slot, k, r, dest_ref[base + r * TOP_K + k]).start()
            return carry

        lax.fori_loop(0, tm, issue, 0, unroll=4)

    @pl.when(i == 0)
    def _():
        issue_tile(0, 0)

    @pl.when(i + 1 < n_tiles)
    def _():
        issue_tile(i + 1, (i + 1) % 2)

    slot = i % 2
    wait_slot(slot)

    route = route_ref[...]
    w0 = route[:, 2:3]
    w1 = route[:, 3:4]
    ssq = jnp.zeros((tm, 1), F32)
    for j in range(SLAB_ROWS):
        rows = _slab_rows(j, tm)
        y0 = _unpack_pair(ybuf_ref[slot, 0, rows, :])
        y1 = _unpack_pair(ybuf_ref[slot, 1, rows, :])
        for part, off in ((0, j * LANES), (1, HALF_D + j * LANES)):
            cols = slice(off, off + LANES)
            ffn = w0 * y0[part] + w1 * y1[part]
            x2 = x1_ref[:, cols] + mod_ref[0, 5:6, cols] * ffn
            o_ref[:, cols] = x2
            ssq = ssq + jnp.sum(x2 * x2, axis=-1, keepdims=True)
    o_ref[...] = o_ref[...] * lax.rsqrt(ssq * (1.0 / D_MODEL) + EPS) * gain_ref[...]


def _combine(dest, x1, route, mod6, gain, y_sorted, seq):
    t, d = x1.shape
    tm = COMBINE_TM
    tiles_per_batch = seq // tm
    return pl.pallas_call(
        _combine_kernel,
        grid_spec=pltpu.PrefetchScalarGridSpec(
            num_scalar_prefetch=1,
            grid=(t // tm,),
            in_specs=[pl.BlockSpec((tm, d), lambda i, *_: (i, 0)),
                      pl.BlockSpec((tm, LANES), lambda i, *_: (i, 0)),
                      pl.BlockSpec((1, 6, d), lambda i, *_: (i // tiles_per_batch, 0, 0)),
                      pl.BlockSpec((1, d), lambda i, *_: (0, 0)),
                      pl.BlockSpec(memory_space=pl.ANY)],
            out_specs=pl.BlockSpec((tm, d), lambda i, *_: (i, 0)),
            scratch_shapes=[pltpu.VMEM((2, TOP_K, tm * SLAB_ROWS, SLAB_LANES), I32),
                            pltpu.SemaphoreType.DMA((2,))]),
        out_shape=jax.ShapeDtypeStruct((t, d), F32),
        compiler_params=_params(("arbitrary",), 24),
        name="combine",
    )(dest, x1, route, mod6, gain, y_sorted)


def _dispatch_tables(meta, counts, t):
    n_assign = t * TOP_K
    n_pad = -(-(n_assign + N_EXPERTS * (MOE_BLOCK - 1)) // MOE_BLOCK) * MOE_BLOCK
    n_blocks = n_pad // MOE_BLOCK
    n_items = N_EXPERTS + n_assign // (ITEM_BLOCKS * MOE_BLOCK)

    cnt = counts[0, :N_EXPERTS].astype(jnp.int32)
    blocks_e = (cnt + MOE_BLOCK - 1) // MOE_BLOCK
    blk_end = jnp.cumsum(blocks_e)
    blk_start = blk_end - blocks_e
    tables = (meta[0], meta[1], meta[4], meta[5], (blk_start * MOE_BLOCK).astype(jnp.int32))
    zstart = (blk_start * MOE_BLOCK + cnt).astype(jnp.int32)
    zcount = (blocks_e * MOE_BLOCK - cnt).astype(jnp.int32)

    items_e = (blocks_e + ITEM_BLOCKS - 1) // ITEM_BLOCKS
    item_end = jnp.cumsum(items_e)
    item_start = item_end - items_e
    w = jnp.arange(n_items, dtype=jnp.int32)
    live = w < item_end[-1]
    w_live = jnp.minimum(w, item_end[-1] - 1)
    e_w = jnp.minimum(jnp.sum((item_end[None, :] <= w_live[:, None]).astype(jnp.int32), axis=1), N_EXPERTS - 1)
    j_w = w_live - item_start[e_w]
    item_blk = (blk_start[e_w] + ITEM_BLOCKS * j_w).astype(jnp.int32)
    item_nb = jnp.where(live, jnp.clip(blocks_e[e_w] - ITEM_BLOCKS * j_w, 0, ITEM_BLOCKS), 0).astype(jnp.int32)
    return tables, zstart, zcount, e_w, item_blk, item_nb, n_blocks


def kernel(x, c, positions, norm1_gain, norm2_gain, final_norm_gain, w_ada, b_ada, w_in, attn_sinks,
           ret_norm_gain, w_branch_attn, w_branch_ret, w_out, w_router_group, b_router_group,
           w_router_expert, b_router_expert, w_expert_gate, w_expert_up, w_expert_down):
    batch, seq, d = x.shape
    t = batch * seq
    depth = w_ada.shape[0]
    half = RET_DIM // 2
    inv_freq = (ROPE_BASE ** (-jnp.arange(half, dtype=F32) / half)).reshape(1, half)
    pos = positions.reshape(t, 1)
    c8 = jnp.pad(c, ((0, 8 - batch), (0, 0)))
    xf = x.reshape(t, d)

    assert depth == 1, "the fused final norm assumes a single layer"
    for layer in range(depth):
        mod6 = _ada(c8, w_ada[layer], b_ada[layer].reshape(1, -1))[:batch].reshape(batch, 6, d)
        proj, proj_kv = _proj(xf, norm1_gain[layer].reshape(1, d), mod6, w_in[layer], seq)
        attn = _attn(proj, proj_kv, attn_sinks[layer], batch, seq)
        ret = _ret(proj, pos, inv_freq, ret_norm_gain[layer].reshape(1, d), batch, seq)

        pad = LANES - N_GROUPS - N_EXPERTS
        w_rt = jnp.concatenate([w_router_group[layer], w_router_expert[layer],
                                jnp.zeros((d, pad), F32)], axis=1)
        b_rt = jnp.concatenate([b_router_group[layer], b_router_expert[layer],
                                jnp.zeros((pad,), F32)]).reshape(1, LANES)
        merged = _branch(attn, ret, proj, w_branch_attn[layer], w_branch_ret[layer])
        x1, h2, route, counts, meta = _mixout(merged, xf, mod6, norm2_gain[layer].reshape(1, d),
                                              w_out[layer].astype(BF16), w_rt, b_rt, seq)
        tables, zstart, zcount, item_e, item_blk, item_nb, n_blocks = _dispatch_tables(meta, counts, t)
        x_sorted, dest = _dispatch(tables, zstart, zcount, h2, n_blocks * MOE_BLOCK)
        y_sorted = _experts(item_e, item_blk, item_nb, x_sorted,
                            w_expert_gate[layer], w_expert_up[layer], w_expert_down[layer], n_blocks)
        xf = _combine(dest, x1, route, mod6, final_norm_gain.reshape(1, d), y_sorted, seq)
    return xf.reshape(batch, seq, d)
```

```python
import functools
import math

import jax
import jax.numpy as jnp
import numpy as np
from jax import lax
from jax.experimental import pallas as pl
from jax.experimental.pallas import tpu as pltpu

F32 = jnp.float32
BF16 = jnp.bfloat16

D_MODEL = 2048
ATTN_HEAD_DIM = 64
ATTN_HEADS = 32
ATTN_KV_HEADS = 4
ATTN_GROUP = 8
WINDOW = 128
RET_HEADS = 8
RET_DIM = 256
RET_CHUNK = 128
ROPE_BASE = 10000.0
N_GROUPS = 4
EXPERTS_PER_GROUP = 16
N_EXPERTS = 64
TOP_K = 2
EXPERT_DIM = 1024
MOE_BLOCK = 128
EPS = 1e-6
NEG = -1e30

MIB = 1024 * 1024
LANES = 128
PROJ_TN = 512
PROJ_TM = 1024
KV_SRC_TILE = 4
BRANCH_TM = 1024
BRANCH_TN = 512
MIXOUT_TM = 512
ITEM_BLOCKS = 4
EXPERT_FC = 512
COMBINE_TM = 128
SLAB_ROWS = 8
SLAB_LANES = LANES
HALF_D = D_MODEL // 2
ROWS_PER_WAIT = 128
I32 = jnp.int32

COL_QA, COL_QR, COL_KR, COL_VR, COL_GR, COL_GA, COL_GRT = 0, 1, 2, 3, 4, 5, 6

LOG_GAMMA = [math.log1p(-(2.0 ** (-5.0 - h))) for h in range(RET_HEADS)]


def _params(sem, vmem_mib):
    return pltpu.CompilerParams(dimension_semantics=sem, vmem_limit_bytes=vmem_mib * MIB)


def _pack_pair(lo, hi):
    lo_b = lax.bitcast_convert_type(lo.astype(BF16).astype(F32), I32)
    hi_b = lax.bitcast_convert_type(hi.astype(BF16).astype(F32), I32)
    return hi_b | lax.shift_right_logical(lo_b, jnp.full_like(lo_b, 16))


def _unpack_pair(w):
    lo = lax.bitcast_convert_type(w << 16, F32)
    hi = lax.bitcast_convert_type(w & jnp.int32(-65536), F32)
    return lo, hi


def _slab_rows(j, n_tokens):
    return pl.ds(j, n_tokens, stride=SLAB_ROWS)


def _ada_kernel(c_ref, w_ref, b_ref, o_ref):
    c = c_ref[...]
    a = (c * jax.nn.sigmoid(c)).astype(BF16)
    o_ref[...] = jnp.dot(a, w_ref[...].astype(BF16), preferred_element_type=F32) + b_ref[...]


def _ada(c8, w_ada, b_ada):
    n = w_ada.shape[1]
    tn = 1024
    return pl.pallas_call(
        _ada_kernel,
        grid=(n // tn,),
        in_specs=[pl.BlockSpec((8, D_MODEL), lambda j: (0, 0)),
                  pl.BlockSpec((D_MODEL, tn), lambda j: (0, j)),
                  pl.BlockSpec((1, tn), lambda j: (0, j))],
        out_specs=pl.BlockSpec((8, tn), lambda j: (0, j)),
        out_shape=jax.ShapeDtypeStruct((8, n), F32),
        compiler_params=_params(("arbitrary",), 40),
        name="ada",
    )(c8, w_ada, b_ada)


def _proj_kernel(x_ref, g_ref, mod_ref, wlo_ref, whi_ref, o_ref, kv_ref, h_ref):
    v = pl.program_id(1)
    last = pl.num_programs(1) - 1

    @pl.when(v == 0)
    def _():
        x = x_ref[...]
        var = jnp.mean(x * x, axis=-1, keepdims=True)
        y = x * lax.rsqrt(var + EPS) * g_ref[...]
        h_ref[...] = (y * (1.0 + mod_ref[0, 1:2, :]) + mod_ref[0, 0:1, :]).astype(BF16)

    @pl.when(v < last)
    def _():
        w = jnp.concatenate([wlo_ref[...].astype(BF16), whi_ref[...].astype(BF16)], axis=1)
        o_ref[...] = jnp.dot(h_ref[...], w, preferred_element_type=F32).astype(BF16)

    @pl.when(v == last)
    def _():
        kv_ref[...] = jnp.dot(h_ref[...], wlo_ref[...].astype(BF16),
                              preferred_element_type=F32).astype(BF16)


def _proj_w_tile(v, n_wide):
    return jnp.where(v < 2, 2 * v, jnp.where(v < n_wide, 2 * v + 1, KV_SRC_TILE))


def _proj(x2, gain, mod6, w_in, seq):
    t = x2.shape[0]
    n = w_in.shape[1]
    tiles_per_batch = seq // PROJ_TM
    n_wide = (n - PROJ_TN) // (2 * PROJ_TN)
    return pl.pallas_call(
        _proj_kernel,
        grid=(t // PROJ_TM, n_wide + 1),
        in_specs=[pl.BlockSpec((PROJ_TM, D_MODEL), lambda i, v: (i, 0)),
                  pl.BlockSpec((1, D_MODEL), lambda i, v: (0, 0)),
                  pl.BlockSpec((1, 6, D_MODEL), lambda i, v: (i // tiles_per_batch, 0, 0)),
                  pl.BlockSpec((D_MODEL, PROJ_TN), lambda i, v: (0, _proj_w_tile(v, n_wide))),
                  pl.BlockSpec((D_MODEL, PROJ_TN),
                               lambda i, v: (0, jnp.where(v < n_wide, _proj_w_tile(v, n_wide) + 1, KV_SRC_TILE)))],
        out_specs=[pl.BlockSpec((PROJ_TM, 2 * PROJ_TN), lambda i, v: (i, jnp.minimum(v, n_wide - 1))),
                   pl.BlockSpec((PROJ_TM, PROJ_TN), lambda i, v: (i, 0))],
        out_shape=[jax.ShapeDtypeStruct((t, n - PROJ_TN), BF16),
                   jax.ShapeDtypeStruct((t, PROJ_TN), BF16)],
        scratch_shapes=[pltpu.VMEM((PROJ_TM, D_MODEL), BF16)],
        compiler_params=_params(("arbitrary", "arbitrary"), 56),
        name="proj",
    )(x2, gain, mod6, w_in, w_in)


def _attn_kernel(sink_ref, q_ref, kvp_ref, kvc_ref, o_ref):
    n = pl.program_id(1)
    kvp = kvp_ref[...]
    kvc = kvc_ref[...]
    qi = lax.broadcasted_iota(jnp.int32, (WINDOW, WINDOW), 0)
    sj = lax.broadcasted_iota(jnp.int32, (WINDOW, WINDOW), 1)
    valid_prev = (sj > qi) & (n > 0)
    valid_cur = sj <= qi
    sink_col = sj == 0
    first_row = lax.broadcasted_iota(jnp.int32, (2 * WINDOW, 1), 0) == 0
    dh = ATTN_HEAD_DIM
    kv_w = ATTN_KV_HEADS * dh
    scale = jnp.asarray(dh ** -0.5, BF16)
    n_pairs = ATTN_GROUP // 2

    def group_operands(kv):
        kband = jnp.concatenate([kvp[:, kv * dh:(kv + 1) * dh],
                                 kvc[:, kv * dh:(kv + 1) * dh]], axis=0) * scale
        vband = jnp.concatenate([kvp[:, kv_w + kv * dh:kv_w + (kv + 1) * dh],
                                 kvc[:, kv_w + kv * dh:kv_w + (kv + 1) * dh]], axis=0)
        vband = jnp.where(first_row, jnp.zeros_like(vband), vband)
        zeros = jnp.zeros_like(kband)
        ones = jnp.ones_like(vband)
        k_pad = (jnp.concatenate([kband, zeros], axis=1), jnp.concatenate([zeros, kband], axis=1))
        pv_rhs = jnp.concatenate(
            [jnp.concatenate([vband, zeros, ones, zeros], axis=1),
             jnp.concatenate([zeros, vband, zeros, ones], axis=1)], axis=0)
        q_rows = jnp.concatenate(
            [q_ref[:, (kv * ATTN_GROUP + 2 * p) * dh:(kv * ATTN_GROUP + 2 * p + 2) * dh]
             for p in range(n_pairs)], axis=0)
        scores = [lax.dot_general(q_rows, k_pad[idx], (((1,), (1,)), ((), ())),
                                  preferred_element_type=F32) for idx in range(2)]
        return scores, pv_rhs

    nxt = group_operands(0)
    for kv in range(ATTN_KV_HEADS):
        scores, pv_rhs = nxt
        if kv + 1 < ATTN_KV_HEADS:
            nxt = group_operands(kv + 1)
        prob_rows = []
        for pair in range(n_pairs):
            rows = slice(pair * WINDOW, (pair + 1) * WINDOW)
            probs = []
            for idx in range(2):
                s = scores[idx][rows]
                sink = sink_ref[kv * ATTN_GROUP + 2 * pair + idx]
                s_prev = jnp.where(sink_col, sink, jnp.where(valid_prev, s[:, :WINDOW], NEG))
                s_cur = jnp.where(valid_cur, s[:, WINDOW:], NEG)
                m = jnp.max(jnp.maximum(s_prev, s_cur), axis=-1, keepdims=True)
                probs += [jnp.exp(s_prev - m).astype(BF16), jnp.exp(s_cur - m).astype(BF16)]
            prob_rows.append(jnp.concatenate(probs, axis=-1))
        r = jnp.dot(jnp.concatenate(prob_rows, axis=0), pv_rhs, preferred_element_type=F32)
        for pair in range(n_pairs):
            rows = slice(pair * WINDOW, (pair + 1) * WINDOW)
            col = (kv * ATTN_GROUP + 2 * pair) * dh
            o_ref[:, col:col + 2 * dh] = (r[rows, :2 * dh] * (1.0 / r[rows, 2 * dh:])).astype(BF16)


def _attn(proj, proj_kv, sinks, batch, seq):
    nb = seq // WINDOW
    t = batch * seq

    def cur(b, n):
        return (b * nb + n, 0)

    def prev(b, n):
        return (b * nb + jnp.maximum(n - 1, 0), 0)

    return pl.pallas_call(
        _attn_kernel,
        grid=(batch, nb),
        in_specs=[pl.BlockSpec(memory_space=pltpu.SMEM),
                  pl.BlockSpec((WINDOW, D_MODEL), lambda b, n: (b * nb + n, COL_QA)),
                  pl.BlockSpec((WINDOW, PROJ_TN), prev),
                  pl.BlockSpec((WINDOW, PROJ_TN), cur)],
        out_specs=pl.BlockSpec((WINDOW, D_MODEL), lambda b, n: (b * nb + n, 0)),
        out_shape=jax.ShapeDtypeStruct((t, D_MODEL), BF16),
        compiler_params=_params(("arbitrary", "arbitrary"), 32),
        name="attn",
    )(sinks, proj, proj_kv, proj_kv)


def _ret_decay_tables():
    lg = np.asarray(LOG_GAMMA, np.float64)[:, None, None]
    i = np.arange(RET_CHUNK, dtype=np.float64)
    diff = i[:, None] - i[None, :]
    k_scale = RET_DIM ** -0.5
    d_intra = np.where(diff >= 0, np.exp(np.maximum(diff, 0.0) * lg), 0.0) * k_scale
    lanes = np.ones((1, 1, RET_DIM // 2))
    d_q = np.exp((i[None, :, None] + 1.0) * lg) * lanes
    d_k = np.exp((RET_CHUNK - 1.0 - i[None, :, None]) * lg) * k_scale * lanes
    return jnp.asarray(d_intra, F32), jnp.asarray(d_q, F32), jnp.asarray(d_k, BF16)


def _ret_kernel(pos_ref, invf_ref, di_ref, dq_ref, dk_ref, q_ref, k_ref, v_ref, g_ref, gain_ref,
                o_ref, state_ref):
    @pl.when(pl.program_id(1) == 0)
    def _():
        state_ref[...] = jnp.zeros_like(state_ref)

    half = RET_DIM // 2
    ang = pos_ref[...].astype(F32) * invf_ref[...]
    cos = jnp.cos(ang).astype(BF16)
    sin = jnp.sin(ang).astype(BF16)

    def rot(t):
        t1, t2 = t[:, :half], t[:, half:]
        return jnp.concatenate([t1 * cos - t2 * sin, t1 * sin + t2 * cos], axis=-1)

    def both_halves(t, factor):
        return jnp.concatenate([t[:, :half] * factor, t[:, half:] * factor], axis=-1)

    for h in range(RET_HEADS):
        sl = slice(h * RET_DIM, (h + 1) * RET_DIM)
        qb = rot(q_ref[:, sl])
        kb = rot(k_ref[:, sl])
        vb = v_ref[:, sl]
        d_chunk = math.exp(RET_CHUNK * LOG_GAMMA[h])
        intra = lax.dot_general(qb, kb, (((1,), (1,)), ((), ())),
                                preferred_element_type=F32) * di_ref[h]
        st = state_ref[h]
        cross = jnp.dot(qb, st.astype(BF16), preferred_element_type=F32)
        o = jnp.dot(intra.astype(BF16), vb, preferred_element_type=F32) + both_halves(cross, dq_ref[h])
        kd = both_halves(kb, dk_ref[h])
        state_ref[h] = st * d_chunk + lax.dot_general(kd, vb, (((0,), (0,)), ((), ())),
                                                      preferred_element_type=F32)
        o = o * lax.rsqrt(jnp.mean(o * o, axis=-1, keepdims=True) + EPS) * gain_ref[:, sl]
        gg = g_ref[:, sl].astype(F32)
        o_ref[:, sl] = (gg * jax.nn.sigmoid(gg) * o).astype(BF16)


def _ret(proj, pos, inv_freq, ret_gain, batch, seq):
    nc = seq // RET_CHUNK
    t = batch * seq

    def col(cb):
        return lambda b, c: (b * nc + c, cb)

    d_intra, d_q, d_k = _ret_decay_tables()
    table = lambda lanes: pl.BlockSpec((RET_HEADS, RET_CHUNK, lanes), lambda b, c: (0, 0, 0))
    return pl.pallas_call(
        _ret_kernel,
        grid=(batch, nc),
        in_specs=[pl.BlockSpec((RET_CHUNK, 1), lambda b, c: (b * nc + c, 0)),
                  pl.BlockSpec((1, RET_DIM // 2), lambda b, c: (0, 0)),
                  table(RET_CHUNK), table(RET_DIM // 2), table(RET_DIM // 2),
                  pl.BlockSpec((RET_CHUNK, D_MODEL), col(COL_QR)),
                  pl.BlockSpec((RET_CHUNK, D_MODEL), col(COL_KR)),
                  pl.BlockSpec((RET_CHUNK, D_MODEL), col(COL_VR)),
                  pl.BlockSpec((RET_CHUNK, D_MODEL), col(COL_GR)),
                  pl.BlockSpec((1, D_MODEL), lambda b, c: (0, 0))],
        out_specs=pl.BlockSpec((RET_CHUNK, D_MODEL), lambda b, c: (b * nc + c, 0)),
        out_shape=jax.ShapeDtypeStruct((t, D_MODEL), BF16),
        scratch_shapes=[pltpu.VMEM((RET_HEADS, RET_DIM, RET_DIM), F32)],
        compiler_params=_params(("arbitrary", "arbitrary"), 32),
        name="ret",
    )(pos, inv_freq, d_intra, d_q, d_k, proj, proj, proj, proj, ret_gain)


def _route(logits):
    lane = lax.broadcasted_iota(jnp.int32, logits.shape, 1)
    lane_f = lane.astype(F32)
    is_g = lane < N_GROUPS
    gl = jnp.where(is_g, logits, NEG)
    gmax = jnp.max(gl, axis=-1, keepdims=True)
    gsel = jnp.min(jnp.where(gl == gmax, lane_f, float(LANES)), axis=-1, keepdims=True)
    gsum = jnp.sum(jnp.where(is_g, jnp.exp(gl - gmax), 0.0), axis=-1, keepdims=True)
    g_w = 1.0 / gsum
    grp = ((lane - N_GROUPS) >> 4).astype(F32)
    is_e = (lane >= N_GROUPS) & (lane < N_GROUPS + N_EXPERTS) & (grp == gsel)
    el = jnp.where(is_e, logits, NEG)
    v1 = jnp.max(el, axis=-1, keepdims=True)
    i1 = jnp.min(jnp.where(el == v1, lane_f, float(LANES)), axis=-1, keepdims=True)
    el2 = jnp.where(lane_f == i1, NEG, el)
    v2 = jnp.max(el2, axis=-1, keepdims=True)
    i2 = jnp.min(jnp.where(el2 == v2, lane_f, float(LANES)), axis=-1, keepdims=True)
    tt = jnp.exp(v2 - v1)
    w1 = g_w / (1.0 + tt)
    w2 = g_w * tt / (1.0 + tt)
    return jnp.where(lane == 0, i1 - N_GROUPS,
                     jnp.where(lane == 1, i2 - N_GROUPS,
                               jnp.where(lane == 2, w1, jnp.where(lane == 3, w2, 0.0))))


def _branch_kernel(attn_ref, ret_ref, ga_ref, gr_ref, wa_ref, wr_ref, o_ref):
    a = jnp.dot(attn_ref[...], wa_ref[...].astype(BF16), preferred_element_type=F32)
    r = jnp.dot(ret_ref[...], wr_ref[...].astype(BF16), preferred_element_type=F32)
    o_ref[...] = (jax.nn.sigmoid(ga_ref[...].astype(F32)) * a
                  + jax.nn.sigmoid(gr_ref[...].astype(F32)) * r).astype(BF16)


def _branch(attn, ret, proj, wa, wr):
    t = attn.shape[0]
    tm, tn = BRANCH_TM, BRANCH_TN
    per_slab = D_MODEL // tn
    return pl.pallas_call(
        _branch_kernel,
        grid=(t // tm, D_MODEL // tn),
        in_specs=[pl.BlockSpec((tm, D_MODEL), lambda i, j: (i, 0)),
                  pl.BlockSpec((tm, D_MODEL), lambda i, j: (i, 0)),
                  pl.BlockSpec((tm, tn), lambda i, j: (i, COL_GA * per_slab + j)),
                  pl.BlockSpec((tm, tn), lambda i, j: (i, COL_GRT * per_slab + j)),
                  pl.BlockSpec((D_MODEL, tn), lambda i, j: (0, j)),
                  pl.BlockSpec((D_MODEL, tn), lambda i, j: (0, j))],
        out_specs=pl.BlockSpec((tm, tn), lambda i, j: (i, j)),
        out_shape=jax.ShapeDtypeStruct((t, D_MODEL), BF16),
        compiler_params=_params(("arbitrary", "arbitrary"), 48),
        name="branch",
    )(attn, ret, proj, proj, wa, wr)


def _mixout_kernel(m_ref, x_ref, mod_ref, g2_ref, wo_ref, wrt_ref, brt_ref,
                   x1_ref, h2_ref, route_ref, count_ref, meta_ref, carry_ref):
    @pl.when(pl.program_id(0) == 0)
    def _():
        carry_ref[...] = jnp.zeros_like(carry_ref)

    mix = jnp.dot(m_ref[...], wo_ref[...], preferred_element_type=F32)
    x1 = x_ref[...] + mod_ref[0, 2:3, :] * mix
    x1_ref[...] = x1
    var = jnp.mean(x1 * x1, axis=-1, keepdims=True)
    h2 = x1 * lax.rsqrt(var + EPS) * g2_ref[...]
    h2 = h2 * (1.0 + mod_ref[0, 4:5, :]) + mod_ref[0, 3:4, :]
    tm = h2.shape[0]
    for j in range(SLAB_ROWS):
        lo = h2[:, j * LANES:(j + 1) * LANES]
        hi = h2[:, HALF_D + j * LANES:HALF_D + (j + 1) * LANES]
        h2_ref[_slab_rows(j, tm), :] = _pack_pair(lo, hi)
    h_hi = h2.astype(BF16)
    h_lo = (h2 - h_hi.astype(F32)).astype(BF16)
    w_rt = wrt_ref[...]
    w_hi = w_rt.astype(BF16)
    w_lo = (w_rt - w_hi.astype(F32)).astype(BF16)
    hi_both = jnp.dot(h_hi, jnp.concatenate([w_hi, w_lo], axis=1), preferred_element_type=F32)
    logits = (hi_both[:, :LANES] + hi_both[:, LANES:]
              + jnp.dot(h_lo, w_hi, preferred_element_type=F32) + brt_ref[...])
    route = _route(logits)

    lane = lax.broadcasted_iota(jnp.int32, route.shape, 1)
    lane_f = lane.astype(F32)
    hot1 = lane_f == route[:, 0:1]
    hot2 = lane_f == route[:, 1:2]
    both = jnp.where(hot1 | hot2, 1.0, 0.0)
    ii = lax.broadcasted_iota(jnp.int32, (tm, tm), 0)
    jj = lax.broadcasted_iota(jnp.int32, (tm, tm), 1)
    lower = jnp.where(ii > jj, 1.0, 0.0).astype(BF16)
    before = jnp.dot(lower, both.astype(BF16), preferred_element_type=F32) + carry_ref[...]
    r1 = jnp.sum(jnp.where(hot1, before, 0.0), axis=-1, keepdims=True)
    r2 = jnp.sum(jnp.where(hot2, before, 0.0), axis=-1, keepdims=True)
    route = jnp.where(lane == 4, r1, jnp.where(lane == 5, r2, route))
    route_ref[...] = route
    meta_ref[...] = route.T[0:SLAB_ROWS, :].astype(I32)
    carry = carry_ref[...] + jnp.sum(both, axis=0, keepdims=True)
    carry_ref[...] = carry
    count_ref[...] = carry


def _mixout(merged, x2, mod6, gain2, wo, w_rt, b_rt, seq):
    t = x2.shape[0]
    tm = MIXOUT_TM
    tiles_per_batch = seq // tm
    row = lambda i: (i, 0)
    const = lambda i: (0, 0)
    return pl.pallas_call(
        _mixout_kernel,
        grid=(t // tm,),
        in_specs=[pl.BlockSpec((tm, D_MODEL), row),
                  pl.BlockSpec((tm, D_MODEL), row),
                  pl.BlockSpec((1, 6, D_MODEL), lambda i: (i // tiles_per_batch, 0, 0)),
                  pl.BlockSpec((1, D_MODEL), const),
                  pl.BlockSpec((D_MODEL, D_MODEL), const, pipeline_mode=pl.Buffered(1)),
                  pl.BlockSpec((D_MODEL, LANES), const),
                  pl.BlockSpec((1, LANES), const)],
        out_specs=[pl.BlockSpec((tm, D_MODEL), row),
                   pl.BlockSpec((tm * SLAB_ROWS, SLAB_LANES), row),
                   pl.BlockSpec((tm, LANES), row),
                   pl.BlockSpec((1, LANES), const),
                   pl.BlockSpec((SLAB_ROWS, tm), lambda i: (0, i))],
        out_shape=[jax.ShapeDtypeStruct((t, D_MODEL), F32),
                   jax.ShapeDtypeStruct((t * SLAB_ROWS, SLAB_LANES), I32),
                   jax.ShapeDtypeStruct((t, LANES), F32),
                   jax.ShapeDtypeStruct((1, LANES), F32),
                   jax.ShapeDtypeStruct((SLAB_ROWS, t), I32)],
        scratch_shapes=[pltpu.VMEM((1, LANES), F32)],
        compiler_params=_params(("arbitrary",), 56),
        name="mixout",
    )(merged, x2, mod6, gain2, wo, w_rt, b_rt)


PAD_BITS = (64, 32, 16, 8, 4, 2, 1)


def _sorted_row(tables, tok, k):
    e_refs, r_refs, blk_row_ref = tables[0:TOP_K], tables[TOP_K:2 * TOP_K], tables[2 * TOP_K]
    return blk_row_ref[e_refs[k][tok]] + r_refs[k][tok]


N_ROUTE_TABLES = 2 * TOP_K + 1


def _dispatch_kernel(*refs):
    tables = refs[:N_ROUTE_TABLES]
    zstart_ref, zcount_ref, h2_ref, xs_ref, dest_ref, zero_ref, sem, zsem = refs[N_ROUTE_TABLES:]
    n_assign = tables[0].shape[0] * TOP_K
    zero_ref[...] = jnp.zeros_like(zero_ref)

    def zero_copy(start, rows):
        return pltpu.make_async_copy(zero_ref.at[pl.ds(0, rows * SLAB_ROWS), :],
                                     xs_ref.at[pl.ds(start * SLAB_ROWS, rows * SLAB_ROWS), :], zsem)

    def fill(e, wait):
        start = zstart_ref[e]
        pad = zcount_ref[e]
        for bit in PAD_BITS:
            @pl.when((pad & bit) != 0)
            def _(start=start, bit=bit):
                cp = zero_copy(start, bit)
                cp.wait() if wait else cp.start()
            start = start + (pad & bit)

    lax.fori_loop(0, N_EXPERTS, lambda e, c: (fill(e, False), c)[1], 0)

    def issue(tok, carry):
        src = h2_ref.at[pl.ds(pl.multiple_of(tok * SLAB_ROWS, SLAB_ROWS), SLAB_ROWS), :]
        for k in range(TOP_K):
            row = _sorted_row(tables, tok, k)
            dest_ref[tok * TOP_K + k] = row
            dst = pl.multiple_of(row * SLAB_ROWS, SLAB_ROWS)
            pltpu.make_async_copy(src, xs_ref.at[pl.ds(dst, SLAB_ROWS), :], sem).start()
        return carry

    lax.fori_loop(0, n_assign // TOP_K, issue, 0, unroll=8)

    def drain(i, carry):
        pltpu.make_async_copy(h2_ref.at[pl.ds(0, ROWS_PER_WAIT * SLAB_ROWS), :],
                              xs_ref.at[pl.ds(0, ROWS_PER_WAIT * SLAB_ROWS), :], sem).wait()
        return carry

    lax.fori_loop(0, n_assign // ROWS_PER_WAIT, drain, 0)
    lax.fori_loop(0, N_EXPERTS, lambda e, c: (fill(e, True), c)[1], 0)


def _dispatch(tables, zstart, zcount, h2_slab, n_pad):
    n_assign = tables[0].shape[0] * TOP_K
    return pl.pallas_call(
        _dispatch_kernel,
        grid_spec=pltpu.PrefetchScalarGridSpec(
            num_scalar_prefetch=N_ROUTE_TABLES + 2,
            grid=(1,),
            in_specs=[pl.BlockSpec(memory_space=pltpu.HBM)],
            out_specs=[pl.BlockSpec(memory_space=pl.ANY),
                       pl.BlockSpec(memory_space=pltpu.SMEM)],
            scratch_shapes=[pltpu.VMEM((PAD_BITS[0] * SLAB_ROWS, SLAB_LANES), I32),
                            pltpu.SemaphoreType.DMA(()),
                            pltpu.SemaphoreType.DMA(())]),
        out_shape=[jax.ShapeDtypeStruct((n_pad * SLAB_ROWS, SLAB_LANES), I32),
                   jax.ShapeDtypeStruct((n_assign,), I32)],
        compiler_params=_params(("arbitrary",), 16),
        name="dispatch",
    )(*tables, zstart, zcount, h2_slab)


def _experts_kernel(item_e_ref, item_blk_ref, item_nb_ref, x_blk_ref,
                    x0_ref, x1_ref, x2_ref, x3_ref, wg_ref, wu_ref, wd_ref, y_ref,
                    xb_ref, acc_ref, yp_ref, sem):
    w = pl.program_id(0)
    c = pl.program_id(1)
    last_c = pl.num_programs(1) - 1
    nb = item_nb_ref[w]
    blk0 = item_blk_ref[w]
    x_refs = (x0_ref, x1_ref, x2_ref, x3_ref)

    blk_rows = MOE_BLOCK * SLAB_ROWS

    def out_copy(s):
        return pltpu.make_async_copy(yp_ref.at[s], y_ref.at[pl.ds((blk0 + s) * blk_rows, blk_rows), :],
                                     sem.at[s])

    def wait_out(count):
        for s in range(ITEM_BLOCKS):
            @pl.when(s < count)
            def _(s=s):
                out_copy(s).wait()

    @pl.when((c == last_c) & (w > 0))
    def _():
        wait_out(item_nb_ref[jnp.maximum(w - 1, 0)])

    def run(n_live):
        rows = n_live * MOE_BLOCK

        @pl.when(c == 0)
        def _():
            for s in range(n_live):
                for j in range(SLAB_ROWS):
                    lo, hi = _unpack_pair(x_refs[s][_slab_rows(j, MOE_BLOCK), :])
                    r0 = s * MOE_BLOCK
                    xb_ref[r0:r0 + MOE_BLOCK, j * LANES:(j + 1) * LANES] = lo.astype(BF16)
                    xb_ref[r0:r0 + MOE_BLOCK, HALF_D + j * LANES:HALF_D + (j + 1) * LANES] = hi.astype(BF16)

        xs = xb_ref[0:rows, :]
        g = jnp.dot(xs, wg_ref[...].astype(BF16), preferred_element_type=F32)
        u = jnp.dot(xs, wu_ref[...].astype(BF16), preferred_element_type=F32)
        hid = (g * jax.nn.sigmoid(g) * u).astype(BF16)
        y = jnp.dot(hid, wd_ref[...].astype(BF16), preferred_element_type=F32)

        @pl.when(c == 0)
        def _():
            acc_ref[0:rows, :] = y

        @pl.when((c > 0) & (c < last_c))
        def _():
            acc_ref[0:rows, :] += y

        @pl.when(c == last_c)
        def _():
            for s in range(n_live):
                r0 = s * MOE_BLOCK
                for j in range(SLAB_ROWS):
                    lo_cols = slice(j * LANES, (j + 1) * LANES)
                    hi_cols = slice(HALF_D + j * LANES, HALF_D + (j + 1) * LANES)
                    yp_ref[s, _slab_rows(j, MOE_BLOCK), :] = _pack_pair(
                        acc_ref[r0:r0 + MOE_BLOCK, lo_cols] + y[r0:r0 + MOE_BLOCK, lo_cols],
                        acc_ref[r0:r0 + MOE_BLOCK, hi_cols] + y[r0:r0 + MOE_BLOCK, hi_cols])
                out_copy(s).start()

    for n_live in range(1, ITEM_BLOCKS + 1):
        pl.when(nb == n_live)(functools.partial(run, n_live))

    @pl.when((c == last_c) & (w == pl.num_programs(0) - 1))
    def _():
        wait_out(nb)


def _experts(item_e, item_blk, item_nb, x_sorted, w_gate, w_up, w_down, n_blocks):
    n_items = item_e.shape[0]
    d = D_MODEL
    fc = EXPERT_FC
    x3 = x_sorted
    blk_rows = MOE_BLOCK * SLAB_ROWS
    n_chunks = EXPERT_DIM // fc
    assert n_chunks >= 2, "the last hidden chunk adds onto the accumulator of the earlier ones"

    slot = jnp.arange(ITEM_BLOCKS, dtype=jnp.int32)[:, None]
    x_blk = jnp.maximum(lax.cummax(jnp.where(slot < item_nb[None, :], item_blk[None, :] + slot, -1), axis=1), 0)
    x_blk = x_blk.reshape(-1).astype(jnp.int32)

    def x_spec(s):
        return pl.BlockSpec((blk_rows, SLAB_LANES),
                            lambda w, c, ie, ib, inb, xb: (xb[s * n_items + w], 0))

    def chunk(w, c, inb):
        return jnp.where(inb[w] > 0, c, n_chunks - 1)

    return pl.pallas_call(
        _experts_kernel,
        grid_spec=pltpu.PrefetchScalarGridSpec(
            num_scalar_prefetch=4,
            grid=(n_items, n_chunks),
            in_specs=[x_spec(0), x_spec(1), x_spec(2), x_spec(3),
                      pl.BlockSpec((None, d, fc), lambda w, c, ie, ib, inb, xb: (ie[w], 0, chunk(w, c, inb))),
                      pl.BlockSpec((None, d, fc), lambda w, c, ie, ib, inb, xb: (ie[w], 0, chunk(w, c, inb))),
                      pl.BlockSpec((None, fc, d), lambda w, c, ie, ib, inb, xb: (ie[w], chunk(w, c, inb), 0))],
            out_specs=pl.BlockSpec(memory_space=pl.ANY),
            scratch_shapes=[pltpu.VMEM((ITEM_BLOCKS * MOE_BLOCK, d), BF16),
                            pltpu.VMEM((ITEM_BLOCKS * MOE_BLOCK, d), F32),
                            pltpu.VMEM((ITEM_BLOCKS, blk_rows, SLAB_LANES), I32),
                            pltpu.SemaphoreType.DMA((ITEM_BLOCKS,))]),
        out_shape=jax.ShapeDtypeStruct((n_blocks * blk_rows, SLAB_LANES), I32),
        compiler_params=_params(("arbitrary", "arbitrary"), 56),
        name="experts",
    )(item_e, item_blk, item_nb, x_blk, x3, x3, x3, x3, w_gate, w_up, w_down)


def _combine_kernel(dest_ref, x1_ref, route_ref, mod_ref, gain_ref, y_ref, o_ref, ybuf_ref, sem):
    tm = COMBINE_TM
    i = pl.program_id(0)
    n_tiles = pl.num_programs(0)

    def row_copy(slot, k, r, src):
        return pltpu.make_async_copy(
            y_ref.at[pl.ds(pl.multiple_of(src * SLAB_ROWS, SLAB_ROWS), SLAB_ROWS), :],
            ybuf_ref.at[slot, k, pl.ds(pl.multiple_of(r * SLAB_ROWS, SLAB_ROWS), SLAB_ROWS), :],
            sem.at[slot])

    def wait_slot(slot):
        for k in range(TOP_K):
            pltpu.make_async_copy(y_ref.at[pl.ds(0, tm * SLAB_ROWS), :], ybuf_ref.at[slot, k],
                                  sem.at[slot]).wait()

    def issue_tile(tile, slot):
        base = tile * tm * TOP_K

        def issue(r, carry):
            for k in range(TOP_K):
                row_copy(slot, k, r, dest_ref[base + r * TOP_K + k]).start()
            return carry

        lax.fori_loop(0, tm, issue, 0, unroll=4)

    @pl.when(i == 0)
    def _():
        issue_tile(0, 0)

    for nxt in range(2):
        @pl.when((i + 1 < n_tiles) & ((i + 1) % 2 == nxt))
        def _(nxt=nxt):
            issue_tile(i + 1, nxt)

    slot = i % 2
    wait_slot(slot)

    route = route_ref[...]
    w0 = route[:, 2:3]
    w1 = route[:, 3:4]
    ssq = jnp.zeros((tm, 1), F32)
    for j in range(SLAB_ROWS):
        rows = _slab_rows(j, tm)
        y0 = _unpack_pair(ybuf_ref[slot, 0, rows, :])
        y1 = _unpack_pair(ybuf_ref[slot, 1, rows, :])
        for part, off in ((0, j * LANES), (1, HALF_D + j * LANES)):
            cols = slice(off, off + LANES)
            ffn = w0 * y0[part] + w1 * y1[part]
            x2 = x1_ref[:, cols] + mod_ref[0, 5:6, cols] * ffn
            o_ref[:, cols] = x2
            ssq = ssq + jnp.sum(x2 * x2, axis=-1, keepdims=True)
    o_ref[...] = o_ref[...] * lax.rsqrt(ssq * (1.0 / D_MODEL) + EPS) * gain_ref[...]


def _combine(dest, x1, route, mod6, gain, y_sorted, seq):
    t, d = x1.shape
    tm = COMBINE_TM
    tiles_per_batch = seq // tm
    return pl.pallas_call(
        _combine_kernel,
        grid_spec=pltpu.PrefetchScalarGridSpec(
            num_scalar_prefetch=1,
            grid=(t // tm,),
            in_specs=[pl.BlockSpec((tm, d), lambda i, *_: (i, 0)),
                      pl.BlockSpec((tm, LANES), lambda i, *_: (i, 0)),
                      pl.BlockSpec((1, 6, d), lambda i, *_: (i // tiles_per_batch, 0, 0)),
                      pl.BlockSpec((1, d), lambda i, *_: (0, 0)),
                      pl.BlockSpec(memory_space=pl.ANY)],
            out_specs=pl.BlockSpec((tm, d), lambda i, *_: (i, 0)),
            scratch_shapes=[pltpu.VMEM((2, TOP_K, tm * SLAB_ROWS, SLAB_LANES), I32),
                            pltpu.SemaphoreType.DMA((2,))]),
        out_shape=jax.ShapeDtypeStruct((t, d), F32),
        compiler_params=_params(("arbitrary",), 24),
        name="combine",
    )(dest, x1, route, mod6, gain, y_sorted)


def _dispatch_tables(meta, counts, t):
    n_assign = t * TOP_K
    n_pad = -(-(n_assign + N_EXPERTS * (MOE_BLOCK - 1)) // MOE_BLOCK) * MOE_BLOCK
    n_blocks = n_pad // MOE_BLOCK
    n_items = N_EXPERTS + n_assign // (ITEM_BLOCKS * MOE_BLOCK)

    cnt = counts[0, :N_EXPERTS].astype(jnp.int32)
    blocks_e = (cnt + MOE_BLOCK - 1) // MOE_BLOCK
    blk_end = jnp.cumsum(blocks_e)
    blk_start = blk_end - blocks_e
    tables = (meta[0], meta[1], meta[4], meta[5], (blk_start * MOE_BLOCK).astype(jnp.int32))
    zstart = (blk_start * MOE_BLOCK + cnt).astype(jnp.int32)
    zcount = (blocks_e * MOE_BLOCK - cnt).astype(jnp.int32)

    items_e = (blocks_e + ITEM_BLOCKS - 1) // ITEM_BLOCKS
    item_end = jnp.cumsum(items_e)
    item_start = item_end - items_e
    w = jnp.arange(n_items, dtype=jnp.int32)
    live = w < item_end[-1]
    w_live = jnp.minimum(w, item_end[-1] - 1)
    e_w = jnp.minimum(jnp.sum((item_end[None, :] <= w_live[:, None]).astype(jnp.int32), axis=1), N_EXPERTS - 1)
    j_w = w_live - item_start[e_w]
    item_blk = (blk_start[e_w] + ITEM_BLOCKS * j_w).astype(jnp.int32)
    item_nb = jnp.where(live, jnp.clip(blocks_e[e_w] - ITEM_BLOCKS * j_w, 0, ITEM_BLOCKS), 0).astype(jnp.int32)
    return tables, zstart, zcount, e_w, item_blk, item_nb, n_blocks


def kernel(x, c, positions, norm1_gain, norm2_gain, final_norm_gain, w_ada, b_ada, w_in, attn_sinks,
           ret_norm_gain, w_branch_attn, w_branch_ret, w_out, w_router_group, b_router_group,
           w_router_expert, b_router_expert, w_expert_gate, w_expert_up, w_expert_down):
    batch, seq, d = x.shape
    t = batch * seq
    depth = w_ada.shape[0]
    half = RET_DIM // 2
    inv_freq = (ROPE_BASE ** (-jnp.arange(half, dtype=F32) / half)).reshape(1, half)
    pos = positions.reshape(t, 1)
    c8 = jnp.pad(c, ((0, 8 - batch), (0, 0)))
    xf = x.reshape(t, d)

    assert depth == 1, "the fused final norm assumes a single layer"
    for layer in range(depth):
        mod6 = _ada(c8, w_ada[layer], b_ada[layer].reshape(1, -1))[:batch].reshape(batch, 6, d)
        proj, proj_kv = _proj(xf, norm1_gain[layer].reshape(1, d), mod6, w_in[layer], seq)
        attn = _attn(proj, proj_kv, attn_sinks[layer], batch, seq)
        ret = _ret(proj, pos, inv_freq, ret_norm_gain[layer].reshape(1, d), batch, seq)

        pad = LANES - N_GROUPS - N_EXPERTS
        w_rt = jnp.concatenate([w_router_group[layer], w_router_expert[layer],
                                jnp.zeros((d, pad), F32)], axis=1)
        b_rt = jnp.concatenate([b_router_group[layer], b_router_expert[layer],
                                jnp.zeros((pad,), F32)]).reshape(1, LANES)
        merged = _branch(attn, ret, proj, w_branch_attn[layer], w_branch_ret[layer])
        x1, h2, route, counts, meta = _mixout(merged, xf, mod6, norm2_gain[layer].reshape(1, d),
                                              w_out[layer].astype(BF16), w_rt, b_rt, seq)
        tables, zstart, zcount, item_e, item_blk, item_nb, n_blocks = _dispatch_tables(meta, counts, t)
        x_sorted, dest = _dispatch(tables, zstart, zcount, h2, n_blocks * MOE_BLOCK)
        y_sorted = _experts(item_e, item_blk, item_nb, x_sorted,
                            w_expert_gate[layer], w_expert_up[layer], w_expert_down[layer], n_blocks)
        xf = _combine(dest, x1, route, mod6, final_norm_gain.reshape(1, d), y_sorted, seq)
    return xf.reshape(batch, seq, d)
```

```python
import functools
import math

import jax
import jax.numpy as jnp
import numpy as np
from jax import lax
from jax.experimental import pallas as pl
from jax.experimental.pallas import tpu as pltpu

F32 = jnp.float32
BF16 = jnp.bfloat16

D_MODEL = 2048
ATTN_HEAD_DIM = 64
ATTN_HEADS = 32
ATTN_KV_HEADS = 4
ATTN_GROUP = 8
WINDOW = 128
RET_HEADS = 8
RET_DIM = 256
RET_CHUNK = 128
ROPE_BASE = 10000.0
N_GROUPS = 4
EXPERTS_PER_GROUP = 16
N_EXPERTS = 64
TOP_K = 2
EXPERT_DIM = 1024
MOE_BLOCK = 128
EPS = 1e-6
NEG = -1e30

MIB = 1024 * 1024
LANES = 128
PROJ_TN = 512
PROJ_TM = 1024
KV_SRC_TILE = 4
BRANCH_TM = 1024
BRANCH_TN = 512
MIXOUT_TM = 512
ITEM_BLOCKS = 4
EXPERT_FC = 512
COMBINE_TM = 128
SLAB_ROWS = 8
SLAB_LANES = LANES
HALF_D = D_MODEL // 2
ROWS_PER_WAIT = 128
I32 = jnp.int32

COL_QA, COL_QR, COL_KR, COL_VR, COL_GR, COL_GA, COL_GRT = 0, 1, 2, 3, 4, 5, 6

LOG_GAMMA = [math.log1p(-(2.0 ** (-5.0 - h))) for h in range(RET_HEADS)]


def _params(sem, vmem_mib):
    return pltpu.CompilerParams(dimension_semantics=sem, vmem_limit_bytes=vmem_mib * MIB)


def _pack_pair(lo, hi):
    lo_b = lax.bitcast_convert_type(lo.astype(BF16).astype(F32), I32)
    hi_b = lax.bitcast_convert_type(hi.astype(BF16).astype(F32), I32)
    return hi_b | lax.shift_right_logical(lo_b, jnp.full_like(lo_b, 16))


def _unpack_pair(w):
    lo = lax.bitcast_convert_type(w << 16, F32)
    hi = lax.bitcast_convert_type(w & jnp.int32(-65536), F32)
    return lo, hi


def _slab_rows(j, n_tokens):
    return pl.ds(j, n_tokens, stride=SLAB_ROWS)


def _ada_kernel(c_ref, w_ref, b_ref, o_ref):
    c = c_ref[...]
    a = (c * jax.nn.sigmoid(c)).astype(BF16)
    o_ref[...] = jnp.dot(a, w_ref[...].astype(BF16), preferred_element_type=F32) + b_ref[...]


def _ada(c8, w_ada, b_ada):
    n = w_ada.shape[1]
    tn = 1024
    return pl.pallas_call(
        _ada_kernel,
        grid=(n // tn,),
        in_specs=[pl.BlockSpec((8, D_MODEL), lambda j: (0, 0)),
                  pl.BlockSpec((D_MODEL, tn), lambda j: (0, j)),
                  pl.BlockSpec((1, tn), lambda j: (0, j))],
        out_specs=pl.BlockSpec((8, tn), lambda j: (0, j)),
        out_shape=jax.ShapeDtypeStruct((8, n), F32),
        compiler_params=_params(("arbitrary",), 40),
        name="ada",
    )(c8, w_ada, b_ada)


def _proj_kernel(x_ref, g_ref, mod_ref, wlo_ref, whi_ref, o_ref, kv_ref, h_ref):
    v = pl.program_id(1)
    last = pl.num_programs(1) - 1

    @pl.when(v == 0)
    def _():
        x = x_ref[...]
        var = jnp.mean(x * x, axis=-1, keepdims=True)
        y = x * lax.rsqrt(var + EPS) * g_ref[...]
        h_ref[...] = (y * (1.0 + mod_ref[0, 1:2, :]) + mod_ref[0, 0:1, :]).astype(BF16)

    @pl.when(v < last)
    def _():
        w = jnp.concatenate([wlo_ref[...].astype(BF16), whi_ref[...].astype(BF16)], axis=1)
        o_ref[...] = jnp.dot(h_ref[...], w, preferred_element_type=F32).astype(BF16)

    @pl.when(v == last)
    def _():
        kv_ref[...] = jnp.dot(h_ref[...], wlo_ref[...].astype(BF16),
                              preferred_element_type=F32).astype(BF16)


def _proj_w_tile(v, n_wide):
    return jnp.where(v < 2, 2 * v, jnp.where(v < n_wide, 2 * v + 1, KV_SRC_TILE))


def _proj(x2, gain, mod6, w_in, seq):
    t = x2.shape[0]
    n = w_in.shape[1]
    tiles_per_batch = seq // PROJ_TM
    n_wide = (n - PROJ_TN) // (2 * PROJ_TN)
    return pl.pallas_call(
        _proj_kernel,
        grid=(t // PROJ_TM, n_wide + 1),
        in_specs=[pl.BlockSpec((PROJ_TM, D_MODEL), lambda i, v: (i, 0)),
                  pl.BlockSpec((1, D_MODEL), lambda i, v: (0, 0)),
                  pl.BlockSpec((1, 6, D_MODEL), lambda i, v: (i // tiles_per_batch, 0, 0)),
                  pl.BlockSpec((D_MODEL, PROJ_TN), lambda i, v: (0, _proj_w_tile(v, n_wide))),
                  pl.BlockSpec((D_MODEL, PROJ_TN),
                               lambda i, v: (0, jnp.where(v < n_wide, _proj_w_tile(v, n_wide) + 1, KV_SRC_TILE)))],
        out_specs=[pl.BlockSpec((PROJ_TM, 2 * PROJ_TN), lambda i, v: (i, jnp.minimum(v, n_wide - 1))),
                   pl.BlockSpec((PROJ_TM, PROJ_TN), lambda i, v: (i, 0))],
        out_shape=[jax.ShapeDtypeStruct((t, n - PROJ_TN), BF16),
                   jax.ShapeDtypeStruct((t, PROJ_TN), BF16)],
        scratch_shapes=[pltpu.VMEM((PROJ_TM, D_MODEL), BF16)],
        compiler_params=_params(("arbitrary", "arbitrary"), 56),
        name="proj",
    )(x2, gain, mod6, w_in, w_in)


def _attn_kernel(sink_ref, q_ref, kvp_ref, kvc_ref, o_ref):
    n = pl.program_id(1)
    kvp = kvp_ref[...]
    kvc = kvc_ref[...]
    qi = lax.broadcasted_iota(jnp.int32, (WINDOW, WINDOW), 0)
    sj = lax.broadcasted_iota(jnp.int32, (WINDOW, WINDOW), 1)
    valid_prev = (sj > qi) & (n > 0)
    valid_cur = sj <= qi
    sink_col = sj == 0
    first_row = lax.broadcasted_iota(jnp.int32, (2 * WINDOW, 1), 0) == 0
    dh = ATTN_HEAD_DIM
    kv_w = ATTN_KV_HEADS * dh
    scale = jnp.asarray(dh ** -0.5, BF16)
    n_pairs = ATTN_GROUP // 2

    def group_operands(kv):
        kband = jnp.concatenate([kvp[:, kv * dh:(kv + 1) * dh],
                                 kvc[:, kv * dh:(kv + 1) * dh]], axis=0) * scale
        vband = jnp.concatenate([kvp[:, kv_w + kv * dh:kv_w + (kv + 1) * dh],
                                 kvc[:, kv_w + kv * dh:kv_w + (kv + 1) * dh]], axis=0)
        vband = jnp.where(first_row, jnp.zeros_like(vband), vband)
        zeros = jnp.zeros_like(kband)
        ones = jnp.ones_like(vband)
        k_pad = (jnp.concatenate([kband, zeros], axis=1), jnp.concatenate([zeros, kband], axis=1))
        pv_rhs = jnp.concatenate(
            [jnp.concatenate([vband, zeros, ones, zeros], axis=1),
             jnp.concatenate([zeros, vband, zeros, ones], axis=1)], axis=0)
        q_rows = jnp.concatenate(
            [q_ref[:, (kv * ATTN_GROUP + 2 * p) * dh:(kv * ATTN_GROUP + 2 * p + 2) * dh]
             for p in range(n_pairs)], axis=0)
        scores = [lax.dot_general(q_rows, k_pad[idx], (((1,), (1,)), ((), ())),
                                  preferred_element_type=F32) for idx in range(2)]
        return scores, pv_rhs

    nxt = group_operands(0)
    for kv in range(ATTN_KV_HEADS):
        scores, pv_rhs = nxt
        if kv + 1 < ATTN_KV_HEADS:
            nxt = group_operands(kv + 1)
        prob_rows = []
        for pair in range(n_pairs):
            rows = slice(pair * WINDOW, (pair + 1) * WINDOW)
            probs = []
            for idx in range(2):
                s = scores[idx][rows]
                sink = sink_ref[kv * ATTN_GROUP + 2 * pair + idx]
                s_prev = jnp.where(sink_col, sink, jnp.where(valid_prev, s[:, :WINDOW], NEG))
                s_cur = jnp.where(valid_cur, s[:, WINDOW:], NEG)
                m = jnp.max(jnp.maximum(s_prev, s_cur), axis=-1, keepdims=True)
                probs += [jnp.exp(s_prev - m).astype(BF16), jnp.exp(s_cur - m).astype(BF16)]
            prob_rows.append(jnp.concatenate(probs, axis=-1))
        r = jnp.dot(jnp.concatenate(prob_rows, axis=0), pv_rhs, preferred_element_type=F32)
        for pair in range(n_pairs):
            rows = slice(pair * WINDOW, (pair + 1) * WINDOW)
            col = (kv * ATTN_GROUP + 2 * pair) * dh
            o_ref[:, col:col + 2 * dh] = (r[rows, :2 * dh] * (1.0 / r[rows, 2 * dh:])).astype(BF16)


def _attn(proj, proj_kv, sinks, batch, seq):
    nb = seq // WINDOW
    t = batch * seq

    def cur(b, n):
        return (b * nb + n, 0)

    def prev(b, n):
        return (b * nb + jnp.maximum(n - 1, 0), 0)

    return pl.pallas_call(
        _attn_kernel,
        grid=(batch, nb),
        in_specs=[pl.BlockSpec(memory_space=pltpu.SMEM),
                  pl.BlockSpec((WINDOW, D_MODEL), lambda b, n: (b * nb + n, COL_QA)),
                  pl.BlockSpec((WINDOW, PROJ_TN), prev),
                  pl.BlockSpec((WINDOW, PROJ_TN), cur)],
        out_specs=pl.BlockSpec((WINDOW, D_MODEL), lambda b, n: (b * nb + n, 0)),
        out_shape=jax.ShapeDtypeStruct((t, D_MODEL), BF16),
        compiler_params=_params(("arbitrary", "arbitrary"), 32),
        name="attn",
    )(sinks, proj, proj_kv, proj_kv)


def _ret_decay_tables():
    lg = np.asarray(LOG_GAMMA, np.float64)[:, None, None]
    i = np.arange(RET_CHUNK, dtype=np.float64)
    diff = i[:, None] - i[None, :]
    k_scale = RET_DIM ** -0.5
    d_intra = np.where(diff >= 0, np.exp(np.maximum(diff, 0.0) * lg), 0.0) * k_scale
    lanes = np.ones((1, 1, RET_DIM // 2))
    d_q = np.exp((i[None, :, None] + 1.0) * lg) * lanes
    d_k = np.exp((RET_CHUNK - 1.0 - i[None, :, None]) * lg) * k_scale * lanes
    return jnp.asarray(d_intra, F32), jnp.asarray(d_q, F32), jnp.asarray(d_k, BF16)


def _ret_kernel(pos_ref, invf_ref, di_ref, dq_ref, dk_ref, q_ref, k_ref, v_ref, g_ref, gain_ref,
                o_ref, state_ref):
    @pl.when(pl.program_id(1) == 0)
    def _():
        state_ref[...] = jnp.zeros_like(state_ref)

    half = RET_DIM // 2
    ang = pos_ref[...].astype(F32) * invf_ref[...]
    cos = jnp.cos(ang).astype(BF16)
    sin = jnp.sin(ang).astype(BF16)

    def rot(t):
        t1, t2 = t[:, :half], t[:, half:]
        return jnp.concatenate([t1 * cos - t2 * sin, t1 * sin + t2 * cos], axis=-1)

    def both_halves(t, factor):
        return jnp.concatenate([t[:, :half] * factor, t[:, half:] * factor], axis=-1)

    def head_front(h):
        sl = slice(h * RET_DIM, (h + 1) * RET_DIM)
        qb = rot(q_ref[:, sl])
        kb = rot(k_ref[:, sl])
        intra = lax.dot_general(qb, kb, (((1,), (1,)), ((), ())),
                                preferred_element_type=F32) * di_ref[h]
        st = state_ref[h]
        cross = jnp.dot(qb, st.astype(BF16), preferred_element_type=F32)
        return kb, intra, st, cross

    front = head_front(0)
    for h in range(RET_HEADS):
        sl = slice(h * RET_DIM, (h + 1) * RET_DIM)
        kb, intra, st, cross = front
        if h + 1 < RET_HEADS:
            front = head_front(h + 1)
        vb = v_ref[:, sl]
        d_chunk = math.exp(RET_CHUNK * LOG_GAMMA[h])
        o = jnp.dot(intra.astype(BF16), vb, preferred_element_type=F32) + both_halves(cross, dq_ref[h])
        kd = both_halves(kb, dk_ref[h])
        state_ref[h] = st * d_chunk + lax.dot_general(kd, vb, (((0,), (0,)), ((), ())),
                                                      preferred_element_type=F32)
        o = o * lax.rsqrt(jnp.mean(o * o, axis=-1, keepdims=True) + EPS) * gain_ref[:, sl]
        gg = g_ref[:, sl].astype(F32)
        o_ref[:, sl] = (gg * jax.nn.sigmoid(gg) * o).astype(BF16)


def _ret(proj, pos, inv_freq, ret_gain, batch, seq):
    nc = seq // RET_CHUNK
    t = batch * seq

    def col(cb):
        return lambda b, c: (b * nc + c, cb)

    d_intra, d_q, d_k = _ret_decay_tables()
    table = lambda lanes: pl.BlockSpec((RET_HEADS, RET_CHUNK, lanes), lambda b, c: (0, 0, 0))
    return pl.pallas_call(
        _ret_kernel,
        grid=(batch, nc),
        in_specs=[pl.BlockSpec((RET_CHUNK, 1), lambda b, c: (b * nc + c, 0)),
                  pl.BlockSpec((1, RET_DIM // 2), lambda b, c: (0, 0)),
                  table(RET_CHUNK), table(RET_DIM // 2), table(RET_DIM // 2),
                  pl.BlockSpec((RET_CHUNK, D_MODEL), col(COL_QR)),
                  pl.BlockSpec((RET_CHUNK, D_MODEL), col(COL_KR)),
                  pl.BlockSpec((RET_CHUNK, D_MODEL), col(COL_VR)),
                  pl.BlockSpec((RET_CHUNK, D_MODEL), col(COL_GR)),
                  pl.BlockSpec((1, D_MODEL), lambda b, c: (0, 0))],
        out_specs=pl.BlockSpec((RET_CHUNK, D_MODEL), lambda b, c: (b * nc + c, 0)),
        out_shape=jax.ShapeDtypeStruct((t, D_MODEL), BF16),
        scratch_shapes=[pltpu.VMEM((RET_HEADS, RET_DIM, RET_DIM), F32)],
        compiler_params=_params(("arbitrary", "arbitrary"), 32),
        name="ret",
    )(pos, inv_freq, d_intra, d_q, d_k, proj, proj, proj, proj, ret_gain)


def _route(logits):
    lane = lax.broadcasted_iota(jnp.int32, logits.shape, 1)
    lane_f = lane.astype(F32)
    is_g = lane < N_GROUPS
    gl = jnp.where(is_g, logits, NEG)
    gmax = jnp.max(gl, axis=-1, keepdims=True)
    gsel = jnp.min(jnp.where(gl == gmax, lane_f, float(LANES)), axis=-1, keepdims=True)
    gsum = jnp.sum(jnp.where(is_g, jnp.exp(gl - gmax), 0.0), axis=-1, keepdims=True)
    g_w = 1.0 / gsum
    grp = ((lane - N_GROUPS) >> 4).astype(F32)
    is_e = (lane >= N_GROUPS) & (lane < N_GROUPS + N_EXPERTS) & (grp == gsel)
    el = jnp.where(is_e, logits, NEG)
    v1 = jnp.max(el, axis=-1, keepdims=True)
    i1 = jnp.min(jnp.where(el == v1, lane_f, float(LANES)), axis=-1, keepdims=True)
    el2 = jnp.where(lane_f == i1, NEG, el)
    v2 = jnp.max(el2, axis=-1, keepdims=True)
    i2 = jnp.min(jnp.where(el2 == v2, lane_f, float(LANES)), axis=-1, keepdims=True)
    tt = jnp.exp(v2 - v1)
    w1 = g_w / (1.0 + tt)
    w2 = g_w * tt / (1.0 + tt)
    return jnp.where(lane == 0, i1 - N_GROUPS,
                     jnp.where(lane == 1, i2 - N_GROUPS,
                               jnp.where(lane == 2, w1, jnp.where(lane == 3, w2, 0.0))))


def _branch_kernel(attn_ref, ret_ref, ga_ref, gr_ref, wa_ref, wr_ref, o_ref):
    a = jnp.dot(attn_ref[...], wa_ref[...].astype(BF16), preferred_element_type=F32)
    r = jnp.dot(ret_ref[...], wr_ref[...].astype(BF16), preferred_element_type=F32)
    o_ref[...] = (jax.nn.sigmoid(ga_ref[...].astype(F32)) * a
                  + jax.nn.sigmoid(gr_ref[...].astype(F32)) * r).astype(BF16)


def _branch(attn, ret, proj, wa, wr):
    t = attn.shape[0]
    tm, tn = BRANCH_TM, BRANCH_TN
    per_slab = D_MODEL // tn
    return pl.pallas_call(
        _branch_kernel,
        grid=(t // tm, D_MODEL // tn),
        in_specs=[pl.BlockSpec((tm, D_MODEL), lambda i, j: (i, 0)),
                  pl.BlockSpec((tm, D_MODEL), lambda i, j: (i, 0)),
                  pl.BlockSpec((tm, tn), lambda i, j: (i, COL_GA * per_slab + j)),
                  pl.BlockSpec((tm, tn), lambda i, j: (i, COL_GRT * per_slab + j)),
                  pl.BlockSpec((D_MODEL, tn), lambda i, j: (0, j)),
                  pl.BlockSpec((D_MODEL, tn), lambda i, j: (0, j))],
        out_specs=pl.BlockSpec((tm, tn), lambda i, j: (i, j)),
        out_shape=jax.ShapeDtypeStruct((t, D_MODEL), BF16),
        compiler_params=_params(("arbitrary", "arbitrary"), 48),
        name="branch",
    )(attn, ret, proj, proj, wa, wr)


def _mixout_kernel(m_ref, x_ref, mod_ref, g2_ref, wo_ref, wrt_ref, brt_ref,
                   x1_ref, h2_ref, route_ref, count_ref, meta_ref, carry_ref):
    @pl.when(pl.program_id(0) == 0)
    def _():
        carry_ref[...] = jnp.zeros_like(carry_ref)

    mix = jnp.dot(m_ref[...], wo_ref[...], preferred_element_type=F32)
    x1 = x_ref[...] + mod_ref[0, 2:3, :] * mix
    x1_ref[...] = x1
    var = jnp.mean(x1 * x1, axis=-1, keepdims=True)
    h2 = x1 * lax.rsqrt(var + EPS) * g2_ref[...]
    h2 = h2 * (1.0 + mod_ref[0, 4:5, :]) + mod_ref[0, 3:4, :]
    tm = h2.shape[0]
    for j in range(SLAB_ROWS):
        lo = h2[:, j * LANES:(j + 1) * LANES]
        hi = h2[:, HALF_D + j * LANES:HALF_D + (j + 1) * LANES]
        h2_ref[_slab_rows(j, tm), :] = _pack_pair(lo, hi)
    h_hi = h2.astype(BF16)
    h_lo = (h2 - h_hi.astype(F32)).astype(BF16)
    w_rt = wrt_ref[...]
    w_hi = w_rt.astype(BF16)
    w_lo = (w_rt - w_hi.astype(F32)).astype(BF16)
    hi_both = jnp.dot(h_hi, jnp.concatenate([w_hi, w_lo], axis=1), preferred_element_type=F32)
    logits = (hi_both[:, :LANES] + hi_both[:, LANES:]
              + jnp.dot(h_lo, w_hi, preferred_element_type=F32) + brt_ref[...])
    route = _route(logits)

    lane = lax.broadcasted_iota(jnp.int32, route.shape, 1)
    lane_f = lane.astype(F32)
    hot1 = lane_f == route[:, 0:1]
    hot2 = lane_f == route[:, 1:2]
    both = jnp.where(hot1 | hot2, 1.0, 0.0)
    ii = lax.broadcasted_iota(jnp.int32, (tm, tm), 0)
    jj = lax.broadcasted_iota(jnp.int32, (tm, tm), 1)
    lower = jnp.where(ii > jj, 1.0, 0.0).astype(BF16)
    before = jnp.dot(lower, both.astype(BF16), preferred_element_type=F32) + carry_ref[...]
    r1 = jnp.sum(jnp.where(hot1, before, 0.0), axis=-1, keepdims=True)
    r2 = jnp.sum(jnp.where(hot2, before, 0.0), axis=-1, keepdims=True)
    route = jnp.where(lane == 4, r1, jnp.where(lane == 5, r2, route))
    route_ref[...] = route
    meta_ref[...] = route.T[0:SLAB_ROWS, :].astype(I32)
    carry = carry_ref[...] + jnp.sum(both, axis=0, keepdims=True)
    carry_ref[...] = carry
    count_ref[...] = carry


def _mixout(merged, x2, mod6, gain2, wo, w_rt, b_rt, seq):
    t = x2.shape[0]
    tm = MIXOUT_TM
    tiles_per_batch = seq // tm
    row = lambda i: (i, 0)
    const = lambda i: (0, 0)
    return pl.pallas_call(
        _mixout_kernel,
        grid=(t // tm,),
        in_specs=[pl.BlockSpec((tm, D_MODEL), row),
                  pl.BlockSpec((tm, D_MODEL), row),
                  pl.BlockSpec((1, 6, D_MODEL), lambda i: (i // tiles_per_batch, 0, 0)),
                  pl.BlockSpec((1, D_MODEL), const),
                  pl.BlockSpec((D_MODEL, D_MODEL), const, pipeline_mode=pl.Buffered(1)),
                  pl.BlockSpec((D_MODEL, LANES), const),
                  pl.BlockSpec((1, LANES), const)],
        out_specs=[pl.BlockSpec((tm, D_MODEL), row),
                   pl.BlockSpec((tm * SLAB_ROWS, SLAB_LANES), row),
                   pl.BlockSpec((tm, LANES), row),
                   pl.BlockSpec((1, LANES), const),
                   pl.BlockSpec((SLAB_ROWS, tm), lambda i: (0, i))],
        out_shape=[jax.ShapeDtypeStruct((t, D_MODEL), F32),
                   jax.ShapeDtypeStruct((t * SLAB_ROWS, SLAB_LANES), I32),
                   jax.ShapeDtypeStruct((t, LANES), F32),
                   jax.ShapeDtypeStruct((1, LANES), F32),
                   jax.ShapeDtypeStruct((SLAB_ROWS, t), I32)],
        scratch_shapes=[pltpu.VMEM((1, LANES), F32)],
        compiler_params=_params(("arbitrary",), 56),
        name="mixout",
    )(merged, x2, mod6, gain2, wo, w_rt, b_rt)


PAD_BITS = (64, 32, 16, 8, 4, 2, 1)


def _sorted_row(tables, tok, k):
    e_refs, r_refs, blk_row_ref = tables[0:TOP_K], tables[TOP_K:2 * TOP_K], tables[2 * TOP_K]
    return blk_row_ref[e_refs[k][tok]] + r_refs[k][tok]


N_ROUTE_TABLES = 2 * TOP_K + 1


def _dispatch_kernel(*refs):
    tables = refs[:N_ROUTE_TABLES]
    zstart_ref, zcount_ref, h2_ref, xs_ref, dest_ref, zero_ref, sem, zsem = refs[N_ROUTE_TABLES:]
    n_assign = tables[0].shape[0] * TOP_K
    zero_ref[...] = jnp.zeros_like(zero_ref)

    def zero_copy(start, rows):
        return pltpu.make_async_copy(zero_ref.at[pl.ds(0, rows * SLAB_ROWS), :],
                                     xs_ref.at[pl.ds(start * SLAB_ROWS, rows * SLAB_ROWS), :], zsem)

    def fill(e, wait):
        start = zstart_ref[e]
        pad = zcount_ref[e]
        for bit in PAD_BITS:
            @pl.when((pad & bit) != 0)
            def _(start=start, bit=bit):
                cp = zero_copy(start, bit)
                cp.wait() if wait else cp.start()
            start = start + (pad & bit)

    lax.fori_loop(0, N_EXPERTS, lambda e, c: (fill(e, False), c)[1], 0)

    def issue(tok, carry):
        src = h2_ref.at[pl.ds(pl.multiple_of(tok * SLAB_ROWS, SLAB_ROWS), SLAB_ROWS), :]
        for k in range(TOP_K):
            row = _sorted_row(tables, tok, k)
            dest_ref[tok * TOP_K + k] = row
            dst = pl.multiple_of(row * SLAB_ROWS, SLAB_ROWS)
            pltpu.make_async_copy(src, xs_ref.at[pl.ds(dst, SLAB_ROWS), :], sem).start()
        return carry

    lax.fori_loop(0, n_assign // TOP_K, issue, 0, unroll=8)

    def drain(i, carry):
        pltpu.make_async_copy(h2_ref.at[pl.ds(0, ROWS_PER_WAIT * SLAB_ROWS), :],
                              xs_ref.at[pl.ds(0, ROWS_PER_WAIT * SLAB_ROWS), :], sem).wait()
        return carry

    lax.fori_loop(0, n_assign // ROWS_PER_WAIT, drain, 0)
    lax.fori_loop(0, N_EXPERTS, lambda e, c: (fill(e, True), c)[1], 0)


def _dispatch(tables, zstart, zcount, h2_slab, n_pad):
    n_assign = tables[0].shape[0] * TOP_K
    return pl.pallas_call(
        _dispatch_kernel,
        grid_spec=pltpu.PrefetchScalarGridSpec(
            num_scalar_prefetch=N_ROUTE_TABLES + 2,
            grid=(1,),
            in_specs=[pl.BlockSpec(memory_space=pltpu.HBM)],
            out_specs=[pl.BlockSpec(memory_space=pl.ANY),
                       pl.BlockSpec(memory_space=pltpu.SMEM)],
            scratch_shapes=[pltpu.VMEM((PAD_BITS[0] * SLAB_ROWS, SLAB_LANES), I32),
                            pltpu.SemaphoreType.DMA(()),
                            pltpu.SemaphoreType.DMA(())]),
        out_shape=[jax.ShapeDtypeStruct((n_pad * SLAB_ROWS, SLAB_LANES), I32),
                   jax.ShapeDtypeStruct((n_assign,), I32)],
        compiler_params=_params(("arbitrary",), 16),
        name="dispatch",
    )(*tables, zstart, zcount, h2_slab)


def _experts_kernel(item_e_ref, item_blk_ref, item_nb_ref, x_blk_ref,
                    x0_ref, x1_ref, x2_ref, x3_ref, wg_ref, wu_ref, wd_ref, y_ref,
                    xb_ref, acc_ref, yp_ref, sem):
    w = pl.program_id(0)
    c = pl.program_id(1)
    last_c = pl.num_programs(1) - 1
    nb = item_nb_ref[w]
    blk0 = item_blk_ref[w]
    x_refs = (x0_ref, x1_ref, x2_ref, x3_ref)

    blk_rows = MOE_BLOCK * SLAB_ROWS

    def out_copy(s):
        return pltpu.make_async_copy(yp_ref.at[s], y_ref.at[pl.ds((blk0 + s) * blk_rows, blk_rows), :],
                                     sem.at[s])

    def wait_out(count):
        for s in range(ITEM_BLOCKS):
            @pl.when(s < count)
            def _(s=s):
                out_copy(s).wait()

    @pl.when((c == last_c) & (w > 0))
    def _():
        wait_out(item_nb_ref[jnp.maximum(w - 1, 0)])

    def run(n_live):
        rows = n_live * MOE_BLOCK

        @pl.when(c == 0)
        def _():
            for s in range(n_live):
                for j in range(SLAB_ROWS):
                    lo, hi = _unpack_pair(x_refs[s][_slab_rows(j, MOE_BLOCK), :])
                    r0 = s * MOE_BLOCK
                    xb_ref[r0:r0 + MOE_BLOCK, j * LANES:(j + 1) * LANES] = lo.astype(BF16)
                    xb_ref[r0:r0 + MOE_BLOCK, HALF_D + j * LANES:HALF_D + (j + 1) * LANES] = hi.astype(BF16)

        xs = xb_ref[0:rows, :]
        g = jnp.dot(xs, wg_ref[...].astype(BF16), preferred_element_type=F32)
        u = jnp.dot(xs, wu_ref[...].astype(BF16), preferred_element_type=F32)
        hid = (g * jax.nn.sigmoid(g) * u).astype(BF16)
        y = jnp.dot(hid, wd_ref[...].astype(BF16), preferred_element_type=F32)

        @pl.when(c == 0)
        def _():
            acc_ref[0:rows, :] = y

        @pl.when((c > 0) & (c < last_c))
        def _():
            acc_ref[0:rows, :] += y

        @pl.when(c == last_c)
        def _():
            for s in range(n_live):
                r0 = s * MOE_BLOCK
                for j in range(SLAB_ROWS):
                    lo_cols = slice(j * LANES, (j + 1) * LANES)
                    hi_cols = slice(HALF_D + j * LANES, HALF_D + (j + 1) * LANES)
                    yp_ref[s, _slab_rows(j, MOE_BLOCK), :] = _pack_pair(
                        acc_ref[r0:r0 + MOE_BLOCK, lo_cols] + y[r0:r0 + MOE_BLOCK, lo_cols],
                        acc_ref[r0:r0 + MOE_BLOCK, hi_cols] + y[r0:r0 + MOE_BLOCK, hi_cols])
                out_copy(s).start()

    for n_live in range(1, ITEM_BLOCKS + 1):
        pl.when(nb == n_live)(functools.partial(run, n_live))

    @pl.when((c == last_c) & (w == pl.num_programs(0) - 1))
    def _():
        wait_out(nb)


def _experts(item_e, item_blk, item_nb, x_sorted, w_gate, w_up, w_down, n_blocks):
    n_items = item_e.shape[0]
    d = D_MODEL
    fc = EXPERT_FC
    x3 = x_sorted
    blk_rows = MOE_BLOCK * SLAB_ROWS
    n_chunks = EXPERT_DIM // fc
    assert n_chunks >= 2, "the last hidden chunk adds onto the accumulator of the earlier ones"

    slot = jnp.arange(ITEM_BLOCKS, dtype=jnp.int32)[:, None]
    x_blk = jnp.maximum(lax.cummax(jnp.where(slot < item_nb[None, :], item_blk[None, :] + slot, -1), axis=1), 0)
    x_blk = x_blk.reshape(-1).astype(jnp.int32)

    def x_spec(s):
        return pl.BlockSpec((blk_rows, SLAB_LANES),
                            lambda w, c, ie, ib, inb, xb: (xb[s * n_items + w], 0))

    def chunk(w, c, inb):
        return jnp.where(inb[w] > 0, c, n_chunks - 1)

    return pl.pallas_call(
        _experts_kernel,
        grid_spec=pltpu.PrefetchScalarGridSpec(
            num_scalar_prefetch=4,
            grid=(n_items, n_chunks),
            in_specs=[x_spec(0), x_spec(1), x_spec(2), x_spec(3),
                      pl.BlockSpec((None, d, fc), lambda w, c, ie, ib, inb, xb: (ie[w], 0, chunk(w, c, inb))),
                      pl.BlockSpec((None, d, fc), lambda w, c, ie, ib, inb, xb: (ie[w], 0, chunk(w, c, inb))),
                      pl.BlockSpec((None, fc, d), lambda w, c, ie, ib, inb, xb: (ie[w], chunk(w, c, inb), 0))],
            out_specs=pl.BlockSpec(memory_space=pl.ANY),
            scratch_shapes=[pltpu.VMEM((ITEM_BLOCKS * MOE_BLOCK, d), BF16),
                            pltpu.VMEM((ITEM_BLOCKS * MOE_BLOCK, d), F32),
                            pltpu.VMEM((ITEM_BLOCKS, blk_rows, SLAB_LANES), I32),
                            pltpu.SemaphoreType.DMA((ITEM_BLOCKS,))]),
        out_shape=jax.ShapeDtypeStruct((n_blocks * blk_rows, SLAB_LANES), I32),
        compiler_params=_params(("arbitrary", "arbitrary"), 56),
        name="experts",
    )(item_e, item_blk, item_nb, x_blk, x3, x3, x3, x3, w_gate, w_up, w_down)


def _combine_kernel(dest_ref, x1_ref, route_ref, mod_ref, gain_ref, y_ref, o_ref, ybuf_ref, sem):
    tm = COMBINE_TM
    i = pl.program_id(0)
    n_tiles = pl.num_programs(0)

    def row_copy(slot, k, r, src):
        return pltpu.make_async_copy(
            y_ref.at[pl.ds(pl.multiple_of(src * SLAB_ROWS, SLAB_ROWS), SLAB_ROWS), :],
            ybuf_ref.at[slot, k, pl.ds(pl.multiple_of(r * SLAB_ROWS, SLAB_ROWS), SLAB_ROWS), :],
            sem.at[slot])

    def wait_slot(slot):
        for k in range(TOP_K):
            pltpu.make_async_copy(y_ref.at[pl.ds(0, tm * SLAB_ROWS), :], ybuf_ref.at[slot, k],
                                  sem.at[slot]).wait()

    def issue_tile(tile, slot):
        base = tile * tm * TOP_K

        def issue(r, carry):
            for k in range(TOP_K):
                row_copy(slot, k, r, dest_ref[base + r * TOP_K + k]).start()
            return carry

        lax.fori_loop(0, tm, issue, 0, unroll=4)

    @pl.when(i == 0)
    def _():
        issue_tile(0, 0)

    for nxt in range(2):
        @pl.when((i + 1 < n_tiles) & ((i + 1) % 2 == nxt))
        def _(nxt=nxt):
            issue_tile(i + 1, nxt)

    slot = i % 2
    wait_slot(slot)

    route = route_ref[...]
    w0 = route[:, 2:3]
    w1 = route[:, 3:4]
    ssq = jnp.zeros((tm, 1), F32)
    for j in range(SLAB_ROWS):
        rows = _slab_rows(j, tm)
        y0 = _unpack_pair(ybuf_ref[slot, 0, rows, :])
        y1 = _unpack_pair(ybuf_ref[slot, 1, rows, :])
        for part, off in ((0, j * LANES), (1, HALF_D + j * LANES)):
            cols = slice(off, off + LANES)
            ffn = w0 * y0[part] + w1 * y1[part]
            x2 = x1_ref[:, cols] + mod_ref[0, 5:6, cols] * ffn
            o_ref[:, cols] = x2
            ssq = ssq + jnp.sum(x2 * x2, axis=-1, keepdims=True)
    o_ref[...] = o_ref[...] * lax.rsqrt(ssq * (1.0 / D_MODEL) + EPS) * gain_ref[...]


def _combine(dest, x1, route, mod6, gain, y_sorted, seq):
    t, d = x1.shape
    tm = COMBINE_TM
    tiles_per_batch = seq // tm
    return pl.pallas_call(
        _combine_kernel,
        grid_spec=pltpu.PrefetchScalarGridSpec(
            num_scalar_prefetch=1,
            grid=(t // tm,),
            in_specs=[pl.BlockSpec((tm, d), lambda i, *_: (i, 0)),
                      pl.BlockSpec((tm, LANES), lambda i, *_: (i, 0)),
                      pl.BlockSpec((1, 6, d), lambda i, *_: (i // tiles_per_batch, 0, 0)),
                      pl.BlockSpec((1, d), lambda i, *_: (0, 0)),
                      pl.BlockSpec(memory_space=pl.ANY)],
            out_specs=pl.BlockSpec((tm, d), lambda i, *_: (i, 0)),
            scratch_shapes=[pltpu.VMEM((2, TOP_K, tm * SLAB_ROWS, SLAB_LANES), I32),
                            pltpu.SemaphoreType.DMA((2,))]),
        out_shape=jax.ShapeDtypeStruct((t, d), F32),
        compiler_params=_params(("arbitrary",), 24),
        name="combine",
    )(dest, x1, route, mod6, gain, y_sorted)


def _dispatch_tables(meta, counts, t):
    n_assign = t * TOP_K
    n_pad = -(-(n_assign + N_EXPERTS * (MOE_BLOCK - 1)) // MOE_BLOCK) * MOE_BLOCK
    n_blocks = n_pad // MOE_BLOCK
    n_items = N_EXPERTS + n_assign // (ITEM_BLOCKS * MOE_BLOCK)

    cnt = counts[0, :N_EXPERTS].astype(jnp.int32)
    blocks_e = (cnt + MOE_BLOCK - 1) // MOE_BLOCK
    blk_end = jnp.cumsum(blocks_e)
    blk_start = blk_end - blocks_e
    tables = (meta[0], meta[1], meta[4], meta[5], (blk_start * MOE_BLOCK).astype(jnp.int32))
    zstart = (blk_start * MOE_BLOCK + cnt).astype(jnp.int32)
    zcount = (blocks_e * MOE_BLOCK - cnt).astype(jnp.int32)

    items_e = (blocks_e + ITEM_BLOCKS - 1) // ITEM_BLOCKS
    item_end = jnp.cumsum(items_e)
    item_start = item_end - items_e
    w = jnp.arange(n_items, dtype=jnp.int32)
    live = w < item_end[-1]
    w_live = jnp.minimum(w, item_end[-1] - 1)
    e_w = jnp.minimum(jnp.sum((item_end[None, :] <= w_live[:, None]).astype(jnp.int32), axis=1), N_EXPERTS - 1)
    j_w = w_live - item_start[e_w]
    item_blk = (blk_start[e_w] + ITEM_BLOCKS * j_w).astype(jnp.int32)
    item_nb = jnp.where(live, jnp.clip(blocks_e[e_w] - ITEM_BLOCKS * j_w, 0, ITEM_BLOCKS), 0).astype(jnp.int32)
    return tables, zstart, zcount, e_w, item_blk, item_nb, n_blocks


def kernel(x, c, positions, norm1_gain, norm2_gain, final_norm_gain, w_ada, b_ada, w_in, attn_sinks,
           ret_norm_gain, w_branch_attn, w_branch_ret, w_out, w_router_group, b_router_group,
           w_router_expert, b_router_expert, w_expert_gate, w_expert_up, w_expert_down):
    batch, seq, d = x.shape
    t = batch * seq
    depth = w_ada.shape[0]
    half = RET_DIM // 2
    inv_freq = (ROPE_BASE ** (-jnp.arange(half, dtype=F32) / half)).reshape(1, half)
    pos = positions.reshape(t, 1)
    c8 = jnp.pad(c, ((0, 8 - batch), (0, 0)))
    xf = x.reshape(t, d)

    assert depth == 1, "the fused final norm assumes a single layer"
    for layer in range(depth):
        mod6 = _ada(c8, w_ada[layer], b_ada[layer].reshape(1, -1))[:batch].reshape(batch, 6, d)
        proj, proj_kv = _proj(xf, norm1_gain[layer].reshape(1, d), mod6, w_in[layer], seq)
        attn = _attn(proj, proj_kv, attn_sinks[layer], batch, seq)
        ret = _ret(proj, pos, inv_freq, ret_norm_gain[layer].reshape(1, d), batch, seq)

        pad = LANES - N_GROUPS - N_EXPERTS
        w_rt = jnp.concatenate([w_router_group[layer], w_router_expert[layer],
                                jnp.zeros((d, pad), F32)], axis=1)
        b_rt = jnp.concatenate([b_router_group[layer], b_router_expert[layer],
                                jnp.zeros((pad,), F32)]).reshape(1, LANES)
        merged = _branch(attn, ret, proj, w_branch_attn[layer], w_branch_ret[layer])
        x1, h2, route, counts, meta = _mixout(merged, xf, mod6, norm2_gain[layer].reshape(1, d),
                                              w_out[layer].astype(BF16), w_rt, b_rt, seq)
        tables, zstart, zcount, item_e, item_blk, item_nb, n_blocks = _dispatch_tables(meta, counts, t)
        x_sorted, dest = _dispatch(tables, zstart, zcount, h2, n_blocks * MOE_BLOCK)
        y_sorted = _experts(item_e, item_blk, item_nb, x_sorted,
                            w_expert_gate[layer], w_expert_up[layer], w_expert_down[layer], n_blocks)
        xf = _combine(dest, x1, route, mod6, final_norm_gain.reshape(1, d), y_sorted, seq)
    return xf.reshape(batch, seq, d)
```

```python
import functools
import math

import jax
import jax.numpy as jnp
import numpy as np
from jax import lax
from jax.experimental import pallas as pl
from jax.experimental.pallas import tpu as pltpu

F32 = jnp.float32
BF16 = jnp.bfloat16

D_MODEL = 2048
ATTN_HEAD_DIM = 64
ATTN_HEADS = 32
ATTN_KV_HEADS = 4
ATTN_GROUP = 8
WINDOW = 128
RET_HEADS = 8
RET_DIM = 256
RET_CHUNK = 128
ROPE_BASE = 10000.0
N_GROUPS = 4
EXPERTS_PER_GROUP = 16
N_EXPERTS = 64
TOP_K = 2
EXPERT_DIM = 1024
MOE_BLOCK = 128
EPS = 1e-6
NEG = -1e30

MIB = 1024 * 1024
LANES = 128
PROJ_TN = 512
PROJ_TM = 1024
KV_SRC_TILE = 4
BRANCH_TM = 1024
BRANCH_TN = 512
MIXOUT_TM = 512
ITEM_BLOCKS = 4
EXPERT_FC = 512
COMBINE_TM = 128
SLAB_ROWS = 8
SLAB_LANES = LANES
HALF_D = D_MODEL // 2
ROWS_PER_WAIT = 128
I32 = jnp.int32

COL_QA, COL_QR, COL_KR, COL_VR, COL_GR, COL_GA, COL_GRT = 0, 1, 2, 3, 4, 5, 6

LOG_GAMMA = [math.log1p(-(2.0 ** (-5.0 - h))) for h in range(RET_HEADS)]


def _params(sem, vmem_mib):
    return pltpu.CompilerParams(dimension_semantics=sem, vmem_limit_bytes=vmem_mib * MIB)


def _pack_pair(lo, hi):
    lo_b = lax.bitcast_convert_type(lo.astype(BF16).astype(F32), I32)
    hi_b = lax.bitcast_convert_type(hi.astype(BF16).astype(F32), I32)
    return hi_b | lax.shift_right_logical(lo_b, jnp.full_like(lo_b, 16))


def _unpack_pair(w):
    lo = lax.bitcast_convert_type(w << 16, F32)
    hi = lax.bitcast_convert_type(w & jnp.int32(-65536), F32)
    return lo, hi


def _slab_rows(j, n_tokens):
    return pl.ds(j, n_tokens, stride=SLAB_ROWS)


def _ada_kernel(c_ref, w_ref, b_ref, o_ref):
    c = c_ref[...]
    a = (c * jax.nn.sigmoid(c)).astype(BF16)
    o_ref[...] = jnp.dot(a, w_ref[...].astype(BF16), preferred_element_type=F32) + b_ref[...]


def _ada(c8, w_ada, b_ada):
    n = w_ada.shape[1]
    tn = 1024
    return pl.pallas_call(
        _ada_kernel,
        grid=(n // tn,),
        in_specs=[pl.BlockSpec((8, D_MODEL), lambda j: (0, 0)),
                  pl.BlockSpec((D_MODEL, tn), lambda j: (0, j)),
                  pl.BlockSpec((1, tn), lambda j: (0, j))],
        out_specs=pl.BlockSpec((8, tn), lambda j: (0, j)),
        out_shape=jax.ShapeDtypeStruct((8, n), F32),
        compiler_params=_params(("arbitrary",), 40),
        name="ada",
    )(c8, w_ada, b_ada)


def _proj_kernel(x_ref, g_ref, mod_ref, wlo_ref, whi_ref, o_ref, kv_ref, h_ref):
    v = pl.program_id(1)
    last = pl.num_programs(1) - 1

    @pl.when(v == 0)
    def _():
        x = x_ref[...]
        var = jnp.mean(x * x, axis=-1, keepdims=True)
        y = x * lax.rsqrt(var + EPS) * g_ref[...]
        h_ref[...] = (y * (1.0 + mod_ref[0, 1:2, :]) + mod_ref[0, 0:1, :]).astype(BF16)

    @pl.when(v < last)
    def _():
        w = jnp.concatenate([wlo_ref[...].astype(BF16), whi_ref[...].astype(BF16)], axis=1)
        o_ref[...] = jnp.dot(h_ref[...], w, preferred_element_type=F32).astype(BF16)

    @pl.when(v == last)
    def _():
        kv_ref[...] = jnp.dot(h_ref[...], wlo_ref[...].astype(BF16),
                              preferred_element_type=F32).astype(BF16)


def _proj_w_tile(v, n_wide):
    return jnp.where(v < 2, 2 * v, jnp.where(v < n_wide, 2 * v + 1, KV_SRC_TILE))


def _proj(x2, gain, mod6, w_in, seq):
    t = x2.shape[0]
    n = w_in.shape[1]
    tiles_per_batch = seq // PROJ_TM
    n_wide = (n - PROJ_TN) // (2 * PROJ_TN)
    return pl.pallas_call(
        _proj_kernel,
        grid=(t // PROJ_TM, n_wide + 1),
        in_specs=[pl.BlockSpec((PROJ_TM, D_MODEL), lambda i, v: (i, 0)),
                  pl.BlockSpec((1, D_MODEL), lambda i, v: (0, 0)),
                  pl.BlockSpec((1, 6, D_MODEL), lambda i, v: (i // tiles_per_batch, 0, 0)),
                  pl.BlockSpec((D_MODEL, PROJ_TN), lambda i, v: (0, _proj_w_tile(v, n_wide))),
                  pl.BlockSpec((D_MODEL, PROJ_TN),
                               lambda i, v: (0, jnp.where(v < n_wide, _proj_w_tile(v, n_wide) + 1, KV_SRC_TILE)))],
        out_specs=[pl.BlockSpec((PROJ_TM, 2 * PROJ_TN), lambda i, v: (i, jnp.minimum(v, n_wide - 1))),
                   pl.BlockSpec((PROJ_TM, PROJ_TN), lambda i, v: (i, 0))],
        out_shape=[jax.ShapeDtypeStruct((t, n - PROJ_TN), BF16),
                   jax.ShapeDtypeStruct((t, PROJ_TN), BF16)],
        scratch_shapes=[pltpu.VMEM((PROJ_TM, D_MODEL), BF16)],
        compiler_params=_params(("arbitrary", "arbitrary"), 56),
        name="proj",
    )(x2, gain, mod6, w_in, w_in)


def _attn_stages(sink_ref, q_ref, kvp_ref, kvc_ref, o_ref):
    n = pl.program_id(1)
    kvp = kvp_ref[...]
    kvc = kvc_ref[...]
    qi = lax.broadcasted_iota(jnp.int32, (WINDOW, WINDOW), 0)
    sj = lax.broadcasted_iota(jnp.int32, (WINDOW, WINDOW), 1)
    valid_prev = (sj > qi) & (n > 0)
    valid_cur = sj <= qi
    sink_col = sj == 0
    first_row = lax.broadcasted_iota(jnp.int32, (2 * WINDOW, 1), 0) == 0
    dh = ATTN_HEAD_DIM
    kv_w = ATTN_KV_HEADS * dh
    scale = jnp.asarray(dh ** -0.5, BF16)
    n_pairs = ATTN_GROUP // 2

    def group_operands(kv):
        kband = jnp.concatenate([kvp[:, kv * dh:(kv + 1) * dh],
                                 kvc[:, kv * dh:(kv + 1) * dh]], axis=0) * scale
        vband = jnp.concatenate([kvp[:, kv_w + kv * dh:kv_w + (kv + 1) * dh],
                                 kvc[:, kv_w + kv * dh:kv_w + (kv + 1) * dh]], axis=0)
        vband = jnp.where(first_row, jnp.zeros_like(vband), vband)
        zeros = jnp.zeros_like(kband)
        ones = jnp.ones_like(vband)
        k_pad = (jnp.concatenate([kband, zeros], axis=1), jnp.concatenate([zeros, kband], axis=1))
        pv_rhs = jnp.concatenate(
            [jnp.concatenate([vband, zeros, ones, zeros], axis=1),
             jnp.concatenate([zeros, vband, zeros, ones], axis=1)], axis=0)
        q_rows = jnp.concatenate(
            [q_ref[:, (kv * ATTN_GROUP + 2 * p) * dh:(kv * ATTN_GROUP + 2 * p + 2) * dh]
             for p in range(n_pairs)], axis=0)
        scores = [lax.dot_general(q_rows, k_pad[idx], (((1,), (1,)), ((), ())),
                                  preferred_element_type=F32) for idx in range(2)]
        return scores, pv_rhs

    nxt = group_operands(0)
    for kv in range(ATTN_KV_HEADS):
        scores, pv_rhs = nxt
        if kv + 1 < ATTN_KV_HEADS:
            nxt = group_operands(kv + 1)
        prob_rows = []
        for pair in range(n_pairs):
            rows = slice(pair * WINDOW, (pair + 1) * WINDOW)
            probs = []
            for idx in range(2):
                s = scores[idx][rows]
                sink = sink_ref[kv * ATTN_GROUP + 2 * pair + idx]
                s_prev = jnp.where(sink_col, sink, jnp.where(valid_prev, s[:, :WINDOW], NEG))
                s_cur = jnp.where(valid_cur, s[:, WINDOW:], NEG)
                m = jnp.max(jnp.maximum(s_prev, s_cur), axis=-1, keepdims=True)
                probs += [jnp.exp(s_prev - m).astype(BF16), jnp.exp(s_cur - m).astype(BF16)]
            prob_rows.append(jnp.concatenate(probs, axis=-1))
        r = jnp.dot(jnp.concatenate(prob_rows, axis=0), pv_rhs, preferred_element_type=F32)
        for pair in range(n_pairs):
            rows = slice(pair * WINDOW, (pair + 1) * WINDOW)
            col = (kv * ATTN_GROUP + 2 * pair) * dh
            o_ref[:, col:col + 2 * dh] = (r[rows, :2 * dh] * (1.0 / r[rows, 2 * dh:])).astype(BF16)
        yield


def _ret_decay_tables():
    lg = np.asarray(LOG_GAMMA, np.float64)[:, None, None]
    i = np.arange(RET_CHUNK, dtype=np.float64)
    diff = i[:, None] - i[None, :]
    k_scale = RET_DIM ** -0.5
    d_intra = np.where(diff >= 0, np.exp(np.maximum(diff, 0.0) * lg), 0.0) * k_scale
    lanes = np.ones((1, 1, RET_DIM // 2))
    d_q = np.exp((i[None, :, None] + 1.0) * lg) * lanes
    d_k = np.exp((RET_CHUNK - 1.0 - i[None, :, None]) * lg) * k_scale * lanes
    return jnp.asarray(d_intra, F32), jnp.asarray(d_q, F32), jnp.asarray(d_k, BF16)


def _ret_stages(pos_ref, invf_ref, di_ref, dq_ref, dk_ref, q_ref, k_ref, v_ref, g_ref, gain_ref,
                o_ref, state_ref):
    half = RET_DIM // 2
    ang = pos_ref[...].astype(F32) * invf_ref[...]
    cos = jnp.cos(ang).astype(BF16)
    sin = jnp.sin(ang).astype(BF16)

    def rot(t):
        t1, t2 = t[:, :half], t[:, half:]
        return jnp.concatenate([t1 * cos - t2 * sin, t1 * sin + t2 * cos], axis=-1)

    def both_halves(t, factor):
        return jnp.concatenate([t[:, :half] * factor, t[:, half:] * factor], axis=-1)

    def head_front(h):
        sl = slice(h * RET_DIM, (h + 1) * RET_DIM)
        qb = rot(q_ref[:, sl])
        kb = rot(k_ref[:, sl])
        intra = lax.dot_general(qb, kb, (((1,), (1,)), ((), ())),
                                preferred_element_type=F32) * di_ref[h]
        st = state_ref[h]
        cross = jnp.dot(qb, st.astype(BF16), preferred_element_type=F32)
        return kb, intra, st, cross

    front = head_front(0)
    for h in range(RET_HEADS):
        sl = slice(h * RET_DIM, (h + 1) * RET_DIM)
        kb, intra, st, cross = front
        if h + 1 < RET_HEADS:
            front = head_front(h + 1)
        vb = v_ref[:, sl]
        d_chunk = math.exp(RET_CHUNK * LOG_GAMMA[h])
        o = jnp.dot(intra.astype(BF16), vb, preferred_element_type=F32) + both_halves(cross, dq_ref[h])
        kd = both_halves(kb, dk_ref[h])
        state_ref[h] = st * d_chunk + lax.dot_general(kd, vb, (((0,), (0,)), ((), ())),
                                                      preferred_element_type=F32)
        o = o * lax.rsqrt(jnp.mean(o * o, axis=-1, keepdims=True) + EPS) * gain_ref[:, sl]
        gg = g_ref[:, sl].astype(F32)
        o_ref[:, sl] = (gg * jax.nn.sigmoid(gg) * o).astype(BF16)
        yield


def _mixers_kernel(sink_ref, qa_ref, kvp_ref, kvc_ref, pos_ref, invf_ref, di_ref, dq_ref, dk_ref,
                   qr_ref, kr_ref, vr_ref, gr_ref, gain_ref, attn_ref, ret_ref, state_ref):
    @pl.when(pl.program_id(1) == 0)
    def _():
        state_ref[...] = jnp.zeros_like(state_ref)

    attn = _attn_stages(sink_ref, qa_ref, kvp_ref, kvc_ref, attn_ref)
    ret = _ret_stages(pos_ref, invf_ref, di_ref, dq_ref, dk_ref, qr_ref, kr_ref, vr_ref, gr_ref,
                      gain_ref, ret_ref, state_ref)
    heads_per_group = RET_HEADS // ATTN_KV_HEADS
    for _ in range(ATTN_KV_HEADS):
        next(attn)
        for _ in range(heads_per_group):
            next(ret)


def _mixers(proj, proj_kv, sinks, pos, inv_freq, ret_gain, batch, seq):
    nb = seq // WINDOW
    t = batch * seq

    def col(cb):
        return lambda b, n: (b * nb + n, cb)

    d_intra, d_q, d_k = _ret_decay_tables()
    table = lambda lanes: pl.BlockSpec((RET_HEADS, RET_CHUNK, lanes), lambda b, n: (0, 0, 0))
    rows = pl.BlockSpec((WINDOW, D_MODEL), col(0))
    return pl.pallas_call(
        _mixers_kernel,
        grid=(batch, nb),
        in_specs=[pl.BlockSpec(memory_space=pltpu.SMEM),
                  pl.BlockSpec((WINDOW, D_MODEL), col(COL_QA)),
                  pl.BlockSpec((WINDOW, PROJ_TN), lambda b, n: (b * nb + jnp.maximum(n - 1, 0), 0)),
                  pl.BlockSpec((WINDOW, PROJ_TN), col(0)),
                  pl.BlockSpec((RET_CHUNK, 1), col(0)),
                  pl.BlockSpec((1, RET_DIM // 2), lambda b, n: (0, 0)),
                  table(RET_CHUNK), table(RET_DIM // 2), table(RET_DIM // 2),
                  pl.BlockSpec((RET_CHUNK, D_MODEL), col(COL_QR)),
                  pl.BlockSpec((RET_CHUNK, D_MODEL), col(COL_KR)),
                  pl.BlockSpec((RET_CHUNK, D_MODEL), col(COL_VR)),
                  pl.BlockSpec((RET_CHUNK, D_MODEL), col(COL_GR)),
                  pl.BlockSpec((1, D_MODEL), lambda b, n: (0, 0))],
        out_specs=[rows, rows],
        out_shape=[jax.ShapeDtypeStruct((t, D_MODEL), BF16), jax.ShapeDtypeStruct((t, D_MODEL), BF16)],
        scratch_shapes=[pltpu.VMEM((RET_HEADS, RET_DIM, RET_DIM), F32)],
        compiler_params=_params(("arbitrary", "arbitrary"), 40),
        name="mixers",
    )(sinks, proj, proj_kv, proj_kv, pos, inv_freq, d_intra, d_q, d_k, proj, proj, proj, proj, ret_gain)


def _route(logits):
    lane = lax.broadcasted_iota(jnp.int32, logits.shape, 1)
    lane_f = lane.astype(F32)
    is_g = lane < N_GROUPS
    gl = jnp.where(is_g, logits, NEG)
    gmax = jnp.max(gl, axis=-1, keepdims=True)
    gsel = jnp.min(jnp.where(gl == gmax, lane_f, float(LANES)), axis=-1, keepdims=True)
    gsum = jnp.sum(jnp.where(is_g, jnp.exp(gl - gmax), 0.0), axis=-1, keepdims=True)
    g_w = 1.0 / gsum
    grp = ((lane - N_GROUPS) >> 4).astype(F32)
    is_e = (lane >= N_GROUPS) & (lane < N_GROUPS + N_EXPERTS) & (grp == gsel)
    el = jnp.where(is_e, logits, NEG)
    v1 = jnp.max(el, axis=-1, keepdims=True)
    i1 = jnp.min(jnp.where(el == v1, lane_f, float(LANES)), axis=-1, keepdims=True)
    el2 = jnp.where(lane_f == i1, NEG, el)
    v2 = jnp.max(el2, axis=-1, keepdims=True)
    i2 = jnp.min(jnp.where(el2 == v2, lane_f, float(LANES)), axis=-1, keepdims=True)
    tt = jnp.exp(v2 - v1)
    w1 = g_w / (1.0 + tt)
    w2 = g_w * tt / (1.0 + tt)
    return jnp.where(lane == 0, i1 - N_GROUPS,
                     jnp.where(lane == 1, i2 - N_GROUPS,
                               jnp.where(lane == 2, w1, jnp.where(lane == 3, w2, 0.0))))


def _branch_kernel(attn_ref, ret_ref, ga_ref, gr_ref, wa_ref, wr_ref, o_ref):
    a = jnp.dot(attn_ref[...], wa_ref[...].astype(BF16), preferred_element_type=F32)
    r = jnp.dot(ret_ref[...], wr_ref[...].astype(BF16), preferred_element_type=F32)
    o_ref[...] = (jax.nn.sigmoid(ga_ref[...].astype(F32)) * a
                  + jax.nn.sigmoid(gr_ref[...].astype(F32)) * r).astype(BF16)


def _branch(attn, ret, proj, wa, wr):
    t = attn.shape[0]
    tm, tn = BRANCH_TM, BRANCH_TN
    per_slab = D_MODEL // tn
    return pl.pallas_call(
        _branch_kernel,
        grid=(t // tm, D_MODEL // tn),
        in_specs=[pl.BlockSpec((tm, D_MODEL), lambda i, j: (i, 0)),
                  pl.BlockSpec((tm, D_MODEL), lambda i, j: (i, 0)),
                  pl.BlockSpec((tm, tn), lambda i, j: (i, COL_GA * per_slab + j)),
                  pl.BlockSpec((tm, tn), lambda i, j: (i, COL_GRT * per_slab + j)),
                  pl.BlockSpec((D_MODEL, tn), lambda i, j: (0, j)),
                  pl.BlockSpec((D_MODEL, tn), lambda i, j: (0, j))],
        out_specs=pl.BlockSpec((tm, tn), lambda i, j: (i, j)),
        out_shape=jax.ShapeDtypeStruct((t, D_MODEL), BF16),
        compiler_params=_params(("arbitrary", "arbitrary"), 48),
        name="branch",
    )(attn, ret, proj, proj, wa, wr)


def _mixout_kernel(m_ref, x_ref, mod_ref, g2_ref, wo_ref, wrt_ref, brt_ref,
                   x1_ref, h2_ref, route_ref, count_ref, meta_ref, carry_ref):
    @pl.when(pl.program_id(0) == 0)
    def _():
        carry_ref[...] = jnp.zeros_like(carry_ref)

    mix = jnp.dot(m_ref[...], wo_ref[...], preferred_element_type=F32)
    x1 = x_ref[...] + mod_ref[0, 2:3, :] * mix
    x1_ref[...] = x1
    var = jnp.mean(x1 * x1, axis=-1, keepdims=True)
    h2 = x1 * lax.rsqrt(var + EPS) * g2_ref[...]
    h2 = h2 * (1.0 + mod_ref[0, 4:5, :]) + mod_ref[0, 3:4, :]
    tm = h2.shape[0]
    for j in range(SLAB_ROWS):
        lo = h2[:, j * LANES:(j + 1) * LANES]
        hi = h2[:, HALF_D + j * LANES:HALF_D + (j + 1) * LANES]
        h2_ref[_slab_rows(j, tm), :] = _pack_pair(lo, hi)
    h_hi = h2.astype(BF16)
    h_lo = (h2 - h_hi.astype(F32)).astype(BF16)
    w_rt = wrt_ref[...]
    w_hi = w_rt.astype(BF16)
    w_lo = (w_rt - w_hi.astype(F32)).astype(BF16)
    hi_both = jnp.dot(h_hi, jnp.concatenate([w_hi, w_lo], axis=1), preferred_element_type=F32)
    logits = (hi_both[:, :LANES] + hi_both[:, LANES:]
              + jnp.dot(h_lo, w_hi, preferred_element_type=F32) + brt_ref[...])
    route = _route(logits)

    lane = lax.broadcasted_iota(jnp.int32, route.shape, 1)
    lane_f = lane.astype(F32)
    hot1 = lane_f == route[:, 0:1]
    hot2 = lane_f == route[:, 1:2]
    both = jnp.where(hot1 | hot2, 1.0, 0.0)
    ii = lax.broadcasted_iota(jnp.int32, (tm, tm), 0)
    jj = lax.broadcasted_iota(jnp.int32, (tm, tm), 1)
    lower = jnp.where(ii > jj, 1.0, 0.0).astype(BF16)
    before = jnp.dot(lower, both.astype(BF16), preferred_element_type=F32) + carry_ref[...]
    r1 = jnp.sum(jnp.where(hot1, before, 0.0), axis=-1, keepdims=True)
    r2 = jnp.sum(jnp.where(hot2, before, 0.0), axis=-1, keepdims=True)
    route = jnp.where(lane == 4, r1, jnp.where(lane == 5, r2, route))
    route_ref[...] = route
    meta_ref[...] = route.T[0:SLAB_ROWS, :].astype(I32)
    carry = carry_ref[...] + jnp.sum(both, axis=0, keepdims=True)
    carry_ref[...] = carry
    count_ref[...] = carry


def _mixout(merged, x2, mod6, gain2, wo, w_rt, b_rt, seq):
    t = x2.shape[0]
    tm = MIXOUT_TM
    tiles_per_batch = seq // tm
    row = lambda i: (i, 0)
    const = lambda i: (0, 0)
    return pl.pallas_call(
        _mixout_kernel,
        grid=(t // tm,),
        in_specs=[pl.BlockSpec((tm, D_MODEL), row),
                  pl.BlockSpec((tm, D_MODEL), row),
                  pl.BlockSpec((1, 6, D_MODEL), lambda i: (i // tiles_per_batch, 0, 0)),
                  pl.BlockSpec((1, D_MODEL), const),
                  pl.BlockSpec((D_MODEL, D_MODEL), const, pipeline_mode=pl.Buffered(1)),
                  pl.BlockSpec((D_MODEL, LANES), const),
                  pl.BlockSpec((1, LANES), const)],
        out_specs=[pl.BlockSpec((tm, D_MODEL), row),
                   pl.BlockSpec((tm * SLAB_ROWS, SLAB_LANES), row),
                   pl.BlockSpec((tm, LANES), row),
                   pl.BlockSpec((1, LANES), const),
                   pl.BlockSpec((SLAB_ROWS, tm), lambda i: (0, i))],
        out_shape=[jax.ShapeDtypeStruct((t, D_MODEL), F32),
                   jax.ShapeDtypeStruct((t * SLAB_ROWS, SLAB_LANES), I32),
                   jax.ShapeDtypeStruct((t, LANES), F32),
                   jax.ShapeDtypeStruct((1, LANES), F32),
                   jax.ShapeDtypeStruct((SLAB_ROWS, t), I32)],
        scratch_shapes=[pltpu.VMEM((1, LANES), F32)],
        compiler_params=_params(("arbitrary",), 56),
        name="mixout",
    )(merged, x2, mod6, gain2, wo, w_rt, b_rt)


PAD_BITS = (64, 32, 16, 8, 4, 2, 1)


def _sorted_row(tables, tok, k):
    e_refs, r_refs, blk_row_ref = tables[0:TOP_K], tables[TOP_K:2 * TOP_K], tables[2 * TOP_K]
    return blk_row_ref[e_refs[k][tok]] + r_refs[k][tok]


N_ROUTE_TABLES = 2 * TOP_K + 1


def _dispatch_kernel(*refs):
    tables = refs[:N_ROUTE_TABLES]
    zstart_ref, zcount_ref, h2_ref, xs_ref, dest_ref, zero_ref, sem, zsem = refs[N_ROUTE_TABLES:]
    n_assign = tables[0].shape[0] * TOP_K
    zero_ref[...] = jnp.zeros_like(zero_ref)

    def zero_copy(start, rows):
        return pltpu.make_async_copy(zero_ref.at[pl.ds(0, rows * SLAB_ROWS), :],
                                     xs_ref.at[pl.ds(start * SLAB_ROWS, rows * SLAB_ROWS), :], zsem)

    def fill(e, wait):
        start = zstart_ref[e]
        pad = zcount_ref[e]
        for bit in PAD_BITS:
            @pl.when((pad & bit) != 0)
            def _(start=start, bit=bit):
                cp = zero_copy(start, bit)
                cp.wait() if wait else cp.start()
            start = start + (pad & bit)

    lax.fori_loop(0, N_EXPERTS, lambda e, c: (fill(e, False), c)[1], 0)

    def issue(tok, carry):
        src = h2_ref.at[pl.ds(pl.multiple_of(tok * SLAB_ROWS, SLAB_ROWS), SLAB_ROWS), :]
        for k in range(TOP_K):
            row = _sorted_row(tables, tok, k)
            dest_ref[tok * TOP_K + k] = row
            dst = pl.multiple_of(row * SLAB_ROWS, SLAB_ROWS)
            pltpu.make_async_copy(src, xs_ref.at[pl.ds(dst, SLAB_ROWS), :], sem).start()
        return carry

    lax.fori_loop(0, n_assign // TOP_K, issue, 0, unroll=8)

    def drain(i, carry):
        pltpu.make_async_copy(h2_ref.at[pl.ds(0, ROWS_PER_WAIT * SLAB_ROWS), :],
                              xs_ref.at[pl.ds(0, ROWS_PER_WAIT * SLAB_ROWS), :], sem).wait()
        return carry

    lax.fori_loop(0, n_assign // ROWS_PER_WAIT, drain, 0)
    lax.fori_loop(0, N_EXPERTS, lambda e, c: (fill(e, True), c)[1], 0)


def _dispatch(tables, zstart, zcount, h2_slab, n_pad):
    n_assign = tables[0].shape[0] * TOP_K
    return pl.pallas_call(
        _dispatch_kernel,
        grid_spec=pltpu.PrefetchScalarGridSpec(
            num_scalar_prefetch=N_ROUTE_TABLES + 2,
            grid=(1,),
            in_specs=[pl.BlockSpec(memory_space=pltpu.HBM)],
            out_specs=[pl.BlockSpec(memory_space=pl.ANY),
                       pl.BlockSpec(memory_space=pltpu.SMEM)],
            scratch_shapes=[pltpu.VMEM((PAD_BITS[0] * SLAB_ROWS, SLAB_LANES), I32),
                            pltpu.SemaphoreType.DMA(()),
                            pltpu.SemaphoreType.DMA(())]),
        out_shape=[jax.ShapeDtypeStruct((n_pad * SLAB_ROWS, SLAB_LANES), I32),
                   jax.ShapeDtypeStruct((n_assign,), I32)],
        compiler_params=_params(("arbitrary",), 16),
        name="dispatch",
    )(*tables, zstart, zcount, h2_slab)


def _experts_kernel(item_e_ref, item_blk_ref, item_nb_ref, x_blk_ref,
                    x0_ref, x1_ref, x2_ref, x3_ref, wg_ref, wu_ref, wd_ref, y_ref,
                    xb_ref, acc_ref, yp_ref, sem):
    w = pl.program_id(0)
    c = pl.program_id(1)
    last_c = pl.num_programs(1) - 1
    nb = item_nb_ref[w]
    blk0 = item_blk_ref[w]
    x_refs = (x0_ref, x1_ref, x2_ref, x3_ref)

    blk_rows = MOE_BLOCK * SLAB_ROWS

    def out_copy(s):
        return pltpu.make_async_copy(yp_ref.at[s], y_ref.at[pl.ds((blk0 + s) * blk_rows, blk_rows), :],
                                     sem.at[s])

    def wait_out(count):
        for s in range(ITEM_BLOCKS):
            @pl.when(s < count)
            def _(s=s):
                out_copy(s).wait()

    @pl.when((c == last_c) & (w > 0))
    def _():
        wait_out(item_nb_ref[jnp.maximum(w - 1, 0)])

    def run(n_live):
        rows = n_live * MOE_BLOCK

        @pl.when(c == 0)
        def _():
            for s in range(n_live):
                for j in range(SLAB_ROWS):
                    lo, hi = _unpack_pair(x_refs[s][_slab_rows(j, MOE_BLOCK), :])
                    r0 = s * MOE_BLOCK
                    xb_ref[r0:r0 + MOE_BLOCK, j * LANES:(j + 1) * LANES] = lo.astype(BF16)
                    xb_ref[r0:r0 + MOE_BLOCK, HALF_D + j * LANES:HALF_D + (j + 1) * LANES] = hi.astype(BF16)

        xs = xb_ref[0:rows, :]
        g = jnp.dot(xs, wg_ref[...].astype(BF16), preferred_element_type=F32)
        u = jnp.dot(xs, wu_ref[...].astype(BF16), preferred_element_type=F32)
        hid = (g * jax.nn.sigmoid(g) * u).astype(BF16)
        y = jnp.dot(hid, wd_ref[...].astype(BF16), preferred_element_type=F32)

        @pl.when(c == 0)
        def _():
            acc_ref[0:rows, :] = y

        @pl.when((c > 0) & (c < last_c))
        def _():
            acc_ref[0:rows, :] += y

        @pl.when(c == last_c)
        def _():
            for s in range(n_live):
                r0 = s * MOE_BLOCK
                for j in range(SLAB_ROWS):
                    lo_cols = slice(j * LANES, (j + 1) * LANES)
                    hi_cols = slice(HALF_D + j * LANES, HALF_D + (j + 1) * LANES)
                    yp_ref[s, _slab_rows(j, MOE_BLOCK), :] = _pack_pair(
                        acc_ref[r0:r0 + MOE_BLOCK, lo_cols] + y[r0:r0 + MOE_BLOCK, lo_cols],
                        acc_ref[r0:r0 + MOE_BLOCK, hi_cols] + y[r0:r0 + MOE_BLOCK, hi_cols])
                out_copy(s).start()

    for n_live in range(1, ITEM_BLOCKS + 1):
        pl.when(nb == n_live)(functools.partial(run, n_live))

    @pl.when((c == last_c) & (w == pl.num_programs(0) - 1))
    def _():
        wait_out(nb)


def _experts(item_e, item_blk, item_nb, x_sorted, w_gate, w_up, w_down, n_blocks):
    n_items = item_e.shape[0]
    d = D_MODEL
    fc = EXPERT_FC
    x3 = x_sorted
    blk_rows = MOE_BLOCK * SLAB_ROWS
    n_chunks = EXPERT_DIM // fc
    assert n_chunks >= 2, "the last hidden chunk adds onto the accumulator of the earlier ones"

    slot = jnp.arange(ITEM_BLOCKS, dtype=jnp.int32)[:, None]
    x_blk = jnp.maximum(lax.cummax(jnp.where(slot < item_nb[None, :], item_blk[None, :] + slot, -1), axis=1), 0)
    x_blk = x_blk.reshape(-1).astype(jnp.int32)

    def x_spec(s):
        return pl.BlockSpec((blk_rows, SLAB_LANES),
                            lambda w, c, ie, ib, inb, xb: (xb[s * n_items + w], 0))

    def chunk(w, c, inb):
        return jnp.where(inb[w] > 0, c, n_chunks - 1)

    return pl.pallas_call(
        _experts_kernel,
        grid_spec=pltpu.PrefetchScalarGridSpec(
            num_scalar_prefetch=4,
            grid=(n_items, n_chunks),
            in_specs=[x_spec(0), x_spec(1), x_spec(2), x_spec(3),
                      pl.BlockSpec((None, d, fc), lambda w, c, ie, ib, inb, xb: (ie[w], 0, chunk(w, c, inb))),
                      pl.BlockSpec((None, d, fc), lambda w, c, ie, ib, inb, xb: (ie[w], 0, chunk(w, c, inb))),
                      pl.BlockSpec((None, fc, d), lambda w, c, ie, ib, inb, xb: (ie[w], chunk(w, c, inb), 0))],
            out_specs=pl.BlockSpec(memory_space=pl.ANY),
            scratch_shapes=[pltpu.VMEM((ITEM_BLOCKS * MOE_BLOCK, d), BF16),
                            pltpu.VMEM((ITEM_BLOCKS * MOE_BLOCK, d), F32),
                            pltpu.VMEM((ITEM_BLOCKS, blk_rows, SLAB_LANES), I32),
                            pltpu.SemaphoreType.DMA((ITEM_BLOCKS,))]),
        out_shape=jax.ShapeDtypeStruct((n_blocks * blk_rows, SLAB_LANES), I32),
        compiler_params=_params(("arbitrary", "arbitrary"), 56),
        name="experts",
    )(item_e, item_blk, item_nb, x_blk, x3, x3, x3, x3, w_gate, w_up, w_down)


def _combine_kernel(dest_ref, x1_ref, route_ref, mod_ref, gain_ref, y_ref, o_ref, ybuf_ref, sem):
    tm = COMBINE_TM
    i = pl.program_id(0)
    n_tiles = pl.num_programs(0)

    def row_copy(slot, k, r, src):
        return pltpu.make_async_copy(
            y_ref.at[pl.ds(pl.multiple_of(src * SLAB_ROWS, SLAB_ROWS), SLAB_ROWS), :],
            ybuf_ref.at[slot, k, pl.ds(pl.multiple_of(r * SLAB_ROWS, SLAB_ROWS), SLAB_ROWS), :],
            sem.at[slot])

    def wait_slot(slot):
        for k in range(TOP_K):
            pltpu.make_async_copy(y_ref.at[pl.ds(0, tm * SLAB_ROWS), :], ybuf_ref.at[slot, k],
                                  sem.at[slot]).wait()

    def issue_tile(tile, slot):
        base = tile * tm * TOP_K

        def issue(r, carry):
            for k in range(TOP_K):
                row_copy(slot, k, r, dest_ref[base + r * TOP_K + k]).start()
            return carry

        lax.fori_loop(0, tm, issue, 0, unroll=4)

    @pl.when(i == 0)
    def _():
        issue_tile(0, 0)

    for nxt in range(2):
        @pl.when((i + 1 < n_tiles) & ((i + 1) % 2 == nxt))
        def _(nxt=nxt):
            issue_tile(i + 1, nxt)

    slot = i % 2
    wait_slot(slot)

    route = route_ref[...]
    w0 = route[:, 2:3]
    w1 = route[:, 3:4]
    ssq = jnp.zeros((tm, 1), F32)
    for j in range(SLAB_ROWS):
        rows = _slab_rows(j, tm)
        y0 = _unpack_pair(ybuf_ref[slot, 0, rows, :])
        y1 = _unpack_pair(ybuf_ref[slot, 1, rows, :])
        for part, off in ((0, j * LANES), (1, HALF_D + j * LANES)):
            cols = slice(off, off + LANES)
            ffn = w0 * y0[part] + w1 * y1[part]
            x2 = x1_ref[:, cols] + mod_ref[0, 5:6, cols] * ffn
            o_ref[:, cols] = x2
            ssq = ssq + jnp.sum(x2 * x2, axis=-1, keepdims=True)
    o_ref[...] = o_ref[...] * lax.rsqrt(ssq * (1.0 / D_MODEL) + EPS) * gain_ref[...]


def _combine(dest, x1, route, mod6, gain, y_sorted, seq):
    t, d = x1.shape
    tm = COMBINE_TM
    tiles_per_batch = seq // tm
    return pl.pallas_call(
        _combine_kernel,
        grid_spec=pltpu.PrefetchScalarGridSpec(
            num_scalar_prefetch=1,
            grid=(t // tm,),
            in_specs=[pl.BlockSpec((tm, d), lambda i, *_: (i, 0)),
                      pl.BlockSpec((tm, LANES), lambda i, *_: (i, 0)),
                      pl.BlockSpec((1, 6, d), lambda i, *_: (i // tiles_per_batch, 0, 0)),
                      pl.BlockSpec((1, d), lambda i, *_: (0, 0)),
                      pl.BlockSpec(memory_space=pl.ANY)],
            out_specs=pl.BlockSpec((tm, d), lambda i, *_: (i, 0)),
            scratch_shapes=[pltpu.VMEM((2, TOP_K, tm * SLAB_ROWS, SLAB_LANES), I32),
                            pltpu.SemaphoreType.DMA((2,))]),
        out_shape=jax.ShapeDtypeStruct((t, d), F32),
        compiler_params=_params(("arbitrary",), 24),
        name="combine",
    )(dest, x1, route, mod6, gain, y_sorted)


def _dispatch_tables(meta, counts, t):
    n_assign = t * TOP_K
    n_pad = -(-(n_assign + N_EXPERTS * (MOE_BLOCK - 1)) // MOE_BLOCK) * MOE_BLOCK
    n_blocks = n_pad // MOE_BLOCK
    n_items = N_EXPERTS + n_assign // (ITEM_BLOCKS * MOE_BLOCK)

    cnt = counts[0, :N_EXPERTS].astype(jnp.int32)
    blocks_e = (cnt + MOE_BLOCK - 1) // MOE_BLOCK
    blk_end = jnp.cumsum(blocks_e)
    blk_start = blk_end - blocks_e
    tables = (meta[0], meta[1], meta[4], meta[5], (blk_start * MOE_BLOCK).astype(jnp.int32))
    zstart = (blk_start * MOE_BLOCK + cnt).astype(jnp.int32)
    zcount = (blocks_e * MOE_BLOCK - cnt).astype(jnp.int32)

    items_e = (blocks_e + ITEM_BLOCKS - 1) // ITEM_BLOCKS
    item_end = jnp.cumsum(items_e)
    item_start = item_end - items_e
    w = jnp.arange(n_items, dtype=jnp.int32)
    live = w < item_end[-1]
    w_live = jnp.minimum(w, item_end[-1] - 1)
    e_w = jnp.minimum(jnp.sum((item_end[None, :] <= w_live[:, None]).astype(jnp.int32), axis=1), N_EXPERTS - 1)
    j_w = w_live - item_start[e_w]
    item_blk = (blk_start[e_w] + ITEM_BLOCKS * j_w).astype(jnp.int32)
    item_nb = jnp.where(live, jnp.clip(blocks_e[e_w] - ITEM_BLOCKS * j_w, 0, ITEM_BLOCKS), 0).astype(jnp.int32)
    return tables, zstart, zcount, e_w, item_blk, item_nb, n_blocks


def kernel(x, c, positions, norm1_gain, norm2_gain, final_norm_gain, w_ada, b_ada, w_in, attn_sinks,
           ret_norm_gain, w_branch_attn, w_branch_ret, w_out, w_router_group, b_router_group,
           w_router_expert, b_router_expert, w_expert_gate, w_expert_up, w_expert_down):
    batch, seq, d = x.shape
    t = batch * seq
    depth = w_ada.shape[0]
    half = RET_DIM // 2
    inv_freq = (ROPE_BASE ** (-jnp.arange(half, dtype=F32) / half)).reshape(1, half)
    pos = positions.reshape(t, 1)
    c8 = jnp.pad(c, ((0, 8 - batch), (0, 0)))
    xf = x.reshape(t, d)

    assert depth == 1, "the fused final norm assumes a single layer"
    for layer in range(depth):
        mod6 = _ada(c8, w_ada[layer], b_ada[layer].reshape(1, -1))[:batch].reshape(batch, 6, d)
        proj, proj_kv = _proj(xf, norm1_gain[layer].reshape(1, d), mod6, w_in[layer], seq)
        attn, ret = _mixers(proj, proj_kv, attn_sinks[layer], pos, inv_freq,
                            ret_norm_gain[layer].reshape(1, d), batch, seq)

        pad = LANES - N_GROUPS - N_EXPERTS
        w_rt = jnp.concatenate([w_router_group[layer], w_router_expert[layer],
                                jnp.zeros((d, pad), F32)], axis=1)
        b_rt = jnp.concatenate([b_router_group[layer], b_router_expert[layer],
                                jnp.zeros((pad,), F32)]).reshape(1, LANES)
        merged = _branch(attn, ret, proj, w_branch_attn[layer], w_branch_ret[layer])
        x1, h2, route, counts, meta = _mixout(merged, xf, mod6, norm2_gain[layer].reshape(1, d),
                                              w_out[layer].astype(BF16), w_rt, b_rt, seq)
        tables, zstart, zcount, item_e, item_blk, item_nb, n_blocks = _dispatch_tables(meta, counts, t)
        x_sorted, dest = _dispatch(tables, zstart, zcount, h2, n_blocks * MOE_BLOCK)
        y_sorted = _experts(item_e, item_blk, item_nb, x_sorted,
                            w_expert_gate[layer], w_expert_up[layer], w_expert_down[layer], n_blocks)
        xf = _combine(dest, x1, route, mod6, final_norm_gain.reshape(1, d), y_sorted, seq)
    return xf.reshape(batch, seq, d)
```

```python
import functools
import math

import jax
import jax.numpy as jnp
import numpy as np
from jax import lax
from jax.experimental import pallas as pl
from jax.experimental.pallas import tpu as pltpu

F32 = jnp.float32
BF16 = jnp.bfloat16

D_MODEL = 2048
ATTN_HEAD_DIM = 64
ATTN_HEADS = 32
ATTN_KV_HEADS = 4
ATTN_GROUP = 8
WINDOW = 128
RET_HEADS = 8
RET_DIM = 256
RET_CHUNK = 128
ROPE_BASE = 10000.0
N_GROUPS = 4
EXPERTS_PER_GROUP = 16
N_EXPERTS = 64
TOP_K = 2
EXPERT_DIM = 1024
MOE_BLOCK = 128
EPS = 1e-6
NEG = -1e30

MIB = 1024 * 1024
LANES = 128
PROJ_TN = 512
PROJ_TM = 1024
KV_SRC_TILE = 4
ADA_EARLY = 2 * D_MODEL
ADA_LATE_TN = 256
ADA_LATE_STEPS = 4 * D_MODEL // ADA_LATE_TN
BRANCH_TM = 1024
BRANCH_TN = 512
MIXOUT_TM = 512
ITEM_BLOCKS = 4
EXPERT_FC = 512
COMBINE_TM = 128
SLAB_ROWS = 8
SLAB_LANES = LANES
HALF_D = D_MODEL // 2
ROWS_PER_WAIT = 128
I32 = jnp.int32

COL_QA, COL_QR, COL_KR, COL_VR, COL_GR, COL_GA, COL_GRT = 0, 1, 2, 3, 4, 5, 6

LOG_GAMMA = [math.log1p(-(2.0 ** (-5.0 - h))) for h in range(RET_HEADS)]


def _params(sem, vmem_mib):
    return pltpu.CompilerParams(dimension_semantics=sem, vmem_limit_bytes=vmem_mib * MIB)


def _pack_pair(lo, hi):
    lo_b = lax.bitcast_convert_type(lo.astype(BF16).astype(F32), I32)
    hi_b = lax.bitcast_convert_type(hi.astype(BF16).astype(F32), I32)
    return hi_b | lax.shift_right_logical(lo_b, jnp.full_like(lo_b, 16))


def _unpack_pair(w):
    lo = lax.bitcast_convert_type(w << 16, F32)
    hi = lax.bitcast_convert_type(w & jnp.int32(-65536), F32)
    return lo, hi


def _slab_rows(j, n_tokens):
    return pl.ds(j, n_tokens, stride=SLAB_ROWS)


def _ada_kernel(c_ref, w_ref, b_ref, o_ref):
    c = c_ref[...]
    a = (c * jax.nn.sigmoid(c)).astype(BF16)
    o_ref[...] = jnp.dot(a, w_ref[...].astype(BF16), preferred_element_type=F32) + b_ref[...]


def _ada(c8, w_ada, b_ada, n):
    tn = 1024
    return pl.pallas_call(
        _ada_kernel,
        grid=(n // tn,),
        in_specs=[pl.BlockSpec((8, D_MODEL), lambda j: (0, 0)),
                  pl.BlockSpec((D_MODEL, tn), lambda j: (0, j)),
                  pl.BlockSpec((1, tn), lambda j: (0, j))],
        out_specs=pl.BlockSpec((8, tn), lambda j: (0, j)),
        out_shape=jax.ShapeDtypeStruct((8, n), F32),
        compiler_params=_params(("arbitrary",), 40),
        name="ada",
    )(c8, w_ada, b_ada)


def _proj_kernel(x_ref, g_ref, mod_ref, wlo_ref, whi_ref, c_ref, wada_ref, bada_ref,
                 o_ref, kv_ref, late_ref, h_ref):
    v = pl.program_id(1)
    last = pl.num_programs(1) - 1

    @pl.when(pl.program_id(0) * pl.num_programs(1) + v < ADA_LATE_STEPS)
    def _():
        _ada_kernel(c_ref, wada_ref, bada_ref, late_ref)

    @pl.when(v == 0)
    def _():
        x = x_ref[...]
        var = jnp.mean(x * x, axis=-1, keepdims=True)
        y = x * lax.rsqrt(var + EPS) * g_ref[...]
        h_ref[...] = (y * (1.0 + mod_ref[0, 1:2, :]) + mod_ref[0, 0:1, :]).astype(BF16)

    @pl.when(v < last)
    def _():
        w = jnp.concatenate([wlo_ref[...].astype(BF16), whi_ref[...].astype(BF16)], axis=1)
        o_ref[...] = jnp.dot(h_ref[...], w, preferred_element_type=F32).astype(BF16)

    @pl.when(v == last)
    def _():
        kv_ref[...] = jnp.dot(h_ref[...], wlo_ref[...].astype(BF16),
                              preferred_element_type=F32).astype(BF16)


def _proj_w_tile(v, n_wide):
    return jnp.where(v < 2, 2 * v, jnp.where(v < n_wide, 2 * v + 1, KV_SRC_TILE))


def _proj(x2, gain, mod_early, w_in, c8, w_ada, b_ada, seq):
    t = x2.shape[0]
    n = w_in.shape[1]
    tiles_per_batch = seq // PROJ_TM
    n_wide = (n - PROJ_TN) // (2 * PROJ_TN)
    n_late = w_ada.shape[1] - ADA_EARLY
    assert n_late == ADA_LATE_STEPS * ADA_LATE_TN and ADA_LATE_STEPS <= (t // PROJ_TM) * (n_wide + 1)
    early_tiles = ADA_EARLY // ADA_LATE_TN

    def late_tile(i, v):
        return jnp.minimum(i * (n_wide + 1) + v, ADA_LATE_STEPS - 1)

    return pl.pallas_call(
        _proj_kernel,
        grid=(t // PROJ_TM, n_wide + 1),
        in_specs=[pl.BlockSpec((PROJ_TM, D_MODEL), lambda i, v: (i, 0)),
                  pl.BlockSpec((1, D_MODEL), lambda i, v: (0, 0)),
                  pl.BlockSpec((1, 2, D_MODEL), lambda i, v: (i // tiles_per_batch, 0, 0)),
                  pl.BlockSpec((D_MODEL, PROJ_TN), lambda i, v: (0, _proj_w_tile(v, n_wide))),
                  pl.BlockSpec((D_MODEL, PROJ_TN),
                               lambda i, v: (0, jnp.where(v < n_wide, _proj_w_tile(v, n_wide) + 1, KV_SRC_TILE))),
                  pl.BlockSpec((8, D_MODEL), lambda i, v: (0, 0)),
                  pl.BlockSpec((D_MODEL, ADA_LATE_TN), lambda i, v: (0, early_tiles + late_tile(i, v))),
                  pl.BlockSpec((1, ADA_LATE_TN), lambda i, v: (0, early_tiles + late_tile(i, v)))],
        out_specs=[pl.BlockSpec((PROJ_TM, 2 * PROJ_TN), lambda i, v: (i, jnp.minimum(v, n_wide - 1))),
                   pl.BlockSpec((PROJ_TM, PROJ_TN), lambda i, v: (i, 0)),
                   pl.BlockSpec((8, ADA_LATE_TN), lambda i, v: (0, late_tile(i, v)))],
        out_shape=[jax.ShapeDtypeStruct((t, n - PROJ_TN), BF16),
                   jax.ShapeDtypeStruct((t, PROJ_TN), BF16),
                   jax.ShapeDtypeStruct((8, n_late), F32)],
        scratch_shapes=[pltpu.VMEM((PROJ_TM, D_MODEL), BF16)],
        compiler_params=_params(("arbitrary", "arbitrary"), 56),
        name="proj",
    )(x2, gain, mod_early, w_in, w_in, c8, w_ada, b_ada)


def _attn_stages(sink_ref, q_ref, kvp_ref, kvc_ref, o_ref):
    n = pl.program_id(1)
    kvp = kvp_ref[...]
    kvc = kvc_ref[...]
    qi = lax.broadcasted_iota(jnp.int32, (WINDOW, WINDOW), 0)
    sj = lax.broadcasted_iota(jnp.int32, (WINDOW, WINDOW), 1)
    valid_prev = (sj > qi) & (n > 0)
    valid_cur = sj <= qi
    sink_col = sj == 0
    first_row = lax.broadcasted_iota(jnp.int32, (2 * WINDOW, 1), 0) == 0
    dh = ATTN_HEAD_DIM
    kv_w = ATTN_KV_HEADS * dh
    scale = jnp.asarray(dh ** -0.5, BF16)
    n_pairs = ATTN_GROUP // 2

    def group_operands(kv):
        kband = jnp.concatenate([kvp[:, kv * dh:(kv + 1) * dh],
                                 kvc[:, kv * dh:(kv + 1) * dh]], axis=0) * scale
        vband = jnp.concatenate([kvp[:, kv_w + kv * dh:kv_w + (kv + 1) * dh],
                                 kvc[:, kv_w + kv * dh:kv_w + (kv + 1) * dh]], axis=0)
        vband = jnp.where(first_row, jnp.zeros_like(vband), vband)
        zeros = jnp.zeros_like(kband)
        ones = jnp.ones_like(vband)
        k_pad = (jnp.concatenate([kband, zeros], axis=1), jnp.concatenate([zeros, kband], axis=1))
        pv_rhs = jnp.concatenate(
            [jnp.concatenate([vband, zeros, ones, zeros], axis=1),
             jnp.concatenate([zeros, vband, zeros, ones], axis=1)], axis=0)
        q_rows = jnp.concatenate(
            [q_ref[:, (kv * ATTN_GROUP + 2 * p) * dh:(kv * ATTN_GROUP + 2 * p + 2) * dh]
             for p in range(n_pairs)], axis=0)
        scores = [lax.dot_general(q_rows, k_pad[idx], (((1,), (1,)), ((), ())),
                                  preferred_element_type=F32) for idx in range(2)]
        return scores, pv_rhs

    nxt = group_operands(0)
    for kv in range(ATTN_KV_HEADS):
        scores, pv_rhs = nxt
        if kv + 1 < ATTN_KV_HEADS:
            nxt = group_operands(kv + 1)
        prob_rows = []
        for pair in range(n_pairs):
            rows = slice(pair * WINDOW, (pair + 1) * WINDOW)
            probs = []
            for idx in range(2):
                s = scores[idx][rows]
                sink = sink_ref[kv * ATTN_GROUP + 2 * pair + idx]
                s_prev = jnp.where(sink_col, sink, jnp.where(valid_prev, s[:, :WINDOW], NEG))
                s_cur = jnp.where(valid_cur, s[:, WINDOW:], NEG)
                m = jnp.max(jnp.maximum(s_prev, s_cur), axis=-1, keepdims=True)
                probs += [jnp.exp(s_prev - m).astype(BF16), jnp.exp(s_cur - m).astype(BF16)]
            prob_rows.append(jnp.concatenate(probs, axis=-1))
        r = jnp.dot(jnp.concatenate(prob_rows, axis=0), pv_rhs, preferred_element_type=F32)
        for pair in range(n_pairs):
            rows = slice(pair * WINDOW, (pair + 1) * WINDOW)
            col = (kv * ATTN_GROUP + 2 * pair) * dh
            o_ref[:, col:col + 2 * dh] = (r[rows, :2 * dh] * (1.0 / r[rows, 2 * dh:])).astype(BF16)
        yield


def _ret_decay_tables():
    lg = np.asarray(LOG_GAMMA, np.float64)[:, None, None]
    i = np.arange(RET_CHUNK, dtype=np.float64)
    diff = i[:, None] - i[None, :]
    k_scale = RET_DIM ** -0.5
    d_intra = np.where(diff >= 0, np.exp(np.maximum(diff, 0.0) * lg), 0.0) * k_scale
    lanes = np.ones((1, 1, RET_DIM // 2))
    d_q = np.exp((i[None, :, None] + 1.0) * lg) * lanes
    d_k = np.exp((RET_CHUNK - 1.0 - i[None, :, None]) * lg) * k_scale * lanes
    return jnp.asarray(d_intra, F32), jnp.asarray(d_q, F32), jnp.asarray(d_k, BF16)


def _ret_stages(pos_ref, invf_ref, di_ref, dq_ref, dk_ref, q_ref, k_ref, v_ref, g_ref, gain_ref,
                o_ref, state_ref):
    half = RET_DIM // 2
    ang = pos_ref[...].astype(F32) * invf_ref[...]
    cos = jnp.cos(ang).astype(BF16)
    sin = jnp.sin(ang).astype(BF16)

    def rot(t):
        t1, t2 = t[:, :half], t[:, half:]
        return jnp.concatenate([t1 * cos - t2 * sin, t1 * sin + t2 * cos], axis=-1)

    def both_halves(t, factor):
        return jnp.concatenate([t[:, :half] * factor, t[:, half:] * factor], axis=-1)

    def head_front(h):
        sl = slice(h * RET_DIM, (h + 1) * RET_DIM)
        qb = rot(q_ref[:, sl])
        kb = rot(k_ref[:, sl])
        intra = lax.dot_general(qb, kb, (((1,), (1,)), ((), ())),
                                preferred_element_type=F32) * di_ref[h]
        st = state_ref[h]
        cross = jnp.dot(qb, st.astype(BF16), preferred_element_type=F32)
        return kb, intra, st, cross

    front = head_front(0)
    for h in range(RET_HEADS):
        sl = slice(h * RET_DIM, (h + 1) * RET_DIM)
        kb, intra, st, cross = front
        if h + 1 < RET_HEADS:
            front = head_front(h + 1)
        vb = v_ref[:, sl]
        d_chunk = math.exp(RET_CHUNK * LOG_GAMMA[h])
        o = jnp.dot(intra.astype(BF16), vb, preferred_element_type=F32) + both_halves(cross, dq_ref[h])
        kd = both_halves(kb, dk_ref[h])
        state_ref[h] = st * d_chunk + lax.dot_general(kd, vb, (((0,), (0,)), ((), ())),
                                                      preferred_element_type=F32)
        o = o * lax.rsqrt(jnp.mean(o * o, axis=-1, keepdims=True) + EPS) * gain_ref[:, sl]
        gg = g_ref[:, sl].astype(F32)
        o_ref[:, sl] = (gg * jax.nn.sigmoid(gg) * o).astype(BF16)
        yield


def _mixers_kernel(sink_ref, qa_ref, kvp_ref, kvc_ref, pos_ref, invf_ref, di_ref, dq_ref, dk_ref,
                   qr_ref, kr_ref, vr_ref, gr_ref, gain_ref, attn_ref, ret_ref, state_ref):
    @pl.when(pl.program_id(1) == 0)
    def _():
        state_ref[...] = jnp.zeros_like(state_ref)

    attn = _attn_stages(sink_ref, qa_ref, kvp_ref, kvc_ref, attn_ref)
    ret = _ret_stages(pos_ref, invf_ref, di_ref, dq_ref, dk_ref, qr_ref, kr_ref, vr_ref, gr_ref,
                      gain_ref, ret_ref, state_ref)
    heads_per_group = RET_HEADS // ATTN_KV_HEADS
    for _ in range(ATTN_KV_HEADS):
        next(attn)
        for _ in range(heads_per_group):
            next(ret)


def _mixers(proj, proj_kv, sinks, pos, inv_freq, ret_gain, batch, seq):
    nb = seq // WINDOW
    t = batch * seq

    def col(cb):
        return lambda b, n: (b * nb + n, cb)

    d_intra, d_q, d_k = _ret_decay_tables()
    table = lambda lanes: pl.BlockSpec((RET_HEADS, RET_CHUNK, lanes), lambda b, n: (0, 0, 0))
    rows = pl.BlockSpec((WINDOW, D_MODEL), col(0))
    return pl.pallas_call(
        _mixers_kernel,
        grid=(batch, nb),
        in_specs=[pl.BlockSpec(memory_space=pltpu.SMEM),
                  pl.BlockSpec((WINDOW, D_MODEL), col(COL_QA)),
                  pl.BlockSpec((WINDOW, PROJ_TN), lambda b, n: (b * nb + jnp.maximum(n - 1, 0), 0)),
                  pl.BlockSpec((WINDOW, PROJ_TN), col(0)),
                  pl.BlockSpec((RET_CHUNK, 1), col(0)),
                  pl.BlockSpec((1, RET_DIM // 2), lambda b, n: (0, 0)),
                  table(RET_CHUNK), table(RET_DIM // 2), table(RET_DIM // 2),
                  pl.BlockSpec((RET_CHUNK, D_MODEL), col(COL_QR)),
                  pl.BlockSpec((RET_CHUNK, D_MODEL), col(COL_KR)),
                  pl.BlockSpec((RET_CHUNK, D_MODEL), col(COL_VR)),
                  pl.BlockSpec((RET_CHUNK, D_MODEL), col(COL_GR)),
                  pl.BlockSpec((1, D_MODEL), lambda b, n: (0, 0))],
        out_specs=[rows, rows],
        out_shape=[jax.ShapeDtypeStruct((t, D_MODEL), BF16), jax.ShapeDtypeStruct((t, D_MODEL), BF16)],
        scratch_shapes=[pltpu.VMEM((RET_HEADS, RET_DIM, RET_DIM), F32)],
        compiler_params=_params(("arbitrary", "arbitrary"), 40),
        name="mixers",
    )(sinks, proj, proj_kv, proj_kv, pos, inv_freq, d_intra, d_q, d_k, proj, proj, proj, proj, ret_gain)


def _route(logits):
    lane = lax.broadcasted_iota(jnp.int32, logits.shape, 1)
    lane_f = lane.astype(F32)
    is_g = lane < N_GROUPS
    gl = jnp.where(is_g, logits, NEG)
    gmax = jnp.max(gl, axis=-1, keepdims=True)
    gsel = jnp.min(jnp.where(gl == gmax, lane_f, float(LANES)), axis=-1, keepdims=True)
    gsum = jnp.sum(jnp.where(is_g, jnp.exp(gl - gmax), 0.0), axis=-1, keepdims=True)
    g_w = 1.0 / gsum
    grp = ((lane - N_GROUPS) >> 4).astype(F32)
    is_e = (lane >= N_GROUPS) & (lane < N_GROUPS + N_EXPERTS) & (grp == gsel)
    el = jnp.where(is_e, logits, NEG)
    v1 = jnp.max(el, axis=-1, keepdims=True)
    i1 = jnp.min(jnp.where(el == v1, lane_f, float(LANES)), axis=-1, keepdims=True)
    el2 = jnp.where(lane_f == i1, NEG, el)
    v2 = jnp.max(el2, axis=-1, keepdims=True)
    i2 = jnp.min(jnp.where(el2 == v2, lane_f, float(LANES)), axis=-1, keepdims=True)
    tt = jnp.exp(v2 - v1)
    w1 = g_w / (1.0 + tt)
    w2 = g_w * tt / (1.0 + tt)
    return jnp.where(lane == 0, i1 - N_GROUPS,
                     jnp.where(lane == 1, i2 - N_GROUPS,
                               jnp.where(lane == 2, w1, jnp.where(lane == 3, w2, 0.0))))


def _branch_kernel(attn_ref, ret_ref, ga_ref, gr_ref, wa_ref, wr_ref, o_ref):
    a = jnp.dot(attn_ref[...], wa_ref[...].astype(BF16), preferred_element_type=F32)
    r = jnp.dot(ret_ref[...], wr_ref[...].astype(BF16), preferred_element_type=F32)
    o_ref[...] = (jax.nn.sigmoid(ga_ref[...].astype(F32)) * a
                  + jax.nn.sigmoid(gr_ref[...].astype(F32)) * r).astype(BF16)


def _branch(attn, ret, proj, wa, wr):
    t = attn.shape[0]
    tm, tn = BRANCH_TM, BRANCH_TN
    per_slab = D_MODEL // tn
    return pl.pallas_call(
        _branch_kernel,
        grid=(t // tm, D_MODEL // tn),
        in_specs=[pl.BlockSpec((tm, D_MODEL), lambda i, j: (i, 0)),
                  pl.BlockSpec((tm, D_MODEL), lambda i, j: (i, 0)),
                  pl.BlockSpec((tm, tn), lambda i, j: (i, COL_GA * per_slab + j)),
                  pl.BlockSpec((tm, tn), lambda i, j: (i, COL_GRT * per_slab + j)),
                  pl.BlockSpec((D_MODEL, tn), lambda i, j: (0, j)),
                  pl.BlockSpec((D_MODEL, tn), lambda i, j: (0, j))],
        out_specs=pl.BlockSpec((tm, tn), lambda i, j: (i, j)),
        out_shape=jax.ShapeDtypeStruct((t, D_MODEL), BF16),
        compiler_params=_params(("arbitrary", "arbitrary"), 48),
        name="branch",
    )(attn, ret, proj, proj, wa, wr)


def _mixout_kernel(m_ref, x_ref, mod_ref, g2_ref, wo_ref, wrt_ref, brt_ref,
                   x1_ref, h2_ref, route_ref, count_ref, meta_ref, carry_ref):
    @pl.when(pl.program_id(0) == 0)
    def _():
        carry_ref[...] = jnp.zeros_like(carry_ref)

    mix = jnp.dot(m_ref[...], wo_ref[...], preferred_element_type=F32)
    x1 = x_ref[...] + mod_ref[0, 2:3, :] * mix
    x1_ref[...] = x1
    var = jnp.mean(x1 * x1, axis=-1, keepdims=True)
    h2 = x1 * lax.rsqrt(var + EPS) * g2_ref[...]
    h2 = h2 * (1.0 + mod_ref[0, 4:5, :]) + mod_ref[0, 3:4, :]
    tm = h2.shape[0]
    for j in range(SLAB_ROWS):
        lo = h2[:, j * LANES:(j + 1) * LANES]
        hi = h2[:, HALF_D + j * LANES:HALF_D + (j + 1) * LANES]
        h2_ref[_slab_rows(j, tm), :] = _pack_pair(lo, hi)
    h_hi = h2.astype(BF16)
    h_lo = (h2 - h_hi.astype(F32)).astype(BF16)
    w_rt = wrt_ref[...]
    w_hi = w_rt.astype(BF16)
    w_lo = (w_rt - w_hi.astype(F32)).astype(BF16)
    hi_both = jnp.dot(h_hi, jnp.concatenate([w_hi, w_lo], axis=1), preferred_element_type=F32)
    logits = (hi_both[:, :LANES] + hi_both[:, LANES:]
              + jnp.dot(h_lo, w_hi, preferred_element_type=F32) + brt_ref[...])
    route = _route(logits)

    lane = lax.broadcasted_iota(jnp.int32, route.shape, 1)
    lane_f = lane.astype(F32)
    hot1 = lane_f == route[:, 0:1]
    hot2 = lane_f == route[:, 1:2]
    both = jnp.where(hot1 | hot2, 1.0, 0.0)
    ii = lax.broadcasted_iota(jnp.int32, (tm, tm), 0)
    jj = lax.broadcasted_iota(jnp.int32, (tm, tm), 1)
    lower = jnp.where(ii > jj, 1.0, 0.0).astype(BF16)
    before = jnp.dot(lower, both.astype(BF16), preferred_element_type=F32) + carry_ref[...]
    r1 = jnp.sum(jnp.where(hot1, before, 0.0), axis=-1, keepdims=True)
    r2 = jnp.sum(jnp.where(hot2, before, 0.0), axis=-1, keepdims=True)
    route = jnp.where(lane == 4, r1, jnp.where(lane == 5, r2, route))
    route_ref[...] = route
    meta_ref[...] = route.T[0:SLAB_ROWS, :].astype(I32)
    carry = carry_ref[...] + jnp.sum(both, axis=0, keepdims=True)
    carry_ref[...] = carry
    count_ref[...] = carry


def _mixout(merged, x2, mod6, gain2, wo, w_rt, b_rt, seq):
    t = x2.shape[0]
    tm = MIXOUT_TM
    tiles_per_batch = seq // tm
    row = lambda i: (i, 0)
    const = lambda i: (0, 0)
    return pl.pallas_call(
        _mixout_kernel,
        grid=(t // tm,),
        in_specs=[pl.BlockSpec((tm, D_MODEL), row),
                  pl.BlockSpec((tm, D_MODEL), row),
                  pl.BlockSpec((1, 6, D_MODEL), lambda i: (i // tiles_per_batch, 0, 0)),
                  pl.BlockSpec((1, D_MODEL), const),
                  pl.BlockSpec((D_MODEL, D_MODEL), const, pipeline_mode=pl.Buffered(1)),
                  pl.BlockSpec((D_MODEL, LANES), const),
                  pl.BlockSpec((1, LANES), const)],
        out_specs=[pl.BlockSpec((tm, D_MODEL), row),
                   pl.BlockSpec((tm * SLAB_ROWS, SLAB_LANES), row),
                   pl.BlockSpec((tm, LANES), row),
                   pl.BlockSpec((1, LANES), const),
                   pl.BlockSpec((SLAB_ROWS, tm), lambda i: (0, i))],
        out_shape=[jax.ShapeDtypeStruct((t, D_MODEL), F32),
                   jax.ShapeDtypeStruct((t * SLAB_ROWS, SLAB_LANES), I32),
                   jax.ShapeDtypeStruct((t, LANES), F32),
                   jax.ShapeDtypeStruct((1, LANES), F32),
                   jax.ShapeDtypeStruct((SLAB_ROWS, t), I32)],
        scratch_shapes=[pltpu.VMEM((1, LANES), F32)],
        compiler_params=_params(("arbitrary",), 56),
        name="mixout",
    )(merged, x2, mod6, gain2, wo, w_rt, b_rt)


PAD_BITS = (64, 32, 16, 8, 4, 2, 1)


def _sorted_row(tables, tok, k):
    e_refs, r_refs, blk_row_ref = tables[0:TOP_K], tables[TOP_K:2 * TOP_K], tables[2 * TOP_K]
    return blk_row_ref[e_refs[k][tok]] + r_refs[k][tok]


N_ROUTE_TABLES = 2 * TOP_K + 1


def _dispatch_kernel(*refs):
    tables = refs[:N_ROUTE_TABLES]
    zstart_ref, zcount_ref, h2_ref, xs_ref, dest_ref, zero_ref, sem, zsem = refs[N_ROUTE_TABLES:]
    n_assign = tables[0].shape[0] * TOP_K
    zero_ref[...] = jnp.zeros_like(zero_ref)

    def zero_copy(start, rows):
        return pltpu.make_async_copy(zero_ref.at[pl.ds(0, rows * SLAB_ROWS), :],
                                     xs_ref.at[pl.ds(start * SLAB_ROWS, rows * SLAB_ROWS), :], zsem)

    def fill(e, wait):
        start = zstart_ref[e]
        pad = zcount_ref[e]
        for bit in PAD_BITS:
            @pl.when((pad & bit) != 0)
            def _(start=start, bit=bit):
                cp = zero_copy(start, bit)
                cp.wait() if wait else cp.start()
            start = start + (pad & bit)

    lax.fori_loop(0, N_EXPERTS, lambda e, c: (fill(e, False), c)[1], 0)

    def issue(tok, carry):
        src = h2_ref.at[pl.ds(pl.multiple_of(tok * SLAB_ROWS, SLAB_ROWS), SLAB_ROWS), :]
        for k in range(TOP_K):
            row = _sorted_row(tables, tok, k)
            dest_ref[tok * TOP_K + k] = row
            dst = pl.multiple_of(row * SLAB_ROWS, SLAB_ROWS)
            pltpu.make_async_copy(src, xs_ref.at[pl.ds(dst, SLAB_ROWS), :], sem).start()
        return carry

    lax.fori_loop(0, n_assign // TOP_K, issue, 0, unroll=8)

    def drain(i, carry):
        pltpu.make_async_copy(h2_ref.at[pl.ds(0, ROWS_PER_WAIT * SLAB_ROWS), :],
                              xs_ref.at[pl.ds(0, ROWS_PER_WAIT * SLAB_ROWS), :], sem).wait()
        return carry

    lax.fori_loop(0, n_assign // ROWS_PER_WAIT, drain, 0)
    lax.fori_loop(0, N_EXPERTS, lambda e, c: (fill(e, True), c)[1], 0)


def _dispatch(tables, zstart, zcount, h2_slab, n_pad):
    n_assign = tables[0].shape[0] * TOP_K
    return pl.pallas_call(
        _dispatch_kernel,
        grid_spec=pltpu.PrefetchScalarGridSpec(
            num_scalar_prefetch=N_ROUTE_TABLES + 2,
            grid=(1,),
            in_specs=[pl.BlockSpec(memory_space=pltpu.HBM)],
            out_specs=[pl.BlockSpec(memory_space=pl.ANY),
                       pl.BlockSpec(memory_space=pltpu.SMEM)],
            scratch_shapes=[pltpu.VMEM((PAD_BITS[0] * SLAB_ROWS, SLAB_LANES), I32),
                            pltpu.SemaphoreType.DMA(()),
                            pltpu.SemaphoreType.DMA(())]),
        out_shape=[jax.ShapeDtypeStruct((n_pad * SLAB_ROWS, SLAB_LANES), I32),
                   jax.ShapeDtypeStruct((n_assign,), I32)],
        compiler_params=_params(("arbitrary",), 16),
        name="dispatch",
    )(*tables, zstart, zcount, h2_slab)


def _experts_kernel(item_e_ref, item_blk_ref, item_nb_ref, x_blk_ref,
                    x0_ref, x1_ref, x2_ref, x3_ref, wg_ref, wu_ref, wd_ref, y_ref,
                    xb_ref, acc_ref, yp_ref, sem):
    w = pl.program_id(0)
    c = pl.program_id(1)
    last_c = pl.num_programs(1) - 1
    nb = item_nb_ref[w]
    blk0 = item_blk_ref[w]
    x_refs = (x0_ref, x1_ref, x2_ref, x3_ref)

    blk_rows = MOE_BLOCK * SLAB_ROWS

    def out_copy(s):
        return pltpu.make_async_copy(yp_ref.at[s], y_ref.at[pl.ds((blk0 + s) * blk_rows, blk_rows), :],
                                     sem.at[s])

    def wait_out(count):
        for s in range(ITEM_BLOCKS):
            @pl.when(s < count)
            def _(s=s):
                out_copy(s).wait()

    @pl.when((c == last_c) & (w > 0))
    def _():
        wait_out(item_nb_ref[jnp.maximum(w - 1, 0)])

    def run(n_live):
        rows = n_live * MOE_BLOCK

        @pl.when(c == 0)
        def _():
            for s in range(n_live):
                for j in range(SLAB_ROWS):
                    lo, hi = _unpack_pair(x_refs[s][_slab_rows(j, MOE_BLOCK), :])
                    r0 = s * MOE_BLOCK
                    xb_ref[r0:r0 + MOE_BLOCK, j * LANES:(j + 1) * LANES] = lo.astype(BF16)
                    xb_ref[r0:r0 + MOE_BLOCK, HALF_D + j * LANES:HALF_D + (j + 1) * LANES] = hi.astype(BF16)

        xs = xb_ref[0:rows, :]
        g = jnp.dot(xs, wg_ref[...].astype(BF16), preferred_element_type=F32)
        u = jnp.dot(xs, wu_ref[...].astype(BF16), preferred_element_type=F32)
        hid = (g * jax.nn.sigmoid(g) * u).astype(BF16)
        y = jnp.dot(hid, wd_ref[...].astype(BF16), preferred_element_type=F32)

        @pl.when(c == 0)
        def _():
            acc_ref[0:rows, :] = y

        @pl.when((c > 0) & (c < last_c))
        def _():
            acc_ref[0:rows, :] += y

        @pl.when(c == last_c)
        def _():
            for s in range(n_live):
                r0 = s * MOE_BLOCK
                for j in range(SLAB_ROWS):
                    lo_cols = slice(j * LANES, (j + 1) * LANES)
                    hi_cols = slice(HALF_D + j * LANES, HALF_D + (j + 1) * LANES)
                    yp_ref[s, _slab_rows(j, MOE_BLOCK), :] = _pack_pair(
                        acc_ref[r0:r0 + MOE_BLOCK, lo_cols] + y[r0:r0 + MOE_BLOCK, lo_cols],
                        acc_ref[r0:r0 + MOE_BLOCK, hi_cols] + y[r0:r0 + MOE_BLOCK, hi_cols])
                out_copy(s).start()

    for n_live in range(1, ITEM_BLOCKS + 1):
        pl.when(nb == n_live)(functools.partial(run, n_live))

    @pl.when((c == last_c) & (w == pl.num_programs(0) - 1))
    def _():
        wait_out(nb)


def _experts(item_e, item_blk, item_nb, x_sorted, w_gate, w_up, w_down, n_blocks):
    n_items = item_e.shape[0]
    d = D_MODEL
    fc = EXPERT_FC
    x3 = x_sorted
    blk_rows = MOE_BLOCK * SLAB_ROWS
    n_chunks = EXPERT_DIM // fc
    assert n_chunks >= 2, "the last hidden chunk adds onto the accumulator of the earlier ones"

    slot = jnp.arange(ITEM_BLOCKS, dtype=jnp.int32)[:, None]
    x_blk = jnp.maximum(lax.cummax(jnp.where(slot < item_nb[None, :], item_blk[None, :] + slot, -1), axis=1), 0)
    x_blk = x_blk.reshape(-1).astype(jnp.int32)

    def x_spec(s):
        return pl.BlockSpec((blk_rows, SLAB_LANES),
                            lambda w, c, ie, ib, inb, xb: (xb[s * n_items + w], 0))

    def chunk(w, c, inb):
        return jnp.where(inb[w] > 0, c, n_chunks - 1)

    return pl.pallas_call(
        _experts_kernel,
        grid_spec=pltpu.PrefetchScalarGridSpec(
            num_scalar_prefetch=4,
            grid=(n_items, n_chunks),
            in_specs=[x_spec(0), x_spec(1), x_spec(2), x_spec(3),
                      pl.BlockSpec((None, d, fc), lambda w, c, ie, ib, inb, xb: (ie[w], 0, chunk(w, c, inb))),
                      pl.BlockSpec((None, d, fc), lambda w, c, ie, ib, inb, xb: (ie[w], 0, chunk(w, c, inb))),
                      pl.BlockSpec((None, fc, d), lambda w, c, ie, ib, inb, xb: (ie[w], chunk(w, c, inb), 0))],
            out_specs=pl.BlockSpec(memory_space=pl.ANY),
            scratch_shapes=[pltpu.VMEM((ITEM_BLOCKS * MOE_BLOCK, d), BF16),
                            pltpu.VMEM((ITEM_BLOCKS * MOE_BLOCK, d), F32),
                            pltpu.VMEM((ITEM_BLOCKS, blk_rows, SLAB_LANES), I32),
                            pltpu.SemaphoreType.DMA((ITEM_BLOCKS,))]),
        out_shape=jax.ShapeDtypeStruct((n_blocks * blk_rows, SLAB_LANES), I32),
        compiler_params=_params(("arbitrary", "arbitrary"), 56),
        name="experts",
    )(item_e, item_blk, item_nb, x_blk, x3, x3, x3, x3, w_gate, w_up, w_down)


def _combine_kernel(dest_ref, x1_ref, route_ref, mod_ref, gain_ref, y_ref, o_ref, ybuf_ref, sem):
    tm = COMBINE_TM
    i = pl.program_id(0)
    n_tiles = pl.num_programs(0)

    def row_copy(slot, k, r, src):
        return pltpu.make_async_copy(
            y_ref.at[pl.ds(pl.multiple_of(src * SLAB_ROWS, SLAB_ROWS), SLAB_ROWS), :],
            ybuf_ref.at[slot, k, pl.ds(pl.multiple_of(r * SLAB_ROWS, SLAB_ROWS), SLAB_ROWS), :],
            sem.at[slot])

    def wait_slot(slot):
        for k in range(TOP_K):
            pltpu.make_async_copy(y_ref.at[pl.ds(0, tm * SLAB_ROWS), :], ybuf_ref.at[slot, k],
                                  sem.at[slot]).wait()

    def issue_tile(tile, slot):
        base = tile * tm * TOP_K

        def issue(r, carry):
            for k in range(TOP_K):
                row_copy(slot, k, r, dest_ref[base + r * TOP_K + k]).start()
            return carry

        lax.fori_loop(0, tm, issue, 0, unroll=4)

    @pl.when(i == 0)
    def _():
        issue_tile(0, 0)

    for nxt in range(2):
        @pl.when((i + 1 < n_tiles) & ((i + 1) % 2 == nxt))
        def _(nxt=nxt):
            issue_tile(i + 1, nxt)

    slot = i % 2
    wait_slot(slot)

    route = route_ref[...]
    w0 = route[:, 2:3]
    w1 = route[:, 3:4]
    ssq = jnp.zeros((tm, 1), F32)
    for j in range(SLAB_ROWS):
        rows = _slab_rows(j, tm)
        y0 = _unpack_pair(ybuf_ref[slot, 0, rows, :])
        y1 = _unpack_pair(ybuf_ref[slot, 1, rows, :])
        for part, off in ((0, j * LANES), (1, HALF_D + j * LANES)):
            cols = slice(off, off + LANES)
            ffn = w0 * y0[part] + w1 * y1[part]
            x2 = x1_ref[:, cols] + mod_ref[0, 5:6, cols] * ffn
            o_ref[:, cols] = x2
            ssq = ssq + jnp.sum(x2 * x2, axis=-1, keepdims=True)
    o_ref[...] = o_ref[...] * lax.rsqrt(ssq * (1.0 / D_MODEL) + EPS) * gain_ref[...]


def _combine(dest, x1, route, mod6, gain, y_sorted, seq):
    t, d = x1.shape
    tm = COMBINE_TM
    tiles_per_batch = seq // tm
    return pl.pallas_call(
        _combine_kernel,
        grid_spec=pltpu.PrefetchScalarGridSpec(
            num_scalar_prefetch=1,
            grid=(t // tm,),
            in_specs=[pl.BlockSpec((tm, d), lambda i, *_: (i, 0)),
                      pl.BlockSpec((tm, LANES), lambda i, *_: (i, 0)),
                      pl.BlockSpec((1, 6, d), lambda i, *_: (i // tiles_per_batch, 0, 0)),
                      pl.BlockSpec((1, d), lambda i, *_: (0, 0)),
                      pl.BlockSpec(memory_space=pl.ANY)],
            out_specs=pl.BlockSpec((tm, d), lambda i, *_: (i, 0)),
            scratch_shapes=[pltpu.VMEM((2, TOP_K, tm * SLAB_ROWS, SLAB_LANES), I32),
                            pltpu.SemaphoreType.DMA((2,))]),
        out_shape=jax.ShapeDtypeStruct((t, d), F32),
        compiler_params=_params(("arbitrary",), 24),
        name="combine",
    )(dest, x1, route, mod6, gain, y_sorted)


def _dispatch_tables(meta, counts, t):
    n_assign = t * TOP_K
    n_pad = -(-(n_assign + N_EXPERTS * (MOE_BLOCK - 1)) // MOE_BLOCK) * MOE_BLOCK
    n_blocks = n_pad // MOE_BLOCK
    n_items = N_EXPERTS + n_assign // (ITEM_BLOCKS * MOE_BLOCK)

    cnt = counts[0, :N_EXPERTS].astype(jnp.int32)
    blocks_e = (cnt + MOE_BLOCK - 1) // MOE_BLOCK
    blk_end = jnp.cumsum(blocks_e)
    blk_start = blk_end - blocks_e
    tables = (meta[0], meta[1], meta[4], meta[5], (blk_start * MOE_BLOCK).astype(jnp.int32))
    zstart = (blk_start * MOE_BLOCK + cnt).astype(jnp.int32)
    zcount = (blocks_e * MOE_BLOCK - cnt).astype(jnp.int32)

    items_e = (blocks_e + ITEM_BLOCKS - 1) // ITEM_BLOCKS
    item_end = jnp.cumsum(items_e)
    item_start = item_end - items_e
    w = jnp.arange(n_items, dtype=jnp.int32)
    live = w < item_end[-1]
    w_live = jnp.minimum(w, item_end[-1] - 1)
    e_w = jnp.minimum(jnp.sum((item_end[None, :] <= w_live[:, None]).astype(jnp.int32), axis=1), N_EXPERTS - 1)
    j_w = w_live - item_start[e_w]
    item_blk = (blk_start[e_w] + ITEM_BLOCKS * j_w).astype(jnp.int32)
    item_nb = jnp.where(live, jnp.clip(blocks_e[e_w] - ITEM_BLOCKS * j_w, 0, ITEM_BLOCKS), 0).astype(jnp.int32)
    return tables, zstart, zcount, e_w, item_blk, item_nb, n_blocks


def kernel(x, c, positions, norm1_gain, norm2_gain, final_norm_gain, w_ada, b_ada, w_in, attn_sinks,
           ret_norm_gain, w_branch_attn, w_branch_ret, w_out, w_router_group, b_router_group,
           w_router_expert, b_router_expert, w_expert_gate, w_expert_up, w_expert_down):
    batch, seq, d = x.shape
    t = batch * seq
    depth = w_ada.shape[0]
    half = RET_DIM // 2
    inv_freq = (ROPE_BASE ** (-jnp.arange(half, dtype=F32) / half)).reshape(1, half)
    pos = positions.reshape(t, 1)
    c8 = jnp.pad(c, ((0, 8 - batch), (0, 0)))
    xf = x.reshape(t, d)

    assert depth == 1, "the fused final norm assumes a single layer"
    for layer in range(depth):
        b_ada2 = b_ada[layer].reshape(1, -1)
        mod_early = _ada(c8, w_ada[layer], b_ada2, ADA_EARLY)[:batch]
        proj, proj_kv, mod_late = _proj(xf, norm1_gain[layer].reshape(1, d), mod_early.reshape(batch, 2, d),
                                        w_in[layer], c8, w_ada[layer], b_ada2, seq)
        mod6 = jnp.concatenate([mod_early, mod_late[:batch]], axis=1).reshape(batch, 6, d)
        attn, ret = _mixers(proj, proj_kv, attn_sinks[layer], pos, inv_freq,
                            ret_norm_gain[layer].reshape(1, d), batch, seq)

        pad = LANES - N_GROUPS - N_EXPERTS
        w_rt = jnp.concatenate([w_router_group[layer], w_router_expert[layer],
                                jnp.zeros((d, pad), F32)], axis=1)
        b_rt = jnp.concatenate([b_router_group[layer], b_router_expert[layer],
                                jnp.zeros((pad,), F32)]).reshape(1, LANES)
        merged = _branch(attn, ret, proj, w_branch_attn[layer], w_branch_ret[layer])
        x1, h2, route, counts, meta = _mixout(merged, xf, mod6, norm2_gain[layer].reshape(1, d),
                                              w_out[layer].astype(BF16), w_rt, b_rt, seq)
        tables, zstart, zcount, item_e, item_blk, item_nb, n_blocks = _dispatch_tables(meta, counts, t)
        x_sorted, dest = _dispatch(tables, zstart, zcount, h2, n_blocks * MOE_BLOCK)
        y_sorted = _experts(item_e, item_blk, item_nb, x_sorted,
                            w_expert_gate[layer], w_expert_up[layer], w_expert_down[layer], n_blocks)
        xf = _combine(dest, x1, route, mod6, final_norm_gain.reshape(1, d), y_sorted, seq)
    return xf.reshape(batch, seq, d)
```

```python
import functools
import math

import jax
import jax.numpy as jnp
import numpy as np
from jax import lax
from jax.experimental import pallas as pl
from jax.experimental.pallas import tpu as pltpu

F32 = jnp.float32
BF16 = jnp.bfloat16

D_MODEL = 2048
ATTN_HEAD_DIM = 64
ATTN_HEADS = 32
ATTN_KV_HEADS = 4
ATTN_GROUP = 8
WINDOW = 128
RET_HEADS = 8
RET_DIM = 256
RET_CHUNK = 128
ROPE_BASE = 10000.0
N_GROUPS = 4
EXPERTS_PER_GROUP = 16
N_EXPERTS = 64
TOP_K = 2
EXPERT_DIM = 1024
MOE_BLOCK = 128
EPS = 1e-6
NEG = -1e30

MIB = 1024 * 1024
LANES = 128
PROJ_TN = 512
PROJ_TM = 1024
KV_SRC_TILE = 4
ADA_EARLY = 2 * D_MODEL
ADA_LATE_TN = 256
ADA_LATE_STEPS = 4 * D_MODEL // ADA_LATE_TN
BRANCH_TM = 1024
BRANCH_TN = 512
MIXOUT_TM = 512
ITEM_BLOCKS = 4
EXPERT_FC = 512
COMBINE_TM = 128
SLAB_ROWS = 8
SLAB_LANES = LANES
HALF_D = D_MODEL // 2
ROWS_PER_WAIT = 128
I32 = jnp.int32

COL_QA, COL_QR, COL_KR, COL_VR, COL_GR, COL_GA, COL_GRT = 0, 1, 2, 3, 4, 5, 6

LOG_GAMMA = [math.log1p(-(2.0 ** (-5.0 - h))) for h in range(RET_HEADS)]


def _params(sem, vmem_mib):
    return pltpu.CompilerParams(dimension_semantics=sem, vmem_limit_bytes=vmem_mib * MIB)


def _pack_pair(lo, hi):
    lo_b = lax.bitcast_convert_type(lo.astype(BF16).astype(F32), I32)
    hi_b = lax.bitcast_convert_type(hi.astype(BF16).astype(F32), I32)
    return hi_b | lax.shift_right_logical(lo_b, jnp.full_like(lo_b, 16))


def _unpack_pair(w):
    lo = lax.bitcast_convert_type(w << 16, F32)
    hi = lax.bitcast_convert_type(w & jnp.int32(-65536), F32)
    return lo, hi


def _slab_rows(j, n_tokens):
    return pl.ds(j, n_tokens, stride=SLAB_ROWS)


def _ada_kernel(c_ref, w_ref, b_ref, o_ref):
    c = c_ref[...]
    a = (c * jax.nn.sigmoid(c)).astype(BF16)
    o_ref[...] = jnp.dot(a, w_ref[...].astype(BF16), preferred_element_type=F32) + b_ref[...]


def _ada(c8, w_ada, b_ada, n):
    tn = 1024
    return pl.pallas_call(
        _ada_kernel,
        grid=(n // tn,),
        in_specs=[pl.BlockSpec((8, D_MODEL), lambda j: (0, 0)),
                  pl.BlockSpec((D_MODEL, tn), lambda j: (0, j)),
                  pl.BlockSpec((1, tn), lambda j: (0, j))],
        out_specs=pl.BlockSpec((8, tn), lambda j: (0, j)),
        out_shape=jax.ShapeDtypeStruct((8, n), F32),
        compiler_params=_params(("arbitrary",), 40),
        name="ada",
    )(c8, w_ada, b_ada)


def _proj_kernel(x_ref, g_ref, mod_ref, wlo_ref, whi_ref, c_ref, wada_ref, bada_ref,
                 o_ref, kv_ref, late_ref, h0_ref, h1_ref):
    i = pl.program_id(0)
    v = pl.program_id(1)
    last = pl.num_programs(1) - 1

    def normed():
        x = x_ref[...]
        var = jnp.mean(x * x, axis=-1, keepdims=True)
        y = x * lax.rsqrt(var + EPS) * g_ref[...]
        return (y * (1.0 + mod_ref[0, 1:2, :]) + mod_ref[0, 0:1, :]).astype(BF16)

    @pl.when(pl.program_id(0) * pl.num_programs(1) + v < ADA_LATE_STEPS)
    def _():
        _ada_kernel(c_ref, wada_ref, bada_ref, late_ref)

    @pl.when((i == 0) & (v == 0))
    def _():
        h0_ref[...] = normed()

    for parity, (h_ref, h_next_ref) in enumerate(((h0_ref, h1_ref), (h1_ref, h0_ref))):
        @pl.when((v < last) & (i % 2 == parity))
        def _(h_ref=h_ref):
            w = jnp.concatenate([wlo_ref[...].astype(BF16), whi_ref[...].astype(BF16)], axis=1)
            o_ref[...] = jnp.dot(h_ref[...], w, preferred_element_type=F32).astype(BF16)

        @pl.when((v == last) & (i % 2 == parity))
        def _(h_ref=h_ref, h_next_ref=h_next_ref):
            kv_ref[...] = jnp.dot(h_ref[...], wlo_ref[...].astype(BF16),
                                  preferred_element_type=F32).astype(BF16)
            h_next_ref[...] = normed()


def _proj_w_tile(v, n_wide):
    return jnp.where(v < 2, 2 * v, jnp.where(v < n_wide, 2 * v + 1, KV_SRC_TILE))


def _proj(x2, gain, mod_early, w_in, c8, w_ada, b_ada, seq):
    t = x2.shape[0]
    n = w_in.shape[1]
    tiles_per_batch = seq // PROJ_TM
    n_wide = (n - PROJ_TN) // (2 * PROJ_TN)
    n_late = w_ada.shape[1] - ADA_EARLY
    assert n_late == ADA_LATE_STEPS * ADA_LATE_TN and ADA_LATE_STEPS <= (t // PROJ_TM) * (n_wide + 1)
    early_tiles = ADA_EARLY // ADA_LATE_TN

    def late_tile(i, v):
        return jnp.minimum(i * (n_wide + 1) + v, ADA_LATE_STEPS - 1)

    n_row_tiles = t // PROJ_TM

    def norm_tile(i, v):
        return jnp.minimum(i + (v == n_wide).astype(jnp.int32), n_row_tiles - 1)

    return pl.pallas_call(
        _proj_kernel,
        grid=(n_row_tiles, n_wide + 1),
        in_specs=[pl.BlockSpec((PROJ_TM, D_MODEL), lambda i, v: (norm_tile(i, v), 0)),
                  pl.BlockSpec((1, D_MODEL), lambda i, v: (0, 0)),
                  pl.BlockSpec((1, 2, D_MODEL), lambda i, v: (norm_tile(i, v) // tiles_per_batch, 0, 0)),
                  pl.BlockSpec((D_MODEL, PROJ_TN), lambda i, v: (0, _proj_w_tile(v, n_wide))),
                  pl.BlockSpec((D_MODEL, PROJ_TN),
                               lambda i, v: (0, jnp.where(v < n_wide, _proj_w_tile(v, n_wide) + 1, KV_SRC_TILE))),
                  pl.BlockSpec((8, D_MODEL), lambda i, v: (0, 0)),
                  pl.BlockSpec((D_MODEL, ADA_LATE_TN), lambda i, v: (0, early_tiles + late_tile(i, v))),
                  pl.BlockSpec((1, ADA_LATE_TN), lambda i, v: (0, early_tiles + late_tile(i, v)))],
        out_specs=[pl.BlockSpec((PROJ_TM, 2 * PROJ_TN), lambda i, v: (i, jnp.minimum(v, n_wide - 1))),
                   pl.BlockSpec((PROJ_TM, PROJ_TN), lambda i, v: (i, 0)),
                   pl.BlockSpec((8, ADA_LATE_TN), lambda i, v: (0, late_tile(i, v)))],
        out_shape=[jax.ShapeDtypeStruct((t, n - PROJ_TN), BF16),
                   jax.ShapeDtypeStruct((t, PROJ_TN), BF16),
                   jax.ShapeDtypeStruct((8, n_late), F32)],
        scratch_shapes=[pltpu.VMEM((PROJ_TM, D_MODEL), BF16), pltpu.VMEM((PROJ_TM, D_MODEL), BF16)],
        compiler_params=_params(("arbitrary", "arbitrary"), 60),
        name="proj",
    )(x2, gain, mod_early, w_in, w_in, c8, w_ada, b_ada)


def _attn_stages(sink_ref, q_ref, kvp_ref, kvc_ref, o_ref):
    n = pl.program_id(1)
    kvp = kvp_ref[...]
    kvc = kvc_ref[...]
    qi = lax.broadcasted_iota(jnp.int32, (WINDOW, WINDOW), 0)
    sj = lax.broadcasted_iota(jnp.int32, (WINDOW, WINDOW), 1)
    valid_prev = (sj > qi) & (n > 0)
    valid_cur = sj <= qi
    sink_col = sj == 0
    first_row = lax.broadcasted_iota(jnp.int32, (2 * WINDOW, 1), 0) == 0
    dh = ATTN_HEAD_DIM
    kv_w = ATTN_KV_HEADS * dh
    scale = jnp.asarray(dh ** -0.5, BF16)
    n_pairs = ATTN_GROUP // 2

    def group_operands(kv):
        kband = jnp.concatenate([kvp[:, kv * dh:(kv + 1) * dh],
                                 kvc[:, kv * dh:(kv + 1) * dh]], axis=0) * scale
        vband = jnp.concatenate([kvp[:, kv_w + kv * dh:kv_w + (kv + 1) * dh],
                                 kvc[:, kv_w + kv * dh:kv_w + (kv + 1) * dh]], axis=0)
        vband = jnp.where(first_row, jnp.zeros_like(vband), vband)
        zeros = jnp.zeros_like(kband)
        ones = jnp.ones_like(vband)
        k_pad = (jnp.concatenate([kband, zeros], axis=1), jnp.concatenate([zeros, kband], axis=1))
        pv_rhs = jnp.concatenate(
            [jnp.concatenate([vband, zeros, ones, zeros], axis=1),
             jnp.concatenate([zeros, vband, zeros, ones], axis=1)], axis=0)
        q_rows = jnp.concatenate(
            [q_ref[:, (kv * ATTN_GROUP + 2 * p) * dh:(kv * ATTN_GROUP + 2 * p + 2) * dh]
             for p in range(n_pairs)], axis=0)
        scores = [lax.dot_general(q_rows, k_pad[idx], (((1,), (1,)), ((), ())),
                                  preferred_element_type=F32) for idx in range(2)]
        return scores, pv_rhs

    nxt = group_operands(0)
    for kv in range(ATTN_KV_HEADS):
        scores, pv_rhs = nxt
        if kv + 1 < ATTN_KV_HEADS:
            nxt = group_operands(kv + 1)
        prob_rows = []
        for pair in range(n_pairs):
            rows = slice(pair * WINDOW, (pair + 1) * WINDOW)
            probs = []
            for idx in range(2):
                s = scores[idx][rows]
                sink = sink_ref[kv * ATTN_GROUP + 2 * pair + idx]
                s_prev = jnp.where(sink_col, sink, jnp.where(valid_prev, s[:, :WINDOW], NEG))
                s_cur = jnp.where(valid_cur, s[:, WINDOW:], NEG)
                m = jnp.max(jnp.maximum(s_prev, s_cur), axis=-1, keepdims=True)
                probs += [jnp.exp(s_prev - m).astype(BF16), jnp.exp(s_cur - m).astype(BF16)]
            prob_rows.append(jnp.concatenate(probs, axis=-1))
        r = jnp.dot(jnp.concatenate(prob_rows, axis=0), pv_rhs, preferred_element_type=F32)
        for pair in range(n_pairs):
            rows = slice(pair * WINDOW, (pair + 1) * WINDOW)
            col = (kv * ATTN_GROUP + 2 * pair) * dh
            o_ref[:, col:col + 2 * dh] = (r[rows, :2 * dh] * (1.0 / r[rows, 2 * dh:])).astype(BF16)
        yield


def _ret_decay_tables():
    lg = np.asarray(LOG_GAMMA, np.float64)[:, None, None]
    i = np.arange(RET_CHUNK, dtype=np.float64)
    diff = i[:, None] - i[None, :]
    k_scale = RET_DIM ** -0.5
    d_intra = np.where(diff >= 0, np.exp(np.maximum(diff, 0.0) * lg), 0.0) * k_scale
    lanes = np.ones((1, 1, RET_DIM // 2))
    d_q = np.exp((i[None, :, None] + 1.0) * lg) * lanes
    d_k = np.exp((RET_CHUNK - 1.0 - i[None, :, None]) * lg) * k_scale * lanes
    return jnp.asarray(d_intra, F32), jnp.asarray(d_q, F32), jnp.asarray(d_k, BF16)


def _ret_stages(pos_ref, invf_ref, di_ref, dq_ref, dk_ref, q_ref, k_ref, v_ref, g_ref, gain_ref,
                o_ref, state_ref):
    half = RET_DIM // 2
    ang = pos_ref[...].astype(F32) * invf_ref[...]
    cos = jnp.cos(ang).astype(BF16)
    sin = jnp.sin(ang).astype(BF16)

    def rot(t):
        t1, t2 = t[:, :half], t[:, half:]
        return jnp.concatenate([t1 * cos - t2 * sin, t1 * sin + t2 * cos], axis=-1)

    def both_halves(t, factor):
        return jnp.concatenate([t[:, :half] * factor, t[:, half:] * factor], axis=-1)

    def head_front(h):
        sl = slice(h * RET_DIM, (h + 1) * RET_DIM)
        qb = rot(q_ref[:, sl])
        kb = rot(k_ref[:, sl])
        intra = lax.dot_general(qb, kb, (((1,), (1,)), ((), ())),
                                preferred_element_type=F32) * di_ref[h]
        st = state_ref[h]
        cross = jnp.dot(qb, st.astype(BF16), preferred_element_type=F32)
        return kb, intra, st, cross

    front = head_front(0)
    for h in range(RET_HEADS):
        sl = slice(h * RET_DIM, (h + 1) * RET_DIM)
        kb, intra, st, cross = front
        if h + 1 < RET_HEADS:
            front = head_front(h + 1)
        vb = v_ref[:, sl]
        d_chunk = math.exp(RET_CHUNK * LOG_GAMMA[h])
        o = jnp.dot(intra.astype(BF16), vb, preferred_element_type=F32) + both_halves(cross, dq_ref[h])
        kd = both_halves(kb, dk_ref[h])
        state_ref[h] = st * d_chunk + lax.dot_general(kd, vb, (((0,), (0,)), ((), ())),
                                                      preferred_element_type=F32)
        o = o * lax.rsqrt(jnp.mean(o * o, axis=-1, keepdims=True) + EPS) * gain_ref[:, sl]
        gg = g_ref[:, sl].astype(F32)
        o_ref[:, sl] = (gg * jax.nn.sigmoid(gg) * o).astype(BF16)
        yield


def _mixers_kernel(sink_ref, qa_ref, kvp_ref, kvc_ref, pos_ref, invf_ref, di_ref, dq_ref, dk_ref,
                   qr_ref, kr_ref, vr_ref, gr_ref, gain_ref, attn_ref, ret_ref, state_ref):
    @pl.when(pl.program_id(1) == 0)
    def _():
        state_ref[...] = jnp.zeros_like(state_ref)

    attn = _attn_stages(sink_ref, qa_ref, kvp_ref, kvc_ref, attn_ref)
    ret = _ret_stages(pos_ref, invf_ref, di_ref, dq_ref, dk_ref, qr_ref, kr_ref, vr_ref, gr_ref,
                      gain_ref, ret_ref, state_ref)
    heads_per_group = RET_HEADS // ATTN_KV_HEADS
    for _ in range(ATTN_KV_HEADS):
        next(attn)
        for _ in range(heads_per_group):
            next(ret)


def _mixers(proj, proj_kv, sinks, pos, inv_freq, ret_gain, batch, seq):
    nb = seq // WINDOW
    t = batch * seq

    def col(cb):
        return lambda b, n: (b * nb + n, cb)

    d_intra, d_q, d_k = _ret_decay_tables()
    table = lambda lanes: pl.BlockSpec((RET_HEADS, RET_CHUNK, lanes), lambda b, n: (0, 0, 0))
    rows = pl.BlockSpec((WINDOW, D_MODEL), col(0))
    return pl.pallas_call(
        _mixers_kernel,
        grid=(batch, nb),
        in_specs=[pl.BlockSpec(memory_space=pltpu.SMEM),
                  pl.BlockSpec((WINDOW, D_MODEL), col(COL_QA)),
                  pl.BlockSpec((WINDOW, PROJ_TN), lambda b, n: (b * nb + jnp.maximum(n - 1, 0), 0)),
                  pl.BlockSpec((WINDOW, PROJ_TN), col(0)),
                  pl.BlockSpec((RET_CHUNK, 1), col(0)),
                  pl.BlockSpec((1, RET_DIM // 2), lambda b, n: (0, 0)),
                  table(RET_CHUNK), table(RET_DIM // 2), table(RET_DIM // 2),
                  pl.BlockSpec((RET_CHUNK, D_MODEL), col(COL_QR)),
                  pl.BlockSpec((RET_CHUNK, D_MODEL), col(COL_KR)),
                  pl.BlockSpec((RET_CHUNK, D_MODEL), col(COL_VR)),
                  pl.BlockSpec((RET_CHUNK, D_MODEL), col(COL_GR)),
                  pl.BlockSpec((1, D_MODEL), lambda b, n: (0, 0))],
        out_specs=[rows, rows],
        out_shape=[jax.ShapeDtypeStruct((t, D_MODEL), BF16), jax.ShapeDtypeStruct((t, D_MODEL), BF16)],
        scratch_shapes=[pltpu.VMEM((RET_HEADS, RET_DIM, RET_DIM), F32)],
        compiler_params=_params(("arbitrary", "arbitrary"), 40),
        name="mixers",
    )(sinks, proj, proj_kv, proj_kv, pos, inv_freq, d_intra, d_q, d_k, proj, proj, proj, proj, ret_gain)


def _route(logits):
    lane = lax.broadcasted_iota(jnp.int32, logits.shape, 1)
    lane_f = lane.astype(F32)
    is_g = lane < N_GROUPS
    gl = jnp.where(is_g, logits, NEG)
    gmax = jnp.max(gl, axis=-1, keepdims=True)
    gsel = jnp.min(jnp.where(gl == gmax, lane_f, float(LANES)), axis=-1, keepdims=True)
    gsum = jnp.sum(jnp.where(is_g, jnp.exp(gl - gmax), 0.0), axis=-1, keepdims=True)
    g_w = 1.0 / gsum
    grp = ((lane - N_GROUPS) >> 4).astype(F32)
    is_e = (lane >= N_GROUPS) & (lane < N_GROUPS + N_EXPERTS) & (grp == gsel)
    el = jnp.where(is_e, logits, NEG)
    v1 = jnp.max(el, axis=-1, keepdims=True)
    i1 = jnp.min(jnp.where(el == v1, lane_f, float(LANES)), axis=-1, keepdims=True)
    el2 = jnp.where(lane_f == i1, NEG, el)
    v2 = jnp.max(el2, axis=-1, keepdims=True)
    i2 = jnp.min(jnp.where(el2 == v2, lane_f, float(LANES)), axis=-1, keepdims=True)
    tt = jnp.exp(v2 - v1)
    w1 = g_w / (1.0 + tt)
    w2 = g_w * tt / (1.0 + tt)
    return jnp.where(lane == 0, i1 - N_GROUPS,
                     jnp.where(lane == 1, i2 - N_GROUPS,
                               jnp.where(lane == 2, w1, jnp.where(lane == 3, w2, 0.0))))


def _branch_kernel(attn_ref, ret_ref, ga_ref, gr_ref, wa_ref, wr_ref, o_ref):
    a = jnp.dot(attn_ref[...], wa_ref[...].astype(BF16), preferred_element_type=F32)
    r = jnp.dot(ret_ref[...], wr_ref[...].astype(BF16), preferred_element_type=F32)
    o_ref[...] = (jax.nn.sigmoid(ga_ref[...].astype(F32)) * a
                  + jax.nn.sigmoid(gr_ref[...].astype(F32)) * r).astype(BF16)


def _branch(attn, ret, proj, wa, wr):
    t = attn.shape[0]
    tm, tn = BRANCH_TM, BRANCH_TN
    per_slab = D_MODEL // tn
    return pl.pallas_call(
        _branch_kernel,
        grid=(t // tm, D_MODEL // tn),
        in_specs=[pl.BlockSpec((tm, D_MODEL), lambda i, j: (i, 0)),
                  pl.BlockSpec((tm, D_MODEL), lambda i, j: (i, 0)),
                  pl.BlockSpec((tm, tn), lambda i, j: (i, COL_GA * per_slab + j)),
                  pl.BlockSpec((tm, tn), lambda i, j: (i, COL_GRT * per_slab + j)),
                  pl.BlockSpec((D_MODEL, tn), lambda i, j: (0, j)),
                  pl.BlockSpec((D_MODEL, tn), lambda i, j: (0, j))],
        out_specs=pl.BlockSpec((tm, tn), lambda i, j: (i, j)),
        out_shape=jax.ShapeDtypeStruct((t, D_MODEL), BF16),
        compiler_params=_params(("arbitrary", "arbitrary"), 48),
        name="branch",
    )(attn, ret, proj, proj, wa, wr)


def _mixout_kernel(m_ref, x_ref, mod_ref, g2_ref, wo_ref, wrt_ref, brt_ref,
                   x1_ref, h2_ref, route_ref, count_ref, meta_ref, carry_ref):
    @pl.when(pl.program_id(0) == 0)
    def _():
        carry_ref[...] = jnp.zeros_like(carry_ref)

    mix = jnp.dot(m_ref[...], wo_ref[...], preferred_element_type=F32)
    x1 = x_ref[...] + mod_ref[0, 2:3, :] * mix
    x1_ref[...] = x1
    var = jnp.mean(x1 * x1, axis=-1, keepdims=True)
    h2 = x1 * lax.rsqrt(var + EPS) * g2_ref[...]
    h2 = h2 * (1.0 + mod_ref[0, 4:5, :]) + mod_ref[0, 3:4, :]
    tm = h2.shape[0]
    for j in range(SLAB_ROWS):
        lo = h2[:, j * LANES:(j + 1) * LANES]
        hi = h2[:, HALF_D + j * LANES:HALF_D + (j + 1) * LANES]
        h2_ref[_slab_rows(j, tm), :] = _pack_pair(lo, hi)
    h_hi = h2.astype(BF16)
    h_lo = (h2 - h_hi.astype(F32)).astype(BF16)
    w_rt = wrt_ref[...]
    w_hi = w_rt.astype(BF16)
    w_lo = (w_rt - w_hi.astype(F32)).astype(BF16)
    hi_both = jnp.dot(h_hi, jnp.concatenate([w_hi, w_lo], axis=1), preferred_element_type=F32)
    logits = (hi_both[:, :LANES] + hi_both[:, LANES:]
              + jnp.dot(h_lo, w_hi, preferred_element_type=F32) + brt_ref[...])
    route = _route(logits)

    lane = lax.broadcasted_iota(jnp.int32, route.shape, 1)
    lane_f = lane.astype(F32)
    hot1 = lane_f == route[:, 0:1]
    hot2 = lane_f == route[:, 1:2]
    both = jnp.where(hot1 | hot2, 1.0, 0.0)
    ii = lax.broadcasted_iota(jnp.int32, (tm, tm), 0)
    jj = lax.broadcasted_iota(jnp.int32, (tm, tm), 1)
    lower = jnp.where(ii > jj, 1.0, 0.0).astype(BF16)
    before = jnp.dot(lower, both.astype(BF16), preferred_element_type=F32) + carry_ref[...]
    r1 = jnp.sum(jnp.where(hot1, before, 0.0), axis=-1, keepdims=True)
    r2 = jnp.sum(jnp.where(hot2, before, 0.0), axis=-1, keepdims=True)
    route = jnp.where(lane == 4, r1, jnp.where(lane == 5, r2, route))
    route_ref[...] = route
    meta_ref[...] = route.T[0:SLAB_ROWS, :].astype(I32)
    carry = carry_ref[...] + jnp.sum(both, axis=0, keepdims=True)
    carry_ref[...] = carry
    count_ref[...] = carry


def _mixout(merged, x2, mod6, gain2, wo, w_rt, b_rt, seq):
    t = x2.shape[0]
    tm = MIXOUT_TM
    tiles_per_batch = seq // tm
    row = lambda i: (i, 0)
    const = lambda i: (0, 0)
    return pl.pallas_call(
        _mixout_kernel,
        grid=(t // tm,),
        in_specs=[pl.BlockSpec((tm, D_MODEL), row),
                  pl.BlockSpec((tm, D_MODEL), row),
                  pl.BlockSpec((1, 6, D_MODEL), lambda i: (i // tiles_per_batch, 0, 0)),
                  pl.BlockSpec((1, D_MODEL), const),
                  pl.BlockSpec((D_MODEL, D_MODEL), const, pipeline_mode=pl.Buffered(1)),
                  pl.BlockSpec((D_MODEL, LANES), const),
                  pl.BlockSpec((1, LANES), const)],
        out_specs=[pl.BlockSpec((tm, D_MODEL), row),
                   pl.BlockSpec((tm * SLAB_ROWS, SLAB_LANES), row),
                   pl.BlockSpec((tm, LANES), row),
                   pl.BlockSpec((1, LANES), const),
                   pl.BlockSpec((SLAB_ROWS, tm), lambda i: (0, i))],
        out_shape=[jax.ShapeDtypeStruct((t, D_MODEL), F32),
                   jax.ShapeDtypeStruct((t * SLAB_ROWS, SLAB_LANES), I32),
                   jax.ShapeDtypeStruct((t, LANES), F32),
                   jax.ShapeDtypeStruct((1, LANES), F32),
                   jax.ShapeDtypeStruct((SLAB_ROWS, t), I32)],
        scratch_shapes=[pltpu.VMEM((1, LANES), F32)],
        compiler_params=_params(("arbitrary",), 56),
        name="mixout",
    )(merged, x2, mod6, gain2, wo, w_rt, b_rt)


PAD_BITS = (64, 32, 16, 8, 4, 2, 1)


def _sorted_row(tables, tok, k):
    e_refs, r_refs, blk_row_ref = tables[0:TOP_K], tables[TOP_K:2 * TOP_K], tables[2 * TOP_K]
    return blk_row_ref[e_refs[k][tok]] + r_refs[k][tok]


N_ROUTE_TABLES = 2 * TOP_K + 1


def _dispatch_kernel(*refs):
    tables = refs[:N_ROUTE_TABLES]
    zstart_ref, zcount_ref, h2_ref, xs_ref, dest_ref, zero_ref, sem, zsem = refs[N_ROUTE_TABLES:]
    n_assign = tables[0].shape[0] * TOP_K
    zero_ref[...] = jnp.zeros_like(zero_ref)

    def zero_copy(start, rows):
        return pltpu.make_async_copy(zero_ref.at[pl.ds(0, rows * SLAB_ROWS), :],
                                     xs_ref.at[pl.ds(start * SLAB_ROWS, rows * SLAB_ROWS), :], zsem)

    def fill(e, wait):
        start = zstart_ref[e]
        pad = zcount_ref[e]
        for bit in PAD_BITS:
            @pl.when((pad & bit) != 0)
            def _(start=start, bit=bit):
                cp = zero_copy(start, bit)
                cp.wait() if wait else cp.start()
            start = start + (pad & bit)

    lax.fori_loop(0, N_EXPERTS, lambda e, c: (fill(e, False), c)[1], 0)

    def issue(tok, carry):
        src = h2_ref.at[pl.ds(pl.multiple_of(tok * SLAB_ROWS, SLAB_ROWS), SLAB_ROWS), :]
        for k in range(TOP_K):
            row = _sorted_row(tables, tok, k)
            dest_ref[tok * TOP_K + k] = row
            dst = pl.multiple_of(row * SLAB_ROWS, SLAB_ROWS)
            pltpu.make_async_copy(src, xs_ref.at[pl.ds(dst, SLAB_ROWS), :], sem).start()
        return carry

    lax.fori_loop(0, n_assign // TOP_K, issue, 0, unroll=8)

    def drain(i, carry):
        pltpu.make_async_copy(h2_ref.at[pl.ds(0, ROWS_PER_WAIT * SLAB_ROWS), :],
                              xs_ref.at[pl.ds(0, ROWS_PER_WAIT * SLAB_ROWS), :], sem).wait()
        return carry

    lax.fori_loop(0, n_assign // ROWS_PER_WAIT, drain, 0)
    lax.fori_loop(0, N_EXPERTS, lambda e, c: (fill(e, True), c)[1], 0)


def _dispatch(tables, zstart, zcount, h2_slab, n_pad):
    n_assign = tables[0].shape[0] * TOP_K
    return pl.pallas_call(
        _dispatch_kernel,
        grid_spec=pltpu.PrefetchScalarGridSpec(
            num_scalar_prefetch=N_ROUTE_TABLES + 2,
            grid=(1,),
            in_specs=[pl.BlockSpec(memory_space=pltpu.HBM)],
            out_specs=[pl.BlockSpec(memory_space=pl.ANY),
                       pl.BlockSpec(memory_space=pltpu.SMEM)],
            scratch_shapes=[pltpu.VMEM((PAD_BITS[0] * SLAB_ROWS, SLAB_LANES), I32),
                            pltpu.SemaphoreType.DMA(()),
                            pltpu.SemaphoreType.DMA(())]),
        out_shape=[jax.ShapeDtypeStruct((n_pad * SLAB_ROWS, SLAB_LANES), I32),
                   jax.ShapeDtypeStruct((n_assign,), I32)],
        compiler_params=_params(("arbitrary",), 16),
        name="dispatch",
    )(*tables, zstart, zcount, h2_slab)


def _experts_kernel(item_e_ref, item_blk_ref, item_nb_ref, x_blk_ref,
                    x0_ref, x1_ref, x2_ref, x3_ref, wg_ref, wu_ref, wd_ref, y_ref,
                    xb_ref, acc_ref, yp_ref, sem):
    w = pl.program_id(0)
    c = pl.program_id(1)
    last_c = pl.num_programs(1) - 1
    nb = item_nb_ref[w]
    blk0 = item_blk_ref[w]
    x_refs = (x0_ref, x1_ref, x2_ref, x3_ref)

    blk_rows = MOE_BLOCK * SLAB_ROWS

    def out_copy(s):
        return pltpu.make_async_copy(yp_ref.at[s], y_ref.at[pl.ds((blk0 + s) * blk_rows, blk_rows), :],
                                     sem.at[s])

    def wait_out(count):
        for s in range(ITEM_BLOCKS):
            @pl.when(s < count)
            def _(s=s):
                out_copy(s).wait()

    @pl.when((c == last_c) & (w > 0))
    def _():
        wait_out(item_nb_ref[jnp.maximum(w - 1, 0)])

    def run(n_live):
        rows = n_live * MOE_BLOCK

        @pl.when(c == 0)
        def _():
            for s in range(n_live):
                for j in range(SLAB_ROWS):
                    lo, hi = _unpack_pair(x_refs[s][_slab_rows(j, MOE_BLOCK), :])
                    r0 = s * MOE_BLOCK
                    xb_ref[r0:r0 + MOE_BLOCK, j * LANES:(j + 1) * LANES] = lo.astype(BF16)
                    xb_ref[r0:r0 + MOE_BLOCK, HALF_D + j * LANES:HALF_D + (j + 1) * LANES] = hi.astype(BF16)

        xs = xb_ref[0:rows, :]
        g = jnp.dot(xs, wg_ref[...].astype(BF16), preferred_element_type=F32)
        u = jnp.dot(xs, wu_ref[...].astype(BF16), preferred_element_type=F32)
        hid = (g * jax.nn.sigmoid(g) * u).astype(BF16)
        y = jnp.dot(hid, wd_ref[...].astype(BF16), preferred_element_type=F32)

        @pl.when(c == 0)
        def _():
            acc_ref[0:rows, :] = y

        @pl.when((c > 0) & (c < last_c))
        def _():
            acc_ref[0:rows, :] += y

        @pl.when(c == last_c)
        def _():
            for s in range(n_live):
                r0 = s * MOE_BLOCK
                for j in range(SLAB_ROWS):
                    lo_cols = slice(j * LANES, (j + 1) * LANES)
                    hi_cols = slice(HALF_D + j * LANES, HALF_D + (j + 1) * LANES)
                    yp_ref[s, _slab_rows(j, MOE_BLOCK), :] = _pack_pair(
                        acc_ref[r0:r0 + MOE_BLOCK, lo_cols] + y[r0:r0 + MOE_BLOCK, lo_cols],
                        acc_ref[r0:r0 + MOE_BLOCK, hi_cols] + y[r0:r0 + MOE_BLOCK, hi_cols])
                out_copy(s).start()

    for n_live in range(1, ITEM_BLOCKS + 1):
        pl.when(nb == n_live)(functools.partial(run, n_live))

    @pl.when((c == last_c) & (w == pl.num_programs(0) - 1))
    def _():
        wait_out(nb)


def _experts(item_e, item_blk, item_nb, x_sorted, w_gate, w_up, w_down, n_blocks):
    n_items = item_e.shape[0]
    d = D_MODEL
    fc = EXPERT_FC
    x3 = x_sorted
    blk_rows = MOE_BLOCK * SLAB_ROWS
    n_chunks = EXPERT_DIM // fc
    assert n_chunks >= 2, "the last hidden chunk adds onto the accumulator of the earlier ones"

    slot = jnp.arange(ITEM_BLOCKS, dtype=jnp.int32)[:, None]
    x_blk = jnp.maximum(lax.cummax(jnp.where(slot < item_nb[None, :], item_blk[None, :] + slot, -1), axis=1), 0)
    x_blk = x_blk.reshape(-1).astype(jnp.int32)

    def x_spec(s):
        return pl.BlockSpec((blk_rows, SLAB_LANES),
                            lambda w, c, ie, ib, inb, xb: (xb[s * n_items + w], 0))

    def chunk(w, c, inb):
        return jnp.where(inb[w] > 0, c, n_chunks - 1)

    return pl.pallas_call(
        _experts_kernel,
        grid_spec=pltpu.PrefetchScalarGridSpec(
            num_scalar_prefetch=4,
            grid=(n_items, n_chunks),
            in_specs=[x_spec(0), x_spec(1), x_spec(2), x_spec(3),
                      pl.BlockSpec((None, d, fc), lambda w, c, ie, ib, inb, xb: (ie[w], 0, chunk(w, c, inb))),
                      pl.BlockSpec((None, d, fc), lambda w, c, ie, ib, inb, xb: (ie[w], 0, chunk(w, c, inb))),
                      pl.BlockSpec((None, fc, d), lambda w, c, ie, ib, inb, xb: (ie[w], chunk(w, c, inb), 0))],
            out_specs=pl.BlockSpec(memory_space=pl.ANY),
            scratch_shapes=[pltpu.VMEM((ITEM_BLOCKS * MOE_BLOCK, d), BF16),
                            pltpu.VMEM((ITEM_BLOCKS * MOE_BLOCK, d), F32),
                            pltpu.VMEM((ITEM_BLOCKS, blk_rows, SLAB_LANES), I32),
                            pltpu.SemaphoreType.DMA((ITEM_BLOCKS,))]),
        out_shape=jax.ShapeDtypeStruct((n_blocks * blk_rows, SLAB_LANES), I32),
        compiler_params=_params(("arbitrary", "arbitrary"), 56),
        name="experts",
    )(item_e, item_blk, item_nb, x_blk, x3, x3, x3, x3, w_gate, w_up, w_down)


def _combine_kernel(dest_ref, x1_ref, route_ref, mod_ref, gain_ref, y_ref, o_ref, ybuf_ref, sem):
    tm = COMBINE_TM
    i = pl.program_id(0)
    n_tiles = pl.num_programs(0)

    def row_copy(slot, k, r, src):
        return pltpu.make_async_copy(
            y_ref.at[pl.ds(pl.multiple_of(src * SLAB_ROWS, SLAB_ROWS), SLAB_ROWS), :],
            ybuf_ref.at[slot, k, pl.ds(pl.multiple_of(r * SLAB_ROWS, SLAB_ROWS), SLAB_ROWS), :],
            sem.at[slot])

    def wait_slot(slot):
        for k in range(TOP_K):
            pltpu.make_async_copy(y_ref.at[pl.ds(0, tm * SLAB_ROWS), :], ybuf_ref.at[slot, k],
                                  sem.at[slot]).wait()

    def issue_tile(tile, slot):
        base = tile * tm * TOP_K

        def issue(r, carry):
            for k in range(TOP_K):
                row_copy(slot, k, r, dest_ref[base + r * TOP_K + k]).start()
            return carry

        lax.fori_loop(0, tm, issue, 0, unroll=4)

    @pl.when(i == 0)
    def _():
        issue_tile(0, 0)

    for nxt in range(2):
        @pl.when((i + 1 < n_tiles) & ((i + 1) % 2 == nxt))
        def _(nxt=nxt):
            issue_tile(i + 1, nxt)

    slot = i % 2
    wait_slot(slot)

    route = route_ref[...]
    w0 = route[:, 2:3]
    w1 = route[:, 3:4]
    ssq = jnp.zeros((tm, 1), F32)
    for j in range(SLAB_ROWS):
        rows = _slab_rows(j, tm)
        y0 = _unpack_pair(ybuf_ref[slot, 0, rows, :])
        y1 = _unpack_pair(ybuf_ref[slot, 1, rows, :])
        for part, off in ((0, j * LANES), (1, HALF_D + j * LANES)):
            cols = slice(off, off + LANES)
            ffn = w0 * y0[part] + w1 * y1[part]
            x2 = x1_ref[:, cols] + mod_ref[0, 5:6, cols] * ffn
            o_ref[:, cols] = x2
            ssq = ssq + jnp.sum(x2 * x2, axis=-1, keepdims=True)
    o_ref[...] = o_ref[...] * lax.rsqrt(ssq * (1.0 / D_MODEL) + EPS) * gain_ref[...]


def _combine(dest, x1, route, mod6, gain, y_sorted, seq):
    t, d = x1.shape
    tm = COMBINE_TM
    tiles_per_batch = seq // tm
    return pl.pallas_call(
        _combine_kernel,
        grid_spec=pltpu.PrefetchScalarGridSpec(
            num_scalar_prefetch=1,
            grid=(t // tm,),
            in_specs=[pl.BlockSpec((tm, d), lambda i, *_: (i, 0)),
                      pl.BlockSpec((tm, LANES), lambda i, *_: (i, 0)),
                      pl.BlockSpec((1, 6, d), lambda i, *_: (i // tiles_per_batch, 0, 0)),
                      pl.BlockSpec((1, d), lambda i, *_: (0, 0)),
                      pl.BlockSpec(memory_space=pl.ANY)],
            out_specs=pl.BlockSpec((tm, d), lambda i, *_: (i, 0)),
            scratch_shapes=[pltpu.VMEM((2, TOP_K, tm * SLAB_ROWS, SLAB_LANES), I32),
                            pltpu.SemaphoreType.DMA((2,))]),
        out_shape=jax.ShapeDtypeStruct((t, d), F32),
        compiler_params=_params(("arbitrary",), 24),
        name="combine",
    )(dest, x1, route, mod6, gain, y_sorted)


def _dispatch_tables(meta, counts, t):
    n_assign = t * TOP_K
    n_pad = -(-(n_assign + N_EXPERTS * (MOE_BLOCK - 1)) // MOE_BLOCK) * MOE_BLOCK
    n_blocks = n_pad // MOE_BLOCK
    n_items = N_EXPERTS + n_assign // (ITEM_BLOCKS * MOE_BLOCK)

    cnt = counts[0, :N_EXPERTS].astype(jnp.int32)
    blocks_e = (cnt + MOE_BLOCK - 1) // MOE_BLOCK
    blk_end = jnp.cumsum(blocks_e)
    blk_start = blk_end - blocks_e
    tables = (meta[0], meta[1], meta[4], meta[5], (blk_start * MOE_BLOCK).astype(jnp.int32))
    zstart = (blk_start * MOE_BLOCK + cnt).astype(jnp.int32)
    zcount = (blocks_e * MOE_BLOCK - cnt).astype(jnp.int32)

    items_e = (blocks_e + ITEM_BLOCKS - 1) // ITEM_BLOCKS
    item_end = jnp.cumsum(items_e)
    item_start = item_end - items_e
    w = jnp.arange(n_items, dtype=jnp.int32)
    live = w < item_end[-1]
    w_live = jnp.minimum(w, item_end[-1] - 1)
    e_w = jnp.minimum(jnp.sum((item_end[None, :] <= w_live[:, None]).astype(jnp.int32), axis=1), N_EXPERTS - 1)
    j_w = w_live - item_start[e_w]
    item_blk = (blk_start[e_w] + ITEM_BLOCKS * j_w).astype(jnp.int32)
    item_nb = jnp.where(live, jnp.clip(blocks_e[e_w] - ITEM_BLOCKS * j_w, 0, ITEM_BLOCKS), 0).astype(jnp.int32)
    return tables, zstart, zcount, e_w, item_blk, item_nb, n_blocks


def kernel(x, c, positions, norm1_gain, norm2_gain, final_norm_gain, w_ada, b_ada, w_in, attn_sinks,
           ret_norm_gain, w_branch_attn, w_branch_ret, w_out, w_router_group, b_router_group,
           w_router_expert, b_router_expert, w_expert_gate, w_expert_up, w_expert_down):
    batch, seq, d = x.shape
    t = batch * seq
    depth = w_ada.shape[0]
    half = RET_DIM // 2
    inv_freq = (ROPE_BASE ** (-jnp.arange(half, dtype=F32) / half)).reshape(1, half)
    pos = positions.reshape(t, 1)
    c8 = jnp.pad(c, ((0, 8 - batch), (0, 0)))
    xf = x.reshape(t, d)

    assert depth == 1, "the fused final norm assumes a single layer"
    for layer in range(depth):
        b_ada2 = b_ada[layer].reshape(1, -1)
        mod_early = _ada(c8, w_ada[layer], b_ada2, ADA_EARLY)[:batch]
        proj, proj_kv, mod_late = _proj(xf, norm1_gain[layer].reshape(1, d), mod_early.reshape(batch, 2, d),
                                        w_in[layer], c8, w_ada[layer], b_ada2, seq)
        mod6 = jnp.concatenate([mod_early, mod_late[:batch]], axis=1).reshape(batch, 6, d)
        attn, ret = _mixers(proj, proj_kv, attn_sinks[layer], pos, inv_freq,
                            ret_norm_gain[layer].reshape(1, d), batch, seq)

        pad = LANES - N_GROUPS - N_EXPERTS
        w_rt = jnp.concatenate([w_router_group[layer], w_router_expert[layer],
                                jnp.zeros((d, pad), F32)], axis=1)
        b_rt = jnp.concatenate([b_router_group[layer], b_router_expert[layer],
                                jnp.zeros((pad,), F32)]).reshape(1, LANES)
        merged = _branch(attn, ret, proj, w_branch_attn[layer], w_branch_ret[layer])
        x1, h2, route, counts, meta = _mixout(merged, xf, mod6, norm2_gain[layer].reshape(1, d),
                                              w_out[layer].astype(BF16), w_rt, b_rt, seq)
        tables, zstart, zcount, item_e, item_blk, item_nb, n_blocks = _dispatch_tables(meta, counts, t)
        x_sorted, dest = _dispatch(tables, zstart, zcount, h2, n_blocks * MOE_BLOCK)
        y_sorted = _experts(item_e, item_blk, item_nb, x_sorted,
                            w_expert_gate[layer], w_expert_up[layer], w_expert_down[layer], n_blocks)
        xf = _combine(dest, x1, route, mod6, final_norm_gain.reshape(1, d), y_sorted, seq)
    return xf.reshape(batch, seq, d)
```

```python
import functools
import math

import jax
import jax.numpy as jnp
import numpy as np
from jax import lax
from jax.experimental import pallas as pl
from jax.experimental.pallas import tpu as pltpu

F32 = jnp.float32
BF16 = jnp.bfloat16

D_MODEL = 2048
ATTN_HEAD_DIM = 64
ATTN_HEADS = 32
ATTN_KV_HEADS = 4
ATTN_GROUP = 8
WINDOW = 128
RET_HEADS = 8
RET_DIM = 256
RET_CHUNK = 128
ROPE_BASE = 10000.0
N_GROUPS = 4
EXPERTS_PER_GROUP = 16
N_EXPERTS = 64
TOP_K = 2
EXPERT_DIM = 1024
MOE_BLOCK = 128
EPS = 1e-6
NEG = -1e30

MIB = 1024 * 1024
LANES = 128
PROJ_TN = 512
PROJ_TM = 1024
KV_SRC_TILE = 4
ADA_EARLY = 2 * D_MODEL
ADA_LATE_TN = 256
ADA_LATE_STEPS = 4 * D_MODEL // ADA_LATE_TN
BRANCH_TM = 1024
BRANCH_TN = 512
MIXOUT_TM = 512
ITEM_BLOCKS = 4
EXPERT_FC = 512
EXPERT_WEIGHT_SLOTS = 3
COMBINE_TM = 128
SLAB_ROWS = 8
SLAB_LANES = LANES
HALF_D = D_MODEL // 2
ROWS_PER_WAIT = 128
I32 = jnp.int32

COL_QA, COL_QR, COL_KR, COL_VR, COL_GR, COL_GA, COL_GRT = 0, 1, 2, 3, 4, 5, 6

LOG_GAMMA = [math.log1p(-(2.0 ** (-5.0 - h))) for h in range(RET_HEADS)]


def _params(sem, vmem_mib):
    return pltpu.CompilerParams(dimension_semantics=sem, vmem_limit_bytes=vmem_mib * MIB)


def _pack_pair(lo, hi):
    lo_b = lax.bitcast_convert_type(lo.astype(BF16).astype(F32), I32)
    hi_b = lax.bitcast_convert_type(hi.astype(BF16).astype(F32), I32)
    return hi_b | lax.shift_right_logical(lo_b, jnp.full_like(lo_b, 16))


def _unpack_pair(w):
    lo = lax.bitcast_convert_type(w << 16, F32)
    hi = lax.bitcast_convert_type(w & jnp.int32(-65536), F32)
    return lo, hi


def _slab_rows(j, n_tokens):
    return pl.ds(j, n_tokens, stride=SLAB_ROWS)


def _ada_kernel(c_ref, w_ref, b_ref, o_ref):
    c = c_ref[...]
    a = (c * jax.nn.sigmoid(c)).astype(BF16)
    o_ref[...] = jnp.dot(a, w_ref[...].astype(BF16), preferred_element_type=F32) + b_ref[...]


def _ada(c8, w_ada, b_ada, n):
    tn = 1024
    return pl.pallas_call(
        _ada_kernel,
        grid=(n // tn,),
        in_specs=[pl.BlockSpec((8, D_MODEL), lambda j: (0, 0)),
                  pl.BlockSpec((D_MODEL, tn), lambda j: (0, j)),
                  pl.BlockSpec((1, tn), lambda j: (0, j))],
        out_specs=pl.BlockSpec((8, tn), lambda j: (0, j)),
        out_shape=jax.ShapeDtypeStruct((8, n), F32),
        compiler_params=_params(("arbitrary",), 40),
        name="ada",
    )(c8, w_ada, b_ada)


def _proj_kernel(x_ref, g_ref, mod_ref, wlo_ref, whi_ref, c_ref, wada_ref, bada_ref,
                 o_ref, kv_ref, late_ref, h0_ref, h1_ref):
    i = pl.program_id(0)
    v = pl.program_id(1)
    last = pl.num_programs(1) - 1

    def normed():
        x = x_ref[...]
        var = jnp.mean(x * x, axis=-1, keepdims=True)
        y = x * lax.rsqrt(var + EPS) * g_ref[...]
        return (y * (1.0 + mod_ref[0, 1:2, :]) + mod_ref[0, 0:1, :]).astype(BF16)

    @pl.when(pl.program_id(0) * pl.num_programs(1) + v < ADA_LATE_STEPS)
    def _():
        _ada_kernel(c_ref, wada_ref, bada_ref, late_ref)

    @pl.when((i == 0) & (v == 0))
    def _():
        h0_ref[...] = normed()

    for parity, (h_ref, h_next_ref) in enumerate(((h0_ref, h1_ref), (h1_ref, h0_ref))):
        @pl.when((v < last) & (i % 2 == parity))
        def _(h_ref=h_ref):
            w = jnp.concatenate([wlo_ref[...].astype(BF16), whi_ref[...].astype(BF16)], axis=1)
            o_ref[...] = jnp.dot(h_ref[...], w, preferred_element_type=F32).astype(BF16)

        @pl.when((v == last) & (i % 2 == parity))
        def _(h_ref=h_ref, h_next_ref=h_next_ref):
            kv_ref[...] = jnp.dot(h_ref[...], wlo_ref[...].astype(BF16),
                                  preferred_element_type=F32).astype(BF16)
            h_next_ref[...] = normed()


def _proj_w_tile(v, n_wide):
    return jnp.where(v < 2, 2 * v, jnp.where(v < n_wide, 2 * v + 1, KV_SRC_TILE))


def _proj(x2, gain, mod_early, w_in, c8, w_ada, b_ada, seq):
    t = x2.shape[0]
    n = w_in.shape[1]
    tiles_per_batch = seq // PROJ_TM
    n_wide = (n - PROJ_TN) // (2 * PROJ_TN)
    n_late = w_ada.shape[1] - ADA_EARLY
    assert n_late == ADA_LATE_STEPS * ADA_LATE_TN and ADA_LATE_STEPS <= (t // PROJ_TM) * (n_wide + 1)
    early_tiles = ADA_EARLY // ADA_LATE_TN

    def late_tile(i, v):
        return jnp.minimum(i * (n_wide + 1) + v, ADA_LATE_STEPS - 1)

    n_row_tiles = t // PROJ_TM

    def norm_tile(i, v):
        return jnp.minimum(i + (v == n_wide).astype(jnp.int32), n_row_tiles - 1)

    return pl.pallas_call(
        _proj_kernel,
        grid=(n_row_tiles, n_wide + 1),
        in_specs=[pl.BlockSpec((PROJ_TM, D_MODEL), lambda i, v: (norm_tile(i, v), 0)),
                  pl.BlockSpec((1, D_MODEL), lambda i, v: (0, 0)),
                  pl.BlockSpec((1, 2, D_MODEL), lambda i, v: (norm_tile(i, v) // tiles_per_batch, 0, 0)),
                  pl.BlockSpec((D_MODEL, PROJ_TN), lambda i, v: (0, _proj_w_tile(v, n_wide))),
                  pl.BlockSpec((D_MODEL, PROJ_TN),
                               lambda i, v: (0, jnp.where(v < n_wide, _proj_w_tile(v, n_wide) + 1, KV_SRC_TILE))),
                  pl.BlockSpec((8, D_MODEL), lambda i, v: (0, 0)),
                  pl.BlockSpec((D_MODEL, ADA_LATE_TN), lambda i, v: (0, early_tiles + late_tile(i, v))),
                  pl.BlockSpec((1, ADA_LATE_TN), lambda i, v: (0, early_tiles + late_tile(i, v)))],
        out_specs=[pl.BlockSpec((PROJ_TM, 2 * PROJ_TN), lambda i, v: (i, jnp.minimum(v, n_wide - 1))),
                   pl.BlockSpec((PROJ_TM, PROJ_TN), lambda i, v: (i, 0)),
                   pl.BlockSpec((8, ADA_LATE_TN), lambda i, v: (0, late_tile(i, v)))],
        out_shape=[jax.ShapeDtypeStruct((t, n - PROJ_TN), BF16),
                   jax.ShapeDtypeStruct((t, PROJ_TN), BF16),
                   jax.ShapeDtypeStruct((8, n_late), F32)],
        scratch_shapes=[pltpu.VMEM((PROJ_TM, D_MODEL), BF16), pltpu.VMEM((PROJ_TM, D_MODEL), BF16)],
        compiler_params=_params(("arbitrary", "arbitrary"), 60),
        name="proj",
    )(x2, gain, mod_early, w_in, w_in, c8, w_ada, b_ada)


def _attn_stages(sink_ref, q_ref, kvp_ref, kvc_ref, o_ref):
    n = pl.program_id(1)
    kvp = kvp_ref[...]
    kvc = kvc_ref[...]
    qi = lax.broadcasted_iota(jnp.int32, (WINDOW, WINDOW), 0)
    sj = lax.broadcasted_iota(jnp.int32, (WINDOW, WINDOW), 1)
    valid_prev = (sj > qi) & (n > 0)
    valid_cur = sj <= qi
    sink_col = sj == 0
    first_row = lax.broadcasted_iota(jnp.int32, (2 * WINDOW, 1), 0) == 0
    dh = ATTN_HEAD_DIM
    kv_w = ATTN_KV_HEADS * dh
    scale = jnp.asarray(dh ** -0.5, BF16)
    n_pairs = ATTN_GROUP // 2

    def group_operands(kv):
        kband = jnp.concatenate([kvp[:, kv * dh:(kv + 1) * dh],
                                 kvc[:, kv * dh:(kv + 1) * dh]], axis=0) * scale
        vband = jnp.concatenate([kvp[:, kv_w + kv * dh:kv_w + (kv + 1) * dh],
                                 kvc[:, kv_w + kv * dh:kv_w + (kv + 1) * dh]], axis=0)
        vband = jnp.where(first_row, jnp.zeros_like(vband), vband)
        zeros = jnp.zeros_like(kband)
        ones = jnp.ones_like(vband)
        k_pad = (jnp.concatenate([kband, zeros], axis=1), jnp.concatenate([zeros, kband], axis=1))
        pv_rhs = jnp.concatenate(
            [jnp.concatenate([vband, zeros, ones, zeros], axis=1),
             jnp.concatenate([zeros, vband, zeros, ones], axis=1)], axis=0)
        q_rows = jnp.concatenate(
            [q_ref[:, (kv * ATTN_GROUP + 2 * p) * dh:(kv * ATTN_GROUP + 2 * p + 2) * dh]
             for p in range(n_pairs)], axis=0)
        scores = [lax.dot_general(q_rows, k_pad[idx], (((1,), (1,)), ((), ())),
                                  preferred_element_type=F32) for idx in range(2)]
        return scores, pv_rhs

    nxt = group_operands(0)
    for kv in range(ATTN_KV_HEADS):
        scores, pv_rhs = nxt
        if kv + 1 < ATTN_KV_HEADS:
            nxt = group_operands(kv + 1)
        prob_rows = []
        for pair in range(n_pairs):
            rows = slice(pair * WINDOW, (pair + 1) * WINDOW)
            probs = []
            for idx in range(2):
                s = scores[idx][rows]
                sink = sink_ref[kv * ATTN_GROUP + 2 * pair + idx]
                s_prev = jnp.where(sink_col, sink, jnp.where(valid_prev, s[:, :WINDOW], NEG))
                s_cur = jnp.where(valid_cur, s[:, WINDOW:], NEG)
                m = jnp.max(jnp.maximum(s_prev, s_cur), axis=-1, keepdims=True)
                probs += [jnp.exp(s_prev - m).astype(BF16), jnp.exp(s_cur - m).astype(BF16)]
            prob_rows.append(jnp.concatenate(probs, axis=-1))
        r = jnp.dot(jnp.concatenate(prob_rows, axis=0), pv_rhs, preferred_element_type=F32)
        for pair in range(n_pairs):
            rows = slice(pair * WINDOW, (pair + 1) * WINDOW)
            col = (kv * ATTN_GROUP + 2 * pair) * dh
            o_ref[:, col:col + 2 * dh] = (r[rows, :2 * dh] * (1.0 / r[rows, 2 * dh:])).astype(BF16)
        yield


def _ret_decay_tables():
    lg = np.asarray(LOG_GAMMA, np.float64)[:, None, None]
    i = np.arange(RET_CHUNK, dtype=np.float64)
    diff = i[:, None] - i[None, :]
    k_scale = RET_DIM ** -0.5
    d_intra = np.where(diff >= 0, np.exp(np.maximum(diff, 0.0) * lg), 0.0) * k_scale
    lanes = np.ones((1, 1, RET_DIM // 2))
    d_q = np.exp((i[None, :, None] + 1.0) * lg) * lanes
    d_k = np.exp((RET_CHUNK - 1.0 - i[None, :, None]) * lg) * k_scale * lanes
    return jnp.asarray(d_intra, F32), jnp.asarray(d_q, F32), jnp.asarray(d_k, BF16)


def _ret_stages(pos_ref, invf_ref, di_ref, dq_ref, dk_ref, q_ref, k_ref, v_ref, g_ref, gain_ref,
                o_ref, state_ref):
    half = RET_DIM // 2
    ang = pos_ref[...].astype(F32) * invf_ref[...]
    cos = jnp.cos(ang).astype(BF16)
    sin = jnp.sin(ang).astype(BF16)

    def rot(t):
        t1, t2 = t[:, :half], t[:, half:]
        return jnp.concatenate([t1 * cos - t2 * sin, t1 * sin + t2 * cos], axis=-1)

    def both_halves(t, factor):
        return jnp.concatenate([t[:, :half] * factor, t[:, half:] * factor], axis=-1)

    def head_front(h):
        sl = slice(h * RET_DIM, (h + 1) * RET_DIM)
        qb = rot(q_ref[:, sl])
        kb = rot(k_ref[:, sl])
        intra = lax.dot_general(qb, kb, (((1,), (1,)), ((), ())),
                                preferred_element_type=F32) * di_ref[h]
        st = state_ref[h]
        cross = jnp.dot(qb, st.astype(BF16), preferred_element_type=F32)
        return kb, intra, st, cross

    front = head_front(0)
    for h in range(RET_HEADS):
        sl = slice(h * RET_DIM, (h + 1) * RET_DIM)
        kb, intra, st, cross = front
        if h + 1 < RET_HEADS:
            front = head_front(h + 1)
        vb = v_ref[:, sl]
        d_chunk = math.exp(RET_CHUNK * LOG_GAMMA[h])
        o = jnp.dot(intra.astype(BF16), vb, preferred_element_type=F32) + both_halves(cross, dq_ref[h])
        kd = both_halves(kb, dk_ref[h])
        state_ref[h] = st * d_chunk + lax.dot_general(kd, vb, (((0,), (0,)), ((), ())),
                                                      preferred_element_type=F32)
        o = o * lax.rsqrt(jnp.mean(o * o, axis=-1, keepdims=True) + EPS) * gain_ref[:, sl]
        gg = g_ref[:, sl].astype(F32)
        o_ref[:, sl] = (gg * jax.nn.sigmoid(gg) * o).astype(BF16)
        yield


def _mixers_kernel(sink_ref, qa_ref, kvp_ref, kvc_ref, pos_ref, invf_ref, di_ref, dq_ref, dk_ref,
                   qr_ref, kr_ref, vr_ref, gr_ref, gain_ref, attn_ref, ret_ref, state_ref):
    @pl.when(pl.program_id(1) == 0)
    def _():
        state_ref[...] = jnp.zeros_like(state_ref)

    attn = _attn_stages(sink_ref, qa_ref, kvp_ref, kvc_ref, attn_ref)
    ret = _ret_stages(pos_ref, invf_ref, di_ref, dq_ref, dk_ref, qr_ref, kr_ref, vr_ref, gr_ref,
                      gain_ref, ret_ref, state_ref)
    heads_per_group = RET_HEADS // ATTN_KV_HEADS
    for _ in range(ATTN_KV_HEADS):
        next(attn)
        for _ in range(heads_per_group):
            next(ret)


def _mixers(proj, proj_kv, sinks, pos, inv_freq, ret_gain, batch, seq):
    nb = seq // WINDOW
    t = batch * seq

    def col(cb):
        return lambda b, n: (b * nb + n, cb)

    d_intra, d_q, d_k = _ret_decay_tables()
    table = lambda lanes: pl.BlockSpec((RET_HEADS, RET_CHUNK, lanes), lambda b, n: (0, 0, 0))
    rows = pl.BlockSpec((WINDOW, D_MODEL), col(0))
    return pl.pallas_call(
        _mixers_kernel,
        grid=(batch, nb),
        in_specs=[pl.BlockSpec(memory_space=pltpu.SMEM),
                  pl.BlockSpec((WINDOW, D_MODEL), col(COL_QA)),
                  pl.BlockSpec((WINDOW, PROJ_TN), lambda b, n: (b * nb + jnp.maximum(n - 1, 0), 0)),
                  pl.BlockSpec((WINDOW, PROJ_TN), col(0)),
                  pl.BlockSpec((RET_CHUNK, 1), col(0)),
                  pl.BlockSpec((1, RET_DIM // 2), lambda b, n: (0, 0)),
                  table(RET_CHUNK), table(RET_DIM // 2), table(RET_DIM // 2),
                  pl.BlockSpec((RET_CHUNK, D_MODEL), col(COL_QR)),
                  pl.BlockSpec((RET_CHUNK, D_MODEL), col(COL_KR)),
                  pl.BlockSpec((RET_CHUNK, D_MODEL), col(COL_VR)),
                  pl.BlockSpec((RET_CHUNK, D_MODEL), col(COL_GR)),
                  pl.BlockSpec((1, D_MODEL), lambda b, n: (0, 0))],
        out_specs=[rows, rows],
        out_shape=[jax.ShapeDtypeStruct((t, D_MODEL), BF16), jax.ShapeDtypeStruct((t, D_MODEL), BF16)],
        scratch_shapes=[pltpu.VMEM((RET_HEADS, RET_DIM, RET_DIM), F32)],
        compiler_params=_params(("arbitrary", "arbitrary"), 40),
        name="mixers",
    )(sinks, proj, proj_kv, proj_kv, pos, inv_freq, d_intra, d_q, d_k, proj, proj, proj, proj, ret_gain)


def _route(logits):
    lane = lax.broadcasted_iota(jnp.int32, logits.shape, 1)
    lane_f = lane.astype(F32)
    is_g = lane < N_GROUPS
    gl = jnp.where(is_g, logits, NEG)
    gmax = jnp.max(gl, axis=-1, keepdims=True)
    gsel = jnp.min(jnp.where(gl == gmax, lane_f, float(LANES)), axis=-1, keepdims=True)
    gsum = jnp.sum(jnp.where(is_g, jnp.exp(gl - gmax), 0.0), axis=-1, keepdims=True)
    g_w = 1.0 / gsum
    grp = ((lane - N_GROUPS) >> 4).astype(F32)
    is_e = (lane >= N_GROUPS) & (lane < N_GROUPS + N_EXPERTS) & (grp == gsel)
    el = jnp.where(is_e, logits, NEG)
    v1 = jnp.max(el, axis=-1, keepdims=True)
    i1 = jnp.min(jnp.where(el == v1, lane_f, float(LANES)), axis=-1, keepdims=True)
    el2 = jnp.where(lane_f == i1, NEG, el)
    v2 = jnp.max(el2, axis=-1, keepdims=True)
    i2 = jnp.min(jnp.where(el2 == v2, lane_f, float(LANES)), axis=-1, keepdims=True)
    tt = jnp.exp(v2 - v1)
    w1 = g_w / (1.0 + tt)
    w2 = g_w * tt / (1.0 + tt)
    return jnp.where(lane == 0, i1 - N_GROUPS,
                     jnp.where(lane == 1, i2 - N_GROUPS,
                               jnp.where(lane == 2, w1, jnp.where(lane == 3, w2, 0.0))))


def _branch_kernel(attn_ref, ret_ref, ga_ref, gr_ref, wa_ref, wr_ref, o_ref):
    a = jnp.dot(attn_ref[...], wa_ref[...].astype(BF16), preferred_element_type=F32)
    r = jnp.dot(ret_ref[...], wr_ref[...].astype(BF16), preferred_element_type=F32)
    o_ref[...] = (jax.nn.sigmoid(ga_ref[...].astype(F32)) * a
                  + jax.nn.sigmoid(gr_ref[...].astype(F32)) * r).astype(BF16)


def _branch(attn, ret, proj, wa, wr):
    t = attn.shape[0]
    tm, tn = BRANCH_TM, BRANCH_TN
    per_slab = D_MODEL // tn
    return pl.pallas_call(
        _branch_kernel,
        grid=(t // tm, D_MODEL // tn),
        in_specs=[pl.BlockSpec((tm, D_MODEL), lambda i, j: (i, 0)),
                  pl.BlockSpec((tm, D_MODEL), lambda i, j: (i, 0)),
                  pl.BlockSpec((tm, tn), lambda i, j: (i, COL_GA * per_slab + j)),
                  pl.BlockSpec((tm, tn), lambda i, j: (i, COL_GRT * per_slab + j)),
                  pl.BlockSpec((D_MODEL, tn), lambda i, j: (0, j)),
                  pl.BlockSpec((D_MODEL, tn), lambda i, j: (0, j))],
        out_specs=pl.BlockSpec((tm, tn), lambda i, j: (i, j)),
        out_shape=jax.ShapeDtypeStruct((t, D_MODEL), BF16),
        compiler_params=_params(("arbitrary", "arbitrary"), 48),
        name="branch",
    )(attn, ret, proj, proj, wa, wr)


def _mixout_kernel(m_ref, x_ref, mod_ref, g2_ref, wo_ref, wrt_ref, brt_ref,
                   x1_ref, h2_ref, route_ref, count_ref, meta_ref, carry_ref):
    @pl.when(pl.program_id(0) == 0)
    def _():
        carry_ref[...] = jnp.zeros_like(carry_ref)

    mix = jnp.dot(m_ref[...], wo_ref[...], preferred_element_type=F32)
    x1 = x_ref[...] + mod_ref[0, 2:3, :] * mix
    x1_ref[...] = x1
    var = jnp.mean(x1 * x1, axis=-1, keepdims=True)
    h2 = x1 * lax.rsqrt(var + EPS) * g2_ref[...]
    h2 = h2 * (1.0 + mod_ref[0, 4:5, :]) + mod_ref[0, 3:4, :]
    tm = h2.shape[0]
    for j in range(SLAB_ROWS):
        lo = h2[:, j * LANES:(j + 1) * LANES]
        hi = h2[:, HALF_D + j * LANES:HALF_D + (j + 1) * LANES]
        h2_ref[_slab_rows(j, tm), :] = _pack_pair(lo, hi)
    h_hi = h2.astype(BF16)
    h_lo = (h2 - h_hi.astype(F32)).astype(BF16)
    w_rt = wrt_ref[...]
    w_hi = w_rt.astype(BF16)
    w_lo = (w_rt - w_hi.astype(F32)).astype(BF16)
    hi_both = jnp.dot(h_hi, jnp.concatenate([w_hi, w_lo], axis=1), preferred_element_type=F32)
    logits = (hi_both[:, :LANES] + hi_both[:, LANES:]
              + jnp.dot(h_lo, w_hi, preferred_element_type=F32) + brt_ref[...])
    route = _route(logits)

    lane = lax.broadcasted_iota(jnp.int32, route.shape, 1)
    lane_f = lane.astype(F32)
    hot1 = lane_f == route[:, 0:1]
    hot2 = lane_f == route[:, 1:2]
    both = jnp.where(hot1 | hot2, 1.0, 0.0)
    ii = lax.broadcasted_iota(jnp.int32, (tm, tm), 0)
    jj = lax.broadcasted_iota(jnp.int32, (tm, tm), 1)
    lower = jnp.where(ii > jj, 1.0, 0.0).astype(BF16)
    before = jnp.dot(lower, both.astype(BF16), preferred_element_type=F32) + carry_ref[...]
    r1 = jnp.sum(jnp.where(hot1, before, 0.0), axis=-1, keepdims=True)
    r2 = jnp.sum(jnp.where(hot2, before, 0.0), axis=-1, keepdims=True)
    route = jnp.where(lane == 4, r1, jnp.where(lane == 5, r2, route))
    route_ref[...] = route
    meta_ref[...] = route.T[0:SLAB_ROWS, :].astype(I32)
    carry = carry_ref[...] + jnp.sum(both, axis=0, keepdims=True)
    carry_ref[...] = carry
    count_ref[...] = carry


def _mixout(merged, x2, mod6, gain2, wo, w_rt, b_rt, seq):
    t = x2.shape[0]
    tm = MIXOUT_TM
    tiles_per_batch = seq // tm
    row = lambda i: (i, 0)
    const = lambda i: (0, 0)
    return pl.pallas_call(
        _mixout_kernel,
        grid=(t // tm,),
        in_specs=[pl.BlockSpec((tm, D_MODEL), row),
                  pl.BlockSpec((tm, D_MODEL), row),
                  pl.BlockSpec((1, 6, D_MODEL), lambda i: (i // tiles_per_batch, 0, 0)),
                  pl.BlockSpec((1, D_MODEL), const),
                  pl.BlockSpec((D_MODEL, D_MODEL), const, pipeline_mode=pl.Buffered(1)),
                  pl.BlockSpec((D_MODEL, LANES), const),
                  pl.BlockSpec((1, LANES), const)],
        out_specs=[pl.BlockSpec((tm, D_MODEL), row),
                   pl.BlockSpec((tm * SLAB_ROWS, SLAB_LANES), row),
                   pl.BlockSpec((tm, LANES), row),
                   pl.BlockSpec((1, LANES), const),
                   pl.BlockSpec((SLAB_ROWS, tm), lambda i: (0, i))],
        out_shape=[jax.ShapeDtypeStruct((t, D_MODEL), F32),
                   jax.ShapeDtypeStruct((t * SLAB_ROWS, SLAB_LANES), I32),
                   jax.ShapeDtypeStruct((t, LANES), F32),
                   jax.ShapeDtypeStruct((1, LANES), F32),
                   jax.ShapeDtypeStruct((SLAB_ROWS, t), I32)],
        scratch_shapes=[pltpu.VMEM((1, LANES), F32)],
        compiler_params=_params(("arbitrary",), 56),
        name="mixout",
    )(merged, x2, mod6, gain2, wo, w_rt, b_rt)


PAD_BITS = (64, 32, 16, 8, 4, 2, 1)


def _sorted_row(tables, tok, k):
    e_refs, r_refs, blk_row_ref = tables[0:TOP_K], tables[TOP_K:2 * TOP_K], tables[2 * TOP_K]
    return blk_row_ref[e_refs[k][tok]] + r_refs[k][tok]


N_ROUTE_TABLES = 2 * TOP_K + 1


def _dispatch_kernel(*refs):
    tables = refs[:N_ROUTE_TABLES]
    zstart_ref, zcount_ref, h2_ref, xs_ref, dest_ref, zero_ref, sem, zsem = refs[N_ROUTE_TABLES:]
    n_assign = tables[0].shape[0] * TOP_K
    zero_ref[...] = jnp.zeros_like(zero_ref)

    def zero_copy(start, rows):
        return pltpu.make_async_copy(zero_ref.at[pl.ds(0, rows * SLAB_ROWS), :],
                                     xs_ref.at[pl.ds(start * SLAB_ROWS, rows * SLAB_ROWS), :], zsem)

    def fill(e, wait):
        start = zstart_ref[e]
        pad = zcount_ref[e]
        for bit in PAD_BITS:
            @pl.when((pad & bit) != 0)
            def _(start=start, bit=bit):
                cp = zero_copy(start, bit)
                cp.wait() if wait else cp.start()
            start = start + (pad & bit)

    lax.fori_loop(0, N_EXPERTS, lambda e, c: (fill(e, False), c)[1], 0)

    def issue(tok, carry):
        src = h2_ref.at[pl.ds(pl.multiple_of(tok * SLAB_ROWS, SLAB_ROWS), SLAB_ROWS), :]
        for k in range(TOP_K):
            row = _sorted_row(tables, tok, k)
            dest_ref[tok * TOP_K + k] = row
            dst = pl.multiple_of(row * SLAB_ROWS, SLAB_ROWS)
            pltpu.make_async_copy(src, xs_ref.at[pl.ds(dst, SLAB_ROWS), :], sem).start()
        return carry

    lax.fori_loop(0, n_assign // TOP_K, issue, 0, unroll=8)

    def drain(i, carry):
        pltpu.make_async_copy(h2_ref.at[pl.ds(0, ROWS_PER_WAIT * SLAB_ROWS), :],
                              xs_ref.at[pl.ds(0, ROWS_PER_WAIT * SLAB_ROWS), :], sem).wait()
        return carry

    lax.fori_loop(0, n_assign // ROWS_PER_WAIT, drain, 0)
    lax.fori_loop(0, N_EXPERTS, lambda e, c: (fill(e, True), c)[1], 0)


def _dispatch(tables, zstart, zcount, h2_slab, n_pad):
    n_assign = tables[0].shape[0] * TOP_K
    return pl.pallas_call(
        _dispatch_kernel,
        grid_spec=pltpu.PrefetchScalarGridSpec(
            num_scalar_prefetch=N_ROUTE_TABLES + 2,
            grid=(1,),
            in_specs=[pl.BlockSpec(memory_space=pltpu.HBM)],
            out_specs=[pl.BlockSpec(memory_space=pl.ANY),
                       pl.BlockSpec(memory_space=pltpu.SMEM)],
            scratch_shapes=[pltpu.VMEM((PAD_BITS[0] * SLAB_ROWS, SLAB_LANES), I32),
                            pltpu.SemaphoreType.DMA(()),
                            pltpu.SemaphoreType.DMA(())]),
        out_shape=[jax.ShapeDtypeStruct((n_pad * SLAB_ROWS, SLAB_LANES), I32),
                   jax.ShapeDtypeStruct((n_assign,), I32)],
        compiler_params=_params(("arbitrary",), 16),
        name="dispatch",
    )(*tables, zstart, zcount, h2_slab)


def _experts_kernel(item_e_ref, item_blk_ref, item_nb_ref, x_blk_ref,
                    x0_ref, x1_ref, x2_ref, x3_ref, wg_hbm, wu_hbm, wd_hbm, y_ref,
                    xb_ref, acc_ref, yp_ref, wg_ref, wu_ref, wd_ref, sem, wsem):
    w = pl.program_id(0)
    c = pl.program_id(1)
    n_items = pl.num_programs(0)
    n_chunks = pl.num_programs(1)
    last_c = n_chunks - 1
    nb = item_nb_ref[w]
    blk0 = item_blk_ref[w]
    x_refs = (x0_ref, x1_ref, x2_ref, x3_ref)

    blk_rows = MOE_BLOCK * SLAB_ROWS

    step = w * n_chunks + c
    fc = wg_ref.shape[2]

    def weight_copies(item, chunk, slot):
        e = item_e_ref[item]
        col = pl.multiple_of(chunk * fc, fc)
        return (pltpu.make_async_copy(wg_hbm.at[e, :, pl.ds(col, fc)], wg_ref.at[slot], wsem.at[slot]),
                pltpu.make_async_copy(wu_hbm.at[e, :, pl.ds(col, fc)], wu_ref.at[slot], wsem.at[slot]),
                pltpu.make_async_copy(wd_hbm.at[e, pl.ds(col, fc), :], wd_ref.at[slot], wsem.at[slot]))

    def start_weights(ahead):
        st = step + ahead
        item = jnp.minimum(st // n_chunks, n_items - 1)

        @pl.when((st < n_items * n_chunks) & (item_nb_ref[item] > 0))
        def _():
            for cp in weight_copies(item, st % n_chunks, st % EXPERT_WEIGHT_SLOTS):
                cp.start()

    @pl.when(step == 0)
    def _():
        for ahead in range(EXPERT_WEIGHT_SLOTS - 1):
            start_weights(ahead)

    start_weights(EXPERT_WEIGHT_SLOTS - 1)
    w_slot = step % EXPERT_WEIGHT_SLOTS

    @pl.when(nb > 0)
    def _():
        for cp in weight_copies(w, c, w_slot):
            cp.wait()

    def out_copy(s):
        return pltpu.make_async_copy(yp_ref.at[s], y_ref.at[pl.ds((blk0 + s) * blk_rows, blk_rows), :],
                                     sem.at[s])

    def wait_out(count):
        for s in range(ITEM_BLOCKS):
            @pl.when(s < count)
            def _(s=s):
                out_copy(s).wait()

    @pl.when((c == last_c) & (w > 0))
    def _():
        wait_out(item_nb_ref[jnp.maximum(w - 1, 0)])

    def run(n_live):
        rows = n_live * MOE_BLOCK

        @pl.when(c == 0)
        def _():
            for s in range(n_live):
                for j in range(SLAB_ROWS):
                    lo, hi = _unpack_pair(x_refs[s][_slab_rows(j, MOE_BLOCK), :])
                    r0 = s * MOE_BLOCK
                    xb_ref[r0:r0 + MOE_BLOCK, j * LANES:(j + 1) * LANES] = lo.astype(BF16)
                    xb_ref[r0:r0 + MOE_BLOCK, HALF_D + j * LANES:HALF_D + (j + 1) * LANES] = hi.astype(BF16)

        xs = xb_ref[0:rows, :]
        g = jnp.dot(xs, wg_ref[w_slot].astype(BF16), preferred_element_type=F32)
        u = jnp.dot(xs, wu_ref[w_slot].astype(BF16), preferred_element_type=F32)
        hid = (g * jax.nn.sigmoid(g) * u).astype(BF16)
        y = jnp.dot(hid, wd_ref[w_slot].astype(BF16), preferred_element_type=F32)

        @pl.when(c == 0)
        def _():
            acc_ref[0:rows, :] = y

        @pl.when((c > 0) & (c < last_c))
        def _():
            acc_ref[0:rows, :] += y

        @pl.when(c == last_c)
        def _():
            for s in range(n_live):
                r0 = s * MOE_BLOCK
                for j in range(SLAB_ROWS):
                    lo_cols = slice(j * LANES, (j + 1) * LANES)
                    hi_cols = slice(HALF_D + j * LANES, HALF_D + (j + 1) * LANES)
                    yp_ref[s, _slab_rows(j, MOE_BLOCK), :] = _pack_pair(
                        acc_ref[r0:r0 + MOE_BLOCK, lo_cols] + y[r0:r0 + MOE_BLOCK, lo_cols],
                        acc_ref[r0:r0 + MOE_BLOCK, hi_cols] + y[r0:r0 + MOE_BLOCK, hi_cols])
                out_copy(s).start()

    for n_live in range(1, ITEM_BLOCKS + 1):
        pl.when(nb == n_live)(functools.partial(run, n_live))

    @pl.when((c == last_c) & (w == pl.num_programs(0) - 1))
    def _():
        wait_out(nb)


def _experts(item_e, item_blk, item_nb, x_sorted, w_gate, w_up, w_down, n_blocks):
    n_items = item_e.shape[0]
    d = D_MODEL
    fc = EXPERT_FC
    x3 = x_sorted
    blk_rows = MOE_BLOCK * SLAB_ROWS
    n_chunks = EXPERT_DIM // fc
    assert n_chunks >= 2, "the last hidden chunk adds onto the accumulator of the earlier ones"

    slot = jnp.arange(ITEM_BLOCKS, dtype=jnp.int32)[:, None]
    x_blk = jnp.maximum(lax.cummax(jnp.where(slot < item_nb[None, :], item_blk[None, :] + slot, -1), axis=1), 0)
    x_blk = x_blk.reshape(-1).astype(jnp.int32)

    def x_spec(s):
        return pl.BlockSpec((blk_rows, SLAB_LANES),
                            lambda w, c, ie, ib, inb, xb: (xb[s * n_items + w], 0))

    hbm = pl.BlockSpec(memory_space=pl.ANY)
    slots = EXPERT_WEIGHT_SLOTS
    return pl.pallas_call(
        _experts_kernel,
        grid_spec=pltpu.PrefetchScalarGridSpec(
            num_scalar_prefetch=4,
            grid=(n_items, n_chunks),
            in_specs=[x_spec(0), x_spec(1), x_spec(2), x_spec(3), hbm, hbm, hbm],
            out_specs=pl.BlockSpec(memory_space=pl.ANY),
            scratch_shapes=[pltpu.VMEM((ITEM_BLOCKS * MOE_BLOCK, d), BF16),
                            pltpu.VMEM((ITEM_BLOCKS * MOE_BLOCK, d), F32),
                            pltpu.VMEM((ITEM_BLOCKS, blk_rows, SLAB_LANES), I32),
                            pltpu.VMEM((slots, d, fc), F32),
                            pltpu.VMEM((slots, d, fc), F32),
                            pltpu.VMEM((slots, fc, d), F32),
                            pltpu.SemaphoreType.DMA((ITEM_BLOCKS,)),
                            pltpu.SemaphoreType.DMA((slots,))]),
        out_shape=jax.ShapeDtypeStruct((n_blocks * blk_rows, SLAB_LANES), I32),
        compiler_params=_params(("arbitrary", "arbitrary"), 56),
        name="experts",
    )(item_e, item_blk, item_nb, x_blk, x3, x3, x3, x3, w_gate, w_up, w_down)


def _combine_kernel(dest_ref, x1_ref, route_ref, mod_ref, gain_ref, y_ref, o_ref, ybuf_ref, sem):
    tm = COMBINE_TM
    i = pl.program_id(0)
    n_tiles = pl.num_programs(0)

    def row_copy(slot, k, r, src):
        return pltpu.make_async_copy(
            y_ref.at[pl.ds(pl.multiple_of(src * SLAB_ROWS, SLAB_ROWS), SLAB_ROWS), :],
            ybuf_ref.at[slot, k, pl.ds(pl.multiple_of(r * SLAB_ROWS, SLAB_ROWS), SLAB_ROWS), :],
            sem.at[slot])

    def wait_slot(slot):
        for k in range(TOP_K):
            pltpu.make_async_copy(y_ref.at[pl.ds(0, tm * SLAB_ROWS), :], ybuf_ref.at[slot, k],
                                  sem.at[slot]).wait()

    def issue_tile(tile, slot):
        base = tile * tm * TOP_K

        def issue(r, carry):
            for k in range(TOP_K):
                row_copy(slot, k, r, dest_ref[base + r * TOP_K + k]).start()
            return carry

        lax.fori_loop(0, tm, issue, 0, unroll=4)

    @pl.when(i == 0)
    def _():
        issue_tile(0, 0)

    for nxt in range(2):
        @pl.when((i + 1 < n_tiles) & ((i + 1) % 2 == nxt))
        def _(nxt=nxt):
            issue_tile(i + 1, nxt)

    slot = i % 2
    wait_slot(slot)

    route = route_ref[...]
    w0 = route[:, 2:3]
    w1 = route[:, 3:4]
    ssq = jnp.zeros((tm, 1), F32)
    for j in range(SLAB_ROWS):
        rows = _slab_rows(j, tm)
        y0 = _unpack_pair(ybuf_ref[slot, 0, rows, :])
        y1 = _unpack_pair(ybuf_ref[slot, 1, rows, :])
        for part, off in ((0, j * LANES), (1, HALF_D + j * LANES)):
            cols = slice(off, off + LANES)
            ffn = w0 * y0[part] + w1 * y1[part]
            x2 = x1_ref[:, cols] + mod_ref[0, 5:6, cols] * ffn
            o_ref[:, cols] = x2
            ssq = ssq + jnp.sum(x2 * x2, axis=-1, keepdims=True)
    o_ref[...] = o_ref[...] * lax.rsqrt(ssq * (1.0 / D_MODEL) + EPS) * gain_ref[...]


def _combine(dest, x1, route, mod6, gain, y_sorted, seq):
    t, d = x1.shape
    tm = COMBINE_TM
    tiles_per_batch = seq // tm
    return pl.pallas_call(
        _combine_kernel,
        grid_spec=pltpu.PrefetchScalarGridSpec(
            num_scalar_prefetch=1,
            grid=(t // tm,),
            in_specs=[pl.BlockSpec((tm, d), lambda i, *_: (i, 0)),
                      pl.BlockSpec((tm, LANES), lambda i, *_: (i, 0)),
                      pl.BlockSpec((1, 6, d), lambda i, *_: (i // tiles_per_batch, 0, 0)),
                      pl.BlockSpec((1, d), lambda i, *_: (0, 0)),
                      pl.BlockSpec(memory_space=pl.ANY)],
            out_specs=pl.BlockSpec((tm, d), lambda i, *_: (i, 0)),
            scratch_shapes=[pltpu.VMEM((2, TOP_K, tm * SLAB_ROWS, SLAB_LANES), I32),
                            pltpu.SemaphoreType.DMA((2,))]),
        out_shape=jax.ShapeDtypeStruct((t, d), F32),
        compiler_params=_params(("arbitrary",), 24),
        name="combine",
    )(dest, x1, route, mod6, gain, y_sorted)


def _dispatch_tables(meta, counts, t):
    n_assign = t * TOP_K
    n_pad = -(-(n_assign + N_EXPERTS * (MOE_BLOCK - 1)) // MOE_BLOCK) * MOE_BLOCK
    n_blocks = n_pad // MOE_BLOCK
    n_items = N_EXPERTS + n_assign // (ITEM_BLOCKS * MOE_BLOCK)

    cnt = counts[0, :N_EXPERTS].astype(jnp.int32)
    blocks_e = (cnt + MOE_BLOCK - 1) // MOE_BLOCK
    blk_end = jnp.cumsum(blocks_e)
    blk_start = blk_end - blocks_e
    tables = (meta[0], meta[1], meta[4], meta[5], (blk_start * MOE_BLOCK).astype(jnp.int32))
    zstart = (blk_start * MOE_BLOCK + cnt).astype(jnp.int32)
    zcount = (blocks_e * MOE_BLOCK - cnt).astype(jnp.int32)

    items_e = (blocks_e + ITEM_BLOCKS - 1) // ITEM_BLOCKS
    item_end = jnp.cumsum(items_e)
    item_start = item_end - items_e
    w = jnp.arange(n_items, dtype=jnp.int32)
    live = w < item_end[-1]
    w_live = jnp.minimum(w, item_end[-1] - 1)
    e_w = jnp.minimum(jnp.sum((item_end[None, :] <= w_live[:, None]).astype(jnp.int32), axis=1), N_EXPERTS - 1)
    j_w = w_live - item_start[e_w]
    item_blk = (blk_start[e_w] + ITEM_BLOCKS * j_w).astype(jnp.int32)
    item_nb = jnp.where(live, jnp.clip(blocks_e[e_w] - ITEM_BLOCKS * j_w, 0, ITEM_BLOCKS), 0).astype(jnp.int32)
    return tables, zstart, zcount, e_w, item_blk, item_nb, n_blocks


def kernel(x, c, positions, norm1_gain, norm2_gain, final_norm_gain, w_ada, b_ada, w_in, attn_sinks,
           ret_norm_gain, w_branch_attn, w_branch_ret, w_out, w_router_group, b_router_group,
           w_router_expert, b_router_expert, w_expert_gate, w_expert_up, w_expert_down):
    batch, seq, d = x.shape
    t = batch * seq
    depth = w_ada.shape[0]
    half = RET_DIM // 2
    inv_freq = (ROPE_BASE ** (-jnp.arange(half, dtype=F32) / half)).reshape(1, half)
    pos = positions.reshape(t, 1)
    c8 = jnp.pad(c, ((0, 8 - batch), (0, 0)))
    xf = x.reshape(t, d)

    assert depth == 1, "the fused final norm assumes a single layer"
    for layer in range(depth):
        b_ada2 = b_ada[layer].reshape(1, -1)
        mod_early = _ada(c8, w_ada[layer], b_ada2, ADA_EARLY)[:batch]
        proj, proj_kv, mod_late = _proj(xf, norm1_gain[layer].reshape(1, d), mod_early.reshape(batch, 2, d),
                                        w_in[layer], c8, w_ada[layer], b_ada2, seq)
        mod6 = jnp.concatenate([mod_early, mod_late[:batch]], axis=1).reshape(batch, 6, d)
        attn, ret = _mixers(proj, proj_kv, attn_sinks[layer], pos, inv_freq,
                            ret_norm_gain[layer].reshape(1, d), batch, seq)

        pad = LANES - N_GROUPS - N_EXPERTS
        w_rt = jnp.concatenate([w_router_group[layer], w_router_expert[layer],
                                jnp.zeros((d, pad), F32)], axis=1)
        b_rt = jnp.concatenate([b_router_group[layer], b_router_expert[layer],
                                jnp.zeros((pad,), F32)]).reshape(1, LANES)
        merged = _branch(attn, ret, proj, w_branch_attn[layer], w_branch_ret[layer])
        x1, h2, route, counts, meta = _mixout(merged, xf, mod6, norm2_gain[layer].reshape(1, d),
                                              w_out[layer].astype(BF16), w_rt, b_rt, seq)
        tables, zstart, zcount, item_e, item_blk, item_nb, n_blocks = _dispatch_tables(meta, counts, t)
        x_sorted, dest = _dispatch(tables, zstart, zcount, h2, n_blocks * MOE_BLOCK)
        y_sorted = _experts(item_e, item_blk, item_nb, x_sorted,
                            w_expert_gate[layer], w_expert_up[layer], w_expert_down[layer], n_blocks)
        xf = _combine(dest, x1, route, mod6, final_norm_gain.reshape(1, d), y_sorted, seq)
    return xf.reshape(batch, seq, d)
```

```python
import functools
import math

import jax
import jax.numpy as jnp
import numpy as np
from jax import lax
from jax.experimental import pallas as pl
from jax.experimental.pallas import tpu as pltpu

F32 = jnp.float32
BF16 = jnp.bfloat16

D_MODEL = 2048
ATTN_HEAD_DIM = 64
ATTN_HEADS = 32
ATTN_KV_HEADS = 4
ATTN_GROUP = 8
WINDOW = 128
RET_HEADS = 8
RET_DIM = 256
RET_CHUNK = 128
ROPE_BASE = 10000.0
N_GROUPS = 4
EXPERTS_PER_GROUP = 16
N_EXPERTS = 64
TOP_K = 2
EXPERT_DIM = 1024
MOE_BLOCK = 128
EPS = 1e-6
NEG = -1e30

MIB = 1024 * 1024
LANES = 128
PROJ_TN = 512
PROJ_TM = 1024
KV_SRC_TILE = 4
ADA_EARLY = 2 * D_MODEL
ADA_LATE_TN = 256
ADA_LATE_STEPS = 4 * D_MODEL // ADA_LATE_TN
BRANCH_TM = 1024
BRANCH_TN = 512
MIXOUT_TM = 512
ITEM_BLOCKS = 4
EXPERT_FC = 512
EXPERT_WEIGHT_SLOTS = 3
COMBINE_TM = 128
COMBINE_SLOTS = 3
SLAB_ROWS = 8
SLAB_LANES = LANES
HALF_D = D_MODEL // 2
ROWS_PER_WAIT = 128
I32 = jnp.int32

COL_QA, COL_QR, COL_KR, COL_VR, COL_GR, COL_GA, COL_GRT = 0, 1, 2, 3, 4, 5, 6

LOG_GAMMA = [math.log1p(-(2.0 ** (-5.0 - h))) for h in range(RET_HEADS)]


def _params(sem, vmem_mib):
    return pltpu.CompilerParams(dimension_semantics=sem, vmem_limit_bytes=vmem_mib * MIB)


def _pack_pair(lo, hi):
    lo_b = lax.bitcast_convert_type(lo.astype(BF16).astype(F32), I32)
    hi_b = lax.bitcast_convert_type(hi.astype(BF16).astype(F32), I32)
    return hi_b | lax.shift_right_logical(lo_b, jnp.full_like(lo_b, 16))


def _unpack_pair(w):
    lo = lax.bitcast_convert_type(w << 16, F32)
    hi = lax.bitcast_convert_type(w & jnp.int32(-65536), F32)
    return lo, hi


def _slab_rows(j, n_tokens):
    return pl.ds(j, n_tokens, stride=SLAB_ROWS)


def _ada_kernel(c_ref, w_ref, b_ref, o_ref):
    c = c_ref[...]
    a = (c * jax.nn.sigmoid(c)).astype(BF16)
    o_ref[...] = jnp.dot(a, w_ref[...].astype(BF16), preferred_element_type=F32) + b_ref[...]


def _ada(c8, w_ada, b_ada, n):
    tn = 1024
    return pl.pallas_call(
        _ada_kernel,
        grid=(n // tn,),
        in_specs=[pl.BlockSpec((8, D_MODEL), lambda j: (0, 0)),
                  pl.BlockSpec((D_MODEL, tn), lambda j: (0, j)),
                  pl.BlockSpec((1, tn), lambda j: (0, j))],
        out_specs=pl.BlockSpec((8, tn), lambda j: (0, j)),
        out_shape=jax.ShapeDtypeStruct((8, n), F32),
        compiler_params=_params(("arbitrary",), 40),
        name="ada",
    )(c8, w_ada, b_ada)


def _proj_kernel(x_ref, g_ref, mod_ref, wlo_ref, whi_ref, c_ref, wada_ref, bada_ref,
                 o_ref, kv_ref, late_ref, h0_ref, h1_ref):
    i = pl.program_id(0)
    v = pl.program_id(1)
    last = pl.num_programs(1) - 1

    def normed():
        x = x_ref[...]
        var = jnp.mean(x * x, axis=-1, keepdims=True)
        y = x * lax.rsqrt(var + EPS) * g_ref[...]
        return (y * (1.0 + mod_ref[0, 1:2, :]) + mod_ref[0, 0:1, :]).astype(BF16)

    @pl.when(pl.program_id(0) * pl.num_programs(1) + v < ADA_LATE_STEPS)
    def _():
        _ada_kernel(c_ref, wada_ref, bada_ref, late_ref)

    @pl.when((i == 0) & (v == 0))
    def _():
        h0_ref[...] = normed()

    for parity, (h_ref, h_next_ref) in enumerate(((h0_ref, h1_ref), (h1_ref, h0_ref))):
        @pl.when((v < last) & (i % 2 == parity))
        def _(h_ref=h_ref):
            w = jnp.concatenate([wlo_ref[...].astype(BF16), whi_ref[...].astype(BF16)], axis=1)
            o_ref[...] = jnp.dot(h_ref[...], w, preferred_element_type=F32).astype(BF16)

        @pl.when((v == last) & (i % 2 == parity))
        def _(h_ref=h_ref, h_next_ref=h_next_ref):
            kv_ref[...] = jnp.dot(h_ref[...], wlo_ref[...].astype(BF16),
                                  preferred_element_type=F32).astype(BF16)
            h_next_ref[...] = normed()


def _proj_w_tile(v, n_wide):
    return jnp.where(v < 2, 2 * v, jnp.where(v < n_wide, 2 * v + 1, KV_SRC_TILE))


def _proj(x2, gain, mod_early, w_in, c8, w_ada, b_ada, seq):
    t = x2.shape[0]
    n = w_in.shape[1]
    tiles_per_batch = seq // PROJ_TM
    n_wide = (n - PROJ_TN) // (2 * PROJ_TN)
    n_late = w_ada.shape[1] - ADA_EARLY
    assert n_late == ADA_LATE_STEPS * ADA_LATE_TN and ADA_LATE_STEPS <= (t // PROJ_TM) * (n_wide + 1)
    early_tiles = ADA_EARLY // ADA_LATE_TN

    def late_tile(i, v):
        return jnp.minimum(i * (n_wide + 1) + v, ADA_LATE_STEPS - 1)

    n_row_tiles = t // PROJ_TM

    def norm_tile(i, v):
        return jnp.minimum(i + (v == n_wide).astype(jnp.int32), n_row_tiles - 1)

    return pl.pallas_call(
        _proj_kernel,
        grid=(n_row_tiles, n_wide + 1),
        in_specs=[pl.BlockSpec((PROJ_TM, D_MODEL), lambda i, v: (norm_tile(i, v), 0)),
                  pl.BlockSpec((1, D_MODEL), lambda i, v: (0, 0)),
                  pl.BlockSpec((1, 2, D_MODEL), lambda i, v: (norm_tile(i, v) // tiles_per_batch, 0, 0)),
                  pl.BlockSpec((D_MODEL, PROJ_TN), lambda i, v: (0, _proj_w_tile(v, n_wide))),
                  pl.BlockSpec((D_MODEL, PROJ_TN),
                               lambda i, v: (0, jnp.where(v < n_wide, _proj_w_tile(v, n_wide) + 1, KV_SRC_TILE))),
                  pl.BlockSpec((8, D_MODEL), lambda i, v: (0, 0)),
                  pl.BlockSpec((D_MODEL, ADA_LATE_TN), lambda i, v: (0, early_tiles + late_tile(i, v))),
                  pl.BlockSpec((1, ADA_LATE_TN), lambda i, v: (0, early_tiles + late_tile(i, v)))],
        out_specs=[pl.BlockSpec((PROJ_TM, 2 * PROJ_TN), lambda i, v: (i, jnp.minimum(v, n_wide - 1))),
                   pl.BlockSpec((PROJ_TM, PROJ_TN), lambda i, v: (i, 0)),
                   pl.BlockSpec((8, ADA_LATE_TN), lambda i, v: (0, late_tile(i, v)))],
        out_shape=[jax.ShapeDtypeStruct((t, n - PROJ_TN), BF16),
                   jax.ShapeDtypeStruct((t, PROJ_TN), BF16),
                   jax.ShapeDtypeStruct((8, n_late), F32)],
        scratch_shapes=[pltpu.VMEM((PROJ_TM, D_MODEL), BF16), pltpu.VMEM((PROJ_TM, D_MODEL), BF16)],
        compiler_params=_params(("arbitrary", "arbitrary"), 60),
        name="proj",
    )(x2, gain, mod_early, w_in, w_in, c8, w_ada, b_ada)


def _attn_stages(sink_ref, q_ref, kvp_ref, kvc_ref, o_ref):
    n = pl.program_id(1)
    kvp = kvp_ref[...]
    kvc = kvc_ref[...]
    qi = lax.broadcasted_iota(jnp.int32, (WINDOW, WINDOW), 0)
    sj = lax.broadcasted_iota(jnp.int32, (WINDOW, WINDOW), 1)
    valid_prev = (sj > qi) & (n > 0)
    valid_cur = sj <= qi
    sink_col = sj == 0
    first_row = lax.broadcasted_iota(jnp.int32, (2 * WINDOW, 1), 0) == 0
    dh = ATTN_HEAD_DIM
    kv_w = ATTN_KV_HEADS * dh
    scale = jnp.asarray(dh ** -0.5, BF16)
    n_pairs = ATTN_GROUP // 2

    def group_operands(kv):
        kband = jnp.concatenate([kvp[:, kv * dh:(kv + 1) * dh],
                                 kvc[:, kv * dh:(kv + 1) * dh]], axis=0) * scale
        vband = jnp.concatenate([kvp[:, kv_w + kv * dh:kv_w + (kv + 1) * dh],
                                 kvc[:, kv_w + kv * dh:kv_w + (kv + 1) * dh]], axis=0)
        vband = jnp.where(first_row, jnp.zeros_like(vband), vband)
        zeros = jnp.zeros_like(kband)
        ones = jnp.ones_like(vband)
        k_pad = (jnp.concatenate([kband, zeros], axis=1), jnp.concatenate([zeros, kband], axis=1))
        pv_rhs = jnp.concatenate(
            [jnp.concatenate([vband, zeros, ones, zeros], axis=1),
             jnp.concatenate([zeros, vband, zeros, ones], axis=1)], axis=0)
        q_rows = jnp.concatenate(
            [q_ref[:, (kv * ATTN_GROUP + 2 * p) * dh:(kv * ATTN_GROUP + 2 * p + 2) * dh]
             for p in range(n_pairs)], axis=0)
        scores = [lax.dot_general(q_rows, k_pad[idx], (((1,), (1,)), ((), ())),
                                  preferred_element_type=F32) for idx in range(2)]
        return scores, pv_rhs

    nxt = group_operands(0)
    for kv in range(ATTN_KV_HEADS):
        scores, pv_rhs = nxt
        if kv + 1 < ATTN_KV_HEADS:
            nxt = group_operands(kv + 1)
        prob_rows = []
        for pair in range(n_pairs):
            rows = slice(pair * WINDOW, (pair + 1) * WINDOW)
            probs = []
            for idx in range(2):
                s = scores[idx][rows]
                sink = sink_ref[kv * ATTN_GROUP + 2 * pair + idx]
                s_prev = jnp.where(sink_col, sink, jnp.where(valid_prev, s[:, :WINDOW], NEG))
                s_cur = jnp.where(valid_cur, s[:, WINDOW:], NEG)
                m = jnp.max(jnp.maximum(s_prev, s_cur), axis=-1, keepdims=True)
                probs += [jnp.exp(s_prev - m).astype(BF16), jnp.exp(s_cur - m).astype(BF16)]
            prob_rows.append(jnp.concatenate(probs, axis=-1))
        r = jnp.dot(jnp.concatenate(prob_rows, axis=0), pv_rhs, preferred_element_type=F32)
        for pair in range(n_pairs):
            rows = slice(pair * WINDOW, (pair + 1) * WINDOW)
            col = (kv * ATTN_GROUP + 2 * pair) * dh
            o_ref[:, col:col + 2 * dh] = (r[rows, :2 * dh] * (1.0 / r[rows, 2 * dh:])).astype(BF16)
        yield


def _ret_decay_tables():
    lg = np.asarray(LOG_GAMMA, np.float64)[:, None, None]
    i = np.arange(RET_CHUNK, dtype=np.float64)
    diff = i[:, None] - i[None, :]
    k_scale = RET_DIM ** -0.5
    d_intra = np.where(diff >= 0, np.exp(np.maximum(diff, 0.0) * lg), 0.0) * k_scale
    lanes = np.ones((1, 1, RET_DIM // 2))
    d_q = np.exp((i[None, :, None] + 1.0) * lg) * lanes
    d_k = np.exp((RET_CHUNK - 1.0 - i[None, :, None]) * lg) * k_scale * lanes
    return jnp.asarray(d_intra, F32), jnp.asarray(d_q, F32), jnp.asarray(d_k, BF16)


def _ret_stages(pos_ref, invf_ref, di_ref, dq_ref, dk_ref, q_ref, k_ref, v_ref, g_ref, gain_ref,
                o_ref, state_ref):
    half = RET_DIM // 2
    ang = pos_ref[...].astype(F32) * invf_ref[...]
    cos = jnp.cos(ang).astype(BF16)
    sin = jnp.sin(ang).astype(BF16)

    def rot(t):
        t1, t2 = t[:, :half], t[:, half:]
        return jnp.concatenate([t1 * cos - t2 * sin, t1 * sin + t2 * cos], axis=-1)

    def both_halves(t, factor):
        return jnp.concatenate([t[:, :half] * factor, t[:, half:] * factor], axis=-1)

    def head_front(h):
        sl = slice(h * RET_DIM, (h + 1) * RET_DIM)
        qb = rot(q_ref[:, sl])
        kb = rot(k_ref[:, sl])
        intra = lax.dot_general(qb, kb, (((1,), (1,)), ((), ())),
                                preferred_element_type=F32) * di_ref[h]
        st = state_ref[h]
        cross = jnp.dot(qb, st.astype(BF16), preferred_element_type=F32)
        return kb, intra, st, cross

    front = head_front(0)
    for h in range(RET_HEADS):
        sl = slice(h * RET_DIM, (h + 1) * RET_DIM)
        kb, intra, st, cross = front
        if h + 1 < RET_HEADS:
            front = head_front(h + 1)
        vb = v_ref[:, sl]
        d_chunk = math.exp(RET_CHUNK * LOG_GAMMA[h])
        o = jnp.dot(intra.astype(BF16), vb, preferred_element_type=F32) + both_halves(cross, dq_ref[h])
        kd = both_halves(kb, dk_ref[h])
        state_ref[h] = st * d_chunk + lax.dot_general(kd, vb, (((0,), (0,)), ((), ())),
                                                      preferred_element_type=F32)
        o = o * lax.rsqrt(jnp.mean(o * o, axis=-1, keepdims=True) + EPS) * gain_ref[:, sl]
        gg = g_ref[:, sl].astype(F32)
        o_ref[:, sl] = (gg * jax.nn.sigmoid(gg) * o).astype(BF16)
        yield


def _mixers_kernel(sink_ref, qa_ref, kvp_ref, kvc_ref, pos_ref, invf_ref, di_ref, dq_ref, dk_ref,
                   qr_ref, kr_ref, vr_ref, gr_ref, gain_ref, attn_ref, ret_ref, state_ref):
    @pl.when(pl.program_id(1) == 0)
    def _():
        state_ref[...] = jnp.zeros_like(state_ref)

    attn = _attn_stages(sink_ref, qa_ref, kvp_ref, kvc_ref, attn_ref)
    ret = _ret_stages(pos_ref, invf_ref, di_ref, dq_ref, dk_ref, qr_ref, kr_ref, vr_ref, gr_ref,
                      gain_ref, ret_ref, state_ref)
    heads_per_group = RET_HEADS // ATTN_KV_HEADS
    for _ in range(ATTN_KV_HEADS):
        next(attn)
        for _ in range(heads_per_group):
            next(ret)


def _mixers(proj, proj_kv, sinks, pos, inv_freq, ret_gain, batch, seq):
    nb = seq // WINDOW
    t = batch * seq

    def col(cb):
        return lambda b, n: (b * nb + n, cb)

    d_intra, d_q, d_k = _ret_decay_tables()
    table = lambda lanes: pl.BlockSpec((RET_HEADS, RET_CHUNK, lanes), lambda b, n: (0, 0, 0))
    rows = pl.BlockSpec((WINDOW, D_MODEL), col(0))
    return pl.pallas_call(
        _mixers_kernel,
        grid=(batch, nb),
        in_specs=[pl.BlockSpec(memory_space=pltpu.SMEM),
                  pl.BlockSpec((WINDOW, D_MODEL), col(COL_QA)),
                  pl.BlockSpec((WINDOW, PROJ_TN), lambda b, n: (b * nb + jnp.maximum(n - 1, 0), 0)),
                  pl.BlockSpec((WINDOW, PROJ_TN), col(0)),
                  pl.BlockSpec((RET_CHUNK, 1), col(0)),
                  pl.BlockSpec((1, RET_DIM // 2), lambda b, n: (0, 0)),
                  table(RET_CHUNK), table(RET_DIM // 2), table(RET_DIM // 2),
                  pl.BlockSpec((RET_CHUNK, D_MODEL), col(COL_QR)),
                  pl.BlockSpec((RET_CHUNK, D_MODEL), col(COL_KR)),
                  pl.BlockSpec((RET_CHUNK, D_MODEL), col(COL_VR)),
                  pl.BlockSpec((RET_CHUNK, D_MODEL), col(COL_GR)),
                  pl.BlockSpec((1, D_MODEL), lambda b, n: (0, 0))],
        out_specs=[rows, rows],
        out_shape=[jax.ShapeDtypeStruct((t, D_MODEL), BF16), jax.ShapeDtypeStruct((t, D_MODEL), BF16)],
        scratch_shapes=[pltpu.VMEM((RET_HEADS, RET_DIM, RET_DIM), F32)],
        compiler_params=_params(("arbitrary", "arbitrary"), 40),
        name="mixers",
    )(sinks, proj, proj_kv, proj_kv, pos, inv_freq, d_intra, d_q, d_k, proj, proj, proj, proj, ret_gain)


def _route(logits):
    lane = lax.broadcasted_iota(jnp.int32, logits.shape, 1)
    lane_f = lane.astype(F32)
    is_g = lane < N_GROUPS
    gl = jnp.where(is_g, logits, NEG)
    gmax = jnp.max(gl, axis=-1, keepdims=True)
    gsel = jnp.min(jnp.where(gl == gmax, lane_f, float(LANES)), axis=-1, keepdims=True)
    gsum = jnp.sum(jnp.where(is_g, jnp.exp(gl - gmax), 0.0), axis=-1, keepdims=True)
    g_w = 1.0 / gsum
    grp = ((lane - N_GROUPS) >> 4).astype(F32)
    is_e = (lane >= N_GROUPS) & (lane < N_GROUPS + N_EXPERTS) & (grp == gsel)
    el = jnp.where(is_e, logits, NEG)
    v1 = jnp.max(el, axis=-1, keepdims=True)
    i1 = jnp.min(jnp.where(el == v1, lane_f, float(LANES)), axis=-1, keepdims=True)
    el2 = jnp.where(lane_f == i1, NEG, el)
    v2 = jnp.max(el2, axis=-1, keepdims=True)
    i2 = jnp.min(jnp.where(el2 == v2, lane_f, float(LANES)), axis=-1, keepdims=True)
    tt = jnp.exp(v2 - v1)
    w1 = g_w / (1.0 + tt)
    w2 = g_w * tt / (1.0 + tt)
    return jnp.where(lane == 0, i1 - N_GROUPS,
                     jnp.where(lane == 1, i2 - N_GROUPS,
                               jnp.where(lane == 2, w1, jnp.where(lane == 3, w2, 0.0))))


def _branch_kernel(attn_ref, ret_ref, ga_ref, gr_ref, wa_ref, wr_ref, o_ref):
    a = jnp.dot(attn_ref[...], wa_ref[...].astype(BF16), preferred_element_type=F32)
    r = jnp.dot(ret_ref[...], wr_ref[...].astype(BF16), preferred_element_type=F32)
    o_ref[...] = (jax.nn.sigmoid(ga_ref[...].astype(F32)) * a
                  + jax.nn.sigmoid(gr_ref[...].astype(F32)) * r).astype(BF16)


def _branch(attn, ret, proj, wa, wr):
    t = attn.shape[0]
    tm, tn = BRANCH_TM, BRANCH_TN
    per_slab = D_MODEL // tn
    return pl.pallas_call(
        _branch_kernel,
        grid=(t // tm, D_MODEL // tn),
        in_specs=[pl.BlockSpec((tm, D_MODEL), lambda i, j: (i, 0)),
                  pl.BlockSpec((tm, D_MODEL), lambda i, j: (i, 0)),
                  pl.BlockSpec((tm, tn), lambda i, j: (i, COL_GA * per_slab + j)),
                  pl.BlockSpec((tm, tn), lambda i, j: (i, COL_GRT * per_slab + j)),
                  pl.BlockSpec((D_MODEL, tn), lambda i, j: (0, j)),
                  pl.BlockSpec((D_MODEL, tn), lambda i, j: (0, j))],
        out_specs=pl.BlockSpec((tm, tn), lambda i, j: (i, j)),
        out_shape=jax.ShapeDtypeStruct((t, D_MODEL), BF16),
        compiler_params=_params(("arbitrary", "arbitrary"), 48),
        name="branch",
    )(attn, ret, proj, proj, wa, wr)


def _mixout_kernel(m_ref, x_ref, mod_ref, g2_ref, wo_ref, wrt_ref, brt_ref,
                   x1_ref, h2_ref, route_ref, count_ref, meta_ref, carry_ref):
    @pl.when(pl.program_id(0) == 0)
    def _():
        carry_ref[...] = jnp.zeros_like(carry_ref)

    mix = jnp.dot(m_ref[...], wo_ref[...], preferred_element_type=F32)
    x1 = x_ref[...] + mod_ref[0, 2:3, :] * mix
    x1_ref[...] = x1
    var = jnp.mean(x1 * x1, axis=-1, keepdims=True)
    h2 = x1 * lax.rsqrt(var + EPS) * g2_ref[...]
    h2 = h2 * (1.0 + mod_ref[0, 4:5, :]) + mod_ref[0, 3:4, :]
    tm = h2.shape[0]
    for j in range(SLAB_ROWS):
        lo = h2[:, j * LANES:(j + 1) * LANES]
        hi = h2[:, HALF_D + j * LANES:HALF_D + (j + 1) * LANES]
        h2_ref[_slab_rows(j, tm), :] = _pack_pair(lo, hi)
    h_hi = h2.astype(BF16)
    h_lo = (h2 - h_hi.astype(F32)).astype(BF16)
    w_rt = wrt_ref[...]
    w_hi = w_rt.astype(BF16)
    w_lo = (w_rt - w_hi.astype(F32)).astype(BF16)
    hi_both = jnp.dot(h_hi, jnp.concatenate([w_hi, w_lo], axis=1), preferred_element_type=F32)
    logits = (hi_both[:, :LANES] + hi_both[:, LANES:]
              + jnp.dot(h_lo, w_hi, preferred_element_type=F32) + brt_ref[...])
    route = _route(logits)

    lane = lax.broadcasted_iota(jnp.int32, route.shape, 1)
    lane_f = lane.astype(F32)
    hot1 = lane_f == route[:, 0:1]
    hot2 = lane_f == route[:, 1:2]
    both = jnp.where(hot1 | hot2, 1.0, 0.0)
    ii = lax.broadcasted_iota(jnp.int32, (tm, tm), 0)
    jj = lax.broadcasted_iota(jnp.int32, (tm, tm), 1)
    lower = jnp.where(ii > jj, 1.0, 0.0).astype(BF16)
    before = jnp.dot(lower, both.astype(BF16), preferred_element_type=F32) + carry_ref[...]
    r1 = jnp.sum(jnp.where(hot1, before, 0.0), axis=-1, keepdims=True)
    r2 = jnp.sum(jnp.where(hot2, before, 0.0), axis=-1, keepdims=True)
    route = jnp.where(lane == 4, r1, jnp.where(lane == 5, r2, route))
    route_ref[...] = route
    meta_ref[...] = route.T[0:SLAB_ROWS, :].astype(I32)
    carry = carry_ref[...] + jnp.sum(both, axis=0, keepdims=True)
    carry_ref[...] = carry
    count_ref[...] = carry


def _mixout(merged, x2, mod6, gain2, wo, w_rt, b_rt, seq):
    t = x2.shape[0]
    tm = MIXOUT_TM
    tiles_per_batch = seq // tm
    row = lambda i: (i, 0)
    const = lambda i: (0, 0)
    return pl.pallas_call(
        _mixout_kernel,
        grid=(t // tm,),
        in_specs=[pl.BlockSpec((tm, D_MODEL), row),
                  pl.BlockSpec((tm, D_MODEL), row),
                  pl.BlockSpec((1, 6, D_MODEL), lambda i: (i // tiles_per_batch, 0, 0)),
                  pl.BlockSpec((1, D_MODEL), const),
                  pl.BlockSpec((D_MODEL, D_MODEL), const, pipeline_mode=pl.Buffered(1)),
                  pl.BlockSpec((D_MODEL, LANES), const),
                  pl.BlockSpec((1, LANES), const)],
        out_specs=[pl.BlockSpec((tm, D_MODEL), row),
                   pl.BlockSpec((tm * SLAB_ROWS, SLAB_LANES), row),
                   pl.BlockSpec((tm, LANES), row),
                   pl.BlockSpec((1, LANES), const),
                   pl.BlockSpec((SLAB_ROWS, tm), lambda i: (0, i))],
        out_shape=[jax.ShapeDtypeStruct((t, D_MODEL), F32),
                   jax.ShapeDtypeStruct((t * SLAB_ROWS, SLAB_LANES), I32),
                   jax.ShapeDtypeStruct((t, LANES), F32),
                   jax.ShapeDtypeStruct((1, LANES), F32),
                   jax.ShapeDtypeStruct((SLAB_ROWS, t), I32)],
        scratch_shapes=[pltpu.VMEM((1, LANES), F32)],
        compiler_params=_params(("arbitrary",), 56),
        name="mixout",
    )(merged, x2, mod6, gain2, wo, w_rt, b_rt)


PAD_BITS = (64, 32, 16, 8, 4, 2, 1)


def _sorted_row(tables, tok, k):
    e_refs, r_refs, blk_row_ref = tables[0:TOP_K], tables[TOP_K:2 * TOP_K], tables[2 * TOP_K]
    return blk_row_ref[e_refs[k][tok]] + r_refs[k][tok]


N_ROUTE_TABLES = 2 * TOP_K + 1


def _dispatch_kernel(*refs):
    tables = refs[:N_ROUTE_TABLES]
    zstart_ref, zcount_ref, h2_ref, xs_ref, dest_ref, zero_ref, sem, zsem = refs[N_ROUTE_TABLES:]
    n_assign = tables[0].shape[0] * TOP_K
    zero_ref[...] = jnp.zeros_like(zero_ref)

    def zero_copy(start, rows):
        return pltpu.make_async_copy(zero_ref.at[pl.ds(0, rows * SLAB_ROWS), :],
                                     xs_ref.at[pl.ds(start * SLAB_ROWS, rows * SLAB_ROWS), :], zsem)

    def fill(e, wait):
        start = zstart_ref[e]
        pad = zcount_ref[e]
        for bit in PAD_BITS:
            @pl.when((pad & bit) != 0)
            def _(start=start, bit=bit):
                cp = zero_copy(start, bit)
                cp.wait() if wait else cp.start()
            start = start + (pad & bit)

    lax.fori_loop(0, N_EXPERTS, lambda e, c: (fill(e, False), c)[1], 0)

    def issue(tok, carry):
        src = h2_ref.at[pl.ds(pl.multiple_of(tok * SLAB_ROWS, SLAB_ROWS), SLAB_ROWS), :]
        for k in range(TOP_K):
            row = _sorted_row(tables, tok, k)
            dest_ref[tok * TOP_K + k] = row
            dst = pl.multiple_of(row * SLAB_ROWS, SLAB_ROWS)
            pltpu.make_async_copy(src, xs_ref.at[pl.ds(dst, SLAB_ROWS), :], sem).start()
        return carry

    lax.fori_loop(0, n_assign // TOP_K, issue, 0, unroll=8)

    def drain(i, carry):
        pltpu.make_async_copy(h2_ref.at[pl.ds(0, ROWS_PER_WAIT * SLAB_ROWS), :],
                              xs_ref.at[pl.ds(0, ROWS_PER_WAIT * SLAB_ROWS), :], sem).wait()
        return carry

    lax.fori_loop(0, n_assign // ROWS_PER_WAIT, drain, 0)
    lax.fori_loop(0, N_EXPERTS, lambda e, c: (fill(e, True), c)[1], 0)


def _dispatch(tables, zstart, zcount, h2_slab, n_pad):
    n_assign = tables[0].shape[0] * TOP_K
    return pl.pallas_call(
        _dispatch_kernel,
        grid_spec=pltpu.PrefetchScalarGridSpec(
            num_scalar_prefetch=N_ROUTE_TABLES + 2,
            grid=(1,),
            in_specs=[pl.BlockSpec(memory_space=pltpu.HBM)],
            out_specs=[pl.BlockSpec(memory_space=pl.ANY),
                       pl.BlockSpec(memory_space=pltpu.SMEM)],
            scratch_shapes=[pltpu.VMEM((PAD_BITS[0] * SLAB_ROWS, SLAB_LANES), I32),
                            pltpu.SemaphoreType.DMA(()),
                            pltpu.SemaphoreType.DMA(())]),
        out_shape=[jax.ShapeDtypeStruct((n_pad * SLAB_ROWS, SLAB_LANES), I32),
                   jax.ShapeDtypeStruct((n_assign,), I32)],
        compiler_params=_params(("arbitrary",), 16),
        name="dispatch",
    )(*tables, zstart, zcount, h2_slab)


def _experts_kernel(item_e_ref, item_blk_ref, item_nb_ref, x_blk_ref,
                    x0_ref, x1_ref, x2_ref, x3_ref, wg_hbm, wu_hbm, wd_hbm, y_ref,
                    xb_ref, acc_ref, yp_ref, wg_ref, wu_ref, wd_ref, sem, wsem):
    w = pl.program_id(0)
    c = pl.program_id(1)
    n_items = pl.num_programs(0)
    n_chunks = pl.num_programs(1)
    last_c = n_chunks - 1
    nb = item_nb_ref[w]
    blk0 = item_blk_ref[w]
    x_refs = (x0_ref, x1_ref, x2_ref, x3_ref)

    blk_rows = MOE_BLOCK * SLAB_ROWS

    step = w * n_chunks + c
    fc = wg_ref.shape[2]

    def weight_copies(item, chunk, slot):
        e = item_e_ref[item]
        col = pl.multiple_of(chunk * fc, fc)
        return (pltpu.make_async_copy(wg_hbm.at[e, :, pl.ds(col, fc)], wg_ref.at[slot], wsem.at[slot]),
                pltpu.make_async_copy(wu_hbm.at[e, :, pl.ds(col, fc)], wu_ref.at[slot], wsem.at[slot]),
                pltpu.make_async_copy(wd_hbm.at[e, pl.ds(col, fc), :], wd_ref.at[slot], wsem.at[slot]))

    def start_weights(ahead):
        st = step + ahead
        item = jnp.minimum(st // n_chunks, n_items - 1)

        @pl.when((st < n_items * n_chunks) & (item_nb_ref[item] > 0))
        def _():
            for cp in weight_copies(item, st % n_chunks, st % EXPERT_WEIGHT_SLOTS):
                cp.start()

    @pl.when(step == 0)
    def _():
        for ahead in range(EXPERT_WEIGHT_SLOTS - 1):
            start_weights(ahead)

    start_weights(EXPERT_WEIGHT_SLOTS - 1)
    w_slot = step % EXPERT_WEIGHT_SLOTS

    @pl.when(nb > 0)
    def _():
        for cp in weight_copies(w, c, w_slot):
            cp.wait()

    def out_copy(s):
        return pltpu.make_async_copy(yp_ref.at[s], y_ref.at[pl.ds((blk0 + s) * blk_rows, blk_rows), :],
                                     sem.at[s])

    def wait_out(count):
        for s in range(ITEM_BLOCKS):
            @pl.when(s < count)
            def _(s=s):
                out_copy(s).wait()

    @pl.when((c == last_c) & (w > 0))
    def _():
        wait_out(item_nb_ref[jnp.maximum(w - 1, 0)])

    def run(n_live):
        rows = n_live * MOE_BLOCK

        @pl.when(c == 0)
        def _():
            for s in range(n_live):
                for j in range(SLAB_ROWS):
                    lo, hi = _unpack_pair(x_refs[s][_slab_rows(j, MOE_BLOCK), :])
                    r0 = s * MOE_BLOCK
                    xb_ref[r0:r0 + MOE_BLOCK, j * LANES:(j + 1) * LANES] = lo.astype(BF16)
                    xb_ref[r0:r0 + MOE_BLOCK, HALF_D + j * LANES:HALF_D + (j + 1) * LANES] = hi.astype(BF16)

        xs = xb_ref[0:rows, :]
        g = jnp.dot(xs, wg_ref[w_slot].astype(BF16), preferred_element_type=F32)
        u = jnp.dot(xs, wu_ref[w_slot].astype(BF16), preferred_element_type=F32)
        hid = (g * jax.nn.sigmoid(g) * u).astype(BF16)
        y = jnp.dot(hid, wd_ref[w_slot].astype(BF16), preferred_element_type=F32)

        @pl.when(c == 0)
        def _():
            acc_ref[0:rows, :] = y

        @pl.when((c > 0) & (c < last_c))
        def _():
            acc_ref[0:rows, :] += y

        @pl.when(c == last_c)
        def _():
            for s in range(n_live):
                r0 = s * MOE_BLOCK
                for j in range(SLAB_ROWS):
                    lo_cols = slice(j * LANES, (j + 1) * LANES)
                    hi_cols = slice(HALF_D + j * LANES, HALF_D + (j + 1) * LANES)
                    yp_ref[s, _slab_rows(j, MOE_BLOCK), :] = _pack_pair(
                        acc_ref[r0:r0 + MOE_BLOCK, lo_cols] + y[r0:r0 + MOE_BLOCK, lo_cols],
                        acc_ref[r0:r0 + MOE_BLOCK, hi_cols] + y[r0:r0 + MOE_BLOCK, hi_cols])
                out_copy(s).start()

    for n_live in range(1, ITEM_BLOCKS + 1):
        pl.when(nb == n_live)(functools.partial(run, n_live))

    @pl.when((c == last_c) & (w == pl.num_programs(0) - 1))
    def _():
        wait_out(nb)


def _experts(item_e, item_blk, item_nb, x_sorted, w_gate, w_up, w_down, n_blocks):
    n_items = item_e.shape[0]
    d = D_MODEL
    fc = EXPERT_FC
    x3 = x_sorted
    blk_rows = MOE_BLOCK * SLAB_ROWS
    n_chunks = EXPERT_DIM // fc
    assert n_chunks >= 2, "the last hidden chunk adds onto the accumulator of the earlier ones"

    slot = jnp.arange(ITEM_BLOCKS, dtype=jnp.int32)[:, None]
    x_blk = jnp.maximum(lax.cummax(jnp.where(slot < item_nb[None, :], item_blk[None, :] + slot, -1), axis=1), 0)
    x_blk = x_blk.reshape(-1).astype(jnp.int32)

    def x_spec(s):
        return pl.BlockSpec((blk_rows, SLAB_LANES),
                            lambda w, c, ie, ib, inb, xb: (xb[s * n_items + w], 0))

    hbm = pl.BlockSpec(memory_space=pl.ANY)
    slots = EXPERT_WEIGHT_SLOTS
    return pl.pallas_call(
        _experts_kernel,
        grid_spec=pltpu.PrefetchScalarGridSpec(
            num_scalar_prefetch=4,
            grid=(n_items, n_chunks),
            in_specs=[x_spec(0), x_spec(1), x_spec(2), x_spec(3), hbm, hbm, hbm],
            out_specs=pl.BlockSpec(memory_space=pl.ANY),
            scratch_shapes=[pltpu.VMEM((ITEM_BLOCKS * MOE_BLOCK, d), BF16),
                            pltpu.VMEM((ITEM_BLOCKS * MOE_BLOCK, d), F32),
                            pltpu.VMEM((ITEM_BLOCKS, blk_rows, SLAB_LANES), I32),
                            pltpu.VMEM((slots, d, fc), F32),
                            pltpu.VMEM((slots, d, fc), F32),
                            pltpu.VMEM((slots, fc, d), F32),
                            pltpu.SemaphoreType.DMA((ITEM_BLOCKS,)),
                            pltpu.SemaphoreType.DMA((slots,))]),
        out_shape=jax.ShapeDtypeStruct((n_blocks * blk_rows, SLAB_LANES), I32),
        compiler_params=_params(("arbitrary", "arbitrary"), 56),
        name="experts",
    )(item_e, item_blk, item_nb, x_blk, x3, x3, x3, x3, w_gate, w_up, w_down)


def _combine_kernel(dest_ref, x1_ref, route_ref, mod_ref, gain_ref, y_ref, o_ref, ybuf_ref, sem):
    tm = COMBINE_TM
    i = pl.program_id(0)
    n_tiles = pl.num_programs(0)

    def row_copy(slot, k, r, src):
        return pltpu.make_async_copy(
            y_ref.at[pl.ds(pl.multiple_of(src * SLAB_ROWS, SLAB_ROWS), SLAB_ROWS), :],
            ybuf_ref.at[slot, k, pl.ds(pl.multiple_of(r * SLAB_ROWS, SLAB_ROWS), SLAB_ROWS), :],
            sem.at[slot])

    def wait_slot(slot):
        for k in range(TOP_K):
            pltpu.make_async_copy(y_ref.at[pl.ds(0, tm * SLAB_ROWS), :], ybuf_ref.at[slot, k],
                                  sem.at[slot]).wait()

    def issue_tile(tile, slot):
        base = tile * tm * TOP_K

        def issue(r, carry):
            for k in range(TOP_K):
                row_copy(slot, k, r, dest_ref[base + r * TOP_K + k]).start()
            return carry

        lax.fori_loop(0, tm, issue, 0, unroll=4)

    @pl.when(i == 0)
    def _():
        for tile in range(COMBINE_SLOTS - 1):
            issue_tile(tile, tile)

    slot = i % COMBINE_SLOTS
    wait_slot(slot)

    ahead = i + COMBINE_SLOTS - 1
    ahead_slot = ahead % COMBINE_SLOTS
    ahead_base = jnp.minimum(ahead, n_tiles - 1) * (tm * TOP_K)
    n_stages = 2 * SLAB_ROWS
    rows_per_stage = tm // n_stages

    route = route_ref[...]
    w0 = route[:, 2:3]
    w1 = route[:, 3:4]
    ssq = jnp.zeros((tm, 1), F32)
    stage = 0
    for j in range(SLAB_ROWS):
        rows = _slab_rows(j, tm)
        y0 = _unpack_pair(ybuf_ref[slot, 0, rows, :])
        y1 = _unpack_pair(ybuf_ref[slot, 1, rows, :])
        for part, off in ((0, j * LANES), (1, HALF_D + j * LANES)):
            cols = slice(off, off + LANES)
            ffn = w0 * y0[part] + w1 * y1[part]
            x2 = x1_ref[:, cols] + mod_ref[0, 5:6, cols] * ffn
            o_ref[:, cols] = x2
            ssq = ssq + jnp.sum(x2 * x2, axis=-1, keepdims=True)
            for r in range(stage * rows_per_stage, (stage + 1) * rows_per_stage):
                for k in range(TOP_K):
                    row_copy(ahead_slot, k, r, dest_ref[ahead_base + r * TOP_K + k]).start()
            stage += 1
    o_ref[...] = o_ref[...] * lax.rsqrt(ssq * (1.0 / D_MODEL) + EPS) * gain_ref[...]

    @pl.when(i == n_tiles - 1)
    def _():
        for extra in range(1, COMBINE_SLOTS):
            wait_slot((i + extra) % COMBINE_SLOTS)


def _combine(dest, x1, route, mod6, gain, y_sorted, seq):
    t, d = x1.shape
    tm = COMBINE_TM
    tiles_per_batch = seq // tm
    return pl.pallas_call(
        _combine_kernel,
        grid_spec=pltpu.PrefetchScalarGridSpec(
            num_scalar_prefetch=1,
            grid=(t // tm,),
            in_specs=[pl.BlockSpec((tm, d), lambda i, *_: (i, 0)),
                      pl.BlockSpec((tm, LANES), lambda i, *_: (i, 0)),
                      pl.BlockSpec((1, 6, d), lambda i, *_: (i // tiles_per_batch, 0, 0)),
                      pl.BlockSpec((1, d), lambda i, *_: (0, 0)),
                      pl.BlockSpec(memory_space=pl.ANY)],
            out_specs=pl.BlockSpec((tm, d), lambda i, *_: (i, 0)),
            scratch_shapes=[pltpu.VMEM((COMBINE_SLOTS, TOP_K, tm * SLAB_ROWS, SLAB_LANES), I32),
                            pltpu.SemaphoreType.DMA((COMBINE_SLOTS,))]),
        out_shape=jax.ShapeDtypeStruct((t, d), F32),
        compiler_params=_params(("arbitrary",), 24),
        name="combine",
    )(dest, x1, route, mod6, gain, y_sorted)


def _dispatch_tables(meta, counts, t):
    n_assign = t * TOP_K
    n_pad = -(-(n_assign + N_EXPERTS * (MOE_BLOCK - 1)) // MOE_BLOCK) * MOE_BLOCK
    n_blocks = n_pad // MOE_BLOCK
    n_items = N_EXPERTS + n_assign // (ITEM_BLOCKS * MOE_BLOCK)

    cnt = counts[0, :N_EXPERTS].astype(jnp.int32)
    blocks_e = (cnt + MOE_BLOCK - 1) // MOE_BLOCK
    blk_end = jnp.cumsum(blocks_e)
    blk_start = blk_end - blocks_e
    tables = (meta[0], meta[1], meta[4], meta[5], (blk_start * MOE_BLOCK).astype(jnp.int32))
    zstart = (blk_start * MOE_BLOCK + cnt).astype(jnp.int32)
    zcount = (blocks_e * MOE_BLOCK - cnt).astype(jnp.int32)

    items_e = (blocks_e + ITEM_BLOCKS - 1) // ITEM_BLOCKS
    item_end = jnp.cumsum(items_e)
    item_start = item_end - items_e
    w = jnp.arange(n_items, dtype=jnp.int32)
    live = w < item_end[-1]
    w_live = jnp.minimum(w, item_end[-1] - 1)
    e_w = jnp.minimum(jnp.sum((item_end[None, :] <= w_live[:, None]).astype(jnp.int32), axis=1), N_EXPERTS - 1)
    j_w = w_live - item_start[e_w]
    item_blk = (blk_start[e_w] + ITEM_BLOCKS * j_w).astype(jnp.int32)
    item_nb = jnp.where(live, jnp.clip(blocks_e[e_w] - ITEM_BLOCKS * j_w, 0, ITEM_BLOCKS), 0).astype(jnp.int32)
    return tables, zstart, zcount, e_w, item_blk, item_nb, n_blocks


def kernel(x, c, positions, norm1_gain, norm2_gain, final_norm_gain, w_ada, b_ada, w_in, attn_sinks,
           ret_norm_gain, w_branch_attn, w_branch_ret, w_out, w_router_group, b_router_group,
           w_router_expert, b_router_expert, w_expert_gate, w_expert_up, w_expert_down):
    batch, seq, d = x.shape
    t = batch * seq
    depth = w_ada.shape[0]
    half = RET_DIM // 2
    inv_freq = (ROPE_BASE ** (-jnp.arange(half, dtype=F32) / half)).reshape(1, half)
    pos = positions.reshape(t, 1)
    c8 = jnp.pad(c, ((0, 8 - batch), (0, 0)))
    xf = x.reshape(t, d)

    assert depth == 1, "the fused final norm assumes a single layer"
    for layer in range(depth):
        b_ada2 = b_ada[layer].reshape(1, -1)
        mod_early = _ada(c8, w_ada[layer], b_ada2, ADA_EARLY)[:batch]
        proj, proj_kv, mod_late = _proj(xf, norm1_gain[layer].reshape(1, d), mod_early.reshape(batch, 2, d),
                                        w_in[layer], c8, w_ada[layer], b_ada2, seq)
        mod6 = jnp.concatenate([mod_early, mod_late[:batch]], axis=1).reshape(batch, 6, d)
        attn, ret = _mixers(proj, proj_kv, attn_sinks[layer], pos, inv_freq,
                            ret_norm_gain[layer].reshape(1, d), batch, seq)

        pad = LANES - N_GROUPS - N_EXPERTS
        w_rt = jnp.concatenate([w_router_group[layer], w_router_expert[layer],
                                jnp.zeros((d, pad), F32)], axis=1)
        b_rt = jnp.concatenate([b_router_group[layer], b_router_expert[layer],
                                jnp.zeros((pad,), F32)]).reshape(1, LANES)
        merged = _branch(attn, ret, proj, w_branch_attn[layer], w_branch_ret[layer])
        x1, h2, route, counts, meta = _mixout(merged, xf, mod6, norm2_gain[layer].reshape(1, d),
                                              w_out[layer].astype(BF16), w_rt, b_rt, seq)
        tables, zstart, zcount, item_e, item_blk, item_nb, n_blocks = _dispatch_tables(meta, counts, t)
        x_sorted, dest = _dispatch(tables, zstart, zcount, h2, n_blocks * MOE_BLOCK)
        y_sorted = _experts(item_e, item_blk, item_nb, x_sorted,
                            w_expert_gate[layer], w_expert_up[layer], w_expert_down[layer], n_blocks)
        xf = _combine(dest, x1, route, mod6, final_norm_gain.reshape(1, d), y_sorted, seq)
    return xf.reshape(batch, seq, d)
```

```python
import functools
import math

import jax
import jax.numpy as jnp
import numpy as np
from jax import lax
from jax.experimental import pallas as pl
from jax.experimental.pallas import tpu as pltpu

F32 = jnp.float32
BF16 = jnp.bfloat16

D_MODEL = 2048
ATTN_HEAD_DIM = 64
ATTN_HEADS = 32
ATTN_KV_HEADS = 4
ATTN_GROUP = 8
WINDOW = 128
RET_HEADS = 8
RET_DIM = 256
RET_CHUNK = 128
ROPE_BASE = 10000.0
N_GROUPS = 4
EXPERTS_PER_GROUP = 16
N_EXPERTS = 64
TOP_K = 2
EXPERT_DIM = 1024
MOE_BLOCK = 128
EPS = 1e-6
NEG = -1e30

MIB = 1024 * 1024
LANES = 128
PROJ_TN = 512
PROJ_TM = 1024
KV_SRC_TILE = 4
ADA_EARLY = 2 * D_MODEL
ADA_LATE_TN = 256
ADA_LATE_STEPS = 4 * D_MODEL // ADA_LATE_TN
BRANCH_TM = 1024
BRANCH_TN = 512
MIXOUT_TM = 512
ITEM_BLOCKS = 4
EXPERT_FC = 512
EXPERT_WEIGHT_SLOTS = 3
COMBINE_TM = 256
COMBINE_SLOTS = 3
SLAB_ROWS = 8
SLAB_LANES = LANES
HALF_D = D_MODEL // 2
ROWS_PER_WAIT = 128
I32 = jnp.int32

COL_QA, COL_QR, COL_KR, COL_VR, COL_GR, COL_GA, COL_GRT = 0, 1, 2, 3, 4, 5, 6

LOG_GAMMA = [math.log1p(-(2.0 ** (-5.0 - h))) for h in range(RET_HEADS)]


def _params(sem, vmem_mib):
    return pltpu.CompilerParams(dimension_semantics=sem, vmem_limit_bytes=vmem_mib * MIB)


def _pack_pair(lo, hi):
    lo_b = lax.bitcast_convert_type(lo.astype(BF16).astype(F32), I32)
    hi_b = lax.bitcast_convert_type(hi.astype(BF16).astype(F32), I32)
    return hi_b | lax.shift_right_logical(lo_b, jnp.full_like(lo_b, 16))


def _unpack_pair(w):
    lo = lax.bitcast_convert_type(w << 16, F32)
    hi = lax.bitcast_convert_type(w & jnp.int32(-65536), F32)
    return lo, hi


def _slab_rows(j, n_tokens):
    return pl.ds(j, n_tokens, stride=SLAB_ROWS)


def _ada_kernel(c_ref, w_ref, b_ref, o_ref):
    c = c_ref[...]
    a = (c * jax.nn.sigmoid(c)).astype(BF16)
    o_ref[...] = jnp.dot(a, w_ref[...].astype(BF16), preferred_element_type=F32) + b_ref[...]


def _ada(c8, w_ada, b_ada, n):
    tn = 1024
    return pl.pallas_call(
        _ada_kernel,
        grid=(n // tn,),
        in_specs=[pl.BlockSpec((8, D_MODEL), lambda j: (0, 0)),
                  pl.BlockSpec((D_MODEL, tn), lambda j: (0, j)),
                  pl.BlockSpec((1, tn), lambda j: (0, j))],
        out_specs=pl.BlockSpec((8, tn), lambda j: (0, j)),
        out_shape=jax.ShapeDtypeStruct((8, n), F32),
        compiler_params=_params(("arbitrary",), 40),
        name="ada",
    )(c8, w_ada, b_ada)


def _proj_kernel(x_ref, g_ref, mod_ref, wlo_ref, whi_ref, c_ref, wada_ref, bada_ref,
                 o_ref, kv_ref, late_ref, h0_ref, h1_ref):
    i = pl.program_id(0)
    v = pl.program_id(1)
    last = pl.num_programs(1) - 1

    def normed():
        x = x_ref[...]
        var = jnp.mean(x * x, axis=-1, keepdims=True)
        y = x * lax.rsqrt(var + EPS) * g_ref[...]
        return (y * (1.0 + mod_ref[0, 1:2, :]) + mod_ref[0, 0:1, :]).astype(BF16)

    @pl.when(pl.program_id(0) * pl.num_programs(1) + v < ADA_LATE_STEPS)
    def _():
        _ada_kernel(c_ref, wada_ref, bada_ref, late_ref)

    @pl.when((i == 0) & (v == 0))
    def _():
        h0_ref[...] = normed()

    for parity, (h_ref, h_next_ref) in enumerate(((h0_ref, h1_ref), (h1_ref, h0_ref))):
        @pl.when((v < last) & (i % 2 == parity))
        def _(h_ref=h_ref):
            w = jnp.concatenate([wlo_ref[...].astype(BF16), whi_ref[...].astype(BF16)], axis=1)
            o_ref[...] = jnp.dot(h_ref[...], w, preferred_element_type=F32).astype(BF16)

        @pl.when((v == last) & (i % 2 == parity))
        def _(h_ref=h_ref, h_next_ref=h_next_ref):
            kv_ref[...] = jnp.dot(h_ref[...], wlo_ref[...].astype(BF16),
                                  preferred_element_type=F32).astype(BF16)
            h_next_ref[...] = normed()


def _proj_w_tile(v, n_wide):
    return jnp.where(v < 2, 2 * v, jnp.where(v < n_wide, 2 * v + 1, KV_SRC_TILE))


def _proj(x2, gain, mod_early, w_in, c8, w_ada, b_ada, seq):
    t = x2.shape[0]
    n = w_in.shape[1]
    tiles_per_batch = seq // PROJ_TM
    n_wide = (n - PROJ_TN) // (2 * PROJ_TN)
    n_late = w_ada.shape[1] - ADA_EARLY
    assert n_late == ADA_LATE_STEPS * ADA_LATE_TN and ADA_LATE_STEPS <= (t // PROJ_TM) * (n_wide + 1)
    early_tiles = ADA_EARLY // ADA_LATE_TN

    def late_tile(i, v):
        return jnp.minimum(i * (n_wide + 1) + v, ADA_LATE_STEPS - 1)

    n_row_tiles = t // PROJ_TM

    def norm_tile(i, v):
        return jnp.minimum(i + (v == n_wide).astype(jnp.int32), n_row_tiles - 1)

    return pl.pallas_call(
        _proj_kernel,
        grid=(n_row_tiles, n_wide + 1),
        in_specs=[pl.BlockSpec((PROJ_TM, D_MODEL), lambda i, v: (norm_tile(i, v), 0)),
                  pl.BlockSpec((1, D_MODEL), lambda i, v: (0, 0)),
                  pl.BlockSpec((1, 2, D_MODEL), lambda i, v: (norm_tile(i, v) // tiles_per_batch, 0, 0)),
                  pl.BlockSpec((D_MODEL, PROJ_TN), lambda i, v: (0, _proj_w_tile(v, n_wide))),
                  pl.BlockSpec((D_MODEL, PROJ_TN),
                               lambda i, v: (0, jnp.where(v < n_wide, _proj_w_tile(v, n_wide) + 1, KV_SRC_TILE))),
                  pl.BlockSpec((8, D_MODEL), lambda i, v: (0, 0)),
                  pl.BlockSpec((D_MODEL, ADA_LATE_TN), lambda i, v: (0, early_tiles + late_tile(i, v))),
                  pl.BlockSpec((1, ADA_LATE_TN), lambda i, v: (0, early_tiles + late_tile(i, v)))],
        out_specs=[pl.BlockSpec((PROJ_TM, 2 * PROJ_TN), lambda i, v: (i, jnp.minimum(v, n_wide - 1))),
                   pl.BlockSpec((PROJ_TM, PROJ_TN), lambda i, v: (i, 0)),
                   pl.BlockSpec((8, ADA_LATE_TN), lambda i, v: (0, late_tile(i, v)))],
        out_shape=[jax.ShapeDtypeStruct((t, n - PROJ_TN), BF16),
                   jax.ShapeDtypeStruct((t, PROJ_TN), BF16),
                   jax.ShapeDtypeStruct((8, n_late), F32)],
        scratch_shapes=[pltpu.VMEM((PROJ_TM, D_MODEL), BF16), pltpu.VMEM((PROJ_TM, D_MODEL), BF16)],
        compiler_params=_params(("arbitrary", "arbitrary"), 60),
        name="proj",
    )(x2, gain, mod_early, w_in, w_in, c8, w_ada, b_ada)


def _attn_stages(sink_ref, q_ref, kvp_ref, kvc_ref, o_ref):
    n = pl.program_id(1)
    kvp = kvp_ref[...]
    kvc = kvc_ref[...]
    qi = lax.broadcasted_iota(jnp.int32, (WINDOW, WINDOW), 0)
    sj = lax.broadcasted_iota(jnp.int32, (WINDOW, WINDOW), 1)
    valid_prev = (sj > qi) & (n > 0)
    valid_cur = sj <= qi
    sink_col = sj == 0
    first_row = lax.broadcasted_iota(jnp.int32, (2 * WINDOW, 1), 0) == 0
    dh = ATTN_HEAD_DIM
    kv_w = ATTN_KV_HEADS * dh
    scale = jnp.asarray(dh ** -0.5, BF16)
    n_pairs = ATTN_GROUP // 2

    def group_operands(kv):
        kband = jnp.concatenate([kvp[:, kv * dh:(kv + 1) * dh],
                                 kvc[:, kv * dh:(kv + 1) * dh]], axis=0) * scale
        vband = jnp.concatenate([kvp[:, kv_w + kv * dh:kv_w + (kv + 1) * dh],
                                 kvc[:, kv_w + kv * dh:kv_w + (kv + 1) * dh]], axis=0)
        vband = jnp.where(first_row, jnp.zeros_like(vband), vband)
        zeros = jnp.zeros_like(kband)
        ones = jnp.ones_like(vband)
        k_pad = (jnp.concatenate([kband, zeros], axis=1), jnp.concatenate([zeros, kband], axis=1))
        pv_rhs = jnp.concatenate(
            [jnp.concatenate([vband, zeros, ones, zeros], axis=1),
             jnp.concatenate([zeros, vband, zeros, ones], axis=1)], axis=0)
        q_rows = jnp.concatenate(
            [q_ref[:, (kv * ATTN_GROUP + 2 * p) * dh:(kv * ATTN_GROUP + 2 * p + 2) * dh]
             for p in range(n_pairs)], axis=0)
        scores = [lax.dot_general(q_rows, k_pad[idx], (((1,), (1,)), ((), ())),
                                  preferred_element_type=F32) for idx in range(2)]
        return scores, pv_rhs

    nxt = group_operands(0)
    for kv in range(ATTN_KV_HEADS):
        scores, pv_rhs = nxt
        if kv + 1 < ATTN_KV_HEADS:
            nxt = group_operands(kv + 1)
        prob_rows = []
        for pair in range(n_pairs):
            rows = slice(pair * WINDOW, (pair + 1) * WINDOW)
            probs = []
            for idx in range(2):
                s = scores[idx][rows]
                sink = sink_ref[kv * ATTN_GROUP + 2 * pair + idx]
                s_prev = jnp.where(sink_col, sink, jnp.where(valid_prev, s[:, :WINDOW], NEG))
                s_cur = jnp.where(valid_cur, s[:, WINDOW:], NEG)
                m = jnp.max(jnp.maximum(s_prev, s_cur), axis=-1, keepdims=True)
                probs += [jnp.exp(s_prev - m).astype(BF16), jnp.exp(s_cur - m).astype(BF16)]
            prob_rows.append(jnp.concatenate(probs, axis=-1))
        r = jnp.dot(jnp.concatenate(prob_rows, axis=0), pv_rhs, preferred_element_type=F32)
        for pair in range(n_pairs):
            rows = slice(pair * WINDOW, (pair + 1) * WINDOW)
            col = (kv * ATTN_GROUP + 2 * pair) * dh
            o_ref[:, col:col + 2 * dh] = (r[rows, :2 * dh] * (1.0 / r[rows, 2 * dh:])).astype(BF16)
        yield


def _ret_decay_tables():
    lg = np.asarray(LOG_GAMMA, np.float64)[:, None, None]
    i = np.arange(RET_CHUNK, dtype=np.float64)
    diff = i[:, None] - i[None, :]
    k_scale = RET_DIM ** -0.5
    d_intra = np.where(diff >= 0, np.exp(np.maximum(diff, 0.0) * lg), 0.0) * k_scale
    lanes = np.ones((1, 1, RET_DIM // 2))
    d_q = np.exp((i[None, :, None] + 1.0) * lg) * lanes
    d_k = np.exp((RET_CHUNK - 1.0 - i[None, :, None]) * lg) * k_scale * lanes
    return jnp.asarray(d_intra, F32), jnp.asarray(d_q, F32), jnp.asarray(d_k, BF16)


def _ret_stages(pos_ref, invf_ref, di_ref, dq_ref, dk_ref, q_ref, k_ref, v_ref, g_ref, gain_ref,
                o_ref, state_ref):
    half = RET_DIM // 2
    ang = pos_ref[...].astype(F32) * invf_ref[...]
    cos = jnp.cos(ang).astype(BF16)
    sin = jnp.sin(ang).astype(BF16)

    def rot(t):
        t1, t2 = t[:, :half], t[:, half:]
        return jnp.concatenate([t1 * cos - t2 * sin, t1 * sin + t2 * cos], axis=-1)

    def both_halves(t, factor):
        return jnp.concatenate([t[:, :half] * factor, t[:, half:] * factor], axis=-1)

    def head_front(h):
        sl = slice(h * RET_DIM, (h + 1) * RET_DIM)
        qb = rot(q_ref[:, sl])
        kb = rot(k_ref[:, sl])
        intra = lax.dot_general(qb, kb, (((1,), (1,)), ((), ())),
                                preferred_element_type=F32) * di_ref[h]
        st = state_ref[h]
        cross = jnp.dot(qb, st.astype(BF16), preferred_element_type=F32)
        return kb, intra, st, cross

    front = head_front(0)
    for h in range(RET_HEADS):
        sl = slice(h * RET_DIM, (h + 1) * RET_DIM)
        kb, intra, st, cross = front
        if h + 1 < RET_HEADS:
            front = head_front(h + 1)
        vb = v_ref[:, sl]
        d_chunk = math.exp(RET_CHUNK * LOG_GAMMA[h])
        o = jnp.dot(intra.astype(BF16), vb, preferred_element_type=F32) + both_halves(cross, dq_ref[h])
        kd = both_halves(kb, dk_ref[h])
        state_ref[h] = st * d_chunk + lax.dot_general(kd, vb, (((0,), (0,)), ((), ())),
                                                      preferred_element_type=F32)
        o = o * lax.rsqrt(jnp.mean(o * o, axis=-1, keepdims=True) + EPS) * gain_ref[:, sl]
        gg = g_ref[:, sl].astype(F32)
        o_ref[:, sl] = (gg * jax.nn.sigmoid(gg) * o).astype(BF16)
        yield


def _mixers_kernel(sink_ref, qa_ref, kvp_ref, kvc_ref, pos_ref, invf_ref, di_ref, dq_ref, dk_ref,
                   qr_ref, kr_ref, vr_ref, gr_ref, gain_ref, attn_ref, ret_ref, state_ref):
    @pl.when(pl.program_id(1) == 0)
    def _():
        state_ref[...] = jnp.zeros_like(state_ref)

    attn = _attn_stages(sink_ref, qa_ref, kvp_ref, kvc_ref, attn_ref)
    ret = _ret_stages(pos_ref, invf_ref, di_ref, dq_ref, dk_ref, qr_ref, kr_ref, vr_ref, gr_ref,
                      gain_ref, ret_ref, state_ref)
    heads_per_group = RET_HEADS // ATTN_KV_HEADS
    for _ in range(ATTN_KV_HEADS):
        next(attn)
        for _ in range(heads_per_group):
            next(ret)


def _mixers(proj, proj_kv, sinks, pos, inv_freq, ret_gain, batch, seq):
    nb = seq // WINDOW
    t = batch * seq

    def col(cb):
        return lambda b, n: (b * nb + n, cb)

    d_intra, d_q, d_k = _ret_decay_tables()
    table = lambda lanes: pl.BlockSpec((RET_HEADS, RET_CHUNK, lanes), lambda b, n: (0, 0, 0))
    rows = pl.BlockSpec((WINDOW, D_MODEL), col(0))
    return pl.pallas_call(
        _mixers_kernel,
        grid=(batch, nb),
        in_specs=[pl.BlockSpec(memory_space=pltpu.SMEM),
                  pl.BlockSpec((WINDOW, D_MODEL), col(COL_QA)),
                  pl.BlockSpec((WINDOW, PROJ_TN), lambda b, n: (b * nb + jnp.maximum(n - 1, 0), 0)),
                  pl.BlockSpec((WINDOW, PROJ_TN), col(0)),
                  pl.BlockSpec((RET_CHUNK, 1), col(0)),
                  pl.BlockSpec((1, RET_DIM // 2), lambda b, n: (0, 0)),
                  table(RET_CHUNK), table(RET_DIM // 2), table(RET_DIM // 2),
                  pl.BlockSpec((RET_CHUNK, D_MODEL), col(COL_QR)),
                  pl.BlockSpec((RET_CHUNK, D_MODEL), col(COL_KR)),
                  pl.BlockSpec((RET_CHUNK, D_MODEL), col(COL_VR)),
                  pl.BlockSpec((RET_CHUNK, D_MODEL), col(COL_GR)),
                  pl.BlockSpec((1, D_MODEL), lambda b, n: (0, 0))],
        out_specs=[rows, rows],
        out_shape=[jax.ShapeDtypeStruct((t, D_MODEL), BF16), jax.ShapeDtypeStruct((t, D_MODEL), BF16)],
        scratch_shapes=[pltpu.VMEM((RET_HEADS, RET_DIM, RET_DIM), F32)],
        compiler_params=_params(("arbitrary", "arbitrary"), 40),
        name="mixers",
    )(sinks, proj, proj_kv, proj_kv, pos, inv_freq, d_intra, d_q, d_k, proj, proj, proj, proj, ret_gain)


def _route(logits):
    lane = lax.broadcasted_iota(jnp.int32, logits.shape, 1)
    lane_f = lane.astype(F32)
    is_g = lane < N_GROUPS
    gl = jnp.where(is_g, logits, NEG)
    gmax = jnp.max(gl, axis=-1, keepdims=True)
    gsel = jnp.min(jnp.where(gl == gmax, lane_f, float(LANES)), axis=-1, keepdims=True)
    gsum = jnp.sum(jnp.where(is_g, jnp.exp(gl - gmax), 0.0), axis=-1, keepdims=True)
    g_w = 1.0 / gsum
    grp = ((lane - N_GROUPS) >> 4).astype(F32)
    is_e = (lane >= N_GROUPS) & (lane < N_GROUPS + N_EXPERTS) & (grp == gsel)
    el = jnp.where(is_e, logits, NEG)
    v1 = jnp.max(el, axis=-1, keepdims=True)
    i1 = jnp.min(jnp.where(el == v1, lane_f, float(LANES)), axis=-1, keepdims=True)
    el2 = jnp.where(lane_f == i1, NEG, el)
    v2 = jnp.max(el2, axis=-1, keepdims=True)
    i2 = jnp.min(jnp.where(el2 == v2, lane_f, float(LANES)), axis=-1, keepdims=True)
    tt = jnp.exp(v2 - v1)
    w1 = g_w / (1.0 + tt)
    w2 = g_w * tt / (1.0 + tt)
    return jnp.where(lane == 0, i1 - N_GROUPS,
                     jnp.where(lane == 1, i2 - N_GROUPS,
                               jnp.where(lane == 2, w1, jnp.where(lane == 3, w2, 0.0))))


def _branch_kernel(attn_ref, ret_ref, ga_ref, gr_ref, wa_ref, wr_ref, o_ref):
    a = jnp.dot(attn_ref[...], wa_ref[...].astype(BF16), preferred_element_type=F32)
    r = jnp.dot(ret_ref[...], wr_ref[...].astype(BF16), preferred_element_type=F32)
    o_ref[...] = (jax.nn.sigmoid(ga_ref[...].astype(F32)) * a
                  + jax.nn.sigmoid(gr_ref[...].astype(F32)) * r).astype(BF16)


def _branch(attn, ret, proj, wa, wr):
    t = attn.shape[0]
    tm, tn = BRANCH_TM, BRANCH_TN
    per_slab = D_MODEL // tn
    return pl.pallas_call(
        _branch_kernel,
        grid=(t // tm, D_MODEL // tn),
        in_specs=[pl.BlockSpec((tm, D_MODEL), lambda i, j: (i, 0)),
                  pl.BlockSpec((tm, D_MODEL), lambda i, j: (i, 0)),
                  pl.BlockSpec((tm, tn), lambda i, j: (i, COL_GA * per_slab + j)),
                  pl.BlockSpec((tm, tn), lambda i, j: (i, COL_GRT * per_slab + j)),
                  pl.BlockSpec((D_MODEL, tn), lambda i, j: (0, j)),
                  pl.BlockSpec((D_MODEL, tn), lambda i, j: (0, j))],
        out_specs=pl.BlockSpec((tm, tn), lambda i, j: (i, j)),
        out_shape=jax.ShapeDtypeStruct((t, D_MODEL), BF16),
        compiler_params=_params(("arbitrary", "arbitrary"), 48),
        name="branch",
    )(attn, ret, proj, proj, wa, wr)


def _mixout_kernel(m_ref, x_ref, mod_ref, g2_ref, wo_ref, wrt_ref, brt_ref,
                   x1_ref, h2_ref, route_ref, count_ref, meta_ref, carry_ref):
    @pl.when(pl.program_id(0) == 0)
    def _():
        carry_ref[...] = jnp.zeros_like(carry_ref)

    mix = jnp.dot(m_ref[...], wo_ref[...], preferred_element_type=F32)
    x1 = x_ref[...] + mod_ref[0, 2:3, :] * mix
    x1_ref[...] = x1
    var = jnp.mean(x1 * x1, axis=-1, keepdims=True)
    h2 = x1 * lax.rsqrt(var + EPS) * g2_ref[...]
    h2 = h2 * (1.0 + mod_ref[0, 4:5, :]) + mod_ref[0, 3:4, :]
    tm = h2.shape[0]
    for j in range(SLAB_ROWS):
        lo = h2[:, j * LANES:(j + 1) * LANES]
        hi = h2[:, HALF_D + j * LANES:HALF_D + (j + 1) * LANES]
        h2_ref[_slab_rows(j, tm), :] = _pack_pair(lo, hi)
    h_hi = h2.astype(BF16)
    h_lo = (h2 - h_hi.astype(F32)).astype(BF16)
    w_rt = wrt_ref[...]
    w_hi = w_rt.astype(BF16)
    w_lo = (w_rt - w_hi.astype(F32)).astype(BF16)
    hi_both = jnp.dot(h_hi, jnp.concatenate([w_hi, w_lo], axis=1), preferred_element_type=F32)
    logits = (hi_both[:, :LANES] + hi_both[:, LANES:]
              + jnp.dot(h_lo, w_hi, preferred_element_type=F32) + brt_ref[...])
    route = _route(logits)

    lane = lax.broadcasted_iota(jnp.int32, route.shape, 1)
    lane_f = lane.astype(F32)
    hot1 = lane_f == route[:, 0:1]
    hot2 = lane_f == route[:, 1:2]
    both = jnp.where(hot1 | hot2, 1.0, 0.0)
    ii = lax.broadcasted_iota(jnp.int32, (tm, tm), 0)
    jj = lax.broadcasted_iota(jnp.int32, (tm, tm), 1)
    lower = jnp.where(ii > jj, 1.0, 0.0).astype(BF16)
    before = jnp.dot(lower, both.astype(BF16), preferred_element_type=F32) + carry_ref[...]
    r1 = jnp.sum(jnp.where(hot1, before, 0.0), axis=-1, keepdims=True)
    r2 = jnp.sum(jnp.where(hot2, before, 0.0), axis=-1, keepdims=True)
    route = jnp.where(lane == 4, r1, jnp.where(lane == 5, r2, route))
    route_ref[...] = route
    meta_ref[...] = route.T[0:SLAB_ROWS, :].astype(I32)
    carry = carry_ref[...] + jnp.sum(both, axis=0, keepdims=True)
    carry_ref[...] = carry
    count_ref[...] = carry


def _mixout(merged, x2, mod6, gain2, wo, w_rt, b_rt, seq):
    t = x2.shape[0]
    tm = MIXOUT_TM
    tiles_per_batch = seq // tm
    row = lambda i: (i, 0)
    const = lambda i: (0, 0)
    return pl.pallas_call(
        _mixout_kernel,
        grid=(t // tm,),
        in_specs=[pl.BlockSpec((tm, D_MODEL), row),
                  pl.BlockSpec((tm, D_MODEL), row),
                  pl.BlockSpec((1, 6, D_MODEL), lambda i: (i // tiles_per_batch, 0, 0)),
                  pl.BlockSpec((1, D_MODEL), const),
                  pl.BlockSpec((D_MODEL, D_MODEL), const, pipeline_mode=pl.Buffered(1)),
                  pl.BlockSpec((D_MODEL, LANES), const),
                  pl.BlockSpec((1, LANES), const)],
        out_specs=[pl.BlockSpec((tm, D_MODEL), row),
                   pl.BlockSpec((tm * SLAB_ROWS, SLAB_LANES), row),
                   pl.BlockSpec((tm, LANES), row),
                   pl.BlockSpec((1, LANES), const),
                   pl.BlockSpec((SLAB_ROWS, tm), lambda i: (0, i))],
        out_shape=[jax.ShapeDtypeStruct((t, D_MODEL), F32),
                   jax.ShapeDtypeStruct((t * SLAB_ROWS, SLAB_LANES), I32),
                   jax.ShapeDtypeStruct((t, LANES), F32),
                   jax.ShapeDtypeStruct((1, LANES), F32),
                   jax.ShapeDtypeStruct((SLAB_ROWS, t), I32)],
        scratch_shapes=[pltpu.VMEM((1, LANES), F32)],
        compiler_params=_params(("arbitrary",), 56),
        name="mixout",
    )(merged, x2, mod6, gain2, wo, w_rt, b_rt)


PAD_BITS = (64, 32, 16, 8, 4, 2, 1)


def _sorted_row(tables, tok, k):
    e_refs, r_refs, blk_row_ref = tables[0:TOP_K], tables[TOP_K:2 * TOP_K], tables[2 * TOP_K]
    return blk_row_ref[e_refs[k][tok]] + r_refs[k][tok]


N_ROUTE_TABLES = 2 * TOP_K + 1


def _dispatch_kernel(*refs):
    tables = refs[:N_ROUTE_TABLES]
    zstart_ref, zcount_ref, h2_ref, xs_ref, dest_ref, zero_ref, sem, zsem = refs[N_ROUTE_TABLES:]
    n_assign = tables[0].shape[0] * TOP_K
    zero_ref[...] = jnp.zeros_like(zero_ref)

    def zero_copy(start, rows):
        return pltpu.make_async_copy(zero_ref.at[pl.ds(0, rows * SLAB_ROWS), :],
                                     xs_ref.at[pl.ds(start * SLAB_ROWS, rows * SLAB_ROWS), :], zsem)

    def fill(e, wait):
        start = zstart_ref[e]
        pad = zcount_ref[e]
        for bit in PAD_BITS:
            @pl.when((pad & bit) != 0)
            def _(start=start, bit=bit):
                cp = zero_copy(start, bit)
                cp.wait() if wait else cp.start()
            start = start + (pad & bit)

    lax.fori_loop(0, N_EXPERTS, lambda e, c: (fill(e, False), c)[1], 0)

    def issue(tok, carry):
        src = h2_ref.at[pl.ds(pl.multiple_of(tok * SLAB_ROWS, SLAB_ROWS), SLAB_ROWS), :]
        for k in range(TOP_K):
            row = _sorted_row(tables, tok, k)
            dest_ref[tok * TOP_K + k] = row
            dst = pl.multiple_of(row * SLAB_ROWS, SLAB_ROWS)
            pltpu.make_async_copy(src, xs_ref.at[pl.ds(dst, SLAB_ROWS), :], sem).start()
        return carry

    lax.fori_loop(0, n_assign // TOP_K, issue, 0, unroll=8)

    def drain(i, carry):
        pltpu.make_async_copy(h2_ref.at[pl.ds(0, ROWS_PER_WAIT * SLAB_ROWS), :],
                              xs_ref.at[pl.ds(0, ROWS_PER_WAIT * SLAB_ROWS), :], sem).wait()
        return carry

    lax.fori_loop(0, n_assign // ROWS_PER_WAIT, drain, 0)
    lax.fori_loop(0, N_EXPERTS, lambda e, c: (fill(e, True), c)[1], 0)


def _dispatch(tables, zstart, zcount, h2_slab, n_pad):
    n_assign = tables[0].shape[0] * TOP_K
    return pl.pallas_call(
        _dispatch_kernel,
        grid_spec=pltpu.PrefetchScalarGridSpec(
            num_scalar_prefetch=N_ROUTE_TABLES + 2,
            grid=(1,),
            in_specs=[pl.BlockSpec(memory_space=pltpu.HBM)],
            out_specs=[pl.BlockSpec(memory_space=pl.ANY),
                       pl.BlockSpec(memory_space=pltpu.SMEM)],
            scratch_shapes=[pltpu.VMEM((PAD_BITS[0] * SLAB_ROWS, SLAB_LANES), I32),
                            pltpu.SemaphoreType.DMA(()),
                            pltpu.SemaphoreType.DMA(())]),
        out_shape=[jax.ShapeDtypeStruct((n_pad * SLAB_ROWS, SLAB_LANES), I32),
                   jax.ShapeDtypeStruct((n_assign,), I32)],
        compiler_params=_params(("arbitrary",), 16),
        name="dispatch",
    )(*tables, zstart, zcount, h2_slab)


def _experts_kernel(item_e_ref, item_blk_ref, item_nb_ref, x_blk_ref,
                    x0_ref, x1_ref, x2_ref, x3_ref, wg_hbm, wu_hbm, wd_hbm, y_ref,
                    xb_ref, acc_ref, yp_ref, wg_ref, wu_ref, wd_ref, sem, wsem):
    w = pl.program_id(0)
    c = pl.program_id(1)
    n_items = pl.num_programs(0)
    n_chunks = pl.num_programs(1)
    last_c = n_chunks - 1
    nb = item_nb_ref[w]
    blk0 = item_blk_ref[w]
    x_refs = (x0_ref, x1_ref, x2_ref, x3_ref)

    blk_rows = MOE_BLOCK * SLAB_ROWS

    step = w * n_chunks + c
    fc = wg_ref.shape[2]

    def weight_copies(item, chunk, slot):
        e = item_e_ref[item]
        col = pl.multiple_of(chunk * fc, fc)
        return (pltpu.make_async_copy(wg_hbm.at[e, :, pl.ds(col, fc)], wg_ref.at[slot], wsem.at[slot]),
                pltpu.make_async_copy(wu_hbm.at[e, :, pl.ds(col, fc)], wu_ref.at[slot], wsem.at[slot]),
                pltpu.make_async_copy(wd_hbm.at[e, pl.ds(col, fc), :], wd_ref.at[slot], wsem.at[slot]))

    def start_weights(ahead):
        st = step + ahead
        item = jnp.minimum(st // n_chunks, n_items - 1)

        @pl.when((st < n_items * n_chunks) & (item_nb_ref[item] > 0))
        def _():
            for cp in weight_copies(item, st % n_chunks, st % EXPERT_WEIGHT_SLOTS):
                cp.start()

    @pl.when(step == 0)
    def _():
        for ahead in range(EXPERT_WEIGHT_SLOTS - 1):
            start_weights(ahead)

    start_weights(EXPERT_WEIGHT_SLOTS - 1)
    w_slot = step % EXPERT_WEIGHT_SLOTS

    @pl.when(nb > 0)
    def _():
        for cp in weight_copies(w, c, w_slot):
            cp.wait()

    def out_copy(s):
        return pltpu.make_async_copy(yp_ref.at[s], y_ref.at[pl.ds((blk0 + s) * blk_rows, blk_rows), :],
                                     sem.at[s])

    def wait_out(count):
        for s in range(ITEM_BLOCKS):
            @pl.when(s < count)
            def _(s=s):
                out_copy(s).wait()

    @pl.when((c == last_c) & (w > 0))
    def _():
        wait_out(item_nb_ref[jnp.maximum(w - 1, 0)])

    def run(n_live):
        rows = n_live * MOE_BLOCK

        @pl.when(c == 0)
        def _():
            for s in range(n_live):
                for j in range(SLAB_ROWS):
                    lo, hi = _unpack_pair(x_refs[s][_slab_rows(j, MOE_BLOCK), :])
                    r0 = s * MOE_BLOCK
                    xb_ref[r0:r0 + MOE_BLOCK, j * LANES:(j + 1) * LANES] = lo.astype(BF16)
                    xb_ref[r0:r0 + MOE_BLOCK, HALF_D + j * LANES:HALF_D + (j + 1) * LANES] = hi.astype(BF16)

        xs = xb_ref[0:rows, :]
        g = jnp.dot(xs, wg_ref[w_slot].astype(BF16), preferred_element_type=F32)
        u = jnp.dot(xs, wu_ref[w_slot].astype(BF16), preferred_element_type=F32)
        hid = (g * jax.nn.sigmoid(g) * u).astype(BF16)
        y = jnp.dot(hid, wd_ref[w_slot].astype(BF16), preferred_element_type=F32)

        @pl.when(c == 0)
        def _():
            acc_ref[0:rows, :] = y

        @pl.when((c > 0) & (c < last_c))
        def _():
            acc_ref[0:rows, :] += y

        @pl.when(c == last_c)
        def _():
            for s in range(n_live):
                r0 = s * MOE_BLOCK
                for j in range(SLAB_ROWS):
                    lo_cols = slice(j * LANES, (j + 1) * LANES)
                    hi_cols = slice(HALF_D + j * LANES, HALF_D + (j + 1) * LANES)
                    yp_ref[s, _slab_rows(j, MOE_BLOCK), :] = _pack_pair(
                        acc_ref[r0:r0 + MOE_BLOCK, lo_cols] + y[r0:r0 + MOE_BLOCK, lo_cols],
                        acc_ref[r0:r0 + MOE_BLOCK, hi_cols] + y[r0:r0 + MOE_BLOCK, hi_cols])
                out_copy(s).start()

    for n_live in range(1, ITEM_BLOCKS + 1):
        pl.when(nb == n_live)(functools.partial(run, n_live))

    @pl.when((c == last_c) & (w == pl.num_programs(0) - 1))
    def _():
        wait_out(nb)


def _experts(item_e, item_blk, item_nb, x_sorted, w_gate, w_up, w_down, n_blocks):
    n_items = item_e.shape[0]
    d = D_MODEL
    fc = EXPERT_FC
    x3 = x_sorted
    blk_rows = MOE_BLOCK * SLAB_ROWS
    n_chunks = EXPERT_DIM // fc
    assert n_chunks >= 2, "the last hidden chunk adds onto the accumulator of the earlier ones"

    slot = jnp.arange(ITEM_BLOCKS, dtype=jnp.int32)[:, None]
    x_blk = jnp.maximum(lax.cummax(jnp.where(slot < item_nb[None, :], item_blk[None, :] + slot, -1), axis=1), 0)
    x_blk = x_blk.reshape(-1).astype(jnp.int32)

    def x_spec(s):
        return pl.BlockSpec((blk_rows, SLAB_LANES),
                            lambda w, c, ie, ib, inb, xb: (xb[s * n_items + w], 0))

    hbm = pl.BlockSpec(memory_space=pl.ANY)
    slots = EXPERT_WEIGHT_SLOTS
    return pl.pallas_call(
        _experts_kernel,
        grid_spec=pltpu.PrefetchScalarGridSpec(
            num_scalar_prefetch=4,
            grid=(n_items, n_chunks),
            in_specs=[x_spec(0), x_spec(1), x_spec(2), x_spec(3), hbm, hbm, hbm],
            out_specs=pl.BlockSpec(memory_space=pl.ANY),
            scratch_shapes=[pltpu.VMEM((ITEM_BLOCKS * MOE_BLOCK, d), BF16),
                            pltpu.VMEM((ITEM_BLOCKS * MOE_BLOCK, d), F32),
                            pltpu.VMEM((ITEM_BLOCKS, blk_rows, SLAB_LANES), I32),
                            pltpu.VMEM((slots, d, fc), F32),
                            pltpu.VMEM((slots, d, fc), F32),
                            pltpu.VMEM((slots, fc, d), F32),
                            pltpu.SemaphoreType.DMA((ITEM_BLOCKS,)),
                            pltpu.SemaphoreType.DMA((slots,))]),
        out_shape=jax.ShapeDtypeStruct((n_blocks * blk_rows, SLAB_LANES), I32),
        compiler_params=_params(("arbitrary", "arbitrary"), 56),
        name="experts",
    )(item_e, item_blk, item_nb, x_blk, x3, x3, x3, x3, w_gate, w_up, w_down)


def _combine_kernel(dest_ref, x1_ref, route_ref, mod_ref, gain_ref, y_ref, o_ref, ybuf_ref, sem):
    tm = COMBINE_TM
    i = pl.program_id(0)
    n_tiles = pl.num_programs(0)

    def row_copy(slot, k, r, src):
        return pltpu.make_async_copy(
            y_ref.at[pl.ds(pl.multiple_of(src * SLAB_ROWS, SLAB_ROWS), SLAB_ROWS), :],
            ybuf_ref.at[slot, k, pl.ds(pl.multiple_of(r * SLAB_ROWS, SLAB_ROWS), SLAB_ROWS), :],
            sem.at[slot])

    def wait_slot(slot):
        for k in range(TOP_K):
            pltpu.make_async_copy(y_ref.at[pl.ds(0, tm * SLAB_ROWS), :], ybuf_ref.at[slot, k],
                                  sem.at[slot]).wait()

    def issue_tile(tile, slot):
        base = tile * tm * TOP_K

        def issue(r, carry):
            for k in range(TOP_K):
                row_copy(slot, k, r, dest_ref[base + r * TOP_K + k]).start()
            return carry

        lax.fori_loop(0, tm, issue, 0, unroll=4)

    @pl.when(i == 0)
    def _():
        for tile in range(COMBINE_SLOTS - 1):
            issue_tile(tile, tile)

    slot = i % COMBINE_SLOTS
    wait_slot(slot)

    ahead = i + COMBINE_SLOTS - 1
    ahead_slot = ahead % COMBINE_SLOTS
    ahead_base = jnp.minimum(ahead, n_tiles - 1) * (tm * TOP_K)
    n_stages = 2 * SLAB_ROWS
    rows_per_stage = tm // n_stages

    route = route_ref[...]
    w0 = route[:, 2:3]
    w1 = route[:, 3:4]
    ssq = jnp.zeros((tm, 1), F32)
    stage = 0
    for j in range(SLAB_ROWS):
        rows = _slab_rows(j, tm)
        y0 = _unpack_pair(ybuf_ref[slot, 0, rows, :])
        y1 = _unpack_pair(ybuf_ref[slot, 1, rows, :])
        for part, off in ((0, j * LANES), (1, HALF_D + j * LANES)):
            cols = slice(off, off + LANES)
            ffn = w0 * y0[part] + w1 * y1[part]
            x2 = x1_ref[:, cols] + mod_ref[0, 5:6, cols] * ffn
            o_ref[:, cols] = x2
            ssq = ssq + jnp.sum(x2 * x2, axis=-1, keepdims=True)
            for r in range(stage * rows_per_stage, (stage + 1) * rows_per_stage):
                for k in range(TOP_K):
                    row_copy(ahead_slot, k, r, dest_ref[ahead_base + r * TOP_K + k]).start()
            stage += 1
    o_ref[...] = o_ref[...] * lax.rsqrt(ssq * (1.0 / D_MODEL) + EPS) * gain_ref[...]

    @pl.when(i == n_tiles - 1)
    def _():
        for extra in range(1, COMBINE_SLOTS):
            wait_slot((i + extra) % COMBINE_SLOTS)


def _combine(dest, x1, route, mod6, gain, y_sorted, seq):
    t, d = x1.shape
    tm = COMBINE_TM
    tiles_per_batch = seq // tm
    return pl.pallas_call(
        _combine_kernel,
        grid_spec=pltpu.PrefetchScalarGridSpec(
            num_scalar_prefetch=1,
            grid=(t // tm,),
            in_specs=[pl.BlockSpec((tm, d), lambda i, *_: (i, 0)),
                      pl.BlockSpec((tm, LANES), lambda i, *_: (i, 0)),
                      pl.BlockSpec((1, 6, d), lambda i, *_: (i // tiles_per_batch, 0, 0)),
                      pl.BlockSpec((1, d), lambda i, *_: (0, 0)),
                      pl.BlockSpec(memory_space=pl.ANY)],
            out_specs=pl.BlockSpec((tm, d), lambda i, *_: (i, 0)),
            scratch_shapes=[pltpu.VMEM((COMBINE_SLOTS, TOP_K, tm * SLAB_ROWS, SLAB_LANES), I32),
                            pltpu.SemaphoreType.DMA((COMBINE_SLOTS,))]),
        out_shape=jax.ShapeDtypeStruct((t, d), F32),
        compiler_params=_params(("arbitrary",), 24),
        name="combine",
    )(dest, x1, route, mod6, gain, y_sorted)


def _dispatch_tables(meta, counts, t):
    n_assign = t * TOP_K
    n_pad = -(-(n_assign + N_EXPERTS * (MOE_BLOCK - 1)) // MOE_BLOCK) * MOE_BLOCK
    n_blocks = n_pad // MOE_BLOCK
    n_items = N_EXPERTS + n_assign // (ITEM_BLOCKS * MOE_BLOCK)

    cnt = counts[0, :N_EXPERTS].astype(jnp.int32)
    blocks_e = (cnt + MOE_BLOCK - 1) // MOE_BLOCK
    blk_end = jnp.cumsum(blocks_e)
    blk_start = blk_end - blocks_e
    tables = (meta[0], meta[1], meta[4], meta[5], (blk_start * MOE_BLOCK).astype(jnp.int32))
    zstart = (blk_start * MOE_BLOCK + cnt).astype(jnp.int32)
    zcount = (blocks_e * MOE_BLOCK - cnt).astype(jnp.int32)

    items_e = (blocks_e + ITEM_BLOCKS - 1) // ITEM_BLOCKS
    item_end = jnp.cumsum(items_e)
    item_start = item_end - items_e
    w = jnp.arange(n_items, dtype=jnp.int32)
    live = w < item_end[-1]
    w_live = jnp.minimum(w, item_end[-1] - 1)
    e_w = jnp.minimum(jnp.sum((item_end[None, :] <= w_live[:, None]).astype(jnp.int32), axis=1), N_EXPERTS - 1)
    j_w = w_live - item_start[e_w]
    item_blk = (blk_start[e_w] + ITEM_BLOCKS * j_w).astype(jnp.int32)
    item_nb = jnp.where(live, jnp.clip(blocks_e[e_w] - ITEM_BLOCKS * j_w, 0, ITEM_BLOCKS), 0).astype(jnp.int32)
    return tables, zstart, zcount, e_w, item_blk, item_nb, n_blocks


def kernel(x, c, positions, norm1_gain, norm2_gain, final_norm_gain, w_ada, b_ada, w_in, attn_sinks,
           ret_norm_gain, w_branch_attn, w_branch_ret, w_out, w_router_group, b_router_group,
           w_router_expert, b_router_expert, w_expert_gate, w_expert_up, w_expert_down):
    batch, seq, d = x.shape
    t = batch * seq
    depth = w_ada.shape[0]
    half = RET_DIM // 2
    inv_freq = (ROPE_BASE ** (-jnp.arange(half, dtype=F32) / half)).reshape(1, half)
    pos = positions.reshape(t, 1)
    c8 = jnp.pad(c, ((0, 8 - batch), (0, 0)))
    xf = x.reshape(t, d)

    assert depth == 1, "the fused final norm assumes a single layer"
    for layer in range(depth):
        b_ada2 = b_ada[layer].reshape(1, -1)
        mod_early = _ada(c8, w_ada[layer], b_ada2, ADA_EARLY)[:batch]
        proj, proj_kv, mod_late = _proj(xf, norm1_gain[layer].reshape(1, d), mod_early.reshape(batch, 2, d),
                                        w_in[layer], c8, w_ada[layer], b_ada2, seq)
        mod6 = jnp.concatenate([mod_early, mod_late[:batch]], axis=1).reshape(batch, 6, d)
        attn, ret = _mixers(proj, proj_kv, attn_sinks[layer], pos, inv_freq,
                            ret_norm_gain[layer].reshape(1, d), batch, seq)

        pad = LANES - N_GROUPS - N_EXPERTS
        w_rt = jnp.concatenate([w_router_group[layer], w_router_expert[layer],
                                jnp.zeros((d, pad), F32)], axis=1)
        b_rt = jnp.concatenate([b_router_group[layer], b_router_expert[layer],
                                jnp.zeros((pad,), F32)]).reshape(1, LANES)
        merged = _branch(attn, ret, proj, w_branch_attn[layer], w_branch_ret[layer])
        x1, h2, route, counts, meta = _mixout(merged, xf, mod6, norm2_gain[layer].reshape(1, d),
                                              w_out[layer].astype(BF16), w_rt, b_rt, seq)
        tables, zstart, zcount, item_e, item_blk, item_nb, n_blocks = _dispatch_tables(meta, counts, t)
        x_sorted, dest = _dispatch(tables, zstart, zcount, h2, n_blocks * MOE_BLOCK)
        y_sorted = _experts(item_e, item_blk, item_nb, x_sorted,
                            w_expert_gate[layer], w_expert_up[layer], w_expert_down[layer], n_blocks)
        xf = _combine(dest, x1, route, mod6, final_norm_gain.reshape(1, d), y_sorted, seq)
    return xf.reshape(batch, seq, d)
```

```python
import functools
import math

import jax
import jax.numpy as jnp
import numpy as np
from jax import lax
from jax.experimental import pallas as pl
from jax.experimental.pallas import tpu as pltpu

F32 = jnp.float32
BF16 = jnp.bfloat16

D_MODEL = 2048
ATTN_HEAD_DIM = 64
ATTN_HEADS = 32
ATTN_KV_HEADS = 4
ATTN_GROUP = 8
WINDOW = 128
RET_HEADS = 8
RET_DIM = 256
RET_CHUNK = 128
ROPE_BASE = 10000.0
N_GROUPS = 4
EXPERTS_PER_GROUP = 16
N_EXPERTS = 64
TOP_K = 2
EXPERT_DIM = 1024
MOE_BLOCK = 128
EPS = 1e-6
NEG = -1e30

MIB = 1024 * 1024
LANES = 128
PROJ_TN = 512
PROJ_TM = 1024
KV_SRC_TILE = 4
ADA_EARLY = 2 * D_MODEL
ADA_LATE_TN = 256
ADA_LATE_STEPS = 4 * D_MODEL // ADA_LATE_TN
BRANCH_TM = 1024
BRANCH_TN = 512
MIXOUT_TM = 512
ITEM_BLOCKS = 4
EXPERT_FC = 512
EXPERT_WEIGHT_SLOTS = 3
COMBINE_TM = 256
COMBINE_SLOTS = 3
SLAB_ROWS = 8
SLAB_LANES = LANES
HALF_D = D_MODEL // 2
ROWS_PER_WAIT = 128
I32 = jnp.int32

COL_QA, COL_QR, COL_KR, COL_VR, COL_GR, COL_GA, COL_GRT = 0, 1, 2, 3, 4, 5, 6

LOG_GAMMA = [math.log1p(-(2.0 ** (-5.0 - h))) for h in range(RET_HEADS)]


def _params(sem, vmem_mib):
    return pltpu.CompilerParams(dimension_semantics=sem, vmem_limit_bytes=vmem_mib * MIB)


def _pack_pair(lo, hi):
    lo_b = lax.bitcast_convert_type(lo.astype(BF16).astype(F32), I32)
    hi_b = lax.bitcast_convert_type(hi.astype(BF16).astype(F32), I32)
    return hi_b | lax.shift_right_logical(lo_b, jnp.full_like(lo_b, 16))


def _unpack_pair(w):
    lo = lax.bitcast_convert_type(w << 16, F32)
    hi = lax.bitcast_convert_type(w & jnp.int32(-65536), F32)
    return lo, hi


def _slab_rows(j, n_tokens):
    return pl.ds(j, n_tokens, stride=SLAB_ROWS)


def _ada_kernel(c_ref, w_ref, b_ref, o_ref):
    c = c_ref[...]
    a = (c * jax.nn.sigmoid(c)).astype(BF16)
    o_ref[...] = jnp.dot(a, w_ref[...].astype(BF16), preferred_element_type=F32) + b_ref[...]


def _ada(c8, w_ada, b_ada, n):
    tn = 1024
    return pl.pallas_call(
        _ada_kernel,
        grid=(n // tn,),
        in_specs=[pl.BlockSpec((8, D_MODEL), lambda j: (0, 0)),
                  pl.BlockSpec((D_MODEL, tn), lambda j: (0, j)),
                  pl.BlockSpec((1, tn), lambda j: (0, j))],
        out_specs=pl.BlockSpec((8, tn), lambda j: (0, j)),
        out_shape=jax.ShapeDtypeStruct((8, n), F32),
        compiler_params=_params(("arbitrary",), 40),
        name="ada",
    )(c8, w_ada, b_ada)


def _proj_kernel(x_ref, g_ref, mod_ref, wlo_ref, whi_ref, c_ref, wada_ref, bada_ref,
                 o_ref, kv_ref, late_ref, h0_ref, h1_ref):
    i = pl.program_id(0)
    v = pl.program_id(1)
    last = pl.num_programs(1) - 1

    def normed():
        x = x_ref[...]
        var = jnp.mean(x * x, axis=-1, keepdims=True)
        y = x * lax.rsqrt(var + EPS) * g_ref[...]
        return (y * (1.0 + mod_ref[0, 1:2, :]) + mod_ref[0, 0:1, :]).astype(BF16)

    @pl.when(pl.program_id(0) * pl.num_programs(1) + v < ADA_LATE_STEPS)
    def _():
        _ada_kernel(c_ref, wada_ref, bada_ref, late_ref)

    @pl.when((i == 0) & (v == 0))
    def _():
        h0_ref[...] = normed()

    for parity, (h_ref, h_next_ref) in enumerate(((h0_ref, h1_ref), (h1_ref, h0_ref))):
        @pl.when((v < last) & (i % 2 == parity))
        def _(h_ref=h_ref):
            w = jnp.concatenate([wlo_ref[...].astype(BF16), whi_ref[...].astype(BF16)], axis=1)
            o_ref[...] = jnp.dot(h_ref[...], w, preferred_element_type=F32).astype(BF16)

        @pl.when((v == last) & (i % 2 == parity))
        def _(h_ref=h_ref, h_next_ref=h_next_ref):
            kv_ref[...] = jnp.dot(h_ref[...], wlo_ref[...].astype(BF16),
                                  preferred_element_type=F32).astype(BF16)
            h_next_ref[...] = normed()


def _proj_w_tile(v, n_wide):
    return jnp.where(v < 2, 2 * v, jnp.where(v < n_wide, 2 * v + 1, KV_SRC_TILE))


def _proj(x2, gain, mod_early, w_in, c8, w_ada, b_ada, seq):
    t = x2.shape[0]
    n = w_in.shape[1]
    tiles_per_batch = seq // PROJ_TM
    n_wide = (n - PROJ_TN) // (2 * PROJ_TN)
    n_late = w_ada.shape[1] - ADA_EARLY
    assert n_late == ADA_LATE_STEPS * ADA_LATE_TN and ADA_LATE_STEPS <= (t // PROJ_TM) * (n_wide + 1)
    early_tiles = ADA_EARLY // ADA_LATE_TN

    def late_tile(i, v):
        return jnp.minimum(i * (n_wide + 1) + v, ADA_LATE_STEPS - 1)

    n_row_tiles = t // PROJ_TM

    def norm_tile(i, v):
        return jnp.minimum(i + (v == n_wide).astype(jnp.int32), n_row_tiles - 1)

    return pl.pallas_call(
        _proj_kernel,
        grid=(n_row_tiles, n_wide + 1),
        in_specs=[pl.BlockSpec((PROJ_TM, D_MODEL), lambda i, v: (norm_tile(i, v), 0)),
                  pl.BlockSpec((1, D_MODEL), lambda i, v: (0, 0)),
                  pl.BlockSpec((1, 2, D_MODEL), lambda i, v: (norm_tile(i, v) // tiles_per_batch, 0, 0)),
                  pl.BlockSpec((D_MODEL, PROJ_TN), lambda i, v: (0, _proj_w_tile(v, n_wide))),
                  pl.BlockSpec((D_MODEL, PROJ_TN),
                               lambda i, v: (0, jnp.where(v < n_wide, _proj_w_tile(v, n_wide) + 1, KV_SRC_TILE))),
                  pl.BlockSpec((8, D_MODEL), lambda i, v: (0, 0)),
                  pl.BlockSpec((D_MODEL, ADA_LATE_TN), lambda i, v: (0, early_tiles + late_tile(i, v))),
                  pl.BlockSpec((1, ADA_LATE_TN), lambda i, v: (0, early_tiles + late_tile(i, v)))],
        out_specs=[pl.BlockSpec((PROJ_TM, 2 * PROJ_TN), lambda i, v: (i, jnp.minimum(v, n_wide - 1))),
                   pl.BlockSpec((PROJ_TM, PROJ_TN), lambda i, v: (i, 0)),
                   pl.BlockSpec((8, ADA_LATE_TN), lambda i, v: (0, late_tile(i, v)))],
        out_shape=[jax.ShapeDtypeStruct((t, n - PROJ_TN), BF16),
                   jax.ShapeDtypeStruct((t, PROJ_TN), BF16),
                   jax.ShapeDtypeStruct((8, n_late), F32)],
        scratch_shapes=[pltpu.VMEM((PROJ_TM, D_MODEL), BF16), pltpu.VMEM((PROJ_TM, D_MODEL), BF16)],
        compiler_params=_params(("arbitrary", "arbitrary"), 60),
        name="proj",
    )(x2, gain, mod_early, w_in, w_in, c8, w_ada, b_ada)


def _attn_stages(sink_ref, q_ref, kvp_ref, kvc_ref, o_ref):
    n = pl.program_id(1)
    kvp = kvp_ref[...]
    kvc = kvc_ref[...]
    qi = lax.broadcasted_iota(jnp.int32, (WINDOW, WINDOW), 0)
    sj = lax.broadcasted_iota(jnp.int32, (WINDOW, WINDOW), 1)
    valid_prev = (sj > qi) & (n > 0)
    valid_cur = sj <= qi
    sink_col = sj == 0
    first_row = lax.broadcasted_iota(jnp.int32, (2 * WINDOW, 1), 0) == 0
    dh = ATTN_HEAD_DIM
    kv_w = ATTN_KV_HEADS * dh
    scale = jnp.asarray(dh ** -0.5, BF16)
    n_pairs = ATTN_GROUP // 2

    def group_operands(kv):
        kband = jnp.concatenate([kvp[:, kv * dh:(kv + 1) * dh],
                                 kvc[:, kv * dh:(kv + 1) * dh]], axis=0) * scale
        vband = jnp.concatenate([kvp[:, kv_w + kv * dh:kv_w + (kv + 1) * dh],
                                 kvc[:, kv_w + kv * dh:kv_w + (kv + 1) * dh]], axis=0)
        vband = jnp.where(first_row, jnp.zeros_like(vband), vband)
        zeros = jnp.zeros_like(kband)
        ones = jnp.ones_like(vband)
        k_pad = (jnp.concatenate([kband, zeros], axis=1), jnp.concatenate([zeros, kband], axis=1))
        pv_rhs = jnp.concatenate(
            [jnp.concatenate([vband, zeros, ones, zeros], axis=1),
             jnp.concatenate([zeros, vband, zeros, ones], axis=1)], axis=0)
        q_rows = jnp.concatenate(
            [q_ref[:, (kv * ATTN_GROUP + 2 * p) * dh:(kv * ATTN_GROUP + 2 * p + 2) * dh]
             for p in range(n_pairs)], axis=0)
        scores = [lax.dot_general(q_rows, k_pad[idx], (((1,), (1,)), ((), ())),
                                  preferred_element_type=F32) for idx in range(2)]
        return scores, pv_rhs

    nxt = group_operands(0)
    for kv in range(ATTN_KV_HEADS):
        scores, pv_rhs = nxt
        if kv + 1 < ATTN_KV_HEADS:
            nxt = group_operands(kv + 1)
        prob_rows = []
        for pair in range(n_pairs):
            rows = slice(pair * WINDOW, (pair + 1) * WINDOW)
            probs = []
            for idx in range(2):
                s = scores[idx][rows]
                sink = sink_ref[kv * ATTN_GROUP + 2 * pair + idx]
                s_prev = jnp.where(sink_col, sink, jnp.where(valid_prev, s[:, :WINDOW], NEG))
                s_cur = jnp.where(valid_cur, s[:, WINDOW:], NEG)
                m = jnp.max(jnp.maximum(s_prev, s_cur), axis=-1, keepdims=True)
                probs += [jnp.exp(s_prev - m).astype(BF16), jnp.exp(s_cur - m).astype(BF16)]
            prob_rows.append(jnp.concatenate(probs, axis=-1))
        r = jnp.dot(jnp.concatenate(prob_rows, axis=0), pv_rhs, preferred_element_type=F32)
        for pair in range(n_pairs):
            rows = slice(pair * WINDOW, (pair + 1) * WINDOW)
            col = (kv * ATTN_GROUP + 2 * pair) * dh
            o_ref[:, col:col + 2 * dh] = (r[rows, :2 * dh] * (1.0 / r[rows, 2 * dh:])).astype(BF16)
        yield


def _ret_decay_tables():
    lg = np.asarray(LOG_GAMMA, np.float64)[:, None, None]
    i = np.arange(RET_CHUNK, dtype=np.float64)
    diff = i[:, None] - i[None, :]
    k_scale = RET_DIM ** -0.5
    d_intra = np.where(diff >= 0, np.exp(np.maximum(diff, 0.0) * lg), 0.0) * k_scale
    lanes = np.ones((1, 1, RET_DIM // 2))
    d_q = np.exp((i[None, :, None] + 1.0) * lg) * lanes
    d_k = np.exp((RET_CHUNK - 1.0 - i[None, :, None]) * lg) * k_scale * lanes
    return jnp.asarray(d_intra, F32), jnp.asarray(d_q, F32), jnp.asarray(d_k, BF16)


def _ret_stages(pos_ref, invf_ref, di_ref, dq_ref, dk_ref, q_ref, k_ref, v_ref, g_ref, gain_ref,
                o_ref, state_ref):
    half = RET_DIM // 2
    ang = pos_ref[...].astype(F32) * invf_ref[...]
    cos = jnp.cos(ang).astype(BF16)
    sin = jnp.sin(ang).astype(BF16)

    def rot(t):
        t1, t2 = t[:, :half], t[:, half:]
        return jnp.concatenate([t1 * cos - t2 * sin, t1 * sin + t2 * cos], axis=-1)

    def both_halves(t, factor):
        return jnp.concatenate([t[:, :half] * factor, t[:, half:] * factor], axis=-1)

    def head_front(h):
        sl = slice(h * RET_DIM, (h + 1) * RET_DIM)
        qb = rot(q_ref[:, sl])
        kb = rot(k_ref[:, sl])
        intra = lax.dot_general(qb, kb, (((1,), (1,)), ((), ())),
                                preferred_element_type=F32) * di_ref[h]
        st = state_ref[h]
        cross = jnp.dot(qb, st.astype(BF16), preferred_element_type=F32)
        return kb, intra, st, cross

    front = head_front(0)
    for h in range(RET_HEADS):
        sl = slice(h * RET_DIM, (h + 1) * RET_DIM)
        kb, intra, st, cross = front
        if h + 1 < RET_HEADS:
            front = head_front(h + 1)
        vb = v_ref[:, sl]
        d_chunk = math.exp(RET_CHUNK * LOG_GAMMA[h])
        o = jnp.dot(intra.astype(BF16), vb, preferred_element_type=F32) + both_halves(cross, dq_ref[h])
        kd = both_halves(kb, dk_ref[h])
        state_ref[h] = st * d_chunk + lax.dot_general(kd, vb, (((0,), (0,)), ((), ())),
                                                      preferred_element_type=F32)
        o = o * lax.rsqrt(jnp.mean(o * o, axis=-1, keepdims=True) + EPS) * gain_ref[:, sl]
        gg = g_ref[:, sl].astype(F32)
        o_ref[:, sl] = (gg * jax.nn.sigmoid(gg) * o).astype(BF16)
        yield


def _mixers_kernel(sink_ref, qa_ref, kvp_ref, kvc_ref, pos_ref, invf_ref, di_ref, dq_ref, dk_ref,
                   qr_ref, kr_ref, vr_ref, gr_ref, gain_ref, attn_ref, ret_ref, state_ref):
    @pl.when(pl.program_id(1) == 0)
    def _():
        state_ref[...] = jnp.zeros_like(state_ref)

    attn = _attn_stages(sink_ref, qa_ref, kvp_ref, kvc_ref, attn_ref)
    ret = _ret_stages(pos_ref, invf_ref, di_ref, dq_ref, dk_ref, qr_ref, kr_ref, vr_ref, gr_ref,
                      gain_ref, ret_ref, state_ref)
    heads_per_group = RET_HEADS // ATTN_KV_HEADS
    for _ in range(ATTN_KV_HEADS):
        next(attn)
        for _ in range(heads_per_group):
            next(ret)


def _mixers(proj, proj_kv, sinks, pos, inv_freq, ret_gain, batch, seq):
    nb = seq // WINDOW
    t = batch * seq

    def col(cb):
        return lambda b, n: (b * nb + n, cb)

    d_intra, d_q, d_k = _ret_decay_tables()
    table = lambda lanes: pl.BlockSpec((RET_HEADS, RET_CHUNK, lanes), lambda b, n: (0, 0, 0))
    rows = pl.BlockSpec((WINDOW, D_MODEL), col(0))
    return pl.pallas_call(
        _mixers_kernel,
        grid=(batch, nb),
        in_specs=[pl.BlockSpec(memory_space=pltpu.SMEM),
                  pl.BlockSpec((WINDOW, D_MODEL), col(COL_QA)),
                  pl.BlockSpec((WINDOW, PROJ_TN), lambda b, n: (b * nb + jnp.maximum(n - 1, 0), 0)),
                  pl.BlockSpec((WINDOW, PROJ_TN), col(0)),
                  pl.BlockSpec((RET_CHUNK, 1), col(0)),
                  pl.BlockSpec((1, RET_DIM // 2), lambda b, n: (0, 0)),
                  table(RET_CHUNK), table(RET_DIM // 2), table(RET_DIM // 2),
                  pl.BlockSpec((RET_CHUNK, D_MODEL), col(COL_QR)),
                  pl.BlockSpec((RET_CHUNK, D_MODEL), col(COL_KR)),
                  pl.BlockSpec((RET_CHUNK, D_MODEL), col(COL_VR)),
                  pl.BlockSpec((RET_CHUNK, D_MODEL), col(COL_GR)),
                  pl.BlockSpec((1, D_MODEL), lambda b, n: (0, 0))],
        out_specs=[rows, rows],
        out_shape=[jax.ShapeDtypeStruct((t, D_MODEL), BF16), jax.ShapeDtypeStruct((t, D_MODEL), BF16)],
        scratch_shapes=[pltpu.VMEM((RET_HEADS, RET_DIM, RET_DIM), F32)],
        compiler_params=_params(("arbitrary", "arbitrary"), 40),
        name="mixers",
    )(sinks, proj, proj_kv, proj_kv, pos, inv_freq, d_intra, d_q, d_k, proj, proj, proj, proj, ret_gain)


def _route(logits):
    lane = lax.broadcasted_iota(jnp.int32, logits.shape, 1)
    lane_f = lane.astype(F32)
    is_g = lane < N_GROUPS
    gl = jnp.where(is_g, logits, NEG)
    gmax = jnp.max(gl, axis=-1, keepdims=True)
    gsel = jnp.min(jnp.where(gl == gmax, lane_f, float(LANES)), axis=-1, keepdims=True)
    gsum = jnp.sum(jnp.where(is_g, jnp.exp(gl - gmax), 0.0), axis=-1, keepdims=True)
    g_w = 1.0 / gsum
    grp = ((lane - N_GROUPS) >> 4).astype(F32)
    is_e = (lane >= N_GROUPS) & (lane < N_GROUPS + N_EXPERTS) & (grp == gsel)
    el = jnp.where(is_e, logits, NEG)
    v1 = jnp.max(el, axis=-1, keepdims=True)
    i1 = jnp.min(jnp.where(el == v1, lane_f, float(LANES)), axis=-1, keepdims=True)
    el2 = jnp.where(lane_f == i1, NEG, el)
    v2 = jnp.max(el2, axis=-1, keepdims=True)
    i2 = jnp.min(jnp.where(el2 == v2, lane_f, float(LANES)), axis=-1, keepdims=True)
    tt = jnp.exp(v2 - v1)
    w1 = g_w / (1.0 + tt)
    w2 = g_w * tt / (1.0 + tt)
    return jnp.where(lane == 0, i1 - N_GROUPS,
                     jnp.where(lane == 1, i2 - N_GROUPS,
                               jnp.where(lane == 2, w1, jnp.where(lane == 3, w2, 0.0))))


def _branch_kernel(attn_ref, ret_ref, ga_ref, gr_ref, wa_ref, wr_ref, o_ref):
    a = jnp.dot(attn_ref[...], wa_ref[...].astype(BF16), preferred_element_type=F32)
    r = jnp.dot(ret_ref[...], wr_ref[...].astype(BF16), preferred_element_type=F32)
    o_ref[...] = (jax.nn.sigmoid(ga_ref[...].astype(F32)) * a
                  + jax.nn.sigmoid(gr_ref[...].astype(F32)) * r).astype(BF16)


def _branch(attn, ret, proj, wa, wr):
    t = attn.shape[0]
    tm, tn = BRANCH_TM, BRANCH_TN
    per_slab = D_MODEL // tn
    return pl.pallas_call(
        _branch_kernel,
        grid=(t // tm, D_MODEL // tn),
        in_specs=[pl.BlockSpec((tm, D_MODEL), lambda i, j: (i, 0)),
                  pl.BlockSpec((tm, D_MODEL), lambda i, j: (i, 0)),
                  pl.BlockSpec((tm, tn), lambda i, j: (i, COL_GA * per_slab + j)),
                  pl.BlockSpec((tm, tn), lambda i, j: (i, COL_GRT * per_slab + j)),
                  pl.BlockSpec((D_MODEL, tn), lambda i, j: (0, j)),
                  pl.BlockSpec((D_MODEL, tn), lambda i, j: (0, j))],
        out_specs=pl.BlockSpec((tm, tn), lambda i, j: (i, j)),
        out_shape=jax.ShapeDtypeStruct((t, D_MODEL), BF16),
        compiler_params=_params(("arbitrary", "arbitrary"), 48),
        name="branch",
    )(attn, ret, proj, proj, wa, wr)


def _mixout_kernel(m_ref, x_ref, mod_ref, g2_ref, wo_ref, wrt_ref, brt_ref,
                   x1_ref, h2_ref, route_ref, count_ref, meta_ref, carry_ref):
    @pl.when(pl.program_id(0) == 0)
    def _():
        carry_ref[...] = jnp.zeros_like(carry_ref)

    mix = jnp.dot(m_ref[...], wo_ref[...], preferred_element_type=F32)
    x1 = x_ref[...] + mod_ref[0, 2:3, :] * mix
    x1_ref[...] = x1
    var = jnp.mean(x1 * x1, axis=-1, keepdims=True)
    h2 = x1 * lax.rsqrt(var + EPS) * g2_ref[...]
    h2 = h2 * (1.0 + mod_ref[0, 4:5, :]) + mod_ref[0, 3:4, :]
    tm = h2.shape[0]
    for j in range(SLAB_ROWS):
        lo = h2[:, j * LANES:(j + 1) * LANES]
        hi = h2[:, HALF_D + j * LANES:HALF_D + (j + 1) * LANES]
        h2_ref[_slab_rows(j, tm), :] = _pack_pair(lo, hi)
    h_hi = h2.astype(BF16)
    h_lo = (h2 - h_hi.astype(F32)).astype(BF16)
    w_rt = wrt_ref[...]
    w_hi = w_rt.astype(BF16)
    w_lo = (w_rt - w_hi.astype(F32)).astype(BF16)
    hi_both = jnp.dot(h_hi, jnp.concatenate([w_hi, w_lo], axis=1), preferred_element_type=F32)
    logits = (hi_both[:, :LANES] + hi_both[:, LANES:]
              + jnp.dot(h_lo, w_hi, preferred_element_type=F32) + brt_ref[...])
    route = _route(logits)

    lane = lax.broadcasted_iota(jnp.int32, route.shape, 1)
    lane_f = lane.astype(F32)
    hot1 = lane_f == route[:, 0:1]
    hot2 = lane_f == route[:, 1:2]
    both = jnp.where(hot1 | hot2, 1.0, 0.0)
    ii = lax.broadcasted_iota(jnp.int32, (tm, tm), 0)
    jj = lax.broadcasted_iota(jnp.int32, (tm, tm), 1)
    lower = jnp.where(ii > jj, 1.0, 0.0).astype(BF16)
    before = jnp.dot(lower, both.astype(BF16), preferred_element_type=F32) + carry_ref[...]
    r1 = jnp.sum(jnp.where(hot1, before, 0.0), axis=-1, keepdims=True)
    r2 = jnp.sum(jnp.where(hot2, before, 0.0), axis=-1, keepdims=True)
    route = jnp.where(lane == 4, r1, jnp.where(lane == 5, r2, route))
    route_ref[...] = route
    meta_ref[...] = route.T[0:SLAB_ROWS, :].astype(I32)
    carry = carry_ref[...] + jnp.sum(both, axis=0, keepdims=True)
    carry_ref[...] = carry
    count_ref[...] = carry


def _mixout(merged, x2, mod6, gain2, wo, w_rt, b_rt, seq):
    t = x2.shape[0]
    tm = MIXOUT_TM
    tiles_per_batch = seq // tm
    row = lambda i: (i, 0)
    const = lambda i: (0, 0)
    return pl.pallas_call(
        _mixout_kernel,
        grid=(t // tm,),
        in_specs=[pl.BlockSpec((tm, D_MODEL), row),
                  pl.BlockSpec((tm, D_MODEL), row),
                  pl.BlockSpec((1, 6, D_MODEL), lambda i: (i // tiles_per_batch, 0, 0)),
                  pl.BlockSpec((1, D_MODEL), const),
                  pl.BlockSpec((D_MODEL, D_MODEL), const, pipeline_mode=pl.Buffered(1)),
                  pl.BlockSpec((D_MODEL, LANES), const),
                  pl.BlockSpec((1, LANES), const)],
        out_specs=[pl.BlockSpec((tm, D_MODEL), row),
                   pl.BlockSpec((tm * SLAB_ROWS, SLAB_LANES), row),
                   pl.BlockSpec((tm, LANES), row),
                   pl.BlockSpec((1, LANES), const),
                   pl.BlockSpec((SLAB_ROWS, tm), lambda i: (0, i))],
        out_shape=[jax.ShapeDtypeStruct((t, D_MODEL), F32),
                   jax.ShapeDtypeStruct((t * SLAB_ROWS, SLAB_LANES), I32),
                   jax.ShapeDtypeStruct((t, LANES), F32),
                   jax.ShapeDtypeStruct((1, LANES), F32),
                   jax.ShapeDtypeStruct((SLAB_ROWS, t), I32)],
        scratch_shapes=[pltpu.VMEM((1, LANES), F32)],
        compiler_params=_params(("arbitrary",), 56),
        name="mixout",
    )(merged, x2, mod6, gain2, wo, w_rt, b_rt)


PAD_BITS = (64, 32, 16, 8, 4, 2, 1)


def _sorted_row(tables, tok, k):
    e_refs, r_refs, blk_row_ref = tables[0:TOP_K], tables[TOP_K:2 * TOP_K], tables[2 * TOP_K]
    return blk_row_ref[e_refs[k][tok]] + r_refs[k][tok]


N_ROUTE_TABLES = 2 * TOP_K + 1


def _dispatch_kernel(*refs):
    tables = refs[:N_ROUTE_TABLES]
    zstart_ref, zcount_ref, h2_ref, xs_ref, dest_ref, zero_ref, sem, zsem = refs[N_ROUTE_TABLES:]
    n_assign = tables[0].shape[0] * TOP_K
    zero_ref[...] = jnp.zeros_like(zero_ref)

    def zero_copy(start, rows):
        return pltpu.make_async_copy(zero_ref.at[pl.ds(0, rows * SLAB_ROWS), :],
                                     xs_ref.at[pl.ds(start * SLAB_ROWS, rows * SLAB_ROWS), :], zsem)

    def fill(e, wait):
        start = zstart_ref[e]
        pad = zcount_ref[e]
        for bit in PAD_BITS:
            @pl.when((pad & bit) != 0)
            def _(start=start, bit=bit):
                cp = zero_copy(start, bit)
                cp.wait() if wait else cp.start()
            start = start + (pad & bit)

    lax.fori_loop(0, N_EXPERTS, lambda e, c: (fill(e, False), c)[1], 0)

    def issue(tok, carry):
        src = h2_ref.at[pl.ds(pl.multiple_of(tok * SLAB_ROWS, SLAB_ROWS), SLAB_ROWS), :]
        for k in range(TOP_K):
            row = _sorted_row(tables, tok, k)
            dest_ref[tok * TOP_K + k] = row
            dst = pl.multiple_of(row * SLAB_ROWS, SLAB_ROWS)
            pltpu.make_async_copy(src, xs_ref.at[pl.ds(dst, SLAB_ROWS), :], sem).start()
        return carry

    lax.fori_loop(0, n_assign // TOP_K, issue, 0, unroll=8)

    def drain(i, carry):
        pltpu.make_async_copy(h2_ref.at[pl.ds(0, ROWS_PER_WAIT * SLAB_ROWS), :],
                              xs_ref.at[pl.ds(0, ROWS_PER_WAIT * SLAB_ROWS), :], sem).wait()
        return carry

    lax.fori_loop(0, n_assign // ROWS_PER_WAIT, drain, 0)
    lax.fori_loop(0, N_EXPERTS, lambda e, c: (fill(e, True), c)[1], 0)


def _dispatch(tables, zstart, zcount, h2_slab, n_pad):
    n_assign = tables[0].shape[0] * TOP_K
    return pl.pallas_call(
        _dispatch_kernel,
        grid_spec=pltpu.PrefetchScalarGridSpec(
            num_scalar_prefetch=N_ROUTE_TABLES + 2,
            grid=(1,),
            in_specs=[pl.BlockSpec(memory_space=pltpu.HBM)],
            out_specs=[pl.BlockSpec(memory_space=pl.ANY),
                       pl.BlockSpec(memory_space=pltpu.SMEM)],
            scratch_shapes=[pltpu.VMEM((PAD_BITS[0] * SLAB_ROWS, SLAB_LANES), I32),
                            pltpu.SemaphoreType.DMA(()),
                            pltpu.SemaphoreType.DMA(())]),
        out_shape=[jax.ShapeDtypeStruct((n_pad * SLAB_ROWS, SLAB_LANES), I32),
                   jax.ShapeDtypeStruct((n_assign,), I32)],
        compiler_params=_params(("arbitrary",), 16),
        name="dispatch",
    )(*tables, zstart, zcount, h2_slab)


def _experts_kernel(item_e_ref, item_blk_ref, item_nb_ref, x_blk_ref,
                    x0_ref, x1_ref, x2_ref, x3_ref, wg_hbm, wu_hbm, wd_hbm, y_ref,
                    xb_ref, yp_ref, wg_ref, wu_ref, wd_ref, sem, wsem):
    w = pl.program_id(0)
    n_items = pl.num_programs(0)
    nb = item_nb_ref[w]
    blk0 = item_blk_ref[w]
    x_refs = (x0_ref, x1_ref, x2_ref, x3_ref)

    blk_rows = MOE_BLOCK * SLAB_ROWS

    fc = wg_ref.shape[2]
    n_chunks = EXPERT_DIM // fc

    def weight_copies(item, chunk, slot):
        e = item_e_ref[item]
        col = pl.multiple_of(chunk * fc, fc)
        return (pltpu.make_async_copy(wg_hbm.at[e, :, pl.ds(col, fc)], wg_ref.at[slot], wsem.at[slot]),
                pltpu.make_async_copy(wu_hbm.at[e, :, pl.ds(col, fc)], wu_ref.at[slot], wsem.at[slot]),
                pltpu.make_async_copy(wd_hbm.at[e, pl.ds(col, fc), :], wd_ref.at[slot], wsem.at[slot]))

    def start_weights(tile):
        item = jnp.minimum(tile // n_chunks, n_items - 1)

        @pl.when((tile < n_items * n_chunks) & (item_nb_ref[item] > 0))
        def _():
            for cp in weight_copies(item, tile % n_chunks, tile % EXPERT_WEIGHT_SLOTS):
                cp.start()

    @pl.when(w == 0)
    def _():
        for tile in range(EXPERT_WEIGHT_SLOTS - 1):
            start_weights(tile)

    def out_copy(s):
        return pltpu.make_async_copy(yp_ref.at[s], y_ref.at[pl.ds((blk0 + s) * blk_rows, blk_rows), :],
                                     sem.at[s])

    def wait_out(count):
        for s in range(ITEM_BLOCKS):
            @pl.when(s < count)
            def _(s=s):
                out_copy(s).wait()

    def run(n_live):
        rows = n_live * MOE_BLOCK
        for s in range(n_live):
            for j in range(SLAB_ROWS):
                lo, hi = _unpack_pair(x_refs[s][_slab_rows(j, MOE_BLOCK), :])
                r0 = s * MOE_BLOCK
                xb_ref[r0:r0 + MOE_BLOCK, j * LANES:(j + 1) * LANES] = lo.astype(BF16)
                xb_ref[r0:r0 + MOE_BLOCK, HALF_D + j * LANES:HALF_D + (j + 1) * LANES] = hi.astype(BF16)
        xs = xb_ref[0:rows, :]

        y = None
        for c in range(n_chunks):
            tile = w * n_chunks + c
            start_weights(tile + EXPERT_WEIGHT_SLOTS - 1)
            slot = tile % EXPERT_WEIGHT_SLOTS
            for cp in weight_copies(w, c, slot):
                cp.wait()
            g = jnp.dot(xs, wg_ref[slot].astype(BF16), preferred_element_type=F32)
            u = jnp.dot(xs, wu_ref[slot].astype(BF16), preferred_element_type=F32)
            hid = (g * jax.nn.sigmoid(g) * u).astype(BF16)
            y_c = jnp.dot(hid, wd_ref[slot].astype(BF16), preferred_element_type=F32)
            y = y_c if y is None else y + y_c

        @pl.when(w > 0)
        def _():
            wait_out(item_nb_ref[jnp.maximum(w - 1, 0)])

        for s in range(n_live):
            r0 = s * MOE_BLOCK
            for j in range(SLAB_ROWS):
                yp_ref[s, _slab_rows(j, MOE_BLOCK), :] = _pack_pair(
                    y[r0:r0 + MOE_BLOCK, j * LANES:(j + 1) * LANES],
                    y[r0:r0 + MOE_BLOCK, HALF_D + j * LANES:HALF_D + (j + 1) * LANES])
            out_copy(s).start()

    for n_live in range(1, ITEM_BLOCKS + 1):
        pl.when(nb == n_live)(functools.partial(run, n_live))

    @pl.when((nb == 0) & (w > 0))
    def _():
        wait_out(item_nb_ref[jnp.maximum(w - 1, 0)])

    @pl.when(w == n_items - 1)
    def _():
        wait_out(nb)


def _experts(item_e, item_blk, item_nb, x_sorted, w_gate, w_up, w_down, n_blocks):
    n_items = item_e.shape[0]
    d = D_MODEL
    fc = EXPERT_FC
    x3 = x_sorted
    blk_rows = MOE_BLOCK * SLAB_ROWS

    slot = jnp.arange(ITEM_BLOCKS, dtype=jnp.int32)[:, None]
    x_blk = jnp.maximum(lax.cummax(jnp.where(slot < item_nb[None, :], item_blk[None, :] + slot, -1), axis=1), 0)
    x_blk = x_blk.reshape(-1).astype(jnp.int32)

    def x_spec(s):
        return pl.BlockSpec((blk_rows, SLAB_LANES),
                            lambda w, ie, ib, inb, xb: (xb[s * n_items + w], 0))

    hbm = pl.BlockSpec(memory_space=pl.ANY)
    slots = EXPERT_WEIGHT_SLOTS
    return pl.pallas_call(
        _experts_kernel,
        grid_spec=pltpu.PrefetchScalarGridSpec(
            num_scalar_prefetch=4,
            grid=(n_items,),
            in_specs=[x_spec(0), x_spec(1), x_spec(2), x_spec(3), hbm, hbm, hbm],
            out_specs=pl.BlockSpec(memory_space=pl.ANY),
            scratch_shapes=[pltpu.VMEM((ITEM_BLOCKS * MOE_BLOCK, d), BF16),
                            pltpu.VMEM((ITEM_BLOCKS, blk_rows, SLAB_LANES), I32),
                            pltpu.VMEM((slots, d, fc), F32),
                            pltpu.VMEM((slots, d, fc), F32),
                            pltpu.VMEM((slots, fc, d), F32),
                            pltpu.SemaphoreType.DMA((ITEM_BLOCKS,)),
                            pltpu.SemaphoreType.DMA((slots,))]),
        out_shape=jax.ShapeDtypeStruct((n_blocks * blk_rows, SLAB_LANES), I32),
        compiler_params=_params(("arbitrary",), 56),
        name="experts",
    )(item_e, item_blk, item_nb, x_blk, x3, x3, x3, x3, w_gate, w_up, w_down)


def _combine_kernel(dest_ref, x1_ref, route_ref, mod_ref, gain_ref, y_ref, o_ref, ybuf_ref, sem):
    tm = COMBINE_TM
    i = pl.program_id(0)
    n_tiles = pl.num_programs(0)

    def row_copy(slot, k, r, src):
        return pltpu.make_async_copy(
            y_ref.at[pl.ds(pl.multiple_of(src * SLAB_ROWS, SLAB_ROWS), SLAB_ROWS), :],
            ybuf_ref.at[slot, k, pl.ds(pl.multiple_of(r * SLAB_ROWS, SLAB_ROWS), SLAB_ROWS), :],
            sem.at[slot])

    def wait_slot(slot):
        for k in range(TOP_K):
            pltpu.make_async_copy(y_ref.at[pl.ds(0, tm * SLAB_ROWS), :], ybuf_ref.at[slot, k],
                                  sem.at[slot]).wait()

    def issue_tile(tile, slot):
        base = tile * tm * TOP_K

        def issue(r, carry):
            for k in range(TOP_K):
                row_copy(slot, k, r, dest_ref[base + r * TOP_K + k]).start()
            return carry

        lax.fori_loop(0, tm, issue, 0, unroll=4)

    @pl.when(i == 0)
    def _():
        for tile in range(COMBINE_SLOTS - 1):
            issue_tile(tile, tile)

    slot = i % COMBINE_SLOTS
    wait_slot(slot)

    ahead = i + COMBINE_SLOTS - 1
    ahead_slot = ahead % COMBINE_SLOTS
    ahead_base = jnp.minimum(ahead, n_tiles - 1) * (tm * TOP_K)
    n_stages = 2 * SLAB_ROWS
    rows_per_stage = tm // n_stages

    route = route_ref[...]
    w0 = route[:, 2:3]
    w1 = route[:, 3:4]
    ssq = jnp.zeros((tm, 1), F32)
    stage = 0
    for j in range(SLAB_ROWS):
        rows = _slab_rows(j, tm)
        y0 = _unpack_pair(ybuf_ref[slot, 0, rows, :])
        y1 = _unpack_pair(ybuf_ref[slot, 1, rows, :])
        for part, off in ((0, j * LANES), (1, HALF_D + j * LANES)):
            cols = slice(off, off + LANES)
            ffn = w0 * y0[part] + w1 * y1[part]
            x2 = x1_ref[:, cols] + mod_ref[0, 5:6, cols] * ffn
            o_ref[:, cols] = x2
            ssq = ssq + jnp.sum(x2 * x2, axis=-1, keepdims=True)
            for r in range(stage * rows_per_stage, (stage + 1) * rows_per_stage):
                for k in range(TOP_K):
                    row_copy(ahead_slot, k, r, dest_ref[ahead_base + r * TOP_K + k]).start()
            stage += 1
    o_ref[...] = o_ref[...] * lax.rsqrt(ssq * (1.0 / D_MODEL) + EPS) * gain_ref[...]

    @pl.when(i == n_tiles - 1)
    def _():
        for extra in range(1, COMBINE_SLOTS):
            wait_slot((i + extra) % COMBINE_SLOTS)


def _combine(dest, x1, route, mod6, gain, y_sorted, seq):
    t, d = x1.shape
    tm = COMBINE_TM
    tiles_per_batch = seq // tm
    return pl.pallas_call(
        _combine_kernel,
        grid_spec=pltpu.PrefetchScalarGridSpec(
            num_scalar_prefetch=1,
            grid=(t // tm,),
            in_specs=[pl.BlockSpec((tm, d), lambda i, *_: (i, 0)),
                      pl.BlockSpec((tm, LANES), lambda i, *_: (i, 0)),
                      pl.BlockSpec((1, 6, d), lambda i, *_: (i // tiles_per_batch, 0, 0)),
                      pl.BlockSpec((1, d), lambda i, *_: (0, 0)),
                      pl.BlockSpec(memory_space=pl.ANY)],
            out_specs=pl.BlockSpec((tm, d), lambda i, *_: (i, 0)),
            scratch_shapes=[pltpu.VMEM((COMBINE_SLOTS, TOP_K, tm * SLAB_ROWS, SLAB_LANES), I32),
                            pltpu.SemaphoreType.DMA((COMBINE_SLOTS,))]),
        out_shape=jax.ShapeDtypeStruct((t, d), F32),
        compiler_params=_params(("arbitrary",), 24),
        name="combine",
    )(dest, x1, route, mod6, gain, y_sorted)


def _dispatch_tables(meta, counts, t):
    n_assign = t * TOP_K
    n_pad = -(-(n_assign + N_EXPERTS * (MOE_BLOCK - 1)) // MOE_BLOCK) * MOE_BLOCK
    n_blocks = n_pad // MOE_BLOCK
    n_items = N_EXPERTS + n_assign // (ITEM_BLOCKS * MOE_BLOCK)

    cnt = counts[0, :N_EXPERTS].astype(jnp.int32)
    blocks_e = (cnt + MOE_BLOCK - 1) // MOE_BLOCK
    blk_end = jnp.cumsum(blocks_e)
    blk_start = blk_end - blocks_e
    tables = (meta[0], meta[1], meta[4], meta[5], (blk_start * MOE_BLOCK).astype(jnp.int32))
    zstart = (blk_start * MOE_BLOCK + cnt).astype(jnp.int32)
    zcount = (blocks_e * MOE_BLOCK - cnt).astype(jnp.int32)

    items_e = (blocks_e + ITEM_BLOCKS - 1) // ITEM_BLOCKS
    item_end = jnp.cumsum(items_e)
    item_start = item_end - items_e
    w = jnp.arange(n_items, dtype=jnp.int32)
    live = w < item_end[-1]
    w_live = jnp.minimum(w, item_end[-1] - 1)
    e_w = jnp.minimum(jnp.sum((item_end[None, :] <= w_live[:, None]).astype(jnp.int32), axis=1), N_EXPERTS - 1)
    j_w = w_live - item_start[e_w]
    item_blk = (blk_start[e_w] + ITEM_BLOCKS * j_w).astype(jnp.int32)
    item_nb = jnp.where(live, jnp.clip(blocks_e[e_w] - ITEM_BLOCKS * j_w, 0, ITEM_BLOCKS), 0).astype(jnp.int32)
    return tables, zstart, zcount, e_w, item_blk, item_nb, n_blocks


def kernel(x, c, positions, norm1_gain, norm2_gain, final_norm_gain, w_ada, b_ada, w_in, attn_sinks,
           ret_norm_gain, w_branch_attn, w_branch_ret, w_out, w_router_group, b_router_group,
           w_router_expert, b_router_expert, w_expert_gate, w_expert_up, w_expert_down):
    batch, seq, d = x.shape
    t = batch * seq
    depth = w_ada.shape[0]
    half = RET_DIM // 2
    inv_freq = (ROPE_BASE ** (-jnp.arange(half, dtype=F32) / half)).reshape(1, half)
    pos = positions.reshape(t, 1)
    c8 = jnp.pad(c, ((0, 8 - batch), (0, 0)))
    xf = x.reshape(t, d)

    assert depth == 1, "the fused final norm assumes a single layer"
    for layer in range(depth):
        b_ada2 = b_ada[layer].reshape(1, -1)
        mod_early = _ada(c8, w_ada[layer], b_ada2, ADA_EARLY)[:batch]
        proj, proj_kv, mod_late = _proj(xf, norm1_gain[layer].reshape(1, d), mod_early.reshape(batch, 2, d),
                                        w_in[layer], c8, w_ada[layer], b_ada2, seq)
        mod6 = jnp.concatenate([mod_early, mod_late[:batch]], axis=1).reshape(batch, 6, d)
        attn, ret = _mixers(proj, proj_kv, attn_sinks[layer], pos, inv_freq,
                            ret_norm_gain[layer].reshape(1, d), batch, seq)

        pad = LANES - N_GROUPS - N_EXPERTS
        w_rt = jnp.concatenate([w_router_group[layer], w_router_expert[layer],
                                jnp.zeros((d, pad), F32)], axis=1)
        b_rt = jnp.concatenate([b_router_group[layer], b_router_expert[layer],
                                jnp.zeros((pad,), F32)]).reshape(1, LANES)
        merged = _branch(attn, ret, proj, w_branch_attn[layer], w_branch_ret[layer])
        x1, h2, route, counts, meta = _mixout(merged, xf, mod6, norm2_gain[layer].reshape(1, d),
                                              w_out[layer].astype(BF16), w_rt, b_rt, seq)
        tables, zstart, zcount, item_e, item_blk, item_nb, n_blocks = _dispatch_tables(meta, counts, t)
        x_sorted, dest = _dispatch(tables, zstart, zcount, h2, n_blocks * MOE_BLOCK)
        y_sorted = _experts(item_e, item_blk, item_nb, x_sorted,
                            w_expert_gate[layer], w_expert_up[layer], w_expert_down[layer], n_blocks)
        xf = _combine(dest, x1, route, mod6, final_norm_gain.reshape(1, d), y_sorted, seq)
    return xf.reshape(batch, seq, d)
```

```python
import functools
import math

import jax
import jax.numpy as jnp
import numpy as np
from jax import lax
from jax.experimental import pallas as pl
from jax.experimental.pallas import tpu as pltpu

F32 = jnp.float32
BF16 = jnp.bfloat16

D_MODEL = 2048
ATTN_HEAD_DIM = 64
ATTN_HEADS = 32
ATTN_KV_HEADS = 4
ATTN_GROUP = 8
WINDOW = 128
RET_HEADS = 8
RET_DIM = 256
RET_CHUNK = 128
ROPE_BASE = 10000.0
N_GROUPS = 4
EXPERTS_PER_GROUP = 16
N_EXPERTS = 64
TOP_K = 2
EXPERT_DIM = 1024
MOE_BLOCK = 128
EPS = 1e-6
NEG = -1e30

MIB = 1024 * 1024
LANES = 128
PROJ_TN = 512
PROJ_TM = 1024
KV_SRC_TILE = 4
ADA_EARLY = 2 * D_MODEL
ADA_LATE_TN = 256
ADA_LATE_STEPS = 4 * D_MODEL // ADA_LATE_TN
BRANCH_TM = 1024
BRANCH_TN = 512
MIXOUT_TM = 512
ITEM_BLOCKS = 4
EXPERT_FC = 512
EXPERT_WEIGHT_SLOTS = 3
COMBINE_TM = 512
COMBINE_SLOTS = 3
SLAB_ROWS = 8
SLAB_LANES = LANES
HALF_D = D_MODEL // 2
ROWS_PER_WAIT = 128
I32 = jnp.int32

COL_QA, COL_QR, COL_KR, COL_VR, COL_GR, COL_GA, COL_GRT = 0, 1, 2, 3, 4, 5, 6

LOG_GAMMA = [math.log1p(-(2.0 ** (-5.0 - h))) for h in range(RET_HEADS)]


def _params(sem, vmem_mib):
    return pltpu.CompilerParams(dimension_semantics=sem, vmem_limit_bytes=vmem_mib * MIB)


def _pack_pair(lo, hi):
    lo_b = lax.bitcast_convert_type(lo.astype(BF16).astype(F32), I32)
    hi_b = lax.bitcast_convert_type(hi.astype(BF16).astype(F32), I32)
    return hi_b | lax.shift_right_logical(lo_b, jnp.full_like(lo_b, 16))


def _unpack_pair(w):
    lo = lax.bitcast_convert_type(w << 16, F32)
    hi = lax.bitcast_convert_type(w & jnp.int32(-65536), F32)
    return lo, hi


def _slab_rows(j, n_tokens):
    return pl.ds(j, n_tokens, stride=SLAB_ROWS)


def _ada_kernel(c_ref, w_ref, b_ref, o_ref):
    c = c_ref[...]
    a = (c * jax.nn.sigmoid(c)).astype(BF16)
    o_ref[...] = jnp.dot(a, w_ref[...].astype(BF16), preferred_element_type=F32) + b_ref[...]


def _ada(c8, w_ada, b_ada, n):
    tn = 1024
    return pl.pallas_call(
        _ada_kernel,
        grid=(n // tn,),
        in_specs=[pl.BlockSpec((8, D_MODEL), lambda j: (0, 0)),
                  pl.BlockSpec((D_MODEL, tn), lambda j: (0, j)),
                  pl.BlockSpec((1, tn), lambda j: (0, j))],
        out_specs=pl.BlockSpec((8, tn), lambda j: (0, j)),
        out_shape=jax.ShapeDtypeStruct((8, n), F32),
        compiler_params=_params(("arbitrary",), 40),
        name="ada",
    )(c8, w_ada, b_ada)


def _proj_kernel(x_ref, g_ref, mod_ref, wlo_ref, whi_ref, c_ref, wada_ref, bada_ref,
                 o_ref, kv_ref, late_ref, h0_ref, h1_ref):
    i = pl.program_id(0)
    v = pl.program_id(1)
    last = pl.num_programs(1) - 1

    def normed():
        x = x_ref[...]
        var = jnp.mean(x * x, axis=-1, keepdims=True)
        y = x * lax.rsqrt(var + EPS) * g_ref[...]
        return (y * (1.0 + mod_ref[0, 1:2, :]) + mod_ref[0, 0:1, :]).astype(BF16)

    @pl.when(pl.program_id(0) * pl.num_programs(1) + v < ADA_LATE_STEPS)
    def _():
        _ada_kernel(c_ref, wada_ref, bada_ref, late_ref)

    @pl.when((i == 0) & (v == 0))
    def _():
        h0_ref[...] = normed()

    for parity, (h_ref, h_next_ref) in enumerate(((h0_ref, h1_ref), (h1_ref, h0_ref))):
        @pl.when((v < last) & (i % 2 == parity))
        def _(h_ref=h_ref):
            w = jnp.concatenate([wlo_ref[...].astype(BF16), whi_ref[...].astype(BF16)], axis=1)
            o_ref[...] = jnp.dot(h_ref[...], w, preferred_element_type=F32).astype(BF16)

        @pl.when((v == last) & (i % 2 == parity))
        def _(h_ref=h_ref, h_next_ref=h_next_ref):
            kv_ref[...] = jnp.dot(h_ref[...], wlo_ref[...].astype(BF16),
                                  preferred_element_type=F32).astype(BF16)
            h_next_ref[...] = normed()


def _proj_w_tile(v, n_wide):
    return jnp.where(v < 2, 2 * v, jnp.where(v < n_wide, 2 * v + 1, KV_SRC_TILE))


def _proj(x2, gain, mod_early, w_in, c8, w_ada, b_ada, seq):
    t = x2.shape[0]
    n = w_in.shape[1]
    tiles_per_batch = seq // PROJ_TM
    n_wide = (n - PROJ_TN) // (2 * PROJ_TN)
    n_late = w_ada.shape[1] - ADA_EARLY
    assert n_late == ADA_LATE_STEPS * ADA_LATE_TN and ADA_LATE_STEPS <= (t // PROJ_TM) * (n_wide + 1)
    early_tiles = ADA_EARLY // ADA_LATE_TN

    def late_tile(i, v):
        return jnp.minimum(i * (n_wide + 1) + v, ADA_LATE_STEPS - 1)

    n_row_tiles = t // PROJ_TM

    def norm_tile(i, v):
        return jnp.minimum(i + (v == n_wide).astype(jnp.int32), n_row_tiles - 1)

    return pl.pallas_call(
        _proj_kernel,
        grid=(n_row_tiles, n_wide + 1),
        in_specs=[pl.BlockSpec((PROJ_TM, D_MODEL), lambda i, v: (norm_tile(i, v), 0)),
                  pl.BlockSpec((1, D_MODEL), lambda i, v: (0, 0)),
                  pl.BlockSpec((1, 2, D_MODEL), lambda i, v: (norm_tile(i, v) // tiles_per_batch, 0, 0)),
                  pl.BlockSpec((D_MODEL, PROJ_TN), lambda i, v: (0, _proj_w_tile(v, n_wide))),
                  pl.BlockSpec((D_MODEL, PROJ_TN),
                               lambda i, v: (0, jnp.where(v < n_wide, _proj_w_tile(v, n_wide) + 1, KV_SRC_TILE))),
                  pl.BlockSpec((8, D_MODEL), lambda i, v: (0, 0)),
                  pl.BlockSpec((D_MODEL, ADA_LATE_TN), lambda i, v: (0, early_tiles + late_tile(i, v))),
                  pl.BlockSpec((1, ADA_LATE_TN), lambda i, v: (0, early_tiles + late_tile(i, v)))],
        out_specs=[pl.BlockSpec((PROJ_TM, 2 * PROJ_TN), lambda i, v: (i, jnp.minimum(v, n_wide - 1))),
                   pl.BlockSpec((PROJ_TM, PROJ_TN), lambda i, v: (i, 0)),
                   pl.BlockSpec((8, ADA_LATE_TN), lambda i, v: (0, late_tile(i, v)))],
        out_shape=[jax.ShapeDtypeStruct((t, n - PROJ_TN), BF16),
                   jax.ShapeDtypeStruct((t, PROJ_TN), BF16),
                   jax.ShapeDtypeStruct((8, n_late), F32)],
        scratch_shapes=[pltpu.VMEM((PROJ_TM, D_MODEL), BF16), pltpu.VMEM((PROJ_TM, D_MODEL), BF16)],
        compiler_params=_params(("arbitrary", "arbitrary"), 60),
        name="proj",
    )(x2, gain, mod_early, w_in, w_in, c8, w_ada, b_ada)


def _attn_stages(sink_ref, q_ref, kvp_ref, kvc_ref, o_ref):
    n = pl.program_id(1)
    kvp = kvp_ref[...]
    kvc = kvc_ref[...]
    qi = lax.broadcasted_iota(jnp.int32, (WINDOW, WINDOW), 0)
    sj = lax.broadcasted_iota(jnp.int32, (WINDOW, WINDOW), 1)
    valid_prev = (sj > qi) & (n > 0)
    valid_cur = sj <= qi
    sink_col = sj == 0
    first_row = lax.broadcasted_iota(jnp.int32, (2 * WINDOW, 1), 0) == 0
    dh = ATTN_HEAD_DIM
    kv_w = ATTN_KV_HEADS * dh
    scale = jnp.asarray(dh ** -0.5, BF16)
    n_pairs = ATTN_GROUP // 2

    def group_operands(kv):
        kband = jnp.concatenate([kvp[:, kv * dh:(kv + 1) * dh],
                                 kvc[:, kv * dh:(kv + 1) * dh]], axis=0) * scale
        vband = jnp.concatenate([kvp[:, kv_w + kv * dh:kv_w + (kv + 1) * dh],
                                 kvc[:, kv_w + kv * dh:kv_w + (kv + 1) * dh]], axis=0)
        vband = jnp.where(first_row, jnp.zeros_like(vband), vband)
        zeros = jnp.zeros_like(kband)
        ones = jnp.ones_like(vband)
        k_pad = (jnp.concatenate([kband, zeros], axis=1), jnp.concatenate([zeros, kband], axis=1))
        pv_rhs = jnp.concatenate(
            [jnp.concatenate([vband, zeros, ones, zeros], axis=1),
             jnp.concatenate([zeros, vband, zeros, ones], axis=1)], axis=0)
        q_rows = jnp.concatenate(
            [q_ref[:, (kv * ATTN_GROUP + 2 * p) * dh:(kv * ATTN_GROUP + 2 * p + 2) * dh]
             for p in range(n_pairs)], axis=0)
        scores = [lax.dot_general(q_rows, k_pad[idx], (((1,), (1,)), ((), ())),
                                  preferred_element_type=F32) for idx in range(2)]
        return scores, pv_rhs

    nxt = group_operands(0)
    for kv in range(ATTN_KV_HEADS):
        scores, pv_rhs = nxt
        if kv + 1 < ATTN_KV_HEADS:
            nxt = group_operands(kv + 1)
        prob_rows = []
        for pair in range(n_pairs):
            rows = slice(pair * WINDOW, (pair + 1) * WINDOW)
            probs = []
            for idx in range(2):
                s = scores[idx][rows]
                sink = sink_ref[kv * ATTN_GROUP + 2 * pair + idx]
                s_prev = jnp.where(sink_col, sink, jnp.where(valid_prev, s[:, :WINDOW], NEG))
                s_cur = jnp.where(valid_cur, s[:, WINDOW:], NEG)
                m = jnp.max(jnp.maximum(s_prev, s_cur), axis=-1, keepdims=True)
                probs += [jnp.exp(s_prev - m).astype(BF16), jnp.exp(s_cur - m).astype(BF16)]
            prob_rows.append(jnp.concatenate(probs, axis=-1))
        r = jnp.dot(jnp.concatenate(prob_rows, axis=0), pv_rhs, preferred_element_type=F32)
        for pair in range(n_pairs):
            rows = slice(pair * WINDOW, (pair + 1) * WINDOW)
            col = (kv * ATTN_GROUP + 2 * pair) * dh
            o_ref[:, col:col + 2 * dh] = (r[rows, :2 * dh] * (1.0 / r[rows, 2 * dh:])).astype(BF16)
        yield


def _ret_decay_tables():
    lg = np.asarray(LOG_GAMMA, np.float64)[:, None, None]
    i = np.arange(RET_CHUNK, dtype=np.float64)
    diff = i[:, None] - i[None, :]
    k_scale = RET_DIM ** -0.5
    d_intra = np.where(diff >= 0, np.exp(np.maximum(diff, 0.0) * lg), 0.0) * k_scale
    lanes = np.ones((1, 1, RET_DIM // 2))
    d_q = np.exp((i[None, :, None] + 1.0) * lg) * lanes
    d_k = np.exp((RET_CHUNK - 1.0 - i[None, :, None]) * lg) * k_scale * lanes
    return jnp.asarray(d_intra, F32), jnp.asarray(d_q, F32), jnp.asarray(d_k, BF16)


def _ret_stages(pos_ref, invf_ref, di_ref, dq_ref, dk_ref, q_ref, k_ref, v_ref, g_ref, gain_ref,
                o_ref, state_ref):
    half = RET_DIM // 2
    ang = pos_ref[...].astype(F32) * invf_ref[...]
    cos = jnp.cos(ang).astype(BF16)
    sin = jnp.sin(ang).astype(BF16)

    def rot(t):
        t1, t2 = t[:, :half], t[:, half:]
        return jnp.concatenate([t1 * cos - t2 * sin, t1 * sin + t2 * cos], axis=-1)

    def both_halves(t, factor):
        return jnp.concatenate([t[:, :half] * factor, t[:, half:] * factor], axis=-1)

    def head_front(h):
        sl = slice(h * RET_DIM, (h + 1) * RET_DIM)
        qb = rot(q_ref[:, sl])
        kb = rot(k_ref[:, sl])
        intra = lax.dot_general(qb, kb, (((1,), (1,)), ((), ())),
                                preferred_element_type=F32) * di_ref[h]
        st = state_ref[h]
        cross = jnp.dot(qb, st.astype(BF16), preferred_element_type=F32)
        return kb, intra, st, cross

    front = head_front(0)
    for h in range(RET_HEADS):
        sl = slice(h * RET_DIM, (h + 1) * RET_DIM)
        kb, intra, st, cross = front
        if h + 1 < RET_HEADS:
            front = head_front(h + 1)
        vb = v_ref[:, sl]
        d_chunk = math.exp(RET_CHUNK * LOG_GAMMA[h])
        o = jnp.dot(intra.astype(BF16), vb, preferred_element_type=F32) + both_halves(cross, dq_ref[h])
        kd = both_halves(kb, dk_ref[h])
        state_ref[h] = st * d_chunk + lax.dot_general(kd, vb, (((0,), (0,)), ((), ())),
                                                      preferred_element_type=F32)
        o = o * lax.rsqrt(jnp.mean(o * o, axis=-1, keepdims=True) + EPS) * gain_ref[:, sl]
        gg = g_ref[:, sl].astype(F32)
        o_ref[:, sl] = (gg * jax.nn.sigmoid(gg) * o).astype(BF16)
        yield


def _mixers_kernel(sink_ref, qa_ref, kvp_ref, kvc_ref, pos_ref, invf_ref, di_ref, dq_ref, dk_ref,
                   qr_ref, kr_ref, vr_ref, gr_ref, gain_ref, attn_ref, ret_ref, state_ref):
    @pl.when(pl.program_id(1) == 0)
    def _():
        state_ref[...] = jnp.zeros_like(state_ref)

    attn = _attn_stages(sink_ref, qa_ref, kvp_ref, kvc_ref, attn_ref)
    ret = _ret_stages(pos_ref, invf_ref, di_ref, dq_ref, dk_ref, qr_ref, kr_ref, vr_ref, gr_ref,
                      gain_ref, ret_ref, state_ref)
    heads_per_group = RET_HEADS // ATTN_KV_HEADS
    for _ in range(ATTN_KV_HEADS):
        next(attn)
        for _ in range(heads_per_group):
            next(ret)


def _mixers(proj, proj_kv, sinks, pos, inv_freq, ret_gain, batch, seq):
    nb = seq // WINDOW
    t = batch * seq

    def col(cb):
        return lambda b, n: (b * nb + n, cb)

    d_intra, d_q, d_k = _ret_decay_tables()
    table = lambda lanes: pl.BlockSpec((RET_HEADS, RET_CHUNK, lanes), lambda b, n: (0, 0, 0))
    rows = pl.BlockSpec((WINDOW, D_MODEL), col(0))
    return pl.pallas_call(
        _mixers_kernel,
        grid=(batch, nb),
        in_specs=[pl.BlockSpec(memory_space=pltpu.SMEM),
                  pl.BlockSpec((WINDOW, D_MODEL), col(COL_QA)),
                  pl.BlockSpec((WINDOW, PROJ_TN), lambda b, n: (b * nb + jnp.maximum(n - 1, 0), 0)),
                  pl.BlockSpec((WINDOW, PROJ_TN), col(0)),
                  pl.BlockSpec((RET_CHUNK, 1), col(0)),
                  pl.BlockSpec((1, RET_DIM // 2), lambda b, n: (0, 0)),
                  table(RET_CHUNK), table(RET_DIM // 2), table(RET_DIM // 2),
                  pl.BlockSpec((RET_CHUNK, D_MODEL), col(COL_QR)),
                  pl.BlockSpec((RET_CHUNK, D_MODEL), col(COL_KR)),
                  pl.BlockSpec((RET_CHUNK, D_MODEL), col(COL_VR)),
                  pl.BlockSpec((RET_CHUNK, D_MODEL), col(COL_GR)),
                  pl.BlockSpec((1, D_MODEL), lambda b, n: (0, 0))],
        out_specs=[rows, rows],
        out_shape=[jax.ShapeDtypeStruct((t, D_MODEL), BF16), jax.ShapeDtypeStruct((t, D_MODEL), BF16)],
        scratch_shapes=[pltpu.VMEM((RET_HEADS, RET_DIM, RET_DIM), F32)],
        compiler_params=_params(("arbitrary", "arbitrary"), 40),
        name="mixers",
    )(sinks, proj, proj_kv, proj_kv, pos, inv_freq, d_intra, d_q, d_k, proj, proj, proj, proj, ret_gain)


def _route(logits):
    lane = lax.broadcasted_iota(jnp.int32, logits.shape, 1)
    lane_f = lane.astype(F32)
    is_g = lane < N_GROUPS
    gl = jnp.where(is_g, logits, NEG)
    gmax = jnp.max(gl, axis=-1, keepdims=True)
    gsel = jnp.min(jnp.where(gl == gmax, lane_f, float(LANES)), axis=-1, keepdims=True)
    gsum = jnp.sum(jnp.where(is_g, jnp.exp(gl - gmax), 0.0), axis=-1, keepdims=True)
    g_w = 1.0 / gsum
    grp = ((lane - N_GROUPS) >> 4).astype(F32)
    is_e = (lane >= N_GROUPS) & (lane < N_GROUPS + N_EXPERTS) & (grp == gsel)
    el = jnp.where(is_e, logits, NEG)
    v1 = jnp.max(el, axis=-1, keepdims=True)
    i1 = jnp.min(jnp.where(el == v1, lane_f, float(LANES)), axis=-1, keepdims=True)
    el2 = jnp.where(lane_f == i1, NEG, el)
    v2 = jnp.max(el2, axis=-1, keepdims=True)
    i2 = jnp.min(jnp.where(el2 == v2, lane_f, float(LANES)), axis=-1, keepdims=True)
    tt = jnp.exp(v2 - v1)
    w1 = g_w / (1.0 + tt)
    w2 = g_w * tt / (1.0 + tt)
    return jnp.where(lane == 0, i1 - N_GROUPS,
                     jnp.where(lane == 1, i2 - N_GROUPS,
                               jnp.where(lane == 2, w1, jnp.where(lane == 3, w2, 0.0))))


def _branch_kernel(attn_ref, ret_ref, ga_ref, gr_ref, wa_ref, wr_ref, o_ref):
    a = jnp.dot(attn_ref[...], wa_ref[...].astype(BF16), preferred_element_type=F32)
    r = jnp.dot(ret_ref[...], wr_ref[...].astype(BF16), preferred_element_type=F32)
    o_ref[...] = (jax.nn.sigmoid(ga_ref[...].astype(F32)) * a
                  + jax.nn.sigmoid(gr_ref[...].astype(F32)) * r).astype(BF16)


def _branch(attn, ret, proj, wa, wr):
    t = attn.shape[0]
    tm, tn = BRANCH_TM, BRANCH_TN
    per_slab = D_MODEL // tn
    return pl.pallas_call(
        _branch_kernel,
        grid=(t // tm, D_MODEL // tn),
        in_specs=[pl.BlockSpec((tm, D_MODEL), lambda i, j: (i, 0)),
                  pl.BlockSpec((tm, D_MODEL), lambda i, j: (i, 0)),
                  pl.BlockSpec((tm, tn), lambda i, j: (i, COL_GA * per_slab + j)),
                  pl.BlockSpec((tm, tn), lambda i, j: (i, COL_GRT * per_slab + j)),
                  pl.BlockSpec((D_MODEL, tn), lambda i, j: (0, j)),
                  pl.BlockSpec((D_MODEL, tn), lambda i, j: (0, j))],
        out_specs=pl.BlockSpec((tm, tn), lambda i, j: (i, j)),
        out_shape=jax.ShapeDtypeStruct((t, D_MODEL), BF16),
        compiler_params=_params(("arbitrary", "arbitrary"), 48),
        name="branch",
    )(attn, ret, proj, proj, wa, wr)


def _mixout_kernel(m_ref, x_ref, mod_ref, g2_ref, wo_ref, wrt_ref, brt_ref,
                   x1_ref, h2_ref, route_ref, count_ref, meta_ref, carry_ref):
    @pl.when(pl.program_id(0) == 0)
    def _():
        carry_ref[...] = jnp.zeros_like(carry_ref)

    mix = jnp.dot(m_ref[...], wo_ref[...], preferred_element_type=F32)
    x1 = x_ref[...] + mod_ref[0, 2:3, :] * mix
    x1_ref[...] = x1
    var = jnp.mean(x1 * x1, axis=-1, keepdims=True)
    h2 = x1 * lax.rsqrt(var + EPS) * g2_ref[...]
    h2 = h2 * (1.0 + mod_ref[0, 4:5, :]) + mod_ref[0, 3:4, :]
    tm = h2.shape[0]
    for j in range(SLAB_ROWS):
        lo = h2[:, j * LANES:(j + 1) * LANES]
        hi = h2[:, HALF_D + j * LANES:HALF_D + (j + 1) * LANES]
        h2_ref[_slab_rows(j, tm), :] = _pack_pair(lo, hi)
    h_hi = h2.astype(BF16)
    h_lo = (h2 - h_hi.astype(F32)).astype(BF16)
    w_rt = wrt_ref[...]
    w_hi = w_rt.astype(BF16)
    w_lo = (w_rt - w_hi.astype(F32)).astype(BF16)
    hi_both = jnp.dot(h_hi, jnp.concatenate([w_hi, w_lo], axis=1), preferred_element_type=F32)
    logits = (hi_both[:, :LANES] + hi_both[:, LANES:]
              + jnp.dot(h_lo, w_hi, preferred_element_type=F32) + brt_ref[...])
    route = _route(logits)

    lane = lax.broadcasted_iota(jnp.int32, route.shape, 1)
    lane_f = lane.astype(F32)
    hot1 = lane_f == route[:, 0:1]
    hot2 = lane_f == route[:, 1:2]
    both = jnp.where(hot1 | hot2, 1.0, 0.0)
    ii = lax.broadcasted_iota(jnp.int32, (tm, tm), 0)
    jj = lax.broadcasted_iota(jnp.int32, (tm, tm), 1)
    lower = jnp.where(ii > jj, 1.0, 0.0).astype(BF16)
    before = jnp.dot(lower, both.astype(BF16), preferred_element_type=F32) + carry_ref[...]
    r1 = jnp.sum(jnp.where(hot1, before, 0.0), axis=-1, keepdims=True)
    r2 = jnp.sum(jnp.where(hot2, before, 0.0), axis=-1, keepdims=True)
    route = jnp.where(lane == 4, r1, jnp.where(lane == 5, r2, route))
    route_ref[...] = route
    meta_ref[...] = route.T[0:SLAB_ROWS, :].astype(I32)
    carry = carry_ref[...] + jnp.sum(both, axis=0, keepdims=True)
    carry_ref[...] = carry
    count_ref[...] = carry


def _mixout(merged, x2, mod6, gain2, wo, w_rt, b_rt, seq):
    t = x2.shape[0]
    tm = MIXOUT_TM
    tiles_per_batch = seq // tm
    row = lambda i: (i, 0)
    const = lambda i: (0, 0)
    return pl.pallas_call(
        _mixout_kernel,
        grid=(t // tm,),
        in_specs=[pl.BlockSpec((tm, D_MODEL), row),
                  pl.BlockSpec((tm, D_MODEL), row),
                  pl.BlockSpec((1, 6, D_MODEL), lambda i: (i // tiles_per_batch, 0, 0)),
                  pl.BlockSpec((1, D_MODEL), const),
                  pl.BlockSpec((D_MODEL, D_MODEL), const, pipeline_mode=pl.Buffered(1)),
                  pl.BlockSpec((D_MODEL, LANES), const),
                  pl.BlockSpec((1, LANES), const)],
        out_specs=[pl.BlockSpec((tm, D_MODEL), row),
                   pl.BlockSpec((tm * SLAB_ROWS, SLAB_LANES), row),
                   pl.BlockSpec((tm, LANES), row),
                   pl.BlockSpec((1, LANES), const),
                   pl.BlockSpec((SLAB_ROWS, tm), lambda i: (0, i))],
        out_shape=[jax.ShapeDtypeStruct((t, D_MODEL), F32),
                   jax.ShapeDtypeStruct((t * SLAB_ROWS, SLAB_LANES), I32),
                   jax.ShapeDtypeStruct((t, LANES), F32),
                   jax.ShapeDtypeStruct((1, LANES), F32),
                   jax.ShapeDtypeStruct((SLAB_ROWS, t), I32)],
        scratch_shapes=[pltpu.VMEM((1, LANES), F32)],
        compiler_params=_params(("arbitrary",), 56),
        name="mixout",
    )(merged, x2, mod6, gain2, wo, w_rt, b_rt)


PAD_BITS = (64, 32, 16, 8, 4, 2, 1)


def _sorted_row(tables, tok, k):
    e_refs, r_refs, blk_row_ref = tables[0:TOP_K], tables[TOP_K:2 * TOP_K], tables[2 * TOP_K]
    return blk_row_ref[e_refs[k][tok]] + r_refs[k][tok]


N_ROUTE_TABLES = 2 * TOP_K + 1


def _dispatch_kernel(*refs):
    tables = refs[:N_ROUTE_TABLES]
    zstart_ref, zcount_ref, h2_ref, xs_ref, dest_ref, zero_ref, sem, zsem = refs[N_ROUTE_TABLES:]
    n_assign = tables[0].shape[0] * TOP_K
    zero_ref[...] = jnp.zeros_like(zero_ref)

    def zero_copy(start, rows):
        return pltpu.make_async_copy(zero_ref.at[pl.ds(0, rows * SLAB_ROWS), :],
                                     xs_ref.at[pl.ds(start * SLAB_ROWS, rows * SLAB_ROWS), :], zsem)

    def fill(e, wait):
        start = zstart_ref[e]
        pad = zcount_ref[e]
        for bit in PAD_BITS:
            @pl.when((pad & bit) != 0)
            def _(start=start, bit=bit):
                cp = zero_copy(start, bit)
                cp.wait() if wait else cp.start()
            start = start + (pad & bit)

    lax.fori_loop(0, N_EXPERTS, lambda e, c: (fill(e, False), c)[1], 0)

    def issue(tok, carry):
        src = h2_ref.at[pl.ds(pl.multiple_of(tok * SLAB_ROWS, SLAB_ROWS), SLAB_ROWS), :]
        for k in range(TOP_K):
            row = _sorted_row(tables, tok, k)
            dest_ref[tok * TOP_K + k] = row
            dst = pl.multiple_of(row * SLAB_ROWS, SLAB_ROWS)
            pltpu.make_async_copy(src, xs_ref.at[pl.ds(dst, SLAB_ROWS), :], sem).start()
        return carry

    lax.fori_loop(0, n_assign // TOP_K, issue, 0, unroll=8)

    def drain(i, carry):
        pltpu.make_async_copy(h2_ref.at[pl.ds(0, ROWS_PER_WAIT * SLAB_ROWS), :],
                              xs_ref.at[pl.ds(0, ROWS_PER_WAIT * SLAB_ROWS), :], sem).wait()
        return carry

    lax.fori_loop(0, n_assign // ROWS_PER_WAIT, drain, 0)
    lax.fori_loop(0, N_EXPERTS, lambda e, c: (fill(e, True), c)[1], 0)


def _dispatch(tables, zstart, zcount, h2_slab, n_pad):
    n_assign = tables[0].shape[0] * TOP_K
    return pl.pallas_call(
        _dispatch_kernel,
        grid_spec=pltpu.PrefetchScalarGridSpec(
            num_scalar_prefetch=N_ROUTE_TABLES + 2,
            grid=(1,),
            in_specs=[pl.BlockSpec(memory_space=pltpu.HBM)],
            out_specs=[pl.BlockSpec(memory_space=pl.ANY),
                       pl.BlockSpec(memory_space=pltpu.SMEM)],
            scratch_shapes=[pltpu.VMEM((PAD_BITS[0] * SLAB_ROWS, SLAB_LANES), I32),
                            pltpu.SemaphoreType.DMA(()),
                            pltpu.SemaphoreType.DMA(())]),
        out_shape=[jax.ShapeDtypeStruct((n_pad * SLAB_ROWS, SLAB_LANES), I32),
                   jax.ShapeDtypeStruct((n_assign,), I32)],
        compiler_params=_params(("arbitrary",), 16),
        name="dispatch",
    )(*tables, zstart, zcount, h2_slab)


def _experts_kernel(item_e_ref, item_blk_ref, item_nb_ref, x_blk_ref,
                    x0_ref, x1_ref, x2_ref, x3_ref, wg_hbm, wu_hbm, wd_hbm, y_ref,
                    xb_ref, yp_ref, wg_ref, wu_ref, wd_ref, sem, wsem):
    w = pl.program_id(0)
    n_items = pl.num_programs(0)
    nb = item_nb_ref[w]
    blk0 = item_blk_ref[w]
    x_refs = (x0_ref, x1_ref, x2_ref, x3_ref)

    blk_rows = MOE_BLOCK * SLAB_ROWS

    fc = wg_ref.shape[2]
    n_chunks = EXPERT_DIM // fc

    def weight_copies(item, chunk, slot):
        e = item_e_ref[item]
        col = pl.multiple_of(chunk * fc, fc)
        return (pltpu.make_async_copy(wg_hbm.at[e, :, pl.ds(col, fc)], wg_ref.at[slot], wsem.at[slot]),
                pltpu.make_async_copy(wu_hbm.at[e, :, pl.ds(col, fc)], wu_ref.at[slot], wsem.at[slot]),
                pltpu.make_async_copy(wd_hbm.at[e, pl.ds(col, fc), :], wd_ref.at[slot], wsem.at[slot]))

    def start_weights(tile):
        item = jnp.minimum(tile // n_chunks, n_items - 1)

        @pl.when((tile < n_items * n_chunks) & (item_nb_ref[item] > 0))
        def _():
            for cp in weight_copies(item, tile % n_chunks, tile % EXPERT_WEIGHT_SLOTS):
                cp.start()

    @pl.when(w == 0)
    def _():
        for tile in range(EXPERT_WEIGHT_SLOTS - 1):
            start_weights(tile)

    def out_copy(s):
        return pltpu.make_async_copy(yp_ref.at[s], y_ref.at[pl.ds((blk0 + s) * blk_rows, blk_rows), :],
                                     sem.at[s])

    def wait_out(count):
        for s in range(ITEM_BLOCKS):
            @pl.when(s < count)
            def _(s=s):
                out_copy(s).wait()

    def run(n_live):
        rows = n_live * MOE_BLOCK
        for s in range(n_live):
            for j in range(SLAB_ROWS):
                lo, hi = _unpack_pair(x_refs[s][_slab_rows(j, MOE_BLOCK), :])
                r0 = s * MOE_BLOCK
                xb_ref[r0:r0 + MOE_BLOCK, j * LANES:(j + 1) * LANES] = lo.astype(BF16)
                xb_ref[r0:r0 + MOE_BLOCK, HALF_D + j * LANES:HALF_D + (j + 1) * LANES] = hi.astype(BF16)
        xs = xb_ref[0:rows, :]

        y = None
        for c in range(n_chunks):
            tile = w * n_chunks + c
            start_weights(tile + EXPERT_WEIGHT_SLOTS - 1)
            slot = tile % EXPERT_WEIGHT_SLOTS
            for cp in weight_copies(w, c, slot):
                cp.wait()
            g = jnp.dot(xs, wg_ref[slot].astype(BF16), preferred_element_type=F32)
            u = jnp.dot(xs, wu_ref[slot].astype(BF16), preferred_element_type=F32)
            hid = (g * jax.nn.sigmoid(g) * u).astype(BF16)
            y_c = jnp.dot(hid, wd_ref[slot].astype(BF16), preferred_element_type=F32)
            y = y_c if y is None else y + y_c

        @pl.when(w > 0)
        def _():
            wait_out(item_nb_ref[jnp.maximum(w - 1, 0)])

        for s in range(n_live):
            r0 = s * MOE_BLOCK
            for j in range(SLAB_ROWS):
                yp_ref[s, _slab_rows(j, MOE_BLOCK), :] = _pack_pair(
                    y[r0:r0 + MOE_BLOCK, j * LANES:(j + 1) * LANES],
                    y[r0:r0 + MOE_BLOCK, HALF_D + j * LANES:HALF_D + (j + 1) * LANES])
            out_copy(s).start()

    for n_live in range(1, ITEM_BLOCKS + 1):
        pl.when(nb == n_live)(functools.partial(run, n_live))

    @pl.when((nb == 0) & (w > 0))
    def _():
        wait_out(item_nb_ref[jnp.maximum(w - 1, 0)])

    @pl.when(w == n_items - 1)
    def _():
        wait_out(nb)


def _experts(item_e, item_blk, item_nb, x_sorted, w_gate, w_up, w_down, n_blocks):
    n_items = item_e.shape[0]
    d = D_MODEL
    fc = EXPERT_FC
    x3 = x_sorted
    blk_rows = MOE_BLOCK * SLAB_ROWS

    slot = jnp.arange(ITEM_BLOCKS, dtype=jnp.int32)[:, None]
    x_blk = jnp.maximum(lax.cummax(jnp.where(slot < item_nb[None, :], item_blk[None, :] + slot, -1), axis=1), 0)
    x_blk = x_blk.reshape(-1).astype(jnp.int32)

    def x_spec(s):
        return pl.BlockSpec((blk_rows, SLAB_LANES),
                            lambda w, ie, ib, inb, xb: (xb[s * n_items + w], 0))

    hbm = pl.BlockSpec(memory_space=pl.ANY)
    slots = EXPERT_WEIGHT_SLOTS
    return pl.pallas_call(
        _experts_kernel,
        grid_spec=pltpu.PrefetchScalarGridSpec(
            num_scalar_prefetch=4,
            grid=(n_items,),
            in_specs=[x_spec(0), x_spec(1), x_spec(2), x_spec(3), hbm, hbm, hbm],
            out_specs=pl.BlockSpec(memory_space=pl.ANY),
            scratch_shapes=[pltpu.VMEM((ITEM_BLOCKS * MOE_BLOCK, d), BF16),
                            pltpu.VMEM((ITEM_BLOCKS, blk_rows, SLAB_LANES), I32),
                            pltpu.VMEM((slots, d, fc), F32),
                            pltpu.VMEM((slots, d, fc), F32),
                            pltpu.VMEM((slots, fc, d), F32),
                            pltpu.SemaphoreType.DMA((ITEM_BLOCKS,)),
                            pltpu.SemaphoreType.DMA((slots,))]),
        out_shape=jax.ShapeDtypeStruct((n_blocks * blk_rows, SLAB_LANES), I32),
        compiler_params=_params(("arbitrary",), 56),
        name="experts",
    )(item_e, item_blk, item_nb, x_blk, x3, x3, x3, x3, w_gate, w_up, w_down)


def _combine_kernel(dest_ref, x1_ref, route_ref, mod_ref, gain_ref, y_ref, o_ref, ybuf_ref, sem):
    tm = COMBINE_TM
    i = pl.program_id(0)
    n_tiles = pl.num_programs(0)

    def row_copy(slot, k, r, src):
        return pltpu.make_async_copy(
            y_ref.at[pl.ds(pl.multiple_of(src * SLAB_ROWS, SLAB_ROWS), SLAB_ROWS), :],
            ybuf_ref.at[slot, k, pl.ds(pl.multiple_of(r * SLAB_ROWS, SLAB_ROWS), SLAB_ROWS), :],
            sem.at[slot])

    def wait_slot(slot):
        for k in range(TOP_K):
            pltpu.make_async_copy(y_ref.at[pl.ds(0, tm * SLAB_ROWS), :], ybuf_ref.at[slot, k],
                                  sem.at[slot]).wait()

    def issue_tile(tile, slot):
        base = tile * tm * TOP_K

        def issue(r, carry):
            for k in range(TOP_K):
                row_copy(slot, k, r, dest_ref[base + r * TOP_K + k]).start()
            return carry

        lax.fori_loop(0, tm, issue, 0, unroll=4)

    @pl.when(i == 0)
    def _():
        for tile in range(COMBINE_SLOTS - 1):
            issue_tile(tile, tile)

    slot = i % COMBINE_SLOTS
    wait_slot(slot)

    ahead = i + COMBINE_SLOTS - 1
    ahead_slot = ahead % COMBINE_SLOTS
    ahead_base = jnp.minimum(ahead, n_tiles - 1) * (tm * TOP_K)
    n_stages = 2 * SLAB_ROWS
    rows_per_stage = tm // n_stages

    route = route_ref[...]
    w0 = route[:, 2:3]
    w1 = route[:, 3:4]
    ssq = jnp.zeros((tm, 1), F32)
    stage = 0
    for j in range(SLAB_ROWS):
        rows = _slab_rows(j, tm)
        y0 = _unpack_pair(ybuf_ref[slot, 0, rows, :])
        y1 = _unpack_pair(ybuf_ref[slot, 1, rows, :])
        for part, off in ((0, j * LANES), (1, HALF_D + j * LANES)):
            cols = slice(off, off + LANES)
            ffn = w0 * y0[part] + w1 * y1[part]
            x2 = x1_ref[:, cols] + mod_ref[0, 5:6, cols] * ffn
            o_ref[:, cols] = x2
            ssq = ssq + jnp.sum(x2 * x2, axis=-1, keepdims=True)
            for r in range(stage * rows_per_stage, (stage + 1) * rows_per_stage):
                for k in range(TOP_K):
                    row_copy(ahead_slot, k, r, dest_ref[ahead_base + r * TOP_K + k]).start()
            stage += 1
    o_ref[...] = o_ref[...] * lax.rsqrt(ssq * (1.0 / D_MODEL) + EPS) * gain_ref[...]

    @pl.when(i == n_tiles - 1)
    def _():
        for extra in range(1, COMBINE_SLOTS):
            wait_slot((i + extra) % COMBINE_SLOTS)


def _combine(dest, x1, route, mod6, gain, y_sorted, seq):
    t, d = x1.shape
    tm = COMBINE_TM
    tiles_per_batch = seq // tm
    return pl.pallas_call(
        _combine_kernel,
        grid_spec=pltpu.PrefetchScalarGridSpec(
            num_scalar_prefetch=1,
            grid=(t // tm,),
            in_specs=[pl.BlockSpec((tm, d), lambda i, *_: (i, 0)),
                      pl.BlockSpec((tm, LANES), lambda i, *_: (i, 0)),
                      pl.BlockSpec((1, 6, d), lambda i, *_: (i // tiles_per_batch, 0, 0)),
                      pl.BlockSpec((1, d), lambda i, *_: (0, 0)),
                      pl.BlockSpec(memory_space=pl.ANY)],
            out_specs=pl.BlockSpec((tm, d), lambda i, *_: (i, 0)),
            scratch_shapes=[pltpu.VMEM((COMBINE_SLOTS, TOP_K, tm * SLAB_ROWS, SLAB_LANES), I32),
                            pltpu.SemaphoreType.DMA((COMBINE_SLOTS,))]),
        out_shape=jax.ShapeDtypeStruct((t, d), F32),
        compiler_params=_params(("arbitrary",), 48),
        name="combine",
    )(dest, x1, route, mod6, gain, y_sorted)


def _dispatch_tables(meta, counts, t):
    n_assign = t * TOP_K
    n_pad = -(-(n_assign + N_EXPERTS * (MOE_BLOCK - 1)) // MOE_BLOCK) * MOE_BLOCK
    n_blocks = n_pad // MOE_BLOCK
    n_items = N_EXPERTS + n_assign // (ITEM_BLOCKS * MOE_BLOCK)

    cnt = counts[0, :N_EXPERTS].astype(jnp.int32)
    blocks_e = (cnt + MOE_BLOCK - 1) // MOE_BLOCK
    blk_end = jnp.cumsum(blocks_e)
    blk_start = blk_end - blocks_e
    tables = (meta[0], meta[1], meta[4], meta[5], (blk_start * MOE_BLOCK).astype(jnp.int32))
    zstart = (blk_start * MOE_BLOCK + cnt).astype(jnp.int32)
    zcount = (blocks_e * MOE_BLOCK - cnt).astype(jnp.int32)

    items_e = (blocks_e + ITEM_BLOCKS - 1) // ITEM_BLOCKS
    item_end = jnp.cumsum(items_e)
    item_start = item_end - items_e
    w = jnp.arange(n_items, dtype=jnp.int32)
    live = w < item_end[-1]
    w_live = jnp.minimum(w, item_end[-1] - 1)
    e_w = jnp.minimum(jnp.sum((item_end[None, :] <= w_live[:, None]).astype(jnp.int32), axis=1), N_EXPERTS - 1)
    j_w = w_live - item_start[e_w]
    item_blk = (blk_start[e_w] + ITEM_BLOCKS * j_w).astype(jnp.int32)
    item_nb = jnp.where(live, jnp.clip(blocks_e[e_w] - ITEM_BLOCKS * j_w, 0, ITEM_BLOCKS), 0).astype(jnp.int32)
    return tables, zstart, zcount, e_w, item_blk, item_nb, n_blocks


def kernel(x, c, positions, norm1_gain, norm2_gain, final_norm_gain, w_ada, b_ada, w_in, attn_sinks,
           ret_norm_gain, w_branch_attn, w_branch_ret, w_out, w_router_group, b_router_group,
           w_router_expert, b_router_expert, w_expert_gate, w_expert_up, w_expert_down):
    batch, seq, d = x.shape
    t = batch * seq
    depth = w_ada.shape[0]
    half = RET_DIM // 2
    inv_freq = (ROPE_BASE ** (-jnp.arange(half, dtype=F32) / half)).reshape(1, half)
    pos = positions.reshape(t, 1)
    c8 = jnp.pad(c, ((0, 8 - batch), (0, 0)))
    xf = x.reshape(t, d)

    assert depth == 1, "the fused final norm assumes a single layer"
    for layer in range(depth):
        b_ada2 = b_ada[layer].reshape(1, -1)
        mod_early = _ada(c8, w_ada[layer], b_ada2, ADA_EARLY)[:batch]
        proj, proj_kv, mod_late = _proj(xf, norm1_gain[layer].reshape(1, d), mod_early.reshape(batch, 2, d),
                                        w_in[layer], c8, w_ada[layer], b_ada2, seq)
        mod6 = jnp.concatenate([mod_early, mod_late[:batch]], axis=1).reshape(batch, 6, d)
        attn, ret = _mixers(proj, proj_kv, attn_sinks[layer], pos, inv_freq,
                            ret_norm_gain[layer].reshape(1, d), batch, seq)

        pad = LANES - N_GROUPS - N_EXPERTS
        w_rt = jnp.concatenate([w_router_group[layer], w_router_expert[layer],
                                jnp.zeros((d, pad), F32)], axis=1)
        b_rt = jnp.concatenate([b_router_group[layer], b_router_expert[layer],
                                jnp.zeros((pad,), F32)]).reshape(1, LANES)
        merged = _branch(attn, ret, proj, w_branch_attn[layer], w_branch_ret[layer])
        x1, h2, route, counts, meta = _mixout(merged, xf, mod6, norm2_gain[layer].reshape(1, d),
                                              w_out[layer].astype(BF16), w_rt, b_rt, seq)
        tables, zstart, zcount, item_e, item_blk, item_nb, n_blocks = _dispatch_tables(meta, counts, t)
        x_sorted, dest = _dispatch(tables, zstart, zcount, h2, n_blocks * MOE_BLOCK)
        y_sorted = _experts(item_e, item_blk, item_nb, x_sorted,
                            w_expert_gate[layer], w_expert_up[layer], w_expert_down[layer], n_blocks)
        xf = _combine(dest, x1, route, mod6, final_norm_gain.reshape(1, d), y_sorted, seq)
    return xf.reshape(batch, seq, d)
```

```python
import functools
import math

import jax
import jax.numpy as jnp
import numpy as np
from jax import lax
from jax.experimental import pallas as pl
from jax.experimental.pallas import tpu as pltpu

F32 = jnp.float32
BF16 = jnp.bfloat16

D_MODEL = 2048
ATTN_HEAD_DIM = 64
ATTN_HEADS = 32
ATTN_KV_HEADS = 4
ATTN_GROUP = 8
WINDOW = 128
RET_HEADS = 8
RET_DIM = 256
RET_CHUNK = 128
ROPE_BASE = 10000.0
N_GROUPS = 4
EXPERTS_PER_GROUP = 16
N_EXPERTS = 64
TOP_K = 2
EXPERT_DIM = 1024
MOE_BLOCK = 128
EPS = 1e-6
NEG = -1e30

MIB = 1024 * 1024
LANES = 128
PROJ_TN = 512
PROJ_TM = 1024
KV_SRC_TILE = 4
ADA_EARLY = 2 * D_MODEL
ADA_LATE_TN = 256
ADA_LATE_STEPS = 4 * D_MODEL // ADA_LATE_TN
BRANCH_TM = 1024
BRANCH_TN = 512
MIXOUT_TM = 512
ITEM_BLOCKS = 4
EXPERT_FC = 512
EXPERT_WEIGHT_SLOTS = 3
COMBINE_TM = 256
COMBINE_SLOTS = 3
SLAB_ROWS = 8
SLAB_LANES = LANES
HALF_D = D_MODEL // 2
ROWS_PER_WAIT = 128
I32 = jnp.int32

COL_QA, COL_QR, COL_KR, COL_VR, COL_GR, COL_GA, COL_GRT = 0, 1, 2, 3, 4, 5, 6

LOG_GAMMA = [math.log1p(-(2.0 ** (-5.0 - h))) for h in range(RET_HEADS)]


def _params(sem, vmem_mib):
    return pltpu.CompilerParams(dimension_semantics=sem, vmem_limit_bytes=vmem_mib * MIB)


def _pack_pair(lo, hi):
    lo_b = lax.bitcast_convert_type(lo.astype(BF16).astype(F32), I32)
    hi_b = lax.bitcast_convert_type(hi.astype(BF16).astype(F32), I32)
    return hi_b | lax.shift_right_logical(lo_b, jnp.full_like(lo_b, 16))


def _unpack_pair(w):
    lo = lax.bitcast_convert_type(w << 16, F32)
    hi = lax.bitcast_convert_type(w & jnp.int32(-65536), F32)
    return lo, hi


def _slab_rows(j, n_tokens):
    return pl.ds(j, n_tokens, stride=SLAB_ROWS)


def _ada_kernel(c_ref, w_ref, b_ref, o_ref):
    c = c_ref[...]
    a = (c * jax.nn.sigmoid(c)).astype(BF16)
    o_ref[...] = jnp.dot(a, w_ref[...].astype(BF16), preferred_element_type=F32) + b_ref[...]


def _ada(c8, w_ada, b_ada, n):
    tn = 1024
    return pl.pallas_call(
        _ada_kernel,
        grid=(n // tn,),
        in_specs=[pl.BlockSpec((8, D_MODEL), lambda j: (0, 0)),
                  pl.BlockSpec((D_MODEL, tn), lambda j: (0, j)),
                  pl.BlockSpec((1, tn), lambda j: (0, j))],
        out_specs=pl.BlockSpec((8, tn), lambda j: (0, j)),
        out_shape=jax.ShapeDtypeStruct((8, n), F32),
        compiler_params=_params(("arbitrary",), 40),
        name="ada",
    )(c8, w_ada, b_ada)


def _proj_kernel(x_ref, g_ref, mod_ref, wlo_ref, whi_ref, c_ref, wada_ref, bada_ref,
                 o_ref, kv_ref, late_ref, h0_ref, h1_ref):
    i = pl.program_id(0)
    v = pl.program_id(1)
    last = pl.num_programs(1) - 1

    def normed():
        x = x_ref[...]
        var = jnp.mean(x * x, axis=-1, keepdims=True)
        y = x * lax.rsqrt(var + EPS) * g_ref[...]
        return (y * (1.0 + mod_ref[0, 1:2, :]) + mod_ref[0, 0:1, :]).astype(BF16)

    @pl.when(pl.program_id(0) * pl.num_programs(1) + v < ADA_LATE_STEPS)
    def _():
        _ada_kernel(c_ref, wada_ref, bada_ref, late_ref)

    @pl.when((i == 0) & (v == 0))
    def _():
        h0_ref[...] = normed()

    for parity, (h_ref, h_next_ref) in enumerate(((h0_ref, h1_ref), (h1_ref, h0_ref))):
        @pl.when((v < last) & (i % 2 == parity))
        def _(h_ref=h_ref):
            w = jnp.concatenate([wlo_ref[...].astype(BF16), whi_ref[...].astype(BF16)], axis=1)
            o_ref[...] = jnp.dot(h_ref[...], w, preferred_element_type=F32).astype(BF16)

        @pl.when((v == last) & (i % 2 == parity))
        def _(h_ref=h_ref, h_next_ref=h_next_ref):
            kv_ref[...] = jnp.dot(h_ref[...], wlo_ref[...].astype(BF16),
                                  preferred_element_type=F32).astype(BF16)
            h_next_ref[...] = normed()


def _proj_w_tile(v, n_wide):
    return jnp.where(v < 2, 2 * v, jnp.where(v < n_wide, 2 * v + 1, KV_SRC_TILE))


def _proj(x2, gain, mod_early, w_in, c8, w_ada, b_ada, seq):
    t = x2.shape[0]
    n = w_in.shape[1]
    tiles_per_batch = seq // PROJ_TM
    n_wide = (n - PROJ_TN) // (2 * PROJ_TN)
    n_late = w_ada.shape[1] - ADA_EARLY
    assert n_late == ADA_LATE_STEPS * ADA_LATE_TN and ADA_LATE_STEPS <= (t // PROJ_TM) * (n_wide + 1)
    early_tiles = ADA_EARLY // ADA_LATE_TN

    def late_tile(i, v):
        return jnp.minimum(i * (n_wide + 1) + v, ADA_LATE_STEPS - 1)

    n_row_tiles = t // PROJ_TM

    def norm_tile(i, v):
        return jnp.minimum(i + (v == n_wide).astype(jnp.int32), n_row_tiles - 1)

    return pl.pallas_call(
        _proj_kernel,
        grid=(n_row_tiles, n_wide + 1),
        in_specs=[pl.BlockSpec((PROJ_TM, D_MODEL), lambda i, v: (norm_tile(i, v), 0)),
                  pl.BlockSpec((1, D_MODEL), lambda i, v: (0, 0)),
                  pl.BlockSpec((1, 2, D_MODEL), lambda i, v: (norm_tile(i, v) // tiles_per_batch, 0, 0)),
                  pl.BlockSpec((D_MODEL, PROJ_TN), lambda i, v: (0, _proj_w_tile(v, n_wide))),
                  pl.BlockSpec((D_MODEL, PROJ_TN),
                               lambda i, v: (0, jnp.where(v < n_wide, _proj_w_tile(v, n_wide) + 1, KV_SRC_TILE))),
                  pl.BlockSpec((8, D_MODEL), lambda i, v: (0, 0)),
                  pl.BlockSpec((D_MODEL, ADA_LATE_TN), lambda i, v: (0, early_tiles + late_tile(i, v))),
                  pl.BlockSpec((1, ADA_LATE_TN), lambda i, v: (0, early_tiles + late_tile(i, v)))],
        out_specs=[pl.BlockSpec((PROJ_TM, 2 * PROJ_TN), lambda i, v: (i, jnp.minimum(v, n_wide - 1))),
                   pl.BlockSpec((PROJ_TM, PROJ_TN), lambda i, v: (i, 0)),
                   pl.BlockSpec((8, ADA_LATE_TN), lambda i, v: (0, late_tile(i, v)))],
        out_shape=[jax.ShapeDtypeStruct((t, n - PROJ_TN), BF16),
                   jax.ShapeDtypeStruct((t, PROJ_TN), BF16),
                   jax.ShapeDtypeStruct((8, n_late), F32)],
        scratch_shapes=[pltpu.VMEM((PROJ_TM, D_MODEL), BF16), pltpu.VMEM((PROJ_TM, D_MODEL), BF16)],
        compiler_params=_params(("arbitrary", "arbitrary"), 60),
        name="proj",
    )(x2, gain, mod_early, w_in, w_in, c8, w_ada, b_ada)


def _attn_stages(sink_ref, q_ref, kvp_ref, kvc_ref, o_ref):
    n = pl.program_id(1)
    kvp = kvp_ref[...]
    kvc = kvc_ref[...]
    qi = lax.broadcasted_iota(jnp.int32, (WINDOW, WINDOW), 0)
    sj = lax.broadcasted_iota(jnp.int32, (WINDOW, WINDOW), 1)
    valid_prev = (sj > qi) & (n > 0)
    valid_cur = sj <= qi
    sink_col = sj == 0
    first_row = lax.broadcasted_iota(jnp.int32, (2 * WINDOW, 1), 0) == 0
    dh = ATTN_HEAD_DIM
    kv_w = ATTN_KV_HEADS * dh
    scale = jnp.asarray(dh ** -0.5, BF16)
    n_pairs = ATTN_GROUP // 2

    def group_operands(kv):
        kband = jnp.concatenate([kvp[:, kv * dh:(kv + 1) * dh],
                                 kvc[:, kv * dh:(kv + 1) * dh]], axis=0) * scale
        vband = jnp.concatenate([kvp[:, kv_w + kv * dh:kv_w + (kv + 1) * dh],
                                 kvc[:, kv_w + kv * dh:kv_w + (kv + 1) * dh]], axis=0)
        vband = jnp.where(first_row, jnp.zeros_like(vband), vband)
        zeros = jnp.zeros_like(kband)
        ones = jnp.ones_like(vband)
        k_pad = (jnp.concatenate([kband, zeros], axis=1), jnp.concatenate([zeros, kband], axis=1))
        pv_rhs = jnp.concatenate(
            [jnp.concatenate([vband, zeros, ones, zeros], axis=1),
             jnp.concatenate([zeros, vband, zeros, ones], axis=1)], axis=0)
        q_rows = jnp.concatenate(
            [q_ref[:, (kv * ATTN_GROUP + 2 * p) * dh:(kv * ATTN_GROUP + 2 * p + 2) * dh]
             for p in range(n_pairs)], axis=0)
        scores = [lax.dot_general(q_rows, k_pad[idx], (((1,), (1,)), ((), ())),
                                  preferred_element_type=F32) for idx in range(2)]
        return scores, pv_rhs

    nxt = group_operands(0)
    for kv in range(ATTN_KV_HEADS):
        scores, pv_rhs = nxt
        if kv + 1 < ATTN_KV_HEADS:
            nxt = group_operands(kv + 1)
        prob_rows = []
        for pair in range(n_pairs):
            rows = slice(pair * WINDOW, (pair + 1) * WINDOW)
            probs = []
            for idx in range(2):
                s = scores[idx][rows]
                sink = sink_ref[kv * ATTN_GROUP + 2 * pair + idx]
                s_prev = jnp.where(sink_col, sink, jnp.where(valid_prev, s[:, :WINDOW], NEG))
                s_cur = jnp.where(valid_cur, s[:, WINDOW:], NEG)
                m = jnp.max(jnp.maximum(s_prev, s_cur), axis=-1, keepdims=True)
                probs += [jnp.exp(s_prev - m).astype(BF16), jnp.exp(s_cur - m).astype(BF16)]
            prob_rows.append(jnp.concatenate(probs, axis=-1))
        r = jnp.dot(jnp.concatenate(prob_rows, axis=0), pv_rhs, preferred_element_type=F32)
        for pair in range(n_pairs):
            rows = slice(pair * WINDOW, (pair + 1) * WINDOW)
            col = (kv * ATTN_GROUP + 2 * pair) * dh
            o_ref[:, col:col + 2 * dh] = (r[rows, :2 * dh] * (1.0 / r[rows, 2 * dh:])).astype(BF16)
        yield


def _ret_decay_tables():
    lg = np.asarray(LOG_GAMMA, np.float64)[:, None, None]
    i = np.arange(RET_CHUNK, dtype=np.float64)
    diff = i[:, None] - i[None, :]
    k_scale = RET_DIM ** -0.5
    d_intra = np.where(diff >= 0, np.exp(np.maximum(diff, 0.0) * lg), 0.0) * k_scale
    lanes = np.ones((1, 1, RET_DIM // 2))
    d_q = np.exp((i[None, :, None] + 1.0) * lg) * lanes
    d_k = np.exp((RET_CHUNK - 1.0 - i[None, :, None]) * lg) * k_scale * lanes
    return jnp.asarray(d_intra, F32), jnp.asarray(d_q, F32), jnp.asarray(d_k, BF16)


def _ret_stages(pos_ref, invf_ref, di_ref, dq_ref, dk_ref, q_ref, k_ref, v_ref, g_ref, gain_ref,
                o_ref, state_ref):
    half = RET_DIM // 2
    ang = pos_ref[...].astype(F32) * invf_ref[...]
    cos = jnp.cos(ang).astype(BF16)
    sin = jnp.sin(ang).astype(BF16)

    def rot(t):
        t1, t2 = t[:, :half], t[:, half:]
        return jnp.concatenate([t1 * cos - t2 * sin, t1 * sin + t2 * cos], axis=-1)

    def both_halves(t, factor):
        return jnp.concatenate([t[:, :half] * factor, t[:, half:] * factor], axis=-1)

    def head_front(h):
        sl = slice(h * RET_DIM, (h + 1) * RET_DIM)
        qb = rot(q_ref[:, sl])
        kb = rot(k_ref[:, sl])
        intra = lax.dot_general(qb, kb, (((1,), (1,)), ((), ())),
                                preferred_element_type=F32) * di_ref[h]
        st = state_ref[h]
        cross = jnp.dot(qb, st.astype(BF16), preferred_element_type=F32)
        return kb, intra, st, cross

    front = head_front(0)
    for h in range(RET_HEADS):
        sl = slice(h * RET_DIM, (h + 1) * RET_DIM)
        kb, intra, st, cross = front
        if h + 1 < RET_HEADS:
            front = head_front(h + 1)
        vb = v_ref[:, sl]
        d_chunk = math.exp(RET_CHUNK * LOG_GAMMA[h])
        o = jnp.dot(intra.astype(BF16), vb, preferred_element_type=F32) + both_halves(cross, dq_ref[h])
        kd = both_halves(kb, dk_ref[h])
        state_ref[h] = st * d_chunk + lax.dot_general(kd, vb, (((0,), (0,)), ((), ())),
                                                      preferred_element_type=F32)
        o = o * lax.rsqrt(jnp.mean(o * o, axis=-1, keepdims=True) + EPS) * gain_ref[:, sl]
        gg = g_ref[:, sl].astype(F32)
        o_ref[:, sl] = (gg * jax.nn.sigmoid(gg) * o).astype(BF16)
        yield


def _mixers_kernel(sink_ref, qa_ref, kvp_ref, kvc_ref, pos_ref, invf_ref, di_ref, dq_ref, dk_ref,
                   qr_ref, kr_ref, vr_ref, gr_ref, gain_ref, attn_ref, ret_ref, state_ref):
    @pl.when(pl.program_id(1) == 0)
    def _():
        state_ref[...] = jnp.zeros_like(state_ref)

    attn = _attn_stages(sink_ref, qa_ref, kvp_ref, kvc_ref, attn_ref)
    ret = _ret_stages(pos_ref, invf_ref, di_ref, dq_ref, dk_ref, qr_ref, kr_ref, vr_ref, gr_ref,
                      gain_ref, ret_ref, state_ref)
    heads_per_group = RET_HEADS // ATTN_KV_HEADS
    for _ in range(ATTN_KV_HEADS):
        next(attn)
        for _ in range(heads_per_group):
            next(ret)


def _mixers(proj, proj_kv, sinks, pos, inv_freq, ret_gain, batch, seq):
    nb = seq // WINDOW
    t = batch * seq

    def col(cb):
        return lambda b, n: (b * nb + n, cb)

    d_intra, d_q, d_k = _ret_decay_tables()
    table = lambda lanes: pl.BlockSpec((RET_HEADS, RET_CHUNK, lanes), lambda b, n: (0, 0, 0))
    rows = pl.BlockSpec((WINDOW, D_MODEL), col(0))
    return pl.pallas_call(
        _mixers_kernel,
        grid=(batch, nb),
        in_specs=[pl.BlockSpec(memory_space=pltpu.SMEM),
                  pl.BlockSpec((WINDOW, D_MODEL), col(COL_QA)),
                  pl.BlockSpec((WINDOW, PROJ_TN), lambda b, n: (b * nb + jnp.maximum(n - 1, 0), 0)),
                  pl.BlockSpec((WINDOW, PROJ_TN), col(0)),
                  pl.BlockSpec((RET_CHUNK, 1), col(0)),
                  pl.BlockSpec((1, RET_DIM // 2), lambda b, n: (0, 0)),
                  table(RET_CHUNK), table(RET_DIM // 2), table(RET_DIM // 2),
                  pl.BlockSpec((RET_CHUNK, D_MODEL), col(COL_QR)),
                  pl.BlockSpec((RET_CHUNK, D_MODEL), col(COL_KR)),
                  pl.BlockSpec((RET_CHUNK, D_MODEL), col(COL_VR)),
                  pl.BlockSpec((RET_CHUNK, D_MODEL), col(COL_GR)),
                  pl.BlockSpec((1, D_MODEL), lambda b, n: (0, 0))],
        out_specs=[rows, rows],
        out_shape=[jax.ShapeDtypeStruct((t, D_MODEL), BF16), jax.ShapeDtypeStruct((t, D_MODEL), BF16)],
        scratch_shapes=[pltpu.VMEM((RET_HEADS, RET_DIM, RET_DIM), F32)],
        compiler_params=_params(("arbitrary", "arbitrary"), 40),
        name="mixers",
    )(sinks, proj, proj_kv, proj_kv, pos, inv_freq, d_intra, d_q, d_k, proj, proj, proj, proj, ret_gain)


def _route(logits):
    lane = lax.broadcasted_iota(jnp.int32, logits.shape, 1)
    lane_f = lane.astype(F32)
    is_g = lane < N_GROUPS
    gl = jnp.where(is_g, logits, NEG)
    gmax = jnp.max(gl, axis=-1, keepdims=True)
    gsel = jnp.min(jnp.where(gl == gmax, lane_f, float(LANES)), axis=-1, keepdims=True)
    gsum = jnp.sum(jnp.where(is_g, jnp.exp(gl - gmax), 0.0), axis=-1, keepdims=True)
    g_w = 1.0 / gsum
    grp = ((lane - N_GROUPS) >> 4).astype(F32)
    is_e = (lane >= N_GROUPS) & (lane < N_GROUPS + N_EXPERTS) & (grp == gsel)
    el = jnp.where(is_e, logits, NEG)
    v1 = jnp.max(el, axis=-1, keepdims=True)
    i1 = jnp.min(jnp.where(el == v1, lane_f, float(LANES)), axis=-1, keepdims=True)
    el2 = jnp.where(lane_f == i1, NEG, el)
    v2 = jnp.max(el2, axis=-1, keepdims=True)
    i2 = jnp.min(jnp.where(el2 == v2, lane_f, float(LANES)), axis=-1, keepdims=True)
    tt = jnp.exp(v2 - v1)
    w1 = g_w / (1.0 + tt)
    w2 = g_w * tt / (1.0 + tt)
    return jnp.where(lane == 0, i1 - N_GROUPS,
                     jnp.where(lane == 1, i2 - N_GROUPS,
                               jnp.where(lane == 2, w1, jnp.where(lane == 3, w2, 0.0))))


def _branch_kernel(attn_ref, ret_ref, ga_ref, gr_ref, wa_ref, wr_ref, o_ref):
    a = jnp.dot(attn_ref[...], wa_ref[...].astype(BF16), preferred_element_type=F32)
    r = jnp.dot(ret_ref[...], wr_ref[...].astype(BF16), preferred_element_type=F32)
    o_ref[...] = (jax.nn.sigmoid(ga_ref[...].astype(F32)) * a
                  + jax.nn.sigmoid(gr_ref[...].astype(F32)) * r).astype(BF16)


def _branch(attn, ret, proj, wa, wr):
    t = attn.shape[0]
    tm, tn = BRANCH_TM, BRANCH_TN
    per_slab = D_MODEL // tn
    return pl.pallas_call(
        _branch_kernel,
        grid=(t // tm, D_MODEL // tn),
        in_specs=[pl.BlockSpec((tm, D_MODEL), lambda i, j: (i, 0)),
                  pl.BlockSpec((tm, D_MODEL), lambda i, j: (i, 0)),
                  pl.BlockSpec((tm, tn), lambda i, j: (i, COL_GA * per_slab + j)),
                  pl.BlockSpec((tm, tn), lambda i, j: (i, COL_GRT * per_slab + j)),
                  pl.BlockSpec((D_MODEL, tn), lambda i, j: (0, j)),
                  pl.BlockSpec((D_MODEL, tn), lambda i, j: (0, j))],
        out_specs=pl.BlockSpec((tm, tn), lambda i, j: (i, j)),
        out_shape=jax.ShapeDtypeStruct((t, D_MODEL), BF16),
        compiler_params=_params(("arbitrary", "arbitrary"), 48),
        name="branch",
    )(attn, ret, proj, proj, wa, wr)


def _mixout_kernel(m_ref, x_ref, mod_ref, g2_ref, wo_ref, wrt_ref, brt_ref,
                   x1_ref, h2_ref, route_ref, count_ref, meta_ref, carry_ref):
    @pl.when(pl.program_id(0) == 0)
    def _():
        carry_ref[...] = jnp.zeros_like(carry_ref)

    mix = jnp.dot(m_ref[...], wo_ref[...], preferred_element_type=F32)
    x1 = x_ref[...] + mod_ref[0, 2:3, :] * mix
    x1_ref[...] = x1
    var = jnp.mean(x1 * x1, axis=-1, keepdims=True)
    h2 = x1 * lax.rsqrt(var + EPS) * g2_ref[...]
    h2 = h2 * (1.0 + mod_ref[0, 4:5, :]) + mod_ref[0, 3:4, :]
    tm = h2.shape[0]
    for j in range(SLAB_ROWS):
        lo = h2[:, j * LANES:(j + 1) * LANES]
        hi = h2[:, HALF_D + j * LANES:HALF_D + (j + 1) * LANES]
        h2_ref[_slab_rows(j, tm), :] = _pack_pair(lo, hi)
    h_hi = h2.astype(BF16)
    h_lo = (h2 - h_hi.astype(F32)).astype(BF16)
    w_rt = wrt_ref[...]
    w_hi = w_rt.astype(BF16)
    w_lo = (w_rt - w_hi.astype(F32)).astype(BF16)
    hi_both = jnp.dot(h_hi, jnp.concatenate([w_hi, w_lo], axis=1), preferred_element_type=F32)
    logits = (hi_both[:, :LANES] + hi_both[:, LANES:]
              + jnp.dot(h_lo, w_hi, preferred_element_type=F32) + brt_ref[...])
    route = _route(logits)

    lane = lax.broadcasted_iota(jnp.int32, route.shape, 1)
    lane_f = lane.astype(F32)
    hot1 = lane_f == route[:, 0:1]
    hot2 = lane_f == route[:, 1:2]
    both = jnp.where(hot1 | hot2, 1.0, 0.0)
    ii = lax.broadcasted_iota(jnp.int32, (tm, tm), 0)
    jj = lax.broadcasted_iota(jnp.int32, (tm, tm), 1)
    lower = jnp.where(ii > jj, 1.0, 0.0).astype(BF16)
    before = jnp.dot(lower, both.astype(BF16), preferred_element_type=F32) + carry_ref[...]
    r1 = jnp.sum(jnp.where(hot1, before, 0.0), axis=-1, keepdims=True)
    r2 = jnp.sum(jnp.where(hot2, before, 0.0), axis=-1, keepdims=True)
    route = jnp.where(lane == 4, r1, jnp.where(lane == 5, r2, route))
    route_ref[...] = route
    meta_ref[...] = route.T[0:SLAB_ROWS, :].astype(I32)
    carry = carry_ref[...] + jnp.sum(both, axis=0, keepdims=True)
    carry_ref[...] = carry
    count_ref[...] = carry


def _mixout(merged, x2, mod6, gain2, wo, w_rt, b_rt, seq):
    t = x2.shape[0]
    tm = MIXOUT_TM
    tiles_per_batch = seq // tm
    row = lambda i: (i, 0)
    const = lambda i: (0, 0)
    return pl.pallas_call(
        _mixout_kernel,
        grid=(t // tm,),
        in_specs=[pl.BlockSpec((tm, D_MODEL), row),
                  pl.BlockSpec((tm, D_MODEL), row),
                  pl.BlockSpec((1, 6, D_MODEL), lambda i: (i // tiles_per_batch, 0, 0)),
                  pl.BlockSpec((1, D_MODEL), const),
                  pl.BlockSpec((D_MODEL, D_MODEL), const, pipeline_mode=pl.Buffered(1)),
                  pl.BlockSpec((D_MODEL, LANES), const),
                  pl.BlockSpec((1, LANES), const)],
        out_specs=[pl.BlockSpec((tm, D_MODEL), row),
                   pl.BlockSpec((tm * SLAB_ROWS, SLAB_LANES), row),
                   pl.BlockSpec((tm, LANES), row),
                   pl.BlockSpec((1, LANES), const),
                   pl.BlockSpec((SLAB_ROWS, tm), lambda i: (0, i))],
        out_shape=[jax.ShapeDtypeStruct((t, D_MODEL), F32),
                   jax.ShapeDtypeStruct((t * SLAB_ROWS, SLAB_LANES), I32),
                   jax.ShapeDtypeStruct((t, LANES), F32),
                   jax.ShapeDtypeStruct((1, LANES), F32),
                   jax.ShapeDtypeStruct((SLAB_ROWS, t), I32)],
        scratch_shapes=[pltpu.VMEM((1, LANES), F32)],
        compiler_params=_params(("arbitrary",), 56),
        name="mixout",
    )(merged, x2, mod6, gain2, wo, w_rt, b_rt)


PAD_BITS = (64, 32, 16, 8, 4, 2, 1)


def _sorted_row(tables, tok, k):
    e_refs, r_refs, blk_row_ref = tables[0:TOP_K], tables[TOP_K:2 * TOP_K], tables[2 * TOP_K]
    return blk_row_ref[e_refs[k][tok]] + r_refs[k][tok]


N_ROUTE_TABLES = 2 * TOP_K + 1


def _dispatch_kernel(*refs):
    tables = refs[:N_ROUTE_TABLES]
    zstart_ref, zcount_ref, h2_ref, xs_ref, dest_ref, zero_ref, sem, zsem = refs[N_ROUTE_TABLES:]
    n_assign = tables[0].shape[0] * TOP_K
    zero_ref[...] = jnp.zeros_like(zero_ref)

    def zero_copy(start, rows):
        return pltpu.make_async_copy(zero_ref.at[pl.ds(0, rows * SLAB_ROWS), :],
                                     xs_ref.at[pl.ds(start * SLAB_ROWS, rows * SLAB_ROWS), :], zsem)

    def fill(e, wait):
        start = zstart_ref[e]
        pad = zcount_ref[e]
        for bit in PAD_BITS:
            @pl.when((pad & bit) != 0)
            def _(start=start, bit=bit):
                cp = zero_copy(start, bit)
                cp.wait() if wait else cp.start()
            start = start + (pad & bit)

    lax.fori_loop(0, N_EXPERTS, lambda e, c: (fill(e, False), c)[1], 0)

    def issue(tok, carry):
        src = h2_ref.at[pl.ds(pl.multiple_of(tok * SLAB_ROWS, SLAB_ROWS), SLAB_ROWS), :]
        for k in range(TOP_K):
            row = _sorted_row(tables, tok, k)
            dest_ref[tok * TOP_K + k] = row
            dst = pl.multiple_of(row * SLAB_ROWS, SLAB_ROWS)
            pltpu.make_async_copy(src, xs_ref.at[pl.ds(dst, SLAB_ROWS), :], sem).start(priority=k % 2)
        return carry

    lax.fori_loop(0, n_assign // TOP_K, issue, 0, unroll=8)

    def drain(i, carry):
        pltpu.make_async_copy(h2_ref.at[pl.ds(0, ROWS_PER_WAIT * SLAB_ROWS), :],
                              xs_ref.at[pl.ds(0, ROWS_PER_WAIT * SLAB_ROWS), :], sem).wait()
        return carry

    lax.fori_loop(0, n_assign // ROWS_PER_WAIT, drain, 0)
    lax.fori_loop(0, N_EXPERTS, lambda e, c: (fill(e, True), c)[1], 0)


def _dispatch(tables, zstart, zcount, h2_slab, n_pad):
    n_assign = tables[0].shape[0] * TOP_K
    return pl.pallas_call(
        _dispatch_kernel,
        grid_spec=pltpu.PrefetchScalarGridSpec(
            num_scalar_prefetch=N_ROUTE_TABLES + 2,
            grid=(1,),
            in_specs=[pl.BlockSpec(memory_space=pltpu.VMEM)],
            out_specs=[pl.BlockSpec(memory_space=pl.ANY),
                       pl.BlockSpec(memory_space=pltpu.SMEM)],
            scratch_shapes=[pltpu.VMEM((PAD_BITS[0] * SLAB_ROWS, SLAB_LANES), I32),
                            pltpu.SemaphoreType.DMA(()),
                            pltpu.SemaphoreType.DMA(())]),
        out_shape=[jax.ShapeDtypeStruct((n_pad * SLAB_ROWS, SLAB_LANES), I32),
                   jax.ShapeDtypeStruct((n_assign,), I32)],
        compiler_params=_params(("arbitrary",), 40),
        name="dispatch",
    )(*tables, zstart, zcount, h2_slab)


def _experts_kernel(item_e_ref, item_blk_ref, item_nb_ref, x_blk_ref,
                    x0_ref, x1_ref, x2_ref, x3_ref, wg_hbm, wu_hbm, wd_hbm, y_ref,
                    xb_ref, yp_ref, wg_ref, wu_ref, wd_ref, sem, wsem):
    w = pl.program_id(0)
    n_items = pl.num_programs(0)
    nb = item_nb_ref[w]
    blk0 = item_blk_ref[w]
    x_refs = (x0_ref, x1_ref, x2_ref, x3_ref)

    blk_rows = MOE_BLOCK * SLAB_ROWS

    fc = wg_ref.shape[2]
    n_chunks = EXPERT_DIM // fc

    def weight_copies(item, chunk, slot):
        e = item_e_ref[item]
        col = pl.multiple_of(chunk * fc, fc)
        return (pltpu.make_async_copy(wg_hbm.at[e, :, pl.ds(col, fc)], wg_ref.at[slot], wsem.at[slot]),
                pltpu.make_async_copy(wu_hbm.at[e, :, pl.ds(col, fc)], wu_ref.at[slot], wsem.at[slot]),
                pltpu.make_async_copy(wd_hbm.at[e, pl.ds(col, fc), :], wd_ref.at[slot], wsem.at[slot]))

    def start_weights(tile):
        item = jnp.minimum(tile // n_chunks, n_items - 1)

        @pl.when((tile < n_items * n_chunks) & (item_nb_ref[item] > 0))
        def _():
            for cp in weight_copies(item, tile % n_chunks, tile % EXPERT_WEIGHT_SLOTS):
                cp.start()

    @pl.when(w == 0)
    def _():
        for tile in range(EXPERT_WEIGHT_SLOTS - 1):
            start_weights(tile)

    def out_copy(s):
        return pltpu.make_async_copy(yp_ref.at[s], y_ref.at[pl.ds((blk0 + s) * blk_rows, blk_rows), :],
                                     sem.at[s])

    def wait_out(count):
        for s in range(ITEM_BLOCKS):
            @pl.when(s < count)
            def _(s=s):
                out_copy(s).wait()

    def run(n_live):
        rows = n_live * MOE_BLOCK
        for s in range(n_live):
            for j in range(SLAB_ROWS):
                lo, hi = _unpack_pair(x_refs[s][_slab_rows(j, MOE_BLOCK), :])
                r0 = s * MOE_BLOCK
                xb_ref[r0:r0 + MOE_BLOCK, j * LANES:(j + 1) * LANES] = lo.astype(BF16)
                xb_ref[r0:r0 + MOE_BLOCK, HALF_D + j * LANES:HALF_D + (j + 1) * LANES] = hi.astype(BF16)
        xs = xb_ref[0:rows, :]

        y = None
        for c in range(n_chunks):
            tile = w * n_chunks + c
            start_weights(tile + EXPERT_WEIGHT_SLOTS - 1)
            slot = tile % EXPERT_WEIGHT_SLOTS
            for cp in weight_copies(w, c, slot):
                cp.wait()
            g = jnp.dot(xs, wg_ref[slot].astype(BF16), preferred_element_type=F32)
            u = jnp.dot(xs, wu_ref[slot].astype(BF16), preferred_element_type=F32)
            hid = (g * jax.nn.sigmoid(g) * u).astype(BF16)
            y_c = jnp.dot(hid, wd_ref[slot].astype(BF16), preferred_element_type=F32)
            y = y_c if y is None else y + y_c

        @pl.when(w > 0)
        def _():
            wait_out(item_nb_ref[jnp.maximum(w - 1, 0)])

        for s in range(n_live):
            r0 = s * MOE_BLOCK
            for j in range(SLAB_ROWS):
                yp_ref[s, _slab_rows(j, MOE_BLOCK), :] = _pack_pair(
                    y[r0:r0 + MOE_BLOCK, j * LANES:(j + 1) * LANES],
                    y[r0:r0 + MOE_BLOCK, HALF_D + j * LANES:HALF_D + (j + 1) * LANES])
            out_copy(s).start()

    for n_live in range(1, ITEM_BLOCKS + 1):
        pl.when(nb == n_live)(functools.partial(run, n_live))

    @pl.when((nb == 0) & (w > 0))
    def _():
        wait_out(item_nb_ref[jnp.maximum(w - 1, 0)])

    @pl.when(w == n_items - 1)
    def _():
        wait_out(nb)


def _experts(item_e, item_blk, item_nb, x_sorted, w_gate, w_up, w_down, n_blocks):
    n_items = item_e.shape[0]
    d = D_MODEL
    fc = EXPERT_FC
    x3 = x_sorted
    blk_rows = MOE_BLOCK * SLAB_ROWS

    slot = jnp.arange(ITEM_BLOCKS, dtype=jnp.int32)[:, None]
    x_blk = jnp.maximum(lax.cummax(jnp.where(slot < item_nb[None, :], item_blk[None, :] + slot, -1), axis=1), 0)
    x_blk = x_blk.reshape(-1).astype(jnp.int32)

    def x_spec(s):
        return pl.BlockSpec((blk_rows, SLAB_LANES),
                            lambda w, ie, ib, inb, xb: (xb[s * n_items + w], 0))

    hbm = pl.BlockSpec(memory_space=pl.ANY)
    slots = EXPERT_WEIGHT_SLOTS
    return pl.pallas_call(
        _experts_kernel,
        grid_spec=pltpu.PrefetchScalarGridSpec(
            num_scalar_prefetch=4,
            grid=(n_items,),
            in_specs=[x_spec(0), x_spec(1), x_spec(2), x_spec(3), hbm, hbm, hbm],
            out_specs=pl.BlockSpec(memory_space=pl.ANY),
            scratch_shapes=[pltpu.VMEM((ITEM_BLOCKS * MOE_BLOCK, d), BF16),
                            pltpu.VMEM((ITEM_BLOCKS, blk_rows, SLAB_LANES), I32),
                            pltpu.VMEM((slots, d, fc), F32),
                            pltpu.VMEM((slots, d, fc), F32),
                            pltpu.VMEM((slots, fc, d), F32),
                            pltpu.SemaphoreType.DMA((ITEM_BLOCKS,)),
                            pltpu.SemaphoreType.DMA((slots,))]),
        out_shape=jax.ShapeDtypeStruct((n_blocks * blk_rows, SLAB_LANES), I32),
        compiler_params=_params(("arbitrary",), 56),
        name="experts",
    )(item_e, item_blk, item_nb, x_blk, x3, x3, x3, x3, w_gate, w_up, w_down)


def _combine_kernel(dest_ref, x1_ref, route_ref, mod_ref, gain_ref, y_ref, o_ref, ybuf_ref, sem):
    tm = COMBINE_TM
    i = pl.program_id(0)
    n_tiles = pl.num_programs(0)

    def row_copy(slot, k, r, src):
        return pltpu.make_async_copy(
            y_ref.at[pl.ds(pl.multiple_of(src * SLAB_ROWS, SLAB_ROWS), SLAB_ROWS), :],
            ybuf_ref.at[slot, k, pl.ds(pl.multiple_of(r * SLAB_ROWS, SLAB_ROWS), SLAB_ROWS), :],
            sem.at[slot])

    def wait_slot(slot):
        for k in range(TOP_K):
            pltpu.make_async_copy(y_ref.at[pl.ds(0, tm * SLAB_ROWS), :], ybuf_ref.at[slot, k],
                                  sem.at[slot]).wait()

    def issue_tile(tile, slot):
        base = tile * tm * TOP_K

        def issue(r, carry):
            for k in range(TOP_K):
                row_copy(slot, k, r, dest_ref[base + r * TOP_K + k]).start(priority=k % 2)
            return carry

        lax.fori_loop(0, tm, issue, 0, unroll=4)

    @pl.when(i == 0)
    def _():
        for tile in range(COMBINE_SLOTS - 1):
            issue_tile(tile, tile)

    slot = i % COMBINE_SLOTS
    wait_slot(slot)

    ahead = i + COMBINE_SLOTS - 1
    ahead_slot = ahead % COMBINE_SLOTS
    ahead_base = jnp.minimum(ahead, n_tiles - 1) * (tm * TOP_K)
    n_stages = 2 * SLAB_ROWS
    rows_per_stage = tm // n_stages

    route = route_ref[...]
    w0 = route[:, 2:3]
    w1 = route[:, 3:4]
    ssq = jnp.zeros((tm, 1), F32)
    stage = 0
    for j in range(SLAB_ROWS):
        rows = _slab_rows(j, tm)
        y0 = _unpack_pair(ybuf_ref[slot, 0, rows, :])
        y1 = _unpack_pair(ybuf_ref[slot, 1, rows, :])
        for part, off in ((0, j * LANES), (1, HALF_D + j * LANES)):
            cols = slice(off, off + LANES)
            ffn = w0 * y0[part] + w1 * y1[part]
            x2 = x1_ref[:, cols] + mod_ref[0, 5:6, cols] * ffn
            o_ref[:, cols] = x2
            ssq = ssq + jnp.sum(x2 * x2, axis=-1, keepdims=True)
            for r in range(stage * rows_per_stage, (stage + 1) * rows_per_stage):
                for k in range(TOP_K):
                    row_copy(ahead_slot, k, r, dest_ref[ahead_base + r * TOP_K + k]).start(priority=k % 2)
            stage += 1
    o_ref[...] = o_ref[...] * lax.rsqrt(ssq * (1.0 / D_MODEL) + EPS) * gain_ref[...]

    @pl.when(i == n_tiles - 1)
    def _():
        for extra in range(1, COMBINE_SLOTS):
            wait_slot((i + extra) % COMBINE_SLOTS)


def _combine(dest, x1, route, mod6, gain, y_sorted, seq):
    t, d = x1.shape
    tm = COMBINE_TM
    tiles_per_batch = seq // tm
    return pl.pallas_call(
        _combine_kernel,
        grid_spec=pltpu.PrefetchScalarGridSpec(
            num_scalar_prefetch=1,
            grid=(t // tm,),
            in_specs=[pl.BlockSpec((tm, d), lambda i, *_: (i, 0)),
                      pl.BlockSpec((tm, LANES), lambda i, *_: (i, 0)),
                      pl.BlockSpec((1, 6, d), lambda i, *_: (i // tiles_per_batch, 0, 0)),
                      pl.BlockSpec((1, d), lambda i, *_: (0, 0)),
                      pl.BlockSpec(memory_space=pl.ANY)],
            out_specs=pl.BlockSpec((tm, d), lambda i, *_: (i, 0)),
            scratch_shapes=[pltpu.VMEM((COMBINE_SLOTS, TOP_K, tm * SLAB_ROWS, SLAB_LANES), I32),
                            pltpu.SemaphoreType.DMA((COMBINE_SLOTS,))]),
        out_shape=jax.ShapeDtypeStruct((t, d), F32),
        compiler_params=_params(("arbitrary",), 24),
        name="combine",
    )(dest, x1, route, mod6, gain, y_sorted)


def _dispatch_tables(meta, counts, t):
    n_assign = t * TOP_K
    n_pad = -(-(n_assign + N_EXPERTS * (MOE_BLOCK - 1)) // MOE_BLOCK) * MOE_BLOCK
    n_blocks = n_pad // MOE_BLOCK
    n_items = N_EXPERTS + n_assign // (ITEM_BLOCKS * MOE_BLOCK)

    cnt = counts[0, :N_EXPERTS].astype(jnp.int32)
    blocks_e = (cnt + MOE_BLOCK - 1) // MOE_BLOCK
    blk_end = jnp.cumsum(blocks_e)
    blk_start = blk_end - blocks_e
    tables = (meta[0], meta[1], meta[4], meta[5], (blk_start * MOE_BLOCK).astype(jnp.int32))
    zstart = (blk_start * MOE_BLOCK + cnt).astype(jnp.int32)
    zcount = (blocks_e * MOE_BLOCK - cnt).astype(jnp.int32)

    items_e = (blocks_e + ITEM_BLOCKS - 1) // ITEM_BLOCKS
    item_end = jnp.cumsum(items_e)
    item_start = item_end - items_e
    w = jnp.arange(n_items, dtype=jnp.int32)
    live = w < item_end[-1]
    w_live = jnp.minimum(w, item_end[-1] - 1)
    e_w = jnp.minimum(jnp.sum((item_end[None, :] <= w_live[:, None]).astype(jnp.int32), axis=1), N_EXPERTS - 1)
    j_w = w_live - item_start[e_w]
    item_blk = (blk_start[e_w] + ITEM_BLOCKS * j_w).astype(jnp.int32)
    item_nb = jnp.where(live, jnp.clip(blocks_e[e_w] - ITEM_BLOCKS * j_w, 0, ITEM_BLOCKS), 0).astype(jnp.int32)
    return tables, zstart, zcount, e_w, item_blk, item_nb, n_blocks


def kernel(x, c, positions, norm1_gain, norm2_gain, final_norm_gain, w_ada, b_ada, w_in, attn_sinks,
           ret_norm_gain, w_branch_attn, w_branch_ret, w_out, w_router_group, b_router_group,
           w_router_expert, b_router_expert, w_expert_gate, w_expert_up, w_expert_down):
    batch, seq, d = x.shape
    t = batch * seq
    depth = w_ada.shape[0]
    half = RET_DIM // 2
    inv_freq = (ROPE_BASE ** (-jnp.arange(half, dtype=F32) / half)).reshape(1, half)
    pos = positions.reshape(t, 1)
    c8 = jnp.pad(c, ((0, 8 - batch), (0, 0)))
    xf = x.reshape(t, d)

    assert depth == 1, "the fused final norm assumes a single layer"
    for layer in range(depth):
        b_ada2 = b_ada[layer].reshape(1, -1)
        mod_early = _ada(c8, w_ada[layer], b_ada2, ADA_EARLY)[:batch]
        proj, proj_kv, mod_late = _proj(xf, norm1_gain[layer].reshape(1, d), mod_early.reshape(batch, 2, d),
                                        w_in[layer], c8, w_ada[layer], b_ada2, seq)
        mod6 = jnp.concatenate([mod_early, mod_late[:batch]], axis=1).reshape(batch, 6, d)
        attn, ret = _mixers(proj, proj_kv, attn_sinks[layer], pos, inv_freq,
                            ret_norm_gain[layer].reshape(1, d), batch, seq)

        pad = LANES - N_GROUPS - N_EXPERTS
        w_rt = jnp.concatenate([w_router_group[layer], w_router_expert[layer],
                                jnp.zeros((d, pad), F32)], axis=1)
        b_rt = jnp.concatenate([b_router_group[layer], b_router_expert[layer],
                                jnp.zeros((pad,), F32)]).reshape(1, LANES)
        merged = _branch(attn, ret, proj, w_branch_attn[layer], w_branch_ret[layer])
        x1, h2, route, counts, meta = _mixout(merged, xf, mod6, norm2_gain[layer].reshape(1, d),
                                              w_out[layer].astype(BF16), w_rt, b_rt, seq)
        tables, zstart, zcount, item_e, item_blk, item_nb, n_blocks = _dispatch_tables(meta, counts, t)
        x_sorted, dest = _dispatch(tables, zstart, zcount, h2, n_blocks * MOE_BLOCK)
        y_sorted = _experts(item_e, item_blk, item_nb, x_sorted,
                            w_expert_gate[layer], w_expert_up[layer], w_expert_down[layer], n_blocks)
        xf = _combine(dest, x1, route, mod6, final_norm_gain.reshape(1, d), y_sorted, seq)
    return xf.reshape(batch, seq, d)
```

```python
import functools
import math

import jax
import jax.numpy as jnp
import numpy as np
from jax import lax
from jax.experimental import pallas as pl
from jax.experimental.pallas import tpu as pltpu

F32 = jnp.float32
BF16 = jnp.bfloat16

D_MODEL = 2048
ATTN_HEAD_DIM = 64
ATTN_HEADS = 32
ATTN_KV_HEADS = 4
ATTN_GROUP = 8
WINDOW = 128
RET_HEADS = 8
RET_DIM = 256
RET_CHUNK = 128
ROPE_BASE = 10000.0
N_GROUPS = 4
EXPERTS_PER_GROUP = 16
N_EXPERTS = 64
TOP_K = 2
EXPERT_DIM = 1024
MOE_BLOCK = 128
EPS = 1e-6
NEG = -1e30

MIB = 1024 * 1024
LANES = 128
PROJ_TN = 512
PROJ_TM = 1024
KV_SRC_TILE = 4
ADA_EARLY = 2 * D_MODEL
ADA_LATE_TN = 256
ADA_LATE_STEPS = 4 * D_MODEL // ADA_LATE_TN
BRANCH_TM = 1024
BRANCH_TN = 512
MIXOUT_TM = 512
ITEM_BLOCKS = 4
EXPERT_FC = 512
EXPERT_WEIGHT_SLOTS = 3
COMBINE_TM = 256
COMBINE_SLOTS = 3
SLAB_ROWS = 8
SLAB_LANES = LANES
HALF_D = D_MODEL // 2
ROWS_PER_WAIT = 128
I32 = jnp.int32

COL_QA, COL_QR, COL_KR, COL_VR, COL_GR, COL_GA, COL_GRT = 0, 1, 2, 3, 4, 5, 6

LOG_GAMMA = [math.log1p(-(2.0 ** (-5.0 - h))) for h in range(RET_HEADS)]


def _params(sem, vmem_mib):
    return pltpu.CompilerParams(dimension_semantics=sem, vmem_limit_bytes=vmem_mib * MIB)


def _pack_pair(lo, hi):
    lo_b = lax.bitcast_convert_type(lo.astype(BF16).astype(F32), I32)
    hi_b = lax.bitcast_convert_type(hi.astype(BF16).astype(F32), I32)
    return hi_b | lax.shift_right_logical(lo_b, jnp.full_like(lo_b, 16))


def _unpack_pair(w):
    lo = lax.bitcast_convert_type(w << 16, F32)
    hi = lax.bitcast_convert_type(w & jnp.int32(-65536), F32)
    return lo, hi


def _slab_rows(j, n_tokens):
    return pl.ds(j, n_tokens, stride=SLAB_ROWS)


def _ada_kernel(c_ref, w_ref, b_ref, o_ref):
    c = c_ref[...]
    a = (c * jax.nn.sigmoid(c)).astype(BF16)
    o_ref[...] = jnp.dot(a, w_ref[...].astype(BF16), preferred_element_type=F32) + b_ref[...]


def _ada(c8, w_ada, b_ada, n):
    tn = 1024
    return pl.pallas_call(
        _ada_kernel,
        grid=(n // tn,),
        in_specs=[pl.BlockSpec((8, D_MODEL), lambda j: (0, 0)),
                  pl.BlockSpec((D_MODEL, tn), lambda j: (0, j)),
                  pl.BlockSpec((1, tn), lambda j: (0, j))],
        out_specs=pl.BlockSpec((8, tn), lambda j: (0, j)),
        out_shape=jax.ShapeDtypeStruct((8, n), F32),
        compiler_params=_params(("arbitrary",), 40),
        name="ada",
    )(c8, w_ada, b_ada)


def _proj_kernel(x_ref, g_ref, mod_ref, wlo_ref, whi_ref, c_ref, wada_ref, bada_ref,
                 o_ref, kv_ref, late_ref, h0_ref, h1_ref):
    i = pl.program_id(0)
    v = pl.program_id(1)
    last = pl.num_programs(1) - 1

    def normed():
        x = x_ref[...]
        var = jnp.mean(x * x, axis=-1, keepdims=True)
        y = x * lax.rsqrt(var + EPS) * g_ref[...]
        return (y * (1.0 + mod_ref[0, 1:2, :]) + mod_ref[0, 0:1, :]).astype(BF16)

    @pl.when(pl.program_id(0) * pl.num_programs(1) + v < ADA_LATE_STEPS)
    def _():
        _ada_kernel(c_ref, wada_ref, bada_ref, late_ref)

    @pl.when((i == 0) & (v == 0))
    def _():
        h0_ref[...] = normed()

    for parity, (h_ref, h_next_ref) in enumerate(((h0_ref, h1_ref), (h1_ref, h0_ref))):
        @pl.when((v < last) & (i % 2 == parity))
        def _(h_ref=h_ref):
            w = jnp.concatenate([wlo_ref[...].astype(BF16), whi_ref[...].astype(BF16)], axis=1)
            o_ref[...] = jnp.dot(h_ref[...], w, preferred_element_type=F32).astype(BF16)

        @pl.when((v == last) & (i % 2 == parity))
        def _(h_ref=h_ref, h_next_ref=h_next_ref):
            kv_ref[...] = jnp.dot(h_ref[...], wlo_ref[...].astype(BF16),
                                  preferred_element_type=F32).astype(BF16)
            h_next_ref[...] = normed()


def _proj_w_tile(v, n_wide):
    return jnp.where(v < 2, 2 * v, jnp.where(v < n_wide, 2 * v + 1, KV_SRC_TILE))


def _proj(x2, gain, mod_early, w_in, c8, w_ada, b_ada, seq):
    t = x2.shape[0]
    n = w_in.shape[1]
    tiles_per_batch = seq // PROJ_TM
    n_wide = (n - PROJ_TN) // (2 * PROJ_TN)
    n_late = w_ada.shape[1] - ADA_EARLY
    assert n_late == ADA_LATE_STEPS * ADA_LATE_TN and ADA_LATE_STEPS <= (t // PROJ_TM) * (n_wide + 1)
    early_tiles = ADA_EARLY // ADA_LATE_TN

    def late_tile(i, v):
        return jnp.minimum(i * (n_wide + 1) + v, ADA_LATE_STEPS - 1)

    n_row_tiles = t // PROJ_TM

    def norm_tile(i, v):
        return jnp.minimum(i + (v == n_wide).astype(jnp.int32), n_row_tiles - 1)

    return pl.pallas_call(
        _proj_kernel,
        grid=(n_row_tiles, n_wide + 1),
        in_specs=[pl.BlockSpec((PROJ_TM, D_MODEL), lambda i, v: (norm_tile(i, v), 0)),
                  pl.BlockSpec((1, D_MODEL), lambda i, v: (0, 0)),
                  pl.BlockSpec((1, 2, D_MODEL), lambda i, v: (norm_tile(i, v) // tiles_per_batch, 0, 0)),
                  pl.BlockSpec((D_MODEL, PROJ_TN), lambda i, v: (0, _proj_w_tile(v, n_wide))),
                  pl.BlockSpec((D_MODEL, PROJ_TN),
                               lambda i, v: (0, jnp.where(v < n_wide, _proj_w_tile(v, n_wide) + 1, KV_SRC_TILE))),
                  pl.BlockSpec((8, D_MODEL), lambda i, v: (0, 0)),
                  pl.BlockSpec((D_MODEL, ADA_LATE_TN), lambda i, v: (0, early_tiles + late_tile(i, v))),
                  pl.BlockSpec((1, ADA_LATE_TN), lambda i, v: (0, early_tiles + late_tile(i, v)))],
        out_specs=[pl.BlockSpec((PROJ_TM, 2 * PROJ_TN), lambda i, v: (i, jnp.minimum(v, n_wide - 1))),
                   pl.BlockSpec((PROJ_TM, PROJ_TN), lambda i, v: (i, 0)),
                   pl.BlockSpec((8, ADA_LATE_TN), lambda i, v: (0, late_tile(i, v)))],
        out_shape=[jax.ShapeDtypeStruct((t, n - PROJ_TN), BF16),
                   jax.ShapeDtypeStruct((t, PROJ_TN), BF16),
                   jax.ShapeDtypeStruct((8, n_late), F32)],
        scratch_shapes=[pltpu.VMEM((PROJ_TM, D_MODEL), BF16), pltpu.VMEM((PROJ_TM, D_MODEL), BF16)],
        compiler_params=_params(("arbitrary", "arbitrary"), 60),
        name="proj",
    )(x2, gain, mod_early, w_in, w_in, c8, w_ada, b_ada)


def _attn_stages(sink_ref, q_ref, kvp_ref, kvc_ref, o_ref):
    n = pl.program_id(1)
    kvp = kvp_ref[...]
    kvc = kvc_ref[...]
    qi = lax.broadcasted_iota(jnp.int32, (WINDOW, WINDOW), 0)
    sj = lax.broadcasted_iota(jnp.int32, (WINDOW, WINDOW), 1)
    valid_prev = (sj > qi) & (n > 0)
    valid_cur = sj <= qi
    sink_col = sj == 0
    first_row = lax.broadcasted_iota(jnp.int32, (2 * WINDOW, 1), 0) == 0
    dh = ATTN_HEAD_DIM
    kv_w = ATTN_KV_HEADS * dh
    scale = jnp.asarray(dh ** -0.5, BF16)
    n_pairs = ATTN_GROUP // 2

    def group_operands(kv):
        kband = jnp.concatenate([kvp[:, kv * dh:(kv + 1) * dh],
                                 kvc[:, kv * dh:(kv + 1) * dh]], axis=0) * scale
        vband = jnp.concatenate([kvp[:, kv_w + kv * dh:kv_w + (kv + 1) * dh],
                                 kvc[:, kv_w + kv * dh:kv_w + (kv + 1) * dh]], axis=0)
        vband = jnp.where(first_row, jnp.zeros_like(vband), vband)
        zeros = jnp.zeros_like(kband)
        ones = jnp.ones_like(vband)
        k_pad = (jnp.concatenate([kband, zeros], axis=1), jnp.concatenate([zeros, kband], axis=1))
        pv_rhs = jnp.concatenate(
            [jnp.concatenate([vband, zeros, ones, zeros], axis=1),
             jnp.concatenate([zeros, vband, zeros, ones], axis=1)], axis=0)
        q_rows = jnp.concatenate(
            [q_ref[:, (kv * ATTN_GROUP + 2 * p) * dh:(kv * ATTN_GROUP + 2 * p + 2) * dh]
             for p in range(n_pairs)], axis=0)
        scores = [lax.dot_general(q_rows, k_pad[idx], (((1,), (1,)), ((), ())),
                                  preferred_element_type=F32) for idx in range(2)]
        return scores, pv_rhs

    nxt = group_operands(0)
    for kv in range(ATTN_KV_HEADS):
        scores, pv_rhs = nxt
        if kv + 1 < ATTN_KV_HEADS:
            nxt = group_operands(kv + 1)
        prob_rows = []
        for pair in range(n_pairs):
            rows = slice(pair * WINDOW, (pair + 1) * WINDOW)
            probs = []
            for idx in range(2):
                s = scores[idx][rows]
                sink = sink_ref[kv * ATTN_GROUP + 2 * pair + idx]
                s_prev = jnp.where(sink_col, sink, jnp.where(valid_prev, s[:, :WINDOW], NEG))
                s_cur = jnp.where(valid_cur, s[:, WINDOW:], NEG)
                m = jnp.max(jnp.maximum(s_prev, s_cur), axis=-1, keepdims=True)
                probs += [jnp.exp(s_prev - m).astype(BF16), jnp.exp(s_cur - m).astype(BF16)]
            prob_rows.append(jnp.concatenate(probs, axis=-1))
        r = jnp.dot(jnp.concatenate(prob_rows, axis=0), pv_rhs, preferred_element_type=F32)
        for pair in range(n_pairs):
            rows = slice(pair * WINDOW, (pair + 1) * WINDOW)
            col = (kv * ATTN_GROUP + 2 * pair) * dh
            o_ref[:, col:col + 2 * dh] = (r[rows, :2 * dh] * (1.0 / r[rows, 2 * dh:])).astype(BF16)
        yield


def _ret_decay_tables():
    lg = np.asarray(LOG_GAMMA, np.float64)[:, None, None]
    i = np.arange(RET_CHUNK, dtype=np.float64)
    diff = i[:, None] - i[None, :]
    k_scale = RET_DIM ** -0.5
    d_intra = np.where(diff >= 0, np.exp(np.maximum(diff, 0.0) * lg), 0.0) * k_scale
    lanes = np.ones((1, 1, RET_DIM // 2))
    d_q = np.exp((i[None, :, None] + 1.0) * lg) * lanes
    d_k = np.exp((RET_CHUNK - 1.0 - i[None, :, None]) * lg) * k_scale * lanes
    return jnp.asarray(d_intra, F32), jnp.asarray(d_q, F32), jnp.asarray(d_k, BF16)


def _ret_stages(pos_ref, invf_ref, di_ref, dq_ref, dk_ref, q_ref, k_ref, v_ref, g_ref, gain_ref,
                o_ref, state_ref):
    half = RET_DIM // 2
    ang = pos_ref[...].astype(F32) * invf_ref[...]
    cos = jnp.cos(ang).astype(BF16)
    sin = jnp.sin(ang).astype(BF16)

    def rot(t):
        t1, t2 = t[:, :half], t[:, half:]
        return jnp.concatenate([t1 * cos - t2 * sin, t1 * sin + t2 * cos], axis=-1)

    def both_halves(t, factor):
        return jnp.concatenate([t[:, :half] * factor, t[:, half:] * factor], axis=-1)

    def head_front(h):
        sl = slice(h * RET_DIM, (h + 1) * RET_DIM)
        qb = rot(q_ref[:, sl])
        kb = rot(k_ref[:, sl])
        intra = lax.dot_general(qb, kb, (((1,), (1,)), ((), ())),
                                preferred_element_type=F32) * di_ref[h]
        st = state_ref[h]
        cross = jnp.dot(qb, st.astype(BF16), preferred_element_type=F32)
        return kb, intra, st, cross

    front = head_front(0)
    for h in range(RET_HEADS):
        sl = slice(h * RET_DIM, (h + 1) * RET_DIM)
        kb, intra, st, cross = front
        if h + 1 < RET_HEADS:
            front = head_front(h + 1)
        vb = v_ref[:, sl]
        d_chunk = math.exp(RET_CHUNK * LOG_GAMMA[h])
        o = jnp.dot(intra.astype(BF16), vb, preferred_element_type=F32) + both_halves(cross, dq_ref[h])
        kd = both_halves(kb, dk_ref[h])
        state_ref[h] = st * d_chunk + lax.dot_general(kd, vb, (((0,), (0,)), ((), ())),
                                                      preferred_element_type=F32)
        o = o * lax.rsqrt(jnp.mean(o * o, axis=-1, keepdims=True) + EPS) * gain_ref[:, sl]
        gg = g_ref[:, sl].astype(F32)
        o_ref[:, sl] = (gg * jax.nn.sigmoid(gg) * o).astype(BF16)
        yield


def _mixers_kernel(sink_ref, qa_ref, kvp_ref, kvc_ref, pos_ref, invf_ref, di_ref, dq_ref, dk_ref,
                   qr_ref, kr_ref, vr_ref, gr_ref, gain_ref, attn_ref, ret_ref, state_ref):
    @pl.when(pl.program_id(1) == 0)
    def _():
        state_ref[...] = jnp.zeros_like(state_ref)

    attn = _attn_stages(sink_ref, qa_ref, kvp_ref, kvc_ref, attn_ref)
    ret = _ret_stages(pos_ref, invf_ref, di_ref, dq_ref, dk_ref, qr_ref, kr_ref, vr_ref, gr_ref,
                      gain_ref, ret_ref, state_ref)
    heads_per_group = RET_HEADS // ATTN_KV_HEADS
    for _ in range(ATTN_KV_HEADS):
        next(attn)
        for _ in range(heads_per_group):
            next(ret)


def _mixers(proj, proj_kv, sinks, pos, inv_freq, ret_gain, batch, seq):
    nb = seq // WINDOW
    t = batch * seq

    def col(cb):
        return lambda b, n: (b * nb + n, cb)

    d_intra, d_q, d_k = _ret_decay_tables()
    table = lambda lanes: pl.BlockSpec((RET_HEADS, RET_CHUNK, lanes), lambda b, n: (0, 0, 0))
    rows = pl.BlockSpec((WINDOW, D_MODEL), col(0))
    return pl.pallas_call(
        _mixers_kernel,
        grid=(batch, nb),
        in_specs=[pl.BlockSpec(memory_space=pltpu.SMEM),
                  pl.BlockSpec((WINDOW, D_MODEL), col(COL_QA)),
                  pl.BlockSpec((WINDOW, PROJ_TN), lambda b, n: (b * nb + jnp.maximum(n - 1, 0), 0)),
                  pl.BlockSpec((WINDOW, PROJ_TN), col(0)),
                  pl.BlockSpec((RET_CHUNK, 1), col(0)),
                  pl.BlockSpec((1, RET_DIM // 2), lambda b, n: (0, 0)),
                  table(RET_CHUNK), table(RET_DIM // 2), table(RET_DIM // 2),
                  pl.BlockSpec((RET_CHUNK, D_MODEL), col(COL_QR)),
                  pl.BlockSpec((RET_CHUNK, D_MODEL), col(COL_KR)),
                  pl.BlockSpec((RET_CHUNK, D_MODEL), col(COL_VR)),
                  pl.BlockSpec((RET_CHUNK, D_MODEL), col(COL_GR)),
                  pl.BlockSpec((1, D_MODEL), lambda b, n: (0, 0))],
        out_specs=[rows, rows],
        out_shape=[jax.ShapeDtypeStruct((t, D_MODEL), BF16), jax.ShapeDtypeStruct((t, D_MODEL), BF16)],
        scratch_shapes=[pltpu.VMEM((RET_HEADS, RET_DIM, RET_DIM), F32)],
        compiler_params=_params(("arbitrary", "arbitrary"), 40),
        name="mixers",
    )(sinks, proj, proj_kv, proj_kv, pos, inv_freq, d_intra, d_q, d_k, proj, proj, proj, proj, ret_gain)


def _route(logits):
    lane = lax.broadcasted_iota(jnp.int32, logits.shape, 1)
    lane_f = lane.astype(F32)
    is_g = lane < N_GROUPS
    gl = jnp.where(is_g, logits, NEG)
    gmax = jnp.max(gl, axis=-1, keepdims=True)
    gsel = jnp.min(jnp.where(gl == gmax, lane_f, float(LANES)), axis=-1, keepdims=True)
    gsum = jnp.sum(jnp.where(is_g, jnp.exp(gl - gmax), 0.0), axis=-1, keepdims=True)
    g_w = 1.0 / gsum
    grp = ((lane - N_GROUPS) >> 4).astype(F32)
    is_e = (lane >= N_GROUPS) & (lane < N_GROUPS + N_EXPERTS) & (grp == gsel)
    el = jnp.where(is_e, logits, NEG)
    v1 = jnp.max(el, axis=-1, keepdims=True)
    i1 = jnp.min(jnp.where(el == v1, lane_f, float(LANES)), axis=-1, keepdims=True)
    el2 = jnp.where(lane_f == i1, NEG, el)
    v2 = jnp.max(el2, axis=-1, keepdims=True)
    i2 = jnp.min(jnp.where(el2 == v2, lane_f, float(LANES)), axis=-1, keepdims=True)
    tt = jnp.exp(v2 - v1)
    w1 = g_w / (1.0 + tt)
    w2 = g_w * tt / (1.0 + tt)
    return jnp.where(lane == 0, i1 - N_GROUPS,
                     jnp.where(lane == 1, i2 - N_GROUPS,
                               jnp.where(lane == 2, w1, jnp.where(lane == 3, w2, 0.0))))


def _branch_kernel(attn_ref, ret_ref, ga_ref, gr_ref, wa_ref, wr_ref, o_ref):
    a = jnp.dot(attn_ref[...], wa_ref[...].astype(BF16), preferred_element_type=F32)
    r = jnp.dot(ret_ref[...], wr_ref[...].astype(BF16), preferred_element_type=F32)
    o_ref[...] = (jax.nn.sigmoid(ga_ref[...].astype(F32)) * a
                  + jax.nn.sigmoid(gr_ref[...].astype(F32)) * r).astype(BF16)


def _branch(attn, ret, proj, wa, wr):
    t = attn.shape[0]
    tm, tn = BRANCH_TM, BRANCH_TN
    per_slab = D_MODEL // tn
    return pl.pallas_call(
        _branch_kernel,
        grid=(t // tm, D_MODEL // tn),
        in_specs=[pl.BlockSpec((tm, D_MODEL), lambda i, j: (i, 0)),
                  pl.BlockSpec((tm, D_MODEL), lambda i, j: (i, 0)),
                  pl.BlockSpec((tm, tn), lambda i, j: (i, COL_GA * per_slab + j)),
                  pl.BlockSpec((tm, tn), lambda i, j: (i, COL_GRT * per_slab + j)),
                  pl.BlockSpec((D_MODEL, tn), lambda i, j: (0, j)),
                  pl.BlockSpec((D_MODEL, tn), lambda i, j: (0, j))],
        out_specs=pl.BlockSpec((tm, tn), lambda i, j: (i, j)),
        out_shape=jax.ShapeDtypeStruct((t, D_MODEL), BF16),
        compiler_params=_params(("arbitrary", "arbitrary"), 48),
        name="branch",
    )(attn, ret, proj, proj, wa, wr)


def _mixout_kernel(m_ref, x_ref, mod_ref, g2_ref, wo_ref, wrt_ref, brt_ref,
                   x1_ref, h2_ref, route_ref, count_ref, meta_ref, carry_ref):
    @pl.when(pl.program_id(0) == 0)
    def _():
        carry_ref[...] = jnp.zeros_like(carry_ref)

    mix = jnp.dot(m_ref[...], wo_ref[...], preferred_element_type=F32)
    x1 = x_ref[...] + mod_ref[0, 2:3, :] * mix
    x1_ref[...] = x1
    var = jnp.mean(x1 * x1, axis=-1, keepdims=True)
    h2 = x1 * lax.rsqrt(var + EPS) * g2_ref[...]
    h2 = h2 * (1.0 + mod_ref[0, 4:5, :]) + mod_ref[0, 3:4, :]
    tm = h2.shape[0]
    for j in range(SLAB_ROWS):
        lo = h2[:, j * LANES:(j + 1) * LANES]
        hi = h2[:, HALF_D + j * LANES:HALF_D + (j + 1) * LANES]
        h2_ref[_slab_rows(j, tm), :] = _pack_pair(lo, hi)
    h_hi = h2.astype(BF16)
    h_lo = (h2 - h_hi.astype(F32)).astype(BF16)
    w_rt = wrt_ref[...]
    w_hi = w_rt.astype(BF16)
    w_lo = (w_rt - w_hi.astype(F32)).astype(BF16)
    hi_both = jnp.dot(h_hi, jnp.concatenate([w_hi, w_lo], axis=1), preferred_element_type=F32)
    logits = (hi_both[:, :LANES] + hi_both[:, LANES:]
              + jnp.dot(h_lo, w_hi, preferred_element_type=F32) + brt_ref[...])
    route = _route(logits)

    lane = lax.broadcasted_iota(jnp.int32, route.shape, 1)
    lane_f = lane.astype(F32)
    hot1 = lane_f == route[:, 0:1]
    hot2 = lane_f == route[:, 1:2]
    both = jnp.where(hot1 | hot2, 1.0, 0.0)
    ii = lax.broadcasted_iota(jnp.int32, (tm, tm), 0)
    jj = lax.broadcasted_iota(jnp.int32, (tm, tm), 1)
    lower = jnp.where(ii > jj, 1.0, 0.0).astype(BF16)
    before = jnp.dot(lower, both.astype(BF16), preferred_element_type=F32) + carry_ref[...]
    r1 = jnp.sum(jnp.where(hot1, before, 0.0), axis=-1, keepdims=True)
    r2 = jnp.sum(jnp.where(hot2, before, 0.0), axis=-1, keepdims=True)
    route = jnp.where(lane == 4, r1, jnp.where(lane == 5, r2, route))
    route_ref[...] = route
    meta_ref[...] = route.T[0:SLAB_ROWS, :].astype(I32)
    carry = carry_ref[...] + jnp.sum(both, axis=0, keepdims=True)
    carry_ref[...] = carry
    count_ref[...] = carry


def _mixout(merged, x2, mod6, gain2, wo, w_rt, b_rt, seq):
    t = x2.shape[0]
    tm = MIXOUT_TM
    tiles_per_batch = seq // tm
    row = lambda i: (i, 0)
    const = lambda i: (0, 0)
    return pl.pallas_call(
        _mixout_kernel,
        grid=(t // tm,),
        in_specs=[pl.BlockSpec((tm, D_MODEL), row),
                  pl.BlockSpec((tm, D_MODEL), row),
                  pl.BlockSpec((1, 6, D_MODEL), lambda i: (i // tiles_per_batch, 0, 0)),
                  pl.BlockSpec((1, D_MODEL), const),
                  pl.BlockSpec((D_MODEL, D_MODEL), const, pipeline_mode=pl.Buffered(1)),
                  pl.BlockSpec((D_MODEL, LANES), const),
                  pl.BlockSpec((1, LANES), const)],
        out_specs=[pl.BlockSpec((tm, D_MODEL), row),
                   pl.BlockSpec((tm * SLAB_ROWS, SLAB_LANES), row),
                   pl.BlockSpec((tm, LANES), row),
                   pl.BlockSpec((1, LANES), const),
                   pl.BlockSpec((SLAB_ROWS, tm), lambda i: (0, i))],
        out_shape=[jax.ShapeDtypeStruct((t, D_MODEL), F32),
                   jax.ShapeDtypeStruct((t * SLAB_ROWS, SLAB_LANES), I32),
                   jax.ShapeDtypeStruct((t, LANES), F32),
                   jax.ShapeDtypeStruct((1, LANES), F32),
                   jax.ShapeDtypeStruct((SLAB_ROWS, t), I32)],
        scratch_shapes=[pltpu.VMEM((1, LANES), F32)],
        compiler_params=_params(("arbitrary",), 56),
        name="mixout",
    )(merged, x2, mod6, gain2, wo, w_rt, b_rt)


PAD_BITS = (64, 32, 16, 8, 4, 2, 1)


def _sorted_row(tables, tok, k):
    e_refs, r_refs, blk_row_ref = tables[0:TOP_K], tables[TOP_K:2 * TOP_K], tables[2 * TOP_K]
    return blk_row_ref[e_refs[k][tok]] + r_refs[k][tok]


N_ROUTE_TABLES = 2 * TOP_K + 1


def _dispatch_kernel(*refs):
    tables = refs[:N_ROUTE_TABLES]
    zstart_ref, zcount_ref, h2_ref, xs_ref, dest_ref, zero_ref, sem, zsem = refs[N_ROUTE_TABLES:]
    n_assign = tables[0].shape[0] * TOP_K
    zero_ref[...] = jnp.zeros_like(zero_ref)

    def zero_copy(start, rows):
        return pltpu.make_async_copy(zero_ref.at[pl.ds(0, rows * SLAB_ROWS), :],
                                     xs_ref.at[pl.ds(start * SLAB_ROWS, rows * SLAB_ROWS), :], zsem)

    def fill(e, wait):
        start = zstart_ref[e]
        pad = zcount_ref[e]
        for bit in PAD_BITS:
            @pl.when((pad & bit) != 0)
            def _(start=start, bit=bit):
                cp = zero_copy(start, bit)
                cp.wait() if wait else cp.start()
            start = start + (pad & bit)

    lax.fori_loop(0, N_EXPERTS, lambda e, c: (fill(e, False), c)[1], 0)

    def issue(tok, carry):
        src = h2_ref.at[pl.ds(pl.multiple_of(tok * SLAB_ROWS, SLAB_ROWS), SLAB_ROWS), :]
        for k in range(TOP_K):
            row = _sorted_row(tables, tok, k)
            dest_ref[tok * TOP_K + k] = row
            dst = pl.multiple_of(row * SLAB_ROWS, SLAB_ROWS)
            pltpu.make_async_copy(src, xs_ref.at[pl.ds(dst, SLAB_ROWS), :], sem).start(priority=k % 2)
        return carry

    lax.fori_loop(0, n_assign // TOP_K, issue, 0, unroll=8)

    def drain(i, carry):
        pltpu.make_async_copy(h2_ref.at[pl.ds(0, ROWS_PER_WAIT * SLAB_ROWS), :],
                              xs_ref.at[pl.ds(0, ROWS_PER_WAIT * SLAB_ROWS), :], sem).wait()
        return carry

    lax.fori_loop(0, n_assign // ROWS_PER_WAIT, drain, 0)
    lax.fori_loop(0, N_EXPERTS, lambda e, c: (fill(e, True), c)[1], 0)


def _dispatch(tables, zstart, zcount, h2_slab, n_pad):
    n_assign = tables[0].shape[0] * TOP_K
    return pl.pallas_call(
        _dispatch_kernel,
        grid_spec=pltpu.PrefetchScalarGridSpec(
            num_scalar_prefetch=N_ROUTE_TABLES + 2,
            grid=(1,),
            in_specs=[pl.BlockSpec(memory_space=pltpu.VMEM)],
            out_specs=[pl.BlockSpec(memory_space=pl.ANY),
                       pl.BlockSpec(memory_space=pltpu.SMEM)],
            scratch_shapes=[pltpu.VMEM((PAD_BITS[0] * SLAB_ROWS, SLAB_LANES), I32),
                            pltpu.SemaphoreType.DMA(()),
                            pltpu.SemaphoreType.DMA(())]),
        out_shape=[jax.ShapeDtypeStruct((n_pad * SLAB_ROWS, SLAB_LANES), I32),
                   jax.ShapeDtypeStruct((n_assign,), I32)],
        compiler_params=_params(("arbitrary",), 40),
        name="dispatch",
    )(*tables, zstart, zcount, h2_slab)


def _experts_kernel(item_e_ref, item_blk_ref, item_nb_ref, x_blk_ref,
                    x0_ref, x1_ref, x2_ref, x3_ref, wg_hbm, wu_hbm, wd_hbm, y_ref,
                    xb_ref, yp_ref, wg_ref, wu_ref, wd_ref, sem, wsem):
    w = pl.program_id(0)
    n_items = pl.num_programs(0)
    nb = item_nb_ref[w]
    blk0 = item_blk_ref[w]
    x_refs = (x0_ref, x1_ref, x2_ref, x3_ref)

    blk_rows = MOE_BLOCK * SLAB_ROWS

    fc = wg_ref.shape[2]
    n_chunks = EXPERT_DIM // fc

    def weight_copies(item, chunk, slot):
        e = item_e_ref[item]
        col = pl.multiple_of(chunk * fc, fc)
        return (pltpu.make_async_copy(wg_hbm.at[e, :, pl.ds(col, fc)], wg_ref.at[slot], wsem.at[slot]),
                pltpu.make_async_copy(wu_hbm.at[e, :, pl.ds(col, fc)], wu_ref.at[slot], wsem.at[slot]),
                pltpu.make_async_copy(wd_hbm.at[e, pl.ds(col, fc), :], wd_ref.at[slot], wsem.at[slot]))

    def start_weights(tile):
        item = jnp.minimum(tile // n_chunks, n_items - 1)

        @pl.when((tile < n_items * n_chunks) & (item_nb_ref[item] > 0))
        def _():
            for cp in weight_copies(item, tile % n_chunks, tile % EXPERT_WEIGHT_SLOTS):
                cp.start(priority=1)

    @pl.when(w == 0)
    def _():
        for tile in range(EXPERT_WEIGHT_SLOTS - 1):
            start_weights(tile)

    def out_copy(s):
        return pltpu.make_async_copy(yp_ref.at[s], y_ref.at[pl.ds((blk0 + s) * blk_rows, blk_rows), :],
                                     sem.at[s])

    def wait_out(count):
        for s in range(ITEM_BLOCKS):
            @pl.when(s < count)
            def _(s=s):
                out_copy(s).wait()

    def run(n_live):
        rows = n_live * MOE_BLOCK
        for s in range(n_live):
            for j in range(SLAB_ROWS):
                lo, hi = _unpack_pair(x_refs[s][_slab_rows(j, MOE_BLOCK), :])
                r0 = s * MOE_BLOCK
                xb_ref[r0:r0 + MOE_BLOCK, j * LANES:(j + 1) * LANES] = lo.astype(BF16)
                xb_ref[r0:r0 + MOE_BLOCK, HALF_D + j * LANES:HALF_D + (j + 1) * LANES] = hi.astype(BF16)
        xs = xb_ref[0:rows, :]

        y = None
        for c in range(n_chunks):
            tile = w * n_chunks + c
            start_weights(tile + EXPERT_WEIGHT_SLOTS - 1)
            slot = tile % EXPERT_WEIGHT_SLOTS
            for cp in weight_copies(w, c, slot):
                cp.wait()
            g = jnp.dot(xs, wg_ref[slot].astype(BF16), preferred_element_type=F32)
            u = jnp.dot(xs, wu_ref[slot].astype(BF16), preferred_element_type=F32)
            hid = (g * jax.nn.sigmoid(g) * u).astype(BF16)
            y_c = jnp.dot(hid, wd_ref[slot].astype(BF16), preferred_element_type=F32)
            y = y_c if y is None else y + y_c

        @pl.when(w > 0)
        def _():
            wait_out(item_nb_ref[jnp.maximum(w - 1, 0)])

        for s in range(n_live):
            r0 = s * MOE_BLOCK
            for j in range(SLAB_ROWS):
                yp_ref[s, _slab_rows(j, MOE_BLOCK), :] = _pack_pair(
                    y[r0:r0 + MOE_BLOCK, j * LANES:(j + 1) * LANES],
                    y[r0:r0 + MOE_BLOCK, HALF_D + j * LANES:HALF_D + (j + 1) * LANES])
            out_copy(s).start()

    for n_live in range(1, ITEM_BLOCKS + 1):
        pl.when(nb == n_live)(functools.partial(run, n_live))

    @pl.when((nb == 0) & (w > 0))
    def _():
        wait_out(item_nb_ref[jnp.maximum(w - 1, 0)])

    @pl.when(w == n_items - 1)
    def _():
        wait_out(nb)


def _experts(item_e, item_blk, item_nb, x_sorted, w_gate, w_up, w_down, n_blocks):
    n_items = item_e.shape[0]
    d = D_MODEL
    fc = EXPERT_FC
    x3 = x_sorted
    blk_rows = MOE_BLOCK * SLAB_ROWS

    slot = jnp.arange(ITEM_BLOCKS, dtype=jnp.int32)[:, None]
    x_blk = jnp.maximum(lax.cummax(jnp.where(slot < item_nb[None, :], item_blk[None, :] + slot, -1), axis=1), 0)
    x_blk = x_blk.reshape(-1).astype(jnp.int32)

    def x_spec(s):
        return pl.BlockSpec((blk_rows, SLAB_LANES),
                            lambda w, ie, ib, inb, xb: (xb[s * n_items + w], 0))

    hbm = pl.BlockSpec(memory_space=pl.ANY)
    slots = EXPERT_WEIGHT_SLOTS
    return pl.pallas_call(
        _experts_kernel,
        grid_spec=pltpu.PrefetchScalarGridSpec(
            num_scalar_prefetch=4,
            grid=(n_items,),
            in_specs=[x_spec(0), x_spec(1), x_spec(2), x_spec(3), hbm, hbm, hbm],
            out_specs=pl.BlockSpec(memory_space=pl.ANY),
            scratch_shapes=[pltpu.VMEM((ITEM_BLOCKS * MOE_BLOCK, d), BF16),
                            pltpu.VMEM((ITEM_BLOCKS, blk_rows, SLAB_LANES), I32),
                            pltpu.VMEM((slots, d, fc), F32),
                            pltpu.VMEM((slots, d, fc), F32),
                            pltpu.VMEM((slots, fc, d), F32),
                            pltpu.SemaphoreType.DMA((ITEM_BLOCKS,)),
                            pltpu.SemaphoreType.DMA((slots,))]),
        out_shape=jax.ShapeDtypeStruct((n_blocks * blk_rows, SLAB_LANES), I32),
        compiler_params=_params(("arbitrary",), 56),
        name="experts",
    )(item_e, item_blk, item_nb, x_blk, x3, x3, x3, x3, w_gate, w_up, w_down)


def _combine_kernel(dest_ref, x1_ref, route_ref, mod_ref, gain_ref, y_ref, o_ref, ybuf_ref, sem):
    tm = COMBINE_TM
    i = pl.program_id(0)
    n_tiles = pl.num_programs(0)

    def row_copy(slot, k, r, src):
        return pltpu.make_async_copy(
            y_ref.at[pl.ds(pl.multiple_of(src * SLAB_ROWS, SLAB_ROWS), SLAB_ROWS), :],
            ybuf_ref.at[slot, k, pl.ds(pl.multiple_of(r * SLAB_ROWS, SLAB_ROWS), SLAB_ROWS), :],
            sem.at[slot])

    def wait_slot(slot):
        for k in range(TOP_K):
            pltpu.make_async_copy(y_ref.at[pl.ds(0, tm * SLAB_ROWS), :], ybuf_ref.at[slot, k],
                                  sem.at[slot]).wait()

    def issue_tile(tile, slot):
        base = tile * tm * TOP_K

        def issue(r, carry):
            for k in range(TOP_K):
                row_copy(slot, k, r, dest_ref[base + r * TOP_K + k]).start(priority=k % 2)
            return carry

        lax.fori_loop(0, tm, issue, 0, unroll=4)

    @pl.when(i == 0)
    def _():
        for tile in range(COMBINE_SLOTS - 1):
            issue_tile(tile, tile)

    slot = i % COMBINE_SLOTS
    wait_slot(slot)

    ahead = i + COMBINE_SLOTS - 1
    ahead_slot = ahead % COMBINE_SLOTS
    ahead_base = jnp.minimum(ahead, n_tiles - 1) * (tm * TOP_K)
    n_stages = 2 * SLAB_ROWS
    rows_per_stage = tm // n_stages

    route = route_ref[...]
    w0 = route[:, 2:3]
    w1 = route[:, 3:4]
    ssq = jnp.zeros((tm, 1), F32)
    stage = 0
    for j in range(SLAB_ROWS):
        rows = _slab_rows(j, tm)
        y0 = _unpack_pair(ybuf_ref[slot, 0, rows, :])
        y1 = _unpack_pair(ybuf_ref[slot, 1, rows, :])
        for part, off in ((0, j * LANES), (1, HALF_D + j * LANES)):
            cols = slice(off, off + LANES)
            ffn = w0 * y0[part] + w1 * y1[part]
            x2 = x1_ref[:, cols] + mod_ref[0, 5:6, cols] * ffn
            o_ref[:, cols] = x2
            ssq = ssq + jnp.sum(x2 * x2, axis=-1, keepdims=True)
            for r in range(stage * rows_per_stage, (stage + 1) * rows_per_stage):
                for k in range(TOP_K):
                    row_copy(ahead_slot, k, r, dest_ref[ahead_base + r * TOP_K + k]).start(priority=k % 2)
            stage += 1
    o_ref[...] = o_ref[...] * lax.rsqrt(ssq * (1.0 / D_MODEL) + EPS) * gain_ref[...]

    @pl.when(i == n_tiles - 1)
    def _():
        for extra in range(1, COMBINE_SLOTS):
            wait_slot((i + extra) % COMBINE_SLOTS)


def _combine(dest, x1, route, mod6, gain, y_sorted, seq):
    t, d = x1.shape
    tm = COMBINE_TM
    tiles_per_batch = seq // tm
    return pl.pallas_call(
        _combine_kernel,
        grid_spec=pltpu.PrefetchScalarGridSpec(
            num_scalar_prefetch=1,
            grid=(t // tm,),
            in_specs=[pl.BlockSpec((tm, d), lambda i, *_: (i, 0)),
                      pl.BlockSpec((tm, LANES), lambda i, *_: (i, 0)),
                      pl.BlockSpec((1, 6, d), lambda i, *_: (i // tiles_per_batch, 0, 0)),
                      pl.BlockSpec((1, d), lambda i, *_: (0, 0)),
                      pl.BlockSpec(memory_space=pl.ANY)],
            out_specs=pl.BlockSpec((tm, d), lambda i, *_: (i, 0)),
            scratch_shapes=[pltpu.VMEM((COMBINE_SLOTS, TOP_K, tm * SLAB_ROWS, SLAB_LANES), I32),
                            pltpu.SemaphoreType.DMA((COMBINE_SLOTS,))]),
        out_shape=jax.ShapeDtypeStruct((t, d), F32),
        compiler_params=_params(("arbitrary",), 24),
        name="combine",
    )(dest, x1, route, mod6, gain, y_sorted)


def _dispatch_tables(meta, counts, t):
    n_assign = t * TOP_K
    n_pad = -(-(n_assign + N_EXPERTS * (MOE_BLOCK - 1)) // MOE_BLOCK) * MOE_BLOCK
    n_blocks = n_pad // MOE_BLOCK
    n_items = N_EXPERTS + n_assign // (ITEM_BLOCKS * MOE_BLOCK)

    cnt = counts[0, :N_EXPERTS].astype(jnp.int32)
    blocks_e = (cnt + MOE_BLOCK - 1) // MOE_BLOCK
    blk_end = jnp.cumsum(blocks_e)
    blk_start = blk_end - blocks_e
    tables = (meta[0], meta[1], meta[4], meta[5], (blk_start * MOE_BLOCK).astype(jnp.int32))
    zstart = (blk_start * MOE_BLOCK + cnt).astype(jnp.int32)
    zcount = (blocks_e * MOE_BLOCK - cnt).astype(jnp.int32)

    items_e = (blocks_e + ITEM_BLOCKS - 1) // ITEM_BLOCKS
    item_end = jnp.cumsum(items_e)
    item_start = item_end - items_e
    w = jnp.arange(n_items, dtype=jnp.int32)
    live = w < item_end[-1]
    w_live = jnp.minimum(w, item_end[-1] - 1)
    e_w = jnp.minimum(jnp.sum((item_end[None, :] <= w_live[:, None]).astype(jnp.int32), axis=1), N_EXPERTS - 1)
    j_w = w_live - item_start[e_w]
    item_blk = (blk_start[e_w] + ITEM_BLOCKS * j_w).astype(jnp.int32)
    item_nb = jnp.where(live, jnp.clip(blocks_e[e_w] - ITEM_BLOCKS * j_w, 0, ITEM_BLOCKS), 0).astype(jnp.int32)
    return tables, zstart, zcount, e_w, item_blk, item_nb, n_blocks


def kernel(x, c, positions, norm1_gain, norm2_gain, final_norm_gain, w_ada, b_ada, w_in, attn_sinks,
           ret_norm_gain, w_branch_attn, w_branch_ret, w_out, w_router_group, b_router_group,
           w_router_expert, b_router_expert, w_expert_gate, w_expert_up, w_expert_down):
    batch, seq, d = x.shape
    t = batch * seq
    depth = w_ada.shape[0]
    half = RET_DIM // 2
    inv_freq = (ROPE_BASE ** (-jnp.arange(half, dtype=F32) / half)).reshape(1, half)
    pos = positions.reshape(t, 1)
    c8 = jnp.pad(c, ((0, 8 - batch), (0, 0)))
    xf = x.reshape(t, d)

    assert depth == 1, "the fused final norm assumes a single layer"
    for layer in range(depth):
        b_ada2 = b_ada[layer].reshape(1, -1)
        mod_early = _ada(c8, w_ada[layer], b_ada2, ADA_EARLY)[:batch]
        proj, proj_kv, mod_late = _proj(xf, norm1_gain[layer].reshape(1, d), mod_early.reshape(batch, 2, d),
                                        w_in[layer], c8, w_ada[layer], b_ada2, seq)
        mod6 = jnp.concatenate([mod_early, mod_late[:batch]], axis=1).reshape(batch, 6, d)
        attn, ret = _mixers(proj, proj_kv, attn_sinks[layer], pos, inv_freq,
                            ret_norm_gain[layer].reshape(1, d), batch, seq)

        pad = LANES - N_GROUPS - N_EXPERTS
        w_rt = jnp.concatenate([w_router_group[layer], w_router_expert[layer],
                                jnp.zeros((d, pad), F32)], axis=1)
        b_rt = jnp.concatenate([b_router_group[layer], b_router_expert[layer],
                                jnp.zeros((pad,), F32)]).reshape(1, LANES)
        merged = _branch(attn, ret, proj, w_branch_attn[layer], w_branch_ret[layer])
        x1, h2, route, counts, meta = _mixout(merged, xf, mod6, norm2_gain[layer].reshape(1, d),
                                              w_out[layer].astype(BF16), w_rt, b_rt, seq)
        tables, zstart, zcount, item_e, item_blk, item_nb, n_blocks = _dispatch_tables(meta, counts, t)
        x_sorted, dest = _dispatch(tables, zstart, zcount, h2, n_blocks * MOE_BLOCK)
        y_sorted = _experts(item_e, item_blk, item_nb, x_sorted,
                            w_expert_gate[layer], w_expert_up[layer], w_expert_down[layer], n_blocks)
        xf = _combine(dest, x1, route, mod6, final_norm_gain.reshape(1, d), y_sorted, seq)
    return xf.reshape(batch, seq, d)
```
